```python
import math
import jax, jax.numpy as jnp
from jax import lax
import numpy as np

D_MODEL = 2048
BATCH = 32
SEQ = 256
DEPTH = 2
DEC_BATCH = 4
DEC_SEQ = 4096
PAST_LEN = 256

GRID_W = 64
Q_BLOCK = 128
ROPE_BASE = 10000.0
NORM_EPS = 1e-6
HEAD_DIM_A = 128
N_HEADS_A = (D_MODEL // 2) // HEAD_DIM_A
N_KV_HEADS_A = 2
N_HEADS_B = 4
DV_B = (D_MODEL // 4) // N_HEADS_B
DK_B = DV_B // 2
N_FOURIER_GROUPS = 4
FOURIER_GROUP_DIM = (D_MODEL // 4) // N_FOURIER_GROUPS
D_FF = 256 * ((8 * D_MODEL // 3 + 255) // 256)
N_MOD = 6
COLS_QA = N_HEADS_A * HEAD_DIM_A
COLS_KVA = N_KV_HEADS_A * HEAD_DIM_A
COLS_QKB = N_HEADS_B * 2 * DK_B
COLS_VB = N_HEADS_B * DV_B
COLS_C = N_FOURIER_GROUPS * FOURIER_GROUP_DIM
D_IN = COLS_QA + 2 * COLS_KVA + 2 * COLS_QKB + COLS_VB + COLS_C
MIX_OUT = COLS_QA + COLS_VB + COLS_C
SPLIT_AT = (COLS_QA,
            COLS_QA + COLS_KVA,
            COLS_QA + 2 * COLS_KVA,
            COLS_QA + 2 * COLS_KVA + COLS_QKB,
            COLS_QA + 2 * COLS_KVA + 2 * COLS_QKB,
            COLS_QA + 2 * COLS_KVA + 2 * COLS_QKB + COLS_VB)

kernel_name = "hybrid_dit_prefix_ctx_step"


def rms_norm(x, g):
    xf = x.astype(jnp.float32)
    y = xf * lax.rsqrt(jnp.mean(xf * xf, axis=-1, keepdims=True) + NORM_EPS)
    return (y * g.astype(jnp.float32)).astype(x.dtype)


def axial_rope_tables(n, head_dim):
    rows = n // GRID_W
    t_row = jnp.repeat(jnp.arange(rows, dtype=jnp.float32), GRID_W)
    t_col = jnp.tile(jnp.arange(GRID_W, dtype=jnp.float32), rows)
    axis_dim = head_dim // 2
    inv = jnp.power(ROPE_BASE, -jnp.arange(0, axis_dim, 2, dtype=jnp.float32) / axis_dim)
    ar = t_row[:, None] * inv[None, :]
    ac = t_col[:, None] * inv[None, :]
    ang = jnp.concatenate([ar, ar, ac, ac], axis=-1)
    return jnp.cos(ang), jnp.sin(ang)


def apply_rope(x, cos, sin):
    xf = x.astype(jnp.float32)
    a, b, c, d = jnp.split(xf, 4, axis=-1)
    rot = jnp.concatenate([-b, a, -d, c], axis=-1)
    return (xf * cos[None, :, None, :] + rot * sin[None, :, None, :]).astype(x.dtype)


def sweep_query_blocks(fn, *qs):
    b, n = qs[0].shape[:2]
    nb = n // Q_BLOCK
    blocks = tuple(jnp.moveaxis(q.reshape(b, nb, Q_BLOCK, *q.shape[2:]), 1, 0) for q in qs)
    out = lax.map(lambda args: fn(*args), blocks)
    return jnp.moveaxis(out, 0, 1).reshape(b, n, *out.shape[3:])


def gqa_attention(q, k, v):
    b = q.shape[0]
    grp = N_HEADS_A // N_KV_HEADS_A
    scale = HEAD_DIM_A ** -0.5

    def block(qb):
        qb = qb.reshape(b, Q_BLOCK, N_KV_HEADS_A, grp, HEAD_DIM_A)
        s = jnp.einsum('bqkgd,bmkd->bkgqm', qb, k).astype(jnp.float32) * scale
        pr = jax.nn.softmax(s, axis=-1).astype(v.dtype)
        o = jnp.einsum('bkgqm,bmkd->bqkgd', pr, v)
        return o.reshape(b, Q_BLOCK, N_HEADS_A * HEAD_DIM_A)

    return sweep_query_blocks(block, q)


def diff_attention(q1, q2, k1, k2, v, lam):
    scale = DK_B ** -0.5

    def block(q1b, q2b):
        s1 = jnp.einsum('bqhd,bmhd->bhqm', q1b, k1).astype(jnp.float32) * scale
        s2 = jnp.einsum('bqhd,bmhd->bhqm', q2b, k2).astype(jnp.float32) * scale
        w = jax.nn.softmax(s1, axis=-1) - lam * jax.nn.softmax(s2, axis=-1)
        return jnp.einsum('bhqm,bmhd->bqhd', w.astype(v.dtype), v)

    return sweep_query_blocks(block, q1, q2)


def fourier_mix(f):
    b, n = f.shape[:2]
    z = jnp.fft.fft2(f.astype(jnp.float32), axes=(1, 3), norm='ortho').real
    return z.astype(f.dtype).reshape(b, n, N_FOURIER_GROUPS * FOURIER_GROUP_DIM)


def conv_ffn(h, w_gate, w_up, conv_w, conv_b, w_down):
    g = h @ w_gate
    gp = jnp.pad(g, ((0, 0), (1, 1), (0, 0)))
    g = gp[:, :-2] * conv_w[0] + gp[:, 1:-1] * conv_w[1] + gp[:, 2:] * conv_w[2] + conv_b
    return (jax.nn.silu(g) * (h @ w_up)) @ w_down


def token_mix(h, p, l, rope_a, rope_b, ctx):
    b, n = h.shape[:2]
    u = h @ p['w_in'][l]
    qa, ka, va, qb, kb, vb, f = jnp.split(u, SPLIT_AT, axis=-1)
    qa = rms_norm(qa.reshape(b, n, N_HEADS_A, HEAD_DIM_A), p['attn_q_norm_g'][l])
    ka = rms_norm(ka.reshape(b, n, N_KV_HEADS_A, HEAD_DIM_A), p['attn_k_norm_g'][l])
    va = va.reshape(b, n, N_KV_HEADS_A, HEAD_DIM_A)
    qb = qb.reshape(b, n, 2 * N_HEADS_B, DK_B)
    kb = kb.reshape(b, n, N_HEADS_B, 2 * DK_B)
    vb = vb.reshape(b, n, N_HEADS_B, DV_B)
    if ctx is None:
        new_ctx = (ka, va, kb, vb)
        keys_a, vals_a, keys_b, vals_b = ka, va, kb, vb
    else:
        cos_a, sin_a = rope_a
        cos_b, sin_b = rope_b
        qa = apply_rope(qa, cos_a, sin_a)
        ka = apply_rope(ka, cos_a, sin_a)
        qb = apply_rope(qb, cos_b, sin_b)
        kb = apply_rope(kb.reshape(b, n, 2 * N_HEADS_B, DK_B), cos_b, sin_b).reshape(b, n, N_HEADS_B, 2 * DK_B)
        ck_a, cv_a, ck_b, cv_b = ctx
        keys_a = jnp.concatenate([ck_a, ka], axis=1)
        vals_a = jnp.concatenate([cv_a, va], axis=1)
        keys_b = jnp.concatenate([ck_b, kb], axis=1)
        vals_b = jnp.concatenate([cv_b, vb], axis=1)
        new_ctx = None
    m = keys_b.shape[1]
    att_a = gqa_attention(qa, keys_a, vals_a)
    lam_init = 0.8 - 0.6 * math.exp(-0.3 * l)
    lam = (jnp.exp(jnp.sum(p['diff_lambda_q1'][l].astype(jnp.float32) * p['diff_lambda_k1'][l].astype(jnp.float32)))
           - jnp.exp(jnp.sum(p['diff_lambda_q2'][l].astype(jnp.float32) * p['diff_lambda_k2'][l].astype(jnp.float32)))
           + lam_init)
    qb5 = qb.reshape(b, n, N_HEADS_B, 2, DK_B)
    kb5 = keys_b.reshape(b, m, N_HEADS_B, 2, DK_B)
    att_b = diff_attention(qb5[..., 0, :], qb5[..., 1, :], kb5[..., 0, :], kb5[..., 1, :], vals_b, lam)
    att_b = (rms_norm(att_b, p['diff_subnorm_g'][l]) * (1.0 - lam_init)).reshape(b, n, COLS_VB)
    four = fourier_mix(f.reshape(b, n, N_FOURIER_GROUPS, FOURIER_GROUP_DIM))
    out = jnp.concatenate([att_a, att_b, four], axis=-1) @ p['w_out'][l]
    return out, new_ctx


def trunk_layer(x, cvec, p, l, rope_a, rope_b, ctx):
    mod = (jax.nn.silu(cvec) @ p['w_ada'][l] + p['b_ada'][l]).reshape(cvec.shape[0], N_MOD, D_MODEL)
    sh1, sc1, g1, sh2, sc2, g2 = [mod[:, i, None, :] for i in range(N_MOD)]
    h = rms_norm(x, p['norm1_g'][l]) * (1.0 + sc1) + sh1
    mix, new_ctx = token_mix(h, p, l, rope_a, rope_b, ctx)
    x = x + g1 * mix
    h = rms_norm(x, p['norm2_g'][l]) * (1.0 + sc2) + sh2
    x = x + g2 * conv_ffn(h, p['ffn_w_gate'][l], p['ffn_w_up'][l], p['ffn_conv_w'][l],
                          p['ffn_conv_b'][l], p['ffn_w_down'][l])
    return x, new_ctx


def setup_inputs(seed: int = 0) -> dict:
    key = jax.random.key(seed)
    ks = jax.random.split(key, 32)

    def nrm(k, shape, s=1.0):
        return jax.random.normal(k, shape, jnp.float32) * s

    return {
        'x_prompt': nrm(ks[0], (BATCH, SEQ, D_MODEL)),
        'x_sample': nrm(ks[1], (DEC_BATCH, DEC_SEQ, D_MODEL)),
        'cache_attn_k': nrm(ks[2], (DEC_BATCH, DEPTH, PAST_LEN, N_KV_HEADS_A, HEAD_DIM_A)),
        'cache_attn_v': nrm(ks[3], (DEC_BATCH, DEPTH, PAST_LEN, N_KV_HEADS_A, HEAD_DIM_A)),
        'cache_diff_k': nrm(ks[4], (DEC_BATCH, DEPTH, PAST_LEN, N_HEADS_B, 2 * DK_B)),
        'cache_diff_v': nrm(ks[5], (DEC_BATCH, DEPTH, PAST_LEN, N_HEADS_B, DV_B)),
        'c': nrm(ks[6], (DEC_BATCH, D_MODEL)),
        'c_ctx': nrm(ks[7], (D_MODEL,)),
        'norm1_g': 1.0 + nrm(ks[8], (DEPTH, D_MODEL), 0.02),
        'norm2_g': 1.0 + nrm(ks[9], (DEPTH, D_MODEL), 0.02),
        'w_ada': nrm(ks[10], (DEPTH, D_MODEL, N_MOD * D_MODEL), 0.5 * D_MODEL ** -0.5),
        'b_ada': nrm(ks[11], (DEPTH, N_MOD * D_MODEL), 0.01),
        'w_in': nrm(ks[12], (DEPTH, D_MODEL, D_IN), D_MODEL ** -0.5),
        'attn_q_norm_g': 1.0 + nrm(ks[13], (DEPTH, HEAD_DIM_A), 0.02),
        'attn_k_norm_g': 1.0 + nrm(ks[14], (DEPTH, HEAD_DIM_A), 0.02),
        'diff_lambda_q1': nrm(ks[15], (DEPTH, DK_B), 0.1),
        'diff_lambda_k1': nrm(ks[16], (DEPTH, DK_B), 0.1),
        'diff_lambda_q2': nrm(ks[17], (DEPTH, DK_B), 0.1),
        'diff_lambda_k2': nrm(ks[18], (DEPTH, DK_B), 0.1),
        'diff_subnorm_g': 1.0 + nrm(ks[19], (DEPTH, DV_B), 0.02),
        'w_out': nrm(ks[20], (DEPTH, MIX_OUT, D_MODEL), MIX_OUT ** -0.5),
        'ffn_w_gate': nrm(ks[21], (DEPTH, D_MODEL, D_FF), D_MODEL ** -0.5),
        'ffn_w_up': nrm(ks[22], (DEPTH, D_MODEL, D_FF), D_MODEL ** -0.5),
        'ffn_conv_w': nrm(ks[23], (DEPTH, 3, D_FF), 3 ** -0.5),
        'ffn_conv_b': nrm(ks[24], (DEPTH, D_FF), 0.01),
        'ffn_w_down': nrm(ks[25], (DEPTH, D_FF, D_MODEL), D_FF ** -0.5),
        'final_norm_g': 1.0 + nrm(ks[26], (D_MODEL,), 0.02),
    }


def reference(x_prompt, x_sample, cache_attn_k, cache_attn_v, cache_diff_k, cache_diff_v, c, c_ctx,
              norm1_g, norm2_g, w_ada, b_ada, w_in, attn_q_norm_g, attn_k_norm_g,
              diff_lambda_q1, diff_lambda_k1, diff_lambda_q2, diff_lambda_k2, diff_subnorm_g,
              w_out, ffn_w_gate, ffn_w_up, ffn_conv_w, ffn_conv_b, ffn_w_down, final_norm_g):
    p = dict(norm1_g=norm1_g, norm2_g=norm2_g, w_ada=w_ada, b_ada=b_ada, w_in=w_in,
             attn_q_norm_g=attn_q_norm_g, attn_k_norm_g=attn_k_norm_g,
             diff_lambda_q1=diff_lambda_q1, diff_lambda_k1=diff_lambda_k1,
             diff_lambda_q2=diff_lambda_q2, diff_lambda_k2=diff_lambda_k2,
             diff_subnorm_g=diff_subnorm_g, w_out=w_out, ffn_w_gate=ffn_w_gate,
             ffn_w_up=ffn_w_up, ffn_conv_w=ffn_conv_w, ffn_conv_b=ffn_conv_b, ffn_w_down=ffn_w_down)

    x = x_prompt
    ctx_cond = c_ctx[None, :]
    ka_l, va_l, kb_l, vb_l = [], [], [], []
    for l in range(DEPTH):
        x, (ka, va, kb, vb) = trunk_layer(x, ctx_cond, p, l, None, None, None)
        ka_l.append(ka)
        va_l.append(va)
        kb_l.append(kb)
        vb_l.append(vb)
    y_prompt = rms_norm(x, final_norm_g)
    new_attn_k = jnp.stack(ka_l, axis=1)
    new_attn_v = jnp.stack(va_l, axis=1)
    new_diff_k = jnp.stack(kb_l, axis=1)
    new_diff_v = jnp.stack(vb_l, axis=1)

    n_lat = x_sample.shape[1]
    rope_a = axial_rope_tables(n_lat, HEAD_DIM_A)
    rope_b = axial_rope_tables(n_lat, DK_B)
    x = x_sample
    for l in range(DEPTH):
        ctx = (cache_attn_k[:, l], cache_attn_v[:, l], cache_diff_k[:, l], cache_diff_v[:, l])
        x, _ = trunk_layer(x, c, p, l, rope_a, rope_b, ctx)
    y_sample = rms_norm(x, final_norm_g)

    return (y_prompt, y_sample, new_attn_k, new_attn_v, new_diff_k, new_diff_v)
```

```python
import functools
import math

import jax
import jax.numpy as jnp
from jax import lax
from jax.experimental import pallas as pl
from jax.experimental.pallas import tpu as pltpu

F32 = jnp.float32
BF16 = jnp.bfloat16

GRID_W = 64
ROPE_BASE = 10000.0
NORM_EPS = 1e-6
HEAD_DIM_A = 128
N_KV_HEADS_A = 2
GQA_GROUP = 4
N_HEADS_B = 4
DK_B = 64
DV_B = 128
N_FOURIER_GROUPS = 4
FOURIER_GROUP_DIM = 128
N_MOD = 6

LANES = 128
VMEM_LIMIT = 56 * 1024 * 1024


def _cparams(sem):
    return pltpu.CompilerParams(dimension_semantics=sem, vmem_limit_bytes=VMEM_LIMIT)


def _rms(x, g):
    return x * lax.rsqrt(jnp.mean(x * x, axis=-1, keepdims=True) + NORM_EPS) * g


def _ada_kernel(c_ref, w_ref, b_ref, o_ref):
    c = c_ref[...]
    s = (c * jax.nn.sigmoid(c)).astype(BF16)
    w = w_ref[...].astype(BF16)
    o_ref[...] = jnp.dot(s, w, preferred_element_type=F32) + b_ref[...]


def _ada_call(cvec, w_ada, b_ada):
    depth, d, n = w_ada.shape
    rows = cvec.shape[0]
    tn = 512
    return pl.pallas_call(
        _ada_kernel,
        out_shape=jax.ShapeDtypeStruct((depth, rows, n), F32),
        grid=(depth, n // tn),
        in_specs=[
            pl.BlockSpec((rows, d), lambda l, j: (0, 0)),
            pl.BlockSpec((None, d, tn), lambda l, j: (l, 0, j)),
            pl.BlockSpec((None, 1, tn), lambda l, j: (l, 0, j)),
        ],
        out_specs=pl.BlockSpec((None, rows, tn), lambda l, j: (l, 0, j)),
        compiler_params=_cparams(("parallel", "parallel")),
        name="ada_mod",
    )(cvec, w_ada, b_ada.reshape(depth, 1, n))


def _rope(xs, cos, sin_signed, shift):
    w = xs.shape[-1]
    lane = lax.broadcasted_iota(jnp.int32, xs.shape, 1)
    first = (lane % (2 * shift)) < shift
    rot = jnp.where(first, pltpu.roll(xs, w - shift, 1), pltpu.roll(xs, shift, 1))
    return xs * cos + rot * sin_signed


def _in_kernel(*refs, rope, ctx_out, cols):
    it = iter(refs)
    x_ref, mod_ref, g_ref, w_ref, gq_ref, gk_ref, dft_ref = (next(it) for _ in range(7))
    if rope:
        cos_a, sin_a, cos_b, sin_b = (next(it)[...] for _ in range(4))
    qkv_ref, y_ref = next(it), next(it)
    kvc_ref = next(it) if ctx_out else None
    c_qa, c_ka, c_va, c_qb, c_kb, c_vb, c_f, c_end = cols
    kvb0 = c_qb - c_ka

    m = mod_ref[...]
    h = _rms(x_ref[...], g_ref[...])
    hb = (h * (1.0 + m[1:2]) + m[0:1]).astype(BF16)
    gq = gq_ref[...]
    gk = gk_ref[...]
    scale_a = HEAD_DIM_A ** -0.5
    scale_b = DK_B ** -0.5
    chunk = 4 * LANES

    for c0 in range(0, c_end, chunk):
        acc = jnp.dot(hb, w_ref[:, c0:c0 + chunk], preferred_element_type=F32)
        for s in range(chunk // LANES):
            col = c0 + s * LANES
            v = acc[:, s * LANES:(s + 1) * LANES]
            if col < c_ka:
                v = _rms(v, gq)
                if rope:
                    v = _rope(v, cos_a, sin_a, HEAD_DIM_A // 4)
                qkv_ref[:, col:col + LANES] = (v * scale_a).astype(BF16)
            elif col < c_va:
                v = _rms(v, gk)
                if ctx_out:
                    kvc_ref[:, col - c_ka:col - c_ka + LANES] = v
                if rope:
                    v = _rope(v, cos_a, sin_a, HEAD_DIM_A // 4)
                qkv_ref[:, col:col + LANES] = v.astype(BF16)
            elif col < c_qb:
                if ctx_out:
                    kvc_ref[:, col - c_ka:col - c_ka + LANES] = v
                qkv_ref[:, col:col + LANES] = v.astype(BF16)
            elif col < c_kb:
                if rope:
                    v = _rope(v, cos_b, sin_b, DK_B // 4)
                qkv_ref[:, col:col + LANES] = (v * scale_b).astype(BF16)
            elif col < c_vb:
                if ctx_out:
                    kvc_ref[:, col - c_kb + kvb0:col - c_kb + kvb0 + LANES] = v
                if rope:
                    v = _rope(v, cos_b, sin_b, DK_B // 4)
                qkv_ref[:, col:col + LANES] = v.astype(BF16)
            elif col < c_f:
                if ctx_out:
                    kvc_ref[:, col - c_kb + kvb0:col - c_kb + kvb0 + LANES] = v
                qkv_ref[:, col:col + LANES] = v.astype(BF16)
            else:
                yy = jnp.dot(v.astype(BF16), dft_ref[...], preferred_element_type=F32)
                gcol = col - c_f
                half = c_end - c_f
                y_ref[:, gcol:gcol + LANES] = yy[:, :LANES].astype(BF16)
                y_ref[:, half + gcol:half + gcol + LANES] = yy[:, LANES:].astype(BF16)


def _in_call(x2d, mod_l, cond_row_fn, norm_g, w_in_b, gq, gk, dft_c, rope_tabs, seq_len, ctx_out, tm):
    t, d = x2d.shape
    d_in = w_in_b.shape[1]
    c_qa = 0
    c_ka = N_KV_HEADS_A * GQA_GROUP * HEAD_DIM_A
    c_va = c_ka + N_KV_HEADS_A * HEAD_DIM_A
    c_qb = c_va + N_KV_HEADS_A * HEAD_DIM_A
    c_kb = c_qb + N_HEADS_B * 2 * DK_B
    c_vb = c_kb + N_HEADS_B * 2 * DK_B
    c_f = c_vb + N_HEADS_B * DV_B
    c_end = c_f + N_FOURIER_GROUPS * FOURIER_GROUP_DIM
    assert c_end == d_in
    cols = (c_qa, c_ka, c_va, c_qb, c_kb, c_vb, c_f, c_end)
    rope = rope_tabs is not None
    n_f = c_end - c_f

    in_specs = [
        pl.BlockSpec((tm, d), lambda i: (i, 0)),
        pl.BlockSpec((None, N_MOD, d), lambda i: (cond_row_fn(i), 0, 0)),
        pl.BlockSpec((1, d), lambda i: (0, 0)),
        pl.BlockSpec((d, d_in), lambda i: (0, 0), pipeline_mode=pl.Buffered(1)),
        pl.BlockSpec((1, HEAD_DIM_A), lambda i: (0, 0)),
        pl.BlockSpec((1, HEAD_DIM_A), lambda i: (0, 0)),
        pl.BlockSpec((FOURIER_GROUP_DIM, 2 * FOURIER_GROUP_DIM), lambda i: (0, 0)),
    ]
    args = [x2d, mod_l, norm_g, w_in_b, gq, gk, dft_c]
    if rope:
        nblk = seq_len // tm
        for tab in rope_tabs:
            in_specs.append(pl.BlockSpec((tm, LANES), lambda i: (i % nblk, 0)))
            args.append(tab)
    out_shape = [jax.ShapeDtypeStruct((t, c_f), BF16), jax.ShapeDtypeStruct((t, 2 * n_f), BF16)]
    out_specs = [pl.BlockSpec((tm, c_f), lambda i: (i, 0)), pl.BlockSpec((tm, 2 * n_f), lambda i: (i, 0))]
    if ctx_out:
        n_kv = (c_qb - c_ka) + (c_f - c_kb)
        out_shape.append(jax.ShapeDtypeStruct((t, n_kv), F32))
        out_specs.append(pl.BlockSpec((tm, n_kv), lambda i: (i, 0)))
    return pl.pallas_call(
        functools.partial(_in_kernel, rope=rope, ctx_out=ctx_out, cols=cols),
        out_shape=out_shape,
        grid=(t // tm,),
        in_specs=in_specs,
        out_specs=out_specs,
        compiler_params=_cparams(("parallel",)),
        name="in_proj_ctx" if ctx_out else "in_proj_lat",
    )(*args)


def _online_softmax(q, srcs):
    r = q.shape[0]
    dv = srcs[0][1].shape[-1]

    def step(k, v, carry):
        m, l, acc = carry
        s = lax.dot_general(q, k, (((1,), (1,)), ((), ())), preferred_element_type=F32)
        m_new = jnp.maximum(m, jnp.max(s, axis=-1, keepdims=True))
        alpha = jnp.exp(m - m_new)
        p = jnp.exp(s - m_new)
        l = alpha * l + jnp.sum(p, axis=-1, keepdims=True)
        acc = alpha * acc + jnp.dot(p.astype(BF16), v, preferred_element_type=F32)
        return m_new, l, acc

    carry = (jnp.full((r, 1), -jnp.inf, F32), jnp.zeros((r, 1), F32), jnp.zeros((r, dv), F32))
    for k_ref, v_ref, n_keys, kc in srcs:
        if n_keys == kc:
            carry = step(k_ref[...].astype(BF16), v_ref[...].astype(BF16), carry)
        else:
            def body(c, carry, k_ref=k_ref, v_ref=v_ref, kc=kc):
                off = pl.multiple_of(c * kc, kc)
                return step(k_ref[pl.ds(off, kc), :].astype(BF16), v_ref[pl.ds(off, kc), :].astype(BF16), carry)
            carry = lax.fori_loop(0, n_keys // kc, body, carry)
    _, l, acc = carry
    return acc / l


def _srcs(cache_refs, k_ref, v_ref, kc):
    srcs = []
    if cache_refs is not None:
        ck, cv = cache_refs
        srcs.append((ck, cv, ck.shape[0], ck.shape[0]))
    n = k_ref.shape[0]
    srcs.append((k_ref, v_ref, n, min(kc, n)))
    return srcs


def _attn_a_kernel(*refs, has_cache, tq, kc):
    if has_cache:
        q_ref, ck_ref, cv_ref, k_ref, v_ref, o_ref = refs
        cache = (ck_ref, cv_ref)
    else:
        q_ref, k_ref, v_ref, o_ref = refs
        cache = None
    q = q_ref[...]
    q4 = jnp.concatenate([q[:, h * LANES:(h + 1) * LANES] for h in range(GQA_GROUP)], axis=0)
    o = _online_softmax(q4, _srcs(cache, k_ref, v_ref, kc))
    for h in range(GQA_GROUP):
        o_ref[:, h * LANES:(h + 1) * LANES] = o[h * tq:(h + 1) * tq].astype(BF16)


def _attn_b_kernel(*refs, has_cache, tq, kc, lam_init):
    if has_cache:
        lam_ref, gs_ref, q_ref, ck_ref, cv_ref, k_ref, v_ref, o_ref = refs
        cache = (ck_ref, cv_ref)
    else:
        lam_ref, gs_ref, q_ref, k_ref, v_ref, o_ref = refs
        cache = None
    q = q_ref[...]
    lane = lax.broadcasted_iota(jnp.int32, q.shape, 1)
    zero = jnp.zeros_like(q)
    qz = jnp.concatenate([jnp.where(lane < DK_B, q, zero), jnp.where(lane >= DK_B, q, zero)], axis=0)
    o = _online_softmax(qz, _srcs(cache, k_ref, v_ref, kc))
    lp = lam_ref[...]
    lam = (jnp.exp(jnp.sum(lp[0:1] * lp[1:2], axis=-1, keepdims=True))
           - jnp.exp(jnp.sum(lp[2:3] * lp[3:4], axis=-1, keepdims=True)) + lam_init)
    dlt = o[:tq] - lam * o[tq:]
    o_ref[...] = (_rms(dlt, gs_ref[...]) * (1.0 - lam_init)).astype(BF16)


def _attn_calls(qkv3, caches, layer, lam_params, g_sub, lam_init, tq_a, tq_b, kc):
    b, n, _ = qkv3.shape
    has_cache = caches is not None
    qa_blk = GQA_GROUP * HEAD_DIM_A // LANES
    k_a0 = N_KV_HEADS_A * qa_blk
    v_a0 = k_a0 + N_KV_HEADS_A
    q_b0 = v_a0 + N_KV_HEADS_A
    k_b0 = q_b0 + N_HEADS_B
    v_b0 = k_b0 + N_HEADS_B

    in_specs = [pl.BlockSpec((None, tq_a, GQA_GROUP * LANES), lambda bi, h, i: (bi, i, h))]
    args = [qkv3]
    if has_cache:
        ck, cv = caches[0], caches[1]
        p = ck.shape[2]
        in_specs += [pl.BlockSpec((None, None, p, LANES), lambda bi, h, i: (bi, layer, 0, h))] * 2
        args += [ck, cv]
    in_specs += [pl.BlockSpec((None, n, LANES), lambda bi, h, i: (bi, 0, k_a0 + h)),
                 pl.BlockSpec((None, n, LANES), lambda bi, h, i: (bi, 0, v_a0 + h))]
    args += [qkv3, qkv3]
    att_a = pl.pallas_call(
        functools.partial(_attn_a_kernel, has_cache=has_cache, tq=tq_a, kc=kc),
        out_shape=jax.ShapeDtypeStruct((b, n, N_KV_HEADS_A * GQA_GROUP * HEAD_DIM_A), BF16),
        grid=(b, N_KV_HEADS_A, n // tq_a),
        in_specs=in_specs,
        out_specs=pl.BlockSpec((None, tq_a, GQA_GROUP * LANES), lambda bi, h, i: (bi, i, h)),
        compiler_params=_cparams(("parallel", "parallel", "arbitrary")),
        name="attn_a_lat" if has_cache else "attn_a_ctx",
    )(*args)

    in_specs = [pl.BlockSpec((4, DK_B), lambda bi, h, i: (0, 0)),
                pl.BlockSpec((1, DV_B), lambda bi, h, i: (0, 0)),
                pl.BlockSpec((None, tq_b, LANES), lambda bi, h, i: (bi, i, q_b0 + h))]
    args = [lam_params, g_sub, qkv3]
    if has_cache:
        ck, cv = caches[2], caches[3]
        p = ck.shape[2]
        in_specs += [pl.BlockSpec((None, None, p, LANES), lambda bi, h, i: (bi, layer, 0, h))] * 2
        args += [ck, cv]
    in_specs += [pl.BlockSpec((None, n, LANES), lambda bi, h, i: (bi, 0, k_b0 + h)),
                 pl.BlockSpec((None, n, LANES), lambda bi, h, i: (bi, 0, v_b0 + h))]
    args += [qkv3, qkv3]
    att_b = pl.pallas_call(
        functools.partial(_attn_b_kernel, has_cache=has_cache, tq=tq_b, kc=kc, lam_init=lam_init),
        out_shape=jax.ShapeDtypeStruct((b, n, N_HEADS_B * DV_B), BF16),
        grid=(b, N_HEADS_B, n // tq_b),
        in_specs=in_specs,
        out_specs=pl.BlockSpec((None, tq_b, LANES), lambda bi, h, i: (bi, i, h)),
        compiler_params=_cparams(("parallel", "parallel", "arbitrary")),
        name="attn_b_lat" if has_cache else "attn_b_ctx",
    )(*args)
    return att_a, att_b


def _dft_stage1_kernel(y_ref, mat_ref, tc_ref, ts_ref, o_ref, *, g1, width):
    n2 = y_ref.shape[0]
    reps = width // LANES
    for r in range(g1):
        base = r * 2 * width
        ys = jnp.concatenate([y_ref[:, base:base + width], y_ref[:, base + width:base + 2 * width]], axis=0)
        u = jnp.dot(mat_ref[...], ys, preferred_element_type=F32)
        ur, ui = u[:n2], u[n2:]
        tc = jnp.concatenate([tc_ref[r]] * reps, axis=1)
        ts = jnp.concatenate([ts_ref[r]] * reps, axis=1)
        o_ref[:, base:base + width] = (ur * tc - ui * ts).astype(BF16)
        o_ref[:, base + width:base + 2 * width] = (ur * ts + ui * tc).astype(BF16)


def _dft_stage2_kernel(t_ref, mat_ref, o_ref, *, g2, width, scale):
    for r in range(g2):
        st = jnp.concatenate([t_ref[r, :, 0:width], t_ref[r, :, width:2 * width]], axis=0)
        xr = jnp.dot(mat_ref[...], st, preferred_element_type=F32)
        o_ref[:, r * width:(r + 1) * width] = (xr * scale).astype(BF16)


def _cos_sin(n_rows, n_cols, period):
    a = jnp.arange(n_rows, dtype=jnp.int32)[:, None]
    b = jnp.arange(n_cols, dtype=jnp.int32)[None, :]
    ang = ((a * b) % period).astype(F32) * (2.0 * math.pi / period)
    return jnp.cos(ang), jnp.sin(ang)


def _fourier_call(y3, n1, n2):
    b, n, w2 = y3.shape
    width = w2 // 2
    scale = 1.0 / math.sqrt(n * FOURIER_GROUP_DIM)
    if n2 > 1:
        g1 = 8
        c2, s2 = _cos_sin(n2, n2, n2)
        mat1 = jnp.concatenate([jnp.concatenate([c2, -s2], axis=1),
                                jnp.concatenate([s2, c2], axis=1)], axis=0).astype(BF16)
        tc, ts = _cos_sin(n1, n2, n)
        tc = jnp.broadcast_to(tc[:, :, None], (n1, n2, LANES))
        ts = jnp.broadcast_to(ts[:, :, None], (n1, n2, LANES))
        t = pl.pallas_call(
            functools.partial(_dft_stage1_kernel, g1=g1, width=width),
            out_shape=jax.ShapeDtypeStruct((b, n2, n1 * w2), BF16),
            grid=(b, n1 // g1),
            in_specs=[pl.BlockSpec((None, n2, g1 * w2), lambda bi, j: (bi, 0, j)),
                      pl.BlockSpec((2 * n2, 2 * n2), lambda bi, j: (0, 0)),
                      pl.BlockSpec((g1, n2, LANES), lambda bi, j: (j, 0, 0)),
                      pl.BlockSpec((g1, n2, LANES), lambda bi, j: (j, 0, 0))],
            out_specs=pl.BlockSpec((None, n2, g1 * w2), lambda bi, j: (bi, 0, j)),
            compiler_params=_cparams(("parallel", "parallel")),
            name="dft_stage1",
        )(y3.reshape(b, n2, n1 * w2), mat1, tc, ts)
        t4 = t.reshape(b, n2, n1, w2)
        g2 = 8
    else:
        t4 = y3.reshape(b, 1, n1, w2)
        g2 = 1
    c1, s1 = _cos_sin(n1, n1, n1)
    mat2 = jnp.concatenate([c1, -s1], axis=1).astype(BF16)
    out = pl.pallas_call(
        functools.partial(_dft_stage2_kernel, g2=g2, width=width, scale=scale),
        out_shape=jax.ShapeDtypeStruct((b, n1, n2 * width), BF16),
        grid=(b, n2 // g2),
        in_specs=[pl.BlockSpec((None, g2, n1, w2), lambda bi, j: (bi, j, 0, 0)),
                  pl.BlockSpec((n1, 2 * n1), lambda bi, j: (0, 0))],
        out_specs=pl.BlockSpec((None, n1, g2 * width), lambda bi, j: (bi, 0, j)),
        compiler_params=_cparams(("parallel", "parallel")),
        name="dft_stage2",
    )(t4, mat2)
    return out.reshape(b, n, width)


def _out_kernel(a_ref, b_ref, f_ref, w_ref, x_ref, mod_ref, g_ref, x1_ref, h2_ref):
    ca = a_ref.shape[1]
    cb = b_ref.shape[1]
    acc = jnp.dot(a_ref[...], w_ref[0:ca, :], preferred_element_type=F32)
    acc += jnp.dot(b_ref[...], w_ref[ca:ca + cb, :], preferred_element_type=F32)
    acc += jnp.dot(f_ref[...], w_ref[ca + cb:, :], preferred_element_type=F32)
    m = mod_ref[...]
    x1 = x_ref[...] + m[2:3] * acc
    x1_ref[...] = x1
    h = _rms(x1, g_ref[...])
    h2_ref[...] = (h * (1.0 + m[4:5]) + m[3:4]).astype(BF16)


def _out_call(att_a, att_b, four, w_out_b, x2d, mod_l, cond_row_fn, norm_g, tm):
    t, d = x2d.shape
    ca, cb, cf = att_a.shape[1], att_b.shape[1], four.shape[1]
    return pl.pallas_call(
        _out_kernel,
        out_shape=[jax.ShapeDtypeStruct((t, d), F32), jax.ShapeDtypeStruct((t, d), BF16)],
        grid=(t // tm,),
        in_specs=[
            pl.BlockSpec((tm, ca), lambda i: (i, 0)),
            pl.BlockSpec((tm, cb), lambda i: (i, 0)),
            pl.BlockSpec((tm, cf), lambda i: (i, 0)),
            pl.BlockSpec((ca + cb + cf, d), lambda i: (0, 0), pipeline_mode=pl.Buffered(1)),
            pl.BlockSpec((tm, d), lambda i: (i, 0)),
            pl.BlockSpec((None, N_MOD, d), lambda i: (cond_row_fn(i), 0, 0)),
            pl.BlockSpec((1, d), lambda i: (0, 0)),
        ],
        out_specs=[pl.BlockSpec((tm, d), lambda i: (i, 0)), pl.BlockSpec((tm, d), lambda i: (i, 0))],
        compiler_params=_cparams(("parallel",)),
        name="out_proj",
    )(att_a, att_b, four, w_out_b, x2d, mod_l, norm_g)


HALO = 16


def _ffn_kernel(h_ref, hp_ref, hn_ref, wg_ref, wu_ref, cw_ref, cb_ref, wd_ref, x1_ref, mod_ref, gf_ref,
                o_ref, hext_ref, *, tm, seq_len, final):
    i = pl.program_id(0)
    j = pl.program_id(1)

    @pl.when(j == 0)
    def _():
        hext_ref[0:HALO, :] = hp_ref[...]
        hext_ref[HALO:HALO + tm, :] = h_ref[...]
        hext_ref[HALO + tm:, :] = hn_ref[...]
        o_ref[...] = jnp.zeros_like(o_ref)

    g = jnp.dot(hext_ref[...], wg_ref[...], preferred_element_type=F32)
    u = jnp.dot(h_ref[...], wu_ref[...], preferred_element_type=F32)
    ext = tm + 2 * HALO
    pos = (i * tm + lax.broadcasted_iota(jnp.int32, (tm, 1), 0)) % seq_len
    g_prev = jnp.where(pos == 0, 0.0, pltpu.roll(g, 1, 0)[HALO:HALO + tm])
    g_next = jnp.where(pos == seq_len - 1, 0.0, pltpu.roll(g, ext - 1, 0)[HALO:HALO + tm])
    cw = cw_ref[...]
    gc = g_prev * cw[0:1] + g[HALO:HALO + tm] * cw[1:2] + g_next * cw[2:3] + cb_ref[...]
    act = (gc * jax.nn.sigmoid(gc)) * u
    o_ref[...] += jnp.dot(act.astype(BF16), wd_ref[...], preferred_element_type=F32)

    @pl.when(j == pl.num_programs(1) - 1)
    def _():
        m = mod_ref[...]
        x2 = x1_ref[...] + m[5:6] * o_ref[...]
        if final:
            x2 = _rms(x2, gf_ref[...])
        o_ref[...] = x2


def _ffn_call(h2, x1, w_gate_b, w_up_b, conv_w, conv_b, w_down_b, mod_l, cond_row_fn, final_g, seq_len, final, tm, tf):
    t, d = x1.shape
    f = w_gate_b.shape[1]
    hb = tm // HALO
    last = t // HALO - 1
    return pl.pallas_call(
        functools.partial(_ffn_kernel, tm=tm, seq_len=seq_len, final=final),
        out_shape=jax.ShapeDtypeStruct((t, d), F32),
        grid=(t // tm, f // tf),
        in_specs=[
            pl.BlockSpec((tm, d), lambda i, j: (i, 0)),
            pl.BlockSpec((HALO, d), lambda i, j: (jnp.maximum(i * hb - 1, 0), 0)),
            pl.BlockSpec((HALO, d), lambda i, j: (jnp.minimum((i + 1) * hb, last), 0)),
            pl.BlockSpec((d, tf), lambda i, j: (0, j)),
            pl.BlockSpec((d, tf), lambda i, j: (0, j)),
            pl.BlockSpec((3, tf), lambda i, j: (0, j)),
            pl.BlockSpec((1, tf), lambda i, j: (0, j)),
            pl.BlockSpec((tf, d), lambda i, j: (j, 0)),
            pl.BlockSpec((tm, d), lambda i, j: (i, 0)),
            pl.BlockSpec((None, N_MOD, d), lambda i, j: (cond_row_fn(i), 0, 0)),
            pl.BlockSpec((1, d), lambda i, j: (0, 0)),
        ],
        out_specs=pl.BlockSpec((tm, d), lambda i, j: (i, 0)),
        scratch_shapes=[pltpu.VMEM((tm + 2 * HALO, d), BF16)],
        compiler_params=_cparams(("parallel", "arbitrary")),
        name="conv_ffn",
    )(h2, h2, h2, w_gate_b, w_up_b, conv_w, conv_b, w_down_b, x1, mod_l, final_g)


def _rope_tables(n, head_dim):
    rows = n // GRID_W
    t_row = jnp.repeat(jnp.arange(rows, dtype=F32), GRID_W)
    t_col = jnp.tile(jnp.arange(GRID_W, dtype=F32), rows)
    axis_dim = head_dim // 2
    inv = jnp.power(ROPE_BASE, -jnp.arange(0, axis_dim, 2, dtype=F32) / axis_dim)
    ar = t_row[:, None] * inv[None, :]
    ac = t_col[:, None] * inv[None, :]
    ang = jnp.concatenate([ar, ar, ac, ac], axis=-1)
    reps = LANES // head_dim
    quarter = head_dim // 4
    sign = jnp.where((jnp.arange(head_dim) % (2 * quarter)) < quarter, -1.0, 1.0).astype(F32)
    cos = jnp.tile(jnp.cos(ang), (1, reps))
    sin_signed = jnp.tile(jnp.sin(ang) * sign[None, :], (1, reps))
    return cos, sin_signed


def kernel(x_prompt, x_sample, cache_attn_k, cache_attn_v, cache_diff_k, cache_diff_v, c, c_ctx, norm1_g, norm2_g, w_ada, b_ada, w_in, attn_q_norm_g, attn_k_norm_g, diff_lambda_q1, diff_lambda_k1, diff_lambda_q2, diff_lambda_k2, diff_subnorm_g, w_out, ffn_w_gate, ffn_w_up, ffn_conv_w, ffn_conv_b, ffn_w_down, final_norm_g):
    depth = w_in.shape[0]
    bc, lc, d = x_prompt.shape
    bl, ll, _ = x_sample.shape
    past = cache_attn_k.shape[2]

    w_in_b = w_in.astype(BF16)
    w_out_b = w_out.astype(BF16)
    w_gate_b = ffn_w_gate.astype(BF16)
    w_up_b = ffn_w_up.astype(BF16)
    w_down_b = ffn_w_down.astype(BF16)

    n_rows = 8 * ((1 + bl + 7) // 8)
    cvec = jnp.concatenate([c_ctx[None, :], c, jnp.zeros((n_rows - 1 - bl, d), F32)], axis=0)
    mod = _ada_call(cvec, w_ada, b_ada).reshape(depth, n_rows, N_MOD, d)

    dft_c = jnp.concatenate(_cos_sin(FOURIER_GROUP_DIM, FOURIER_GROUP_DIM, FOURIER_GROUP_DIM), axis=1).astype(BF16)
    rope_tabs = _rope_tables(ll, HEAD_DIM_A) + _rope_tables(ll, DK_B)
    caches = (cache_attn_k.reshape(bl, depth, past, N_KV_HEADS_A * HEAD_DIM_A),
              cache_attn_v.reshape(bl, depth, past, N_KV_HEADS_A * HEAD_DIM_A),
              cache_diff_k.reshape(bl, depth, past, N_HEADS_B * 2 * DK_B),
              cache_diff_v.reshape(bl, depth, past, N_HEADS_B * DV_B))
    lam_all = jnp.stack([diff_lambda_q1, diff_lambda_k1, diff_lambda_q2, diff_lambda_k2], axis=1)

    def run_pass(x3, is_ctx):
        b, n, _ = x3.shape
        t = b * n
        tm = 512
        x = x3.reshape(t, d)
        if is_ctx:
            cond_row_fn = lambda i: 0
        else:
            cond_row_fn = lambda i: 1 + (i * tm) // n
        new_kv = []
        for l in range(depth):
            lam_init = 0.8 - 0.6 * math.exp(-0.3 * l)
            res = _in_call(x, mod[l], cond_row_fn, norm1_g[l][None, :], w_in_b[l],
                           attn_q_norm_g[l][None, :], attn_k_norm_g[l][None, :], dft_c,
                           None if is_ctx else rope_tabs, n, is_ctx, tm)
            qkv, y12 = res[0], res[1]
            if is_ctx:
                new_kv.append(res[2])
            qkv3 = qkv.reshape(b, n, qkv.shape[1])
            att_a, att_b = _attn_calls(qkv3, None if is_ctx else caches, l, lam_all[l],
                                       diff_subnorm_g[l][None, :], lam_init,
                                       tq_a=min(128, n) if not is_ctx else n, tq_b=min(256, n), kc=512)
            if is_ctx:
                four = _fourier_call(y12.reshape(b, n, y12.shape[1]), n, 1)
            else:
                four = _fourier_call(y12.reshape(b, n, y12.shape[1]), GRID_W, n // GRID_W)
            x1, h2 = _out_call(att_a.reshape(t, -1), att_b.reshape(t, -1), four.reshape(t, -1), w_out_b[l],
                               x, mod[l], cond_row_fn, norm2_g[l][None, :], tm)
            x = _ffn_call(h2, x1, w_gate_b[l], w_up_b[l], ffn_conv_w[l], ffn_conv_b[l][None, :], w_down_b[l],
                          mod[l], cond_row_fn, final_norm_g[None, :], n, l == depth - 1, tm, 512)
        return x.reshape(b, n, d), new_kv

    y_prompt, kvs = run_pass(x_prompt, True)
    ca = N_KV_HEADS_A * HEAD_DIM_A
    cb = N_HEADS_B * 2 * DK_B
    kv = jnp.stack([a.reshape(bc, lc, -1) for a in kvs], axis=1)
    new_attn_k = kv[..., 0:ca].reshape(bc, depth, lc, N_KV_HEADS_A, HEAD_DIM_A)
    new_attn_v = kv[..., ca:2 * ca].reshape(bc, depth, lc, N_KV_HEADS_A, HEAD_DIM_A)
    new_diff_k = kv[..., 2 * ca:2 * ca + cb].reshape(bc, depth, lc, N_HEADS_B, 2 * DK_B)
    new_diff_v = kv[..., 2 * ca + cb:].reshape(bc, depth, lc, N_HEADS_B, DV_B)

    y_sample, _ = run_pass(x_sample, False)
    return (y_prompt, y_sample, new_attn_k, new_attn_v, new_diff_k, new_diff_v)
```

```python
import functools
import math

import jax
import jax.numpy as jnp
from jax import lax
from jax.experimental import pallas as pl
from jax.experimental.pallas import tpu as pltpu

F32 = jnp.float32
BF16 = jnp.bfloat16

GRID_W = 64
ROPE_BASE = 10000.0
NORM_EPS = 1e-6
HEAD_DIM_A = 128
N_KV_HEADS_A = 2
GQA_GROUP = 4
N_HEADS_B = 4
DK_B = 64
DV_B = 128
N_FOURIER_GROUPS = 4
FOURIER_GROUP_DIM = 128
N_MOD = 6

LOG2E = math.log2(math.e)
LANES = 128
VMEM_LIMIT = 56 * 1024 * 1024


def _cparams(sem):
    return pltpu.CompilerParams(dimension_semantics=sem, vmem_limit_bytes=VMEM_LIMIT)


def _rms(x, g):
    return x * lax.rsqrt(jnp.mean(x * x, axis=-1, keepdims=True) + NORM_EPS) * g


def _ada_kernel(c_ref, w_ref, b_ref, o_ref):
    c = c_ref[...]
    s = (c * jax.nn.sigmoid(c)).astype(BF16)
    w = w_ref[...].astype(BF16)
    o_ref[...] = jnp.dot(s, w, preferred_element_type=F32) + b_ref[...]


def _ada_call(cvec, w_ada, b_ada):
    depth, d, n = w_ada.shape
    rows = cvec.shape[0]
    tn = 512
    return pl.pallas_call(
        _ada_kernel,
        out_shape=jax.ShapeDtypeStruct((depth, rows, n), F32),
        grid=(depth, n // tn),
        in_specs=[
            pl.BlockSpec((rows, d), lambda l, j: (0, 0)),
            pl.BlockSpec((None, d, tn), lambda l, j: (l, 0, j)),
            pl.BlockSpec((None, 1, tn), lambda l, j: (l, 0, j)),
        ],
        out_specs=pl.BlockSpec((None, rows, tn), lambda l, j: (l, 0, j)),
        compiler_params=_cparams(("parallel", "parallel")),
        name="ada_mod",
    )(cvec, w_ada, b_ada.reshape(depth, 1, n))


def _rope(xs, cos, sin_signed, shift):
    w = xs.shape[-1]
    lane = lax.broadcasted_iota(jnp.int32, xs.shape, 1)
    first = (lane % (2 * shift)) < shift
    rot = jnp.where(first, pltpu.roll(xs, w - shift, 1), pltpu.roll(xs, shift, 1))
    return xs * cos + rot * sin_signed


def _in_kernel(*refs, rope, ctx_out, cols):
    it = iter(refs)
    x_ref, mod_ref, g_ref, w_ref, gq_ref, gk_ref, dft_ref = (next(it) for _ in range(7))
    if rope:
        cos_a, sin_a, cos_b, sin_b = (next(it)[...] for _ in range(4))
    qkv_ref, y_ref = next(it), next(it)
    kvc_ref = next(it) if ctx_out else None
    c_qa, c_ka, c_va, c_qb, c_kb, c_vb, c_f, c_end = cols
    kvb0 = c_qb - c_ka

    m = mod_ref[...]
    h = _rms(x_ref[...], g_ref[...])
    hb = (h * (1.0 + m[1:2]) + m[0:1]).astype(BF16)
    gq = gq_ref[...]
    gk = gk_ref[...]
    scale_a = HEAD_DIM_A ** -0.5 * LOG2E
    scale_b = DK_B ** -0.5 * LOG2E
    chunk = 4 * LANES

    for c0 in range(0, c_end, chunk):
        acc = jnp.dot(hb, w_ref[:, c0:c0 + chunk], preferred_element_type=F32)
        for s in range(chunk // LANES):
            col = c0 + s * LANES
            v = acc[:, s * LANES:(s + 1) * LANES]
            if col < c_ka:
                v = _rms(v, gq)
                if rope:
                    v = _rope(v, cos_a, sin_a, HEAD_DIM_A // 4)
                qkv_ref[:, col:col + LANES] = (v * scale_a).astype(BF16)
            elif col < c_va:
                v = _rms(v, gk)
                if ctx_out:
                    kvc_ref[:, col - c_ka:col - c_ka + LANES] = v
                if rope:
                    v = _rope(v, cos_a, sin_a, HEAD_DIM_A // 4)
                qkv_ref[:, col:col + LANES] = v.astype(BF16)
            elif col < c_qb:
                if ctx_out:
                    kvc_ref[:, col - c_ka:col - c_ka + LANES] = v
                qkv_ref[:, col:col + LANES] = v.astype(BF16)
            elif col < c_kb:
                if rope:
                    v = _rope(v, cos_b, sin_b, DK_B // 4)
                qkv_ref[:, col:col + LANES] = (v * scale_b).astype(BF16)
            elif col < c_vb:
                if ctx_out:
                    kvc_ref[:, col - c_kb + kvb0:col - c_kb + kvb0 + LANES] = v
                if rope:
                    v = _rope(v, cos_b, sin_b, DK_B // 4)
                qkv_ref[:, col:col + LANES] = v.astype(BF16)
            elif col < c_f:
                if ctx_out:
                    kvc_ref[:, col - c_kb + kvb0:col - c_kb + kvb0 + LANES] = v
                qkv_ref[:, col:col + LANES] = v.astype(BF16)
            else:
                yy = jnp.dot(v.astype(BF16), dft_ref[...], preferred_element_type=F32)
                gcol = col - c_f
                half = c_end - c_f
                y_ref[:, gcol:gcol + LANES] = yy[:, :LANES].astype(BF16)
                y_ref[:, half + gcol:half + gcol + LANES] = yy[:, LANES:].astype(BF16)


def _in_call(x2d, mod_l, cond_row_fn, norm_g, w_in_b, gq, gk, dft_c, rope_tabs, seq_len, ctx_out, tm):
    t, d = x2d.shape
    d_in = w_in_b.shape[1]
    c_qa = 0
    c_ka = N_KV_HEADS_A * GQA_GROUP * HEAD_DIM_A
    c_va = c_ka + N_KV_HEADS_A * HEAD_DIM_A
    c_qb = c_va + N_KV_HEADS_A * HEAD_DIM_A
    c_kb = c_qb + N_HEADS_B * 2 * DK_B
    c_vb = c_kb + N_HEADS_B * 2 * DK_B
    c_f = c_vb + N_HEADS_B * DV_B
    c_end = c_f + N_FOURIER_GROUPS * FOURIER_GROUP_DIM
    assert c_end == d_in
    cols = (c_qa, c_ka, c_va, c_qb, c_kb, c_vb, c_f, c_end)
    rope = rope_tabs is not None
    n_f = c_end - c_f

    in_specs = [
        pl.BlockSpec((tm, d), lambda i: (i, 0)),
        pl.BlockSpec((None, N_MOD, d), lambda i: (cond_row_fn(i), 0, 0)),
        pl.BlockSpec((1, d), lambda i: (0, 0)),
        pl.BlockSpec((d, d_in), lambda i: (0, 0), pipeline_mode=pl.Buffered(1)),
        pl.BlockSpec((1, HEAD_DIM_A), lambda i: (0, 0)),
        pl.BlockSpec((1, HEAD_DIM_A), lambda i: (0, 0)),
        pl.BlockSpec((FOURIER_GROUP_DIM, 2 * FOURIER_GROUP_DIM), lambda i: (0, 0)),
    ]
    args = [x2d, mod_l, norm_g, w_in_b, gq, gk, dft_c]
    if rope:
        nblk = seq_len // tm
        for tab in rope_tabs:
            in_specs.append(pl.BlockSpec((tm, LANES), lambda i: (i % nblk, 0)))
            args.append(tab)
    out_shape = [jax.ShapeDtypeStruct((t, c_f), BF16), jax.ShapeDtypeStruct((t, 2 * n_f), BF16)]
    out_specs = [pl.BlockSpec((tm, c_f), lambda i: (i, 0)), pl.BlockSpec((tm, 2 * n_f), lambda i: (i, 0))]
    if ctx_out:
        n_kv = (c_qb - c_ka) + (c_f - c_kb)
        out_shape.append(jax.ShapeDtypeStruct((t, n_kv), F32))
        out_specs.append(pl.BlockSpec((tm, n_kv), lambda i: (i, 0)))
    return pl.pallas_call(
        functools.partial(_in_kernel, rope=rope, ctx_out=ctx_out, cols=cols),
        out_shape=out_shape,
        grid=(t // tm,),
        in_specs=in_specs,
        out_specs=out_specs,
        compiler_params=_cparams(("parallel",)),
        name="in_proj_ctx" if ctx_out else "in_proj_lat",
    )(*args)


def _softmax_pv(q, srcs):
    chunks = []
    for k_ref, v_ref, n_keys, kc in srcs:
        for c in range(n_keys // kc):
            chunks.append((k_ref, v_ref, c * kc, kc))

    def scores(ch):
        k_ref, _, off, kc = ch
        k = k_ref[off:off + kc, :].astype(BF16)
        return lax.dot_general(q, k, (((1,), (1,)), ((), ())), preferred_element_type=F32)

    def lane_fold(x, op):
        out = x[:, 0:LANES]
        for t in range(1, x.shape[1] // LANES):
            out = op(out, x[:, t * LANES:(t + 1) * LANES])
        return out

    m = l_part = acc = None
    s_next = scores(chunks[0])
    for i, ch in enumerate(chunks):
        s = s_next
        if i + 1 < len(chunks):
            s_next = scores(chunks[i + 1])
        _, v_ref, off, kc = ch
        row_max = jnp.max(lane_fold(s, jnp.maximum), axis=-1, keepdims=True)
        m_new = row_max if m is None else jnp.maximum(m, row_max)
        p = jnp.exp2(s - m_new)
        p_sum = lane_fold(p, jnp.add)
        pv = jnp.dot(p.astype(BF16), v_ref[off:off + kc, :].astype(BF16), preferred_element_type=F32)
        if m is None:
            l_part, acc = p_sum, pv
        else:
            alpha = jnp.exp2(m - m_new)
            l_part = alpha * l_part + p_sum
            acc = alpha * acc + pv
        m = m_new
    return acc / jnp.sum(l_part, axis=-1, keepdims=True)


def _srcs(cache_refs, k_ref, v_ref, kc):
    srcs = []
    if cache_refs is not None:
        ck, cv = cache_refs
        srcs.append((ck, cv, ck.shape[0], ck.shape[0]))
    n = k_ref.shape[0]
    srcs.append((k_ref, v_ref, n, min(kc, n)))
    return srcs


def _attn_a_kernel(*refs, has_cache, tq, kc):
    if has_cache:
        q_ref, ck_ref, cv_ref, k_ref, v_ref, o_ref = refs
        cache = (ck_ref, cv_ref)
    else:
        q_ref, k_ref, v_ref, o_ref = refs
        cache = None
    q = q_ref[...]
    q4 = jnp.concatenate([q[:, h * LANES:(h + 1) * LANES] for h in range(GQA_GROUP)], axis=0)
    o = _softmax_pv(q4, _srcs(cache, k_ref, v_ref, kc))
    for h in range(GQA_GROUP):
        o_ref[:, h * LANES:(h + 1) * LANES] = o[h * tq:(h + 1) * tq].astype(BF16)


def _attn_b_kernel(*refs, has_cache, tq, kc, lam_init):
    if has_cache:
        lam_ref, gs_ref, q_ref, ck_ref, cv_ref, k_ref, v_ref, o_ref = refs
        cache = (ck_ref, cv_ref)
    else:
        lam_ref, gs_ref, q_ref, k_ref, v_ref, o_ref = refs
        cache = None
    q = q_ref[...]
    lane = lax.broadcasted_iota(jnp.int32, q.shape, 1)
    zero = jnp.zeros_like(q)
    qz = jnp.concatenate([jnp.where(lane < DK_B, q, zero), jnp.where(lane >= DK_B, q, zero)], axis=0)
    o = _softmax_pv(qz, _srcs(cache, k_ref, v_ref, kc))
    lp = lam_ref[...]
    lam = (jnp.exp(jnp.sum(lp[0:1] * lp[1:2], axis=-1, keepdims=True))
           - jnp.exp(jnp.sum(lp[2:3] * lp[3:4], axis=-1, keepdims=True)) + lam_init)
    dlt = o[:tq] - lam * o[tq:]
    o_ref[...] = (_rms(dlt, gs_ref[...]) * (1.0 - lam_init)).astype(BF16)


def _attn_calls(qkv3, caches, layer, lam_params, g_sub, lam_init, tq_a, tq_b, kc):
    b, n, _ = qkv3.shape
    has_cache = caches is not None
    qa_blk =GQA_GROUP * HEAD_DIM_A // LANES
    k_a0 = N_KV_HEADS_A * qa_blk
    v_a0 = k_a0 + N_KV_HEADS_A
    q_b0 = v_a0 + N_KV_HEADS_A
    k_b0 = q_b0 + N_HEADS_B
    v_b0 = k_b0 + N_HEADS_B

    in_specs = [pl.BlockSpec((None, tq_a, GQA_GROUP * LANES), lambda bi, h, i: (bi, i, h))]
    args = [qkv3]
    if has_cache:
        ck, cv = caches[0], caches[1]
        p = ck.shape[2]
        in_specs += [pl.BlockSpec((None, None, p, LANES), lambda bi, h, i: (bi, layer, 0, h))] * 2
        args += [ck, cv]
    in_specs += [pl.BlockSpec((None, n, LANES), lambda bi, h, i: (bi, 0, k_a0 + h)),
                 pl.BlockSpec((None, n, LANES), lambda bi, h, i: (bi, 0, v_a0 + h))]
    args += [qkv3, qkv3]
    att_a = pl.pallas_call(
        functools.partial(_attn_a_kernel, has_cache=has_cache, tq=tq_a, kc=kc),
        out_shape=jax.ShapeDtypeStruct((b, n, N_KV_HEADS_A * GQA_GROUP * HEAD_DIM_A), BF16),
        grid=(b, N_KV_HEADS_A, n // tq_a),
        in_specs=in_specs,
        out_specs=pl.BlockSpec((None, tq_a, GQA_GROUP * LANES), lambda bi, h, i: (bi, i, h)),
        compiler_params=_cparams(("parallel", "parallel", "arbitrary")),
        name="attn_a_lat" if has_cache else "attn_a_ctx",
    )(*args)

    in_specs = [pl.BlockSpec((4, DK_B), lambda bi, h, i: (0, 0)),
                pl.BlockSpec((1, DV_B), lambda bi, h, i: (0, 0)),
                pl.BlockSpec((None, tq_b, LANES), lambda bi, h, i: (bi, i, q_b0 + h))]
    args = [lam_params, g_sub, qkv3]
    if has_cache:
        ck, cv = caches[2], caches[3]
        p = ck.shape[2]
        in_specs += [pl.BlockSpec((None, None, p, LANES), lambda bi, h, i: (bi, layer, 0, h))] * 2
        args += [ck, cv]
    in_specs += [pl.BlockSpec((None, n, LANES), lambda bi, h, i: (bi, 0, k_b0 + h)),
                 pl.BlockSpec((None, n, LANES), lambda bi, h, i: (bi, 0, v_b0 + h))]
    args += [qkv3, qkv3]
    att_b = pl.pallas_call(
        functools.partial(_attn_b_kernel, has_cache=has_cache, tq=tq_b, kc=kc, lam_init=lam_init),
        out_shape=jax.ShapeDtypeStruct((b, n, N_HEADS_B * DV_B), BF16),
        grid=(b, N_HEADS_B, n // tq_b),
        in_specs=in_specs,
        out_specs=pl.BlockSpec((None, tq_b, LANES), lambda bi, h, i: (bi, i, h)),
        compiler_params=_cparams(("parallel", "parallel", "arbitrary")),
        name="attn_b_lat" if has_cache else "attn_b_ctx",
    )(*args)
    return att_a, att_b


def _dft_stage1_kernel(y_ref, mat_ref, tc_ref, ts_ref, o_ref, *, g1, width):
    n2 = y_ref.shape[0]
    reps = width // LANES
    for r in range(g1):
        base = r * 2 * width
        ys = jnp.concatenate([y_ref[:, base:base + width], y_ref[:, base + width:base + 2 * width]], axis=0)
        u = jnp.dot(mat_ref[...], ys, preferred_element_type=F32)
        ur, ui = u[:n2], u[n2:]
        tc = jnp.concatenate([tc_ref[r]] * reps, axis=1)
        ts = jnp.concatenate([ts_ref[r]] * reps, axis=1)
        o_ref[:, base:base + width] = (ur * tc - ui * ts).astype(BF16)
        o_ref[:, base + width:base + 2 * width] = (ur * ts + ui * tc).astype(BF16)


def _dft_stage2_kernel(t_ref, mat_ref, o_ref, *, g2, width, scale):
    for r in range(g2):
        st = jnp.concatenate([t_ref[r, :, 0:width], t_ref[r, :, width:2 * width]], axis=0)
        xr = jnp.dot(mat_ref[...], st, preferred_element_type=F32)
        o_ref[:, r * width:(r + 1) * width] = (xr * scale).astype(BF16)


def _cos_sin(n_rows, n_cols, period):
    a = jnp.arange(n_rows, dtype=jnp.int32)[:, None]
    b = jnp.arange(n_cols, dtype=jnp.int32)[None, :]
    ang = ((a * b) % period).astype(F32) * (2.0 * math.pi / period)
    return jnp.cos(ang), jnp.sin(ang)


def _fourier_call(y3, n1, n2):
    b, n, w2 = y3.shape
    width = w2 // 2
    scale = 1.0 / math.sqrt(n * FOURIER_GROUP_DIM)
    if n2 > 1:
        g1 = 8
        c2, s2 = _cos_sin(n2, n2, n2)
        mat1 = jnp.concatenate([jnp.concatenate([c2, -s2], axis=1),
                                jnp.concatenate([s2, c2], axis=1)], axis=0).astype(BF16)
        tc, ts = _cos_sin(n1, n2, n)
        tc = jnp.broadcast_to(tc[:, :, None], (n1, n2, LANES))
        ts = jnp.broadcast_to(ts[:, :, None], (n1, n2, LANES))
        t = pl.pallas_call(
            functools.partial(_dft_stage1_kernel, g1=g1, width=width),
            out_shape=jax.ShapeDtypeStruct((b, n2, n1 * w2), BF16),
            grid=(b, n1 // g1),
            in_specs=[pl.BlockSpec((None, n2, g1 * w2), lambda bi, j: (bi, 0, j)),
                      pl.BlockSpec((2 * n2, 2 * n2), lambda bi, j: (0, 0)),
                      pl.BlockSpec((g1, n2, LANES), lambda bi, j: (j, 0, 0)),
                      pl.BlockSpec((g1, n2, LANES), lambda bi, j: (j, 0, 0))],
            out_specs=pl.BlockSpec((None, n2, g1 * w2), lambda bi, j: (bi, 0, j)),
            compiler_params=_cparams(("parallel", "parallel")),
            name="dft_stage1",
        )(y3.reshape(b, n2, n1 * w2), mat1, tc, ts)
        t4 = t.reshape(b, n2, n1, w2)
        g2 = 8
    else:
        t4 = y3.reshape(b, 1, n1, w2)
        g2 = 1
    c1, s1 = _cos_sin(n1, n1, n1)
    mat2 = jnp.concatenate([c1, -s1], axis=1).astype(BF16)
    out = pl.pallas_call(
        functools.partial(_dft_stage2_kernel, g2=g2, width=width, scale=scale),
        out_shape=jax.ShapeDtypeStruct((b, n1, n2 * width), BF16),
        grid=(b, n2 // g2),
        in_specs=[pl.BlockSpec((None, g2, n1, w2), lambda bi, j: (bi, j, 0, 0)),
                  pl.BlockSpec((n1, 2 * n1), lambda bi, j: (0, 0))],
        out_specs=pl.BlockSpec((None, n1, g2 * width), lambda bi, j: (bi, 0, j)),
        compiler_params=_cparams(("parallel", "parallel")),
        name="dft_stage2",
    )(t4, mat2)
    return out.reshape(b, n, width)


def _out_kernel(a_ref, b_ref, f_ref, w_ref, x_ref, mod_ref, g_ref, x1_ref, h2_ref):
    ca = a_ref.shape[1]
    cb = b_ref.shape[1]
    acc = jnp.dot(a_ref[...], w_ref[0:ca, :], preferred_element_type=F32)
    acc += jnp.dot(b_ref[...], w_ref[ca:ca + cb, :], preferred_element_type=F32)
    acc += jnp.dot(f_ref[...], w_ref[ca + cb:, :], preferred_element_type=F32)
    m = mod_ref[...]
    x1 = x_ref[...] + m[2:3] * acc
    x1_ref[...] = x1
    h = _rms(x1, g_ref[...])
    h2_ref[...] = (h * (1.0 + m[4:5]) + m[3:4]).astype(BF16)


def _out_call(att_a, att_b, four, w_out_b, x2d, mod_l, cond_row_fn, norm_g, tm):
    t, d = x2d.shape
    ca, cb, cf = att_a.shape[1], att_b.shape[1], four.shape[1]
    return pl.pallas_call(
        _out_kernel,
        out_shape=[jax.ShapeDtypeStruct((t, d), F32), jax.ShapeDtypeStruct((t, d), BF16)],
        grid=(t // tm,),
        in_specs=[
            pl.BlockSpec((tm, ca), lambda i: (i, 0)),
            pl.BlockSpec((tm, cb), lambda i: (i, 0)),
            pl.BlockSpec((tm, cf), lambda i: (i, 0)),
            pl.BlockSpec((ca + cb + cf, d), lambda i: (0, 0), pipeline_mode=pl.Buffered(1)),
            pl.BlockSpec((tm, d), lambda i: (i, 0)),
            pl.BlockSpec((None, N_MOD, d), lambda i: (cond_row_fn(i), 0, 0)),
            pl.BlockSpec((1, d), lambda i: (0, 0)),
        ],
        out_specs=[pl.BlockSpec((tm, d), lambda i: (i, 0)), pl.BlockSpec((tm, d), lambda i: (i, 0))],
        compiler_params=_cparams(("parallel",)),
        name="out_proj",
    )(att_a, att_b, four, w_out_b, x2d, mod_l, norm_g)


HALO = 16


def _ffn_kernel(h_ref, hp_ref, hn_ref, wg_ref, wu_ref, cw_ref, cb_ref, wd_ref, x1_ref, mod_ref, gf_ref,
                o_ref, hext_ref, *, tm, seq_len, final):
    i = pl.program_id(0)
    j = pl.program_id(1)

    @pl.when(j == 0)
    def _():
        hext_ref[0:HALO, :] = hp_ref[...]
        hext_ref[HALO:HALO + tm, :] = h_ref[...]
        hext_ref[HALO + tm:, :] = hn_ref[...]
        o_ref[...] = jnp.zeros_like(o_ref)

    g = jnp.dot(hext_ref[...], wg_ref[...], preferred_element_type=F32)
    u = jnp.dot(h_ref[...], wu_ref[...], preferred_element_type=F32)
    ext = tm + 2 * HALO
    pos = (i * tm + lax.broadcasted_iota(jnp.int32, (tm, 1), 0)) % seq_len
    g_prev = jnp.where(pos == 0, 0.0, pltpu.roll(g, 1, 0)[HALO:HALO + tm])
    g_next = jnp.where(pos == seq_len - 1, 0.0, pltpu.roll(g, ext - 1, 0)[HALO:HALO + tm])
    cw = cw_ref[...]
    gc = g_prev * cw[0:1] + g[HALO:HALO + tm] * cw[1:2] + g_next * cw[2:3] + cb_ref[...]
    act = (gc * jax.nn.sigmoid(gc)) * u
    o_ref[...] += jnp.dot(act.astype(BF16), wd_ref[...], preferred_element_type=F32)

    @pl.when(j == pl.num_programs(1) - 1)
    def _():
        m = mod_ref[...]
        x2 = x1_ref[...] + m[5:6] * o_ref[...]
        if final:
            x2 = _rms(x2, gf_ref[...])
        o_ref[...] = x2


def _ffn_call(h2, x1, w_gate_b, w_up_b, conv_w, conv_b, w_down_b, mod_l, cond_row_fn, final_g, seq_len, final, tm, tf):
    t, d = x1.shape
    f = w_gate_b.shape[1]
    hb = tm // HALO
    last = t // HALO - 1
    return pl.pallas_call(
        functools.partial(_ffn_kernel, tm=tm, seq_len=seq_len, final=final),
        out_shape=jax.ShapeDtypeStruct((t, d), F32),
        grid=(t // tm, f // tf),
        in_specs=[
            pl.BlockSpec((tm, d), lambda i, j: (i, 0)),
            pl.BlockSpec((HALO, d), lambda i, j: (jnp.maximum(i * hb - 1, 0), 0)),
            pl.BlockSpec((HALO, d), lambda i, j: (jnp.minimum((i + 1) * hb, last), 0)),
            pl.BlockSpec((d, tf), lambda i, j: (0, j)),
            pl.BlockSpec((d, tf), lambda i, j: (0, j)),
            pl.BlockSpec((3, tf), lambda i, j: (0, j)),
            pl.BlockSpec((1, tf), lambda i, j: (0, j)),
            pl.BlockSpec((tf, d), lambda i, j: (j, 0)),
            pl.BlockSpec((tm, d), lambda i, j: (i, 0)),
            pl.BlockSpec((None, N_MOD, d), lambda i, j: (cond_row_fn(i), 0, 0)),
            pl.BlockSpec((1, d), lambda i, j: (0, 0)),
        ],
        out_specs=pl.BlockSpec((tm, d), lambda i, j: (i, 0)),
        scratch_shapes=[pltpu.VMEM((tm + 2 * HALO, d), BF16)],
        compiler_params=_cparams(("parallel", "arbitrary")),
        name="conv_ffn",
    )(h2, h2, h2, w_gate_b, w_up_b, conv_w, conv_b, w_down_b, x1, mod_l, final_g)


def _rope_tables(n, head_dim):
    rows = n // GRID_W
    t_row = jnp.repeat(jnp.arange(rows, dtype=F32), GRID_W)
    t_col = jnp.tile(jnp.arange(GRID_W, dtype=F32), rows)
    axis_dim = head_dim // 2
    inv = jnp.power(ROPE_BASE, -jnp.arange(0, axis_dim, 2, dtype=F32) / axis_dim)
    ar = t_row[:, None] * inv[None, :]
    ac = t_col[:, None] * inv[None, :]
    ang = jnp.concatenate([ar, ar, ac, ac], axis=-1)
    reps = LANES // head_dim
    quarter = head_dim // 4
    sign = jnp.where((jnp.arange(head_dim) % (2 * quarter)) < quarter, -1.0, 1.0).astype(F32)
    cos = jnp.tile(jnp.cos(ang), (1, reps))
    sin_signed = jnp.tile(jnp.sin(ang) * sign[None, :], (1, reps))
    return cos, sin_signed


def kernel(x_prompt, x_sample, cache_attn_k, cache_attn_v, cache_diff_k, cache_diff_v, c, c_ctx, norm1_g, norm2_g, w_ada, b_ada, w_in, attn_q_norm_g, attn_k_norm_g, diff_lambda_q1, diff_lambda_k1, diff_lambda_q2, diff_lambda_k2, diff_subnorm_g, w_out, ffn_w_gate, ffn_w_up, ffn_conv_w, ffn_conv_b, ffn_w_down, final_norm_g):
    depth = w_in.shape[0]
    bc, lc, d = x_prompt.shape
    bl, ll, _ = x_sample.shape
    past = cache_attn_k.shape[2]

    w_in_b = w_in.astype(BF16)
    w_out_b = w_out.astype(BF16)
    w_gate_b = ffn_w_gate.astype(BF16)
    w_up_b = ffn_w_up.astype(BF16)
    w_down_b = ffn_w_down.astype(BF16)

    n_rows = 8 * ((1 + bl + 7) // 8)
    cvec = jnp.concatenate([c_ctx[None, :], c, jnp.zeros((n_rows - 1 - bl, d), F32)], axis=0)
    mod = _ada_call(cvec, w_ada, b_ada).reshape(depth, n_rows, N_MOD, d)

    dft_c = jnp.concatenate(_cos_sin(FOURIER_GROUP_DIM, FOURIER_GROUP_DIM, FOURIER_GROUP_DIM), axis=1).astype(BF16)
    rope_tabs = _rope_tables(ll, HEAD_DIM_A) + _rope_tables(ll, DK_B)
    caches = (cache_attn_k.reshape(bl, depth, past, N_KV_HEADS_A * HEAD_DIM_A),
              cache_attn_v.reshape(bl, depth, past, N_KV_HEADS_A * HEAD_DIM_A),
              cache_diff_k.reshape(bl, depth, past, N_HEADS_B * 2 * DK_B),
              cache_diff_v.reshape(bl, depth, past, N_HEADS_B * DV_B))
    lam_all = jnp.stack([diff_lambda_q1, diff_lambda_k1, diff_lambda_q2, diff_lambda_k2], axis=1)

    def run_pass(x3, is_ctx):
        b, n, _ = x3.shape
        t = b * n
        tm = 512
        x = x3.reshape(t, d)
        if is_ctx:
            cond_row_fn = lambda i: 0
        else:
            cond_row_fn = lambda i: 1 + (i * tm) // n
        new_kv = []
        for l in range(depth):
            lam_init = 0.8 - 0.6 * math.exp(-0.3 * l)
            res = _in_call(x, mod[l], cond_row_fn, norm1_g[l][None, :], w_in_b[l],
                           attn_q_norm_g[l][None, :], attn_k_norm_g[l][None, :], dft_c,
                           None if is_ctx else rope_tabs, n, is_ctx, tm)
            qkv, y12 = res[0], res[1]
            if is_ctx:
                new_kv.append(res[2])
            qkv3 = qkv.reshape(b, n, qkv.shape[1])
            att_a, att_b = _attn_calls(qkv3, None if is_ctx else caches, l, lam_all[l],
                                       diff_subnorm_g[l][None, :], lam_init,
                                       tq_a=min(256, n), tq_b=min(512, n), kc=1024)
            if is_ctx:
                four = _fourier_call(y12.reshape(b, n, y12.shape[1]), n, 1)
            else:
                four = _fourier_call(y12.reshape(b, n, y12.shape[1]), GRID_W, n // GRID_W)
            x1, h2 = _out_call(att_a.reshape(t, -1), att_b.reshape(t, -1), four.reshape(t, -1), w_out_b[l],
                               x, mod[l], cond_row_fn, norm2_g[l][None, :], tm)
            x = _ffn_call(h2, x1, w_gate_b[l], w_up_b[l], ffn_conv_w[l], ffn_conv_b[l][None, :], w_down_b[l],
                          mod[l], cond_row_fn, final_norm_g[None, :], n, l == depth - 1, tm, 512)
        return x.reshape(b, n, d), new_kv

    y_prompt, kvs = run_pass(x_prompt, True)
    ca = N_KV_HEADS_A * HEAD_DIM_A
    cb = N_HEADS_B * 2 * DK_B
    kv = jnp.stack([a.reshape(bc, lc, -1) for a in kvs], axis=1)
    new_attn_k = kv[..., 0:ca].reshape(bc, depth, lc, N_KV_HEADS_A, HEAD_DIM_A)
    new_attn_v = kv[..., ca:2 * ca].reshape(bc, depth, lc, N_KV_HEADS_A, HEAD_DIM_A)
    new_diff_k = kv[..., 2 * ca:2 * ca + cb].reshape(bc, depth, lc, N_HEADS_B, 2 * DK_B)
    new_diff_v = kv[..., 2 * ca + cb:].reshape(bc, depth, lc, N_HEADS_B, DV_B)

    y_sample, _ = run_pass(x_sample, False)
    return (y_prompt, y_sample, new_attn_k, new_attn_v, new_diff_k, new_diff_v)
```

```python
import functools
import math

import jax
import jax.numpy as jnp
from jax import lax
from jax.experimental import pallas as pl
from jax.experimental.pallas import tpu as pltpu

F32 = jnp.float32
BF16 = jnp.bfloat16

GRID_W = 64
ROPE_BASE = 10000.0
NORM_EPS = 1e-6
HEAD_DIM_A = 128
N_KV_HEADS_A = 2
GQA_GROUP = 4
N_HEADS_B = 4
DK_B = 64
DV_B = 128
N_FOURIER_GROUPS = 4
FOURIER_GROUP_DIM = 128
N_MOD = 6

LOG2E = math.log2(math.e)
LANES = 128
BF16_SUBLANES = 16
VMEM_LIMIT = 56 * 1024 * 1024


def _cparams(sem):
    return pltpu.CompilerParams(dimension_semantics=sem, vmem_limit_bytes=VMEM_LIMIT)


def _rms(x, g):
    return x * lax.rsqrt(jnp.mean(x * x, axis=-1, keepdims=True) + NORM_EPS) * g


def _ada_kernel(c_ref, w_ref, b_ref, o_ref):
    c = c_ref[...]
    s = (c * jax.nn.sigmoid(c)).astype(BF16)
    w = w_ref[...].astype(BF16)
    o_ref[...] = jnp.dot(s, w, preferred_element_type=F32) + b_ref[...]


def _ada_call(cvec, w_ada, b_ada):
    depth, d, n = w_ada.shape
    rows = cvec.shape[0]
    tn = 512
    return pl.pallas_call(
        _ada_kernel,
        out_shape=jax.ShapeDtypeStruct((depth, rows, n), F32),
        grid=(depth, n // tn),
        in_specs=[
            pl.BlockSpec((rows, d), lambda l, j: (0, 0)),
            pl.BlockSpec((None, d, tn), lambda l, j: (l, 0, j)),
            pl.BlockSpec((None, 1, tn), lambda l, j: (l, 0, j)),
        ],
        out_specs=pl.BlockSpec((None, rows, tn), lambda l, j: (l, 0, j)),
        compiler_params=_cparams(("parallel", "parallel")),
        name="ada_mod",
    )(cvec, w_ada, b_ada.reshape(depth, 1, n))


def _rope(xs, cos, sin_signed, shift):
    w = xs.shape[-1]
    lane = lax.broadcasted_iota(jnp.int32, xs.shape, 1)
    first = (lane % (2 * shift)) < shift
    rot = jnp.where(first, pltpu.roll(xs, w - shift, 1), pltpu.roll(xs, shift, 1))
    return xs * cos + rot * sin_signed


def _in_kernel(*refs, rope, ctx_out, n_alias, cols):
    it = iter(refs)
    x_ref, mod_ref, g_ref, w_ref, gq_ref, gk_ref, dft_ref = (next(it) for _ in range(7))
    if rope:
        cos_a, sin_a, cos_b, sin_b = (next(it)[...] for _ in range(4))
    for _ in range(n_alias):
        next(it)
    qkv_ref, y_ref = next(it), next(it)
    if ctx_out:
        ka_ref, va_ref, kb_ref, vb_ref = (next(it) for _ in range(4))
    c_qa, c_ka, c_va, c_qb, c_kb, c_vb, c_f, c_end = cols

    def put_cache(ref, off, v):
        head = off // LANES
        seqs, rows, _ = ref.shape
        heads = rows * seqs // v.shape[0]
        n = rows // heads
        for s in range(seqs):
            ref[s, pl.ds(head, n, stride=heads), :] = v[s * n:(s + 1) * n, :]

    m = mod_ref[...]
    h = _rms(x_ref[...], g_ref[...])
    hb = (h * (1.0 + m[1:2]) + m[0:1]).astype(BF16)
    gq = gq_ref[...]
    gk = gk_ref[...]
    scale_a = HEAD_DIM_A ** -0.5 * LOG2E
    scale_b = DK_B ** -0.5 * LOG2E
    chunk = 4 * LANES

    for c0 in range(0, c_end, chunk):
        acc = jnp.dot(hb, w_ref[:, c0:c0 + chunk], preferred_element_type=F32)
        for s in range(chunk // LANES):
            col = c0 + s * LANES
            v = acc[:, s * LANES:(s + 1) * LANES]
            if col < c_ka:
                v = _rms(v, gq)
                if rope:
                    v = _rope(v, cos_a, sin_a, HEAD_DIM_A // 4)
                qkv_ref[:, col:col + LANES] = (v * scale_a).astype(BF16)
            elif col < c_va:
                v = _rms(v, gk)
                if ctx_out:
                    put_cache(ka_ref, col - c_ka, v)
                if rope:
                    v = _rope(v, cos_a, sin_a, HEAD_DIM_A // 4)
                qkv_ref[:, col:col + LANES] = v.astype(BF16)
            elif col < c_qb:
                if ctx_out:
                    put_cache(va_ref, col - c_va, v)
                qkv_ref[:, col:col + LANES] = v.astype(BF16)
            elif col < c_kb:
                if rope:
                    v = _rope(v, cos_b, sin_b, DK_B // 4)
                qkv_ref[:, col:col + LANES] = (v * scale_b).astype(BF16)
            elif col < c_vb:
                if ctx_out:
                    put_cache(kb_ref, col - c_kb, v)
                if rope:
                    v = _rope(v, cos_b, sin_b, DK_B // 4)
                qkv_ref[:, col:col + LANES] = v.astype(BF16)
            elif col < c_f:
                if ctx_out:
                    put_cache(vb_ref, col - c_vb, v)
                qkv_ref[:, col:col + LANES] = v.astype(BF16)
            else:
                yy = jnp.dot(v.astype(BF16), dft_ref[...], preferred_element_type=F32)
                gcol = col - c_f
                half = c_end - c_f
                y_ref[:, gcol:gcol + LANES] = yy[:, :LANES].astype(BF16)
                y_ref[:, half + gcol:half + gcol + LANES] = yy[:, LANES:].astype(BF16)


def _in_call(x2d, mod_l, cond_row_fn, norm_g, w_in_b, gq, gk, dft_c, rope_tabs, seq_len, new_cache, tm):
    ctx_out = new_cache is not None
    t, d = x2d.shape
    d_in = w_in_b.shape[1]
    c_qa = 0
    c_ka = N_KV_HEADS_A * GQA_GROUP * HEAD_DIM_A
    c_va = c_ka + N_KV_HEADS_A * HEAD_DIM_A
    c_qb = c_va + N_KV_HEADS_A * HEAD_DIM_A
    c_kb = c_qb + N_HEADS_B * 2 * DK_B
    c_vb = c_kb + N_HEADS_B * 2 * DK_B
    c_f = c_vb + N_HEADS_B * DV_B
    c_end = c_f + N_FOURIER_GROUPS * FOURIER_GROUP_DIM
    assert c_end == d_in
    cols = (c_qa, c_ka, c_va, c_qb, c_kb, c_vb, c_f, c_end)
    rope = rope_tabs is not None
    n_f = c_end - c_f

    in_specs = [
        pl.BlockSpec((tm, d), lambda i: (i, 0)),
        pl.BlockSpec((None, N_MOD, d), lambda i: (cond_row_fn(i), 0, 0)),
        pl.BlockSpec((1, d), lambda i: (0, 0)),
        pl.BlockSpec((d, d_in), lambda i: (0, 0), pipeline_mode=pl.Buffered(1)),
        pl.BlockSpec((1, HEAD_DIM_A), lambda i: (0, 0)),
        pl.BlockSpec((1, HEAD_DIM_A), lambda i: (0, 0)),
        pl.BlockSpec((FOURIER_GROUP_DIM, 2 * FOURIER_GROUP_DIM), lambda i: (0, 0)),
    ]
    args = [x2d, mod_l, norm_g, w_in_b, gq, gk, dft_c]
    if rope:
        nblk = seq_len // tm
        for tab in rope_tabs:
            in_specs.append(pl.BlockSpec((tm, LANES), lambda i: (i % nblk, 0)))
            args.append(tab)
    out_shape = [jax.ShapeDtypeStruct((t, c_f), BF16), jax.ShapeDtypeStruct((t, 2 * n_f), BF16)]
    out_specs = [pl.BlockSpec((tm, c_f), lambda i: (i, 0)), pl.BlockSpec((tm, 2 * n_f), lambda i: (i, 0))]
    aliases = {}
    n_alias = 0
    if ctx_out:
        depth, layer, prev = new_cache
        nb = t // seq_len
        spt = tm // seq_len
        for width in (c_va - c_ka, c_qb - c_va, c_vb - c_kb, c_f - c_vb):
            rows = seq_len * (width // LANES)
            out_shape.append(jax.ShapeDtypeStruct((nb, depth, rows, LANES), F32))
            out_specs.append(pl.BlockSpec((spt, None, rows, LANES), lambda i: (i, layer, 0, 0)))
        if prev is not None:
            n_alias = len(prev)
            for a, arr in enumerate(prev):
                aliases[len(args)] = 2 + a
                in_specs.append(pl.BlockSpec(memory_space=pl.ANY))
                args.append(arr)
    return pl.pallas_call(
        functools.partial(_in_kernel, rope=rope, ctx_out=ctx_out, n_alias=n_alias, cols=cols),
        out_shape=out_shape,
        grid=(t // tm,),
        in_specs=in_specs,
        out_specs=out_specs,
        input_output_aliases=aliases,
        compiler_params=_cparams(("parallel",)),
        name="in_proj_ctx" if ctx_out else "in_proj_lat",
    )(*args)


def _softmax_pv(q, srcs):
    chunks = []
    for k_ref, v_ref, n_keys, kc, col in srcs:
        for c in range(n_keys // kc):
            chunks.append((k_ref, v_ref, c * kc, kc, col))

    def scores(ch):
        k_ref, _, off, kc, col = ch
        k = k_ref[off:off + kc, col:col + LANES].astype(BF16)
        return lax.dot_general(q, k, (((1,), (1,)), ((), ())), preferred_element_type=F32)

    def lane_fold(x, op):
        out = x[:, 0:LANES]
        for t in range(1, x.shape[1] // LANES):
            out = op(out, x[:, t * LANES:(t + 1) * LANES])
        return out

    m = l_part = acc = None
    s_next = scores(chunks[0])
    for i, ch in enumerate(chunks):
        s = s_next
        if i + 1 < len(chunks):
            s_next = scores(chunks[i + 1])
        _, v_ref, off, kc, col = ch
        row_max = jnp.max(lane_fold(s, jnp.maximum), axis=-1, keepdims=True)
        m_new = row_max if m is None else jnp.maximum(m, row_max)
        p = jnp.exp2(s - m_new)
        p_sum = lane_fold(p, jnp.add)
        v = v_ref[off:off + kc, col:col + LANES].astype(BF16)
        pv = jnp.dot(p.astype(BF16), v, preferred_element_type=F32)
        if m is None:
            l_part, acc = p_sum, pv
        else:
            alpha = jnp.exp2(m - m_new)
            l_part = alpha * l_part + p_sum
            acc = alpha * acc + pv
        m = m_new
    return acc / jnp.sum(l_part, axis=-1, keepdims=True)


def _srcs(cache_refs, k_ref, v_ref, kc, head):
    srcs = []
    col = head * LANES
    if cache_refs is not None:
        ck, cv = cache_refs
        srcs.append((ck, cv, ck.shape[0], ck.shape[0], col))
    n = k_ref.shape[0]
    srcs.append((k_ref, v_ref, n, min(kc, n), col))
    return srcs


def _attn_a_kernel(*refs, has_cache, tq, kc, heads):
    if has_cache:
        q_ref, ck_ref, cv_ref, k_ref, v_ref, o_ref = refs
        cache = (ck_ref, cv_ref)
    else:
        q_ref, k_ref, v_ref, o_ref = refs
        cache = None
    for kh in range(heads):
        base = kh * GQA_GROUP * LANES
        q4 = jnp.concatenate([q_ref[:, base + h * LANES:base + (h + 1) * LANES] for h in range(GQA_GROUP)], axis=0)
        o = _softmax_pv(q4, _srcs(cache, k_ref, v_ref, kc, kh))
        for h in range(GQA_GROUP):
            o_ref[:, base + h * LANES:base + (h + 1) * LANES] = o[h * tq:(h + 1) * tq].astype(BF16)


def _attn_b_kernel(*refs, has_cache, tq, kc, heads, lam_init):
    if has_cache:
        lam_ref, gs_ref, q_ref, ck_ref, cv_ref, k_ref, v_ref, o_ref = refs
        cache = (ck_ref, cv_ref)
    else:
        lam_ref, gs_ref, q_ref, k_ref, v_ref, o_ref = refs
        cache = None
    lp = lam_ref[...]
    lam = (jnp.exp(jnp.sum(lp[0:1] * lp[1:2], axis=-1, keepdims=True))
           - jnp.exp(jnp.sum(lp[2:3] * lp[3:4], axis=-1, keepdims=True)) + lam_init)
    for hd in range(heads):
        q = q_ref[:, hd * LANES:(hd + 1) * LANES]
        lane = lax.broadcasted_iota(jnp.int32, q.shape, 1)
        zero = jnp.zeros_like(q)
        qz = jnp.concatenate([jnp.where(lane < DK_B, q, zero), jnp.where(lane >= DK_B, q, zero)], axis=0)
        o = _softmax_pv(qz, _srcs(cache, k_ref, v_ref, kc, hd))
        dlt = o[:tq] - lam * o[tq:]
        o_ref[:, hd * LANES:(hd + 1) * LANES] = (_rms(dlt, gs_ref[...]) * (1.0 - lam_init)).astype(BF16)


def _attn_calls(qkv3, caches, layer, lam_params, g_sub, lam_init, tq_a, tq_b, kc, ha, hb):
    b, n, _ = qkv3.shape
    has_cache = caches is not None
    qa_blk =GQA_GROUP * HEAD_DIM_A // LANES
    k_a0 = N_KV_HEADS_A * qa_blk
    v_a0 = k_a0 + N_KV_HEADS_A
    q_b0 = v_a0 + N_KV_HEADS_A
    k_b0 = q_b0 + N_HEADS_B
    v_b0 = k_b0 + N_HEADS_B
    assert all(x % ha == 0 for x in (N_KV_HEADS_A, k_a0, v_a0)) and all(x % hb == 0 for x in (N_HEADS_B, q_b0, k_b0, v_b0))

    qw, kw = ha * GQA_GROUP * LANES, ha * LANES
    in_specs = [pl.BlockSpec((None, tq_a, qw), lambda bi, h, i: (bi, i, h))]
    args = [qkv3]
    if has_cache:
        ck, cv = caches[0], caches[1]
        p = ck.shape[2]
        in_specs += [pl.BlockSpec((None, None, p, kw), lambda bi, h, i: (bi, layer, 0, h))] * 2
        args += [ck, cv]
    in_specs += [pl.BlockSpec((None, n, kw), lambda bi, h, i: (bi, 0, k_a0 // ha + h)),
                 pl.BlockSpec((None, n, kw), lambda bi, h, i: (bi, 0, v_a0 // ha + h))]
    args += [qkv3, qkv3]
    att_a = pl.pallas_call(
        functools.partial(_attn_a_kernel, has_cache=has_cache, tq=tq_a, kc=kc, heads=ha),
        out_shape=jax.ShapeDtypeStruct((b, n, N_KV_HEADS_A * GQA_GROUP * HEAD_DIM_A), BF16),
        grid=(b, N_KV_HEADS_A // ha, n // tq_a),
        in_specs=in_specs,
        out_specs=pl.BlockSpec((None, tq_a, qw), lambda bi, h, i: (bi, i, h)),
        compiler_params=_cparams(("parallel", "parallel", "arbitrary")),
        name="attn_a_lat" if has_cache else "attn_a_ctx",
    )(*args)

    bw = hb * LANES
    in_specs = [pl.BlockSpec((4, DK_B), lambda bi, h, i: (0, 0)),
                pl.BlockSpec((1, DV_B), lambda bi, h, i: (0, 0)),
                pl.BlockSpec((None, tq_b, bw), lambda bi, h, i: (bi, i, q_b0 // hb + h))]
    args = [lam_params, g_sub, qkv3]
    if has_cache:
        ck, cv = caches[2], caches[3]
        p = ck.shape[2]
        in_specs += [pl.BlockSpec((None, None, p, bw), lambda bi, h, i: (bi, layer, 0, h))] * 2
        args += [ck, cv]
    in_specs += [pl.BlockSpec((None, n, bw), lambda bi, h, i: (bi, 0, k_b0 // hb + h)),
                 pl.BlockSpec((None, n, bw), lambda bi, h, i: (bi, 0, v_b0 // hb + h))]
    args += [qkv3, qkv3]
    att_b = pl.pallas_call(
        functools.partial(_attn_b_kernel, has_cache=has_cache, tq=tq_b, kc=kc, heads=hb, lam_init=lam_init),
        out_shape=jax.ShapeDtypeStruct((b, n, N_HEADS_B * DV_B), BF16),
        grid=(b, N_HEADS_B // hb, n // tq_b),
        in_specs=in_specs,
        out_specs=pl.BlockSpec((None, tq_b, bw), lambda bi, h, i: (bi, i, h)),
        compiler_params=_cparams(("parallel", "parallel", "arbitrary")),
        name="attn_b_lat" if has_cache else "attn_b_ctx",
    )(*args)
    return att_a, att_b


def _dft_kernel(x_ref, mat_ref, *rest, width, scale):
    x = x_ref[...]
    if x.ndim == 3:
        x = x.reshape(x.shape[0] * x.shape[1], x.shape[2])
    xs = jnp.concatenate([x[:, :width], x[:, width:]], axis=0)
    u = jnp.dot(mat_ref[...], xs, preferred_element_type=F32)
    if len(rest) == 3:
        tc_ref, ts_ref, o_ref = rest
        half = u.shape[0] // 2
        ur, ui = u[:half], u[half:]
        reps = width // LANES
        tc = jnp.concatenate([tc_ref[...].reshape(half, LANES)] * reps, axis=1)
        ts = jnp.concatenate([ts_ref[...].reshape(half, LANES)] * reps, axis=1)
        o_ref[:, :, :width] = (ur * tc - ui * ts).astype(BF16).reshape(o_ref.shape[0], o_ref.shape[1], width)
        o_ref[:, :, width:] = (ur * ts + ui * tc).astype(BF16).reshape(o_ref.shape[0], o_ref.shape[1], width)
    else:
        (o_ref,) = rest
        o_ref[...] = (u * scale).astype(BF16).reshape(o_ref.shape)


def _cos_sin(n_rows, n_cols, period):
    a = jnp.arange(n_rows, dtype=jnp.int32)[:, None]
    b = jnp.arange(n_cols, dtype=jnp.int32)[None, :]
    ang = ((a * b) % period).astype(F32) * (2.0 * math.pi / period)
    return jnp.cos(ang), jnp.sin(ang)


def _fourier_call(y3, n1, n2):
    b, n, w2 = y3.shape
    width = w2 // 2
    scale = 1.0 / math.sqrt(n * FOURIER_GROUP_DIM)
    c1, s1 = _cos_sin(n1, n1, n1)
    w_real = jnp.stack([c1, -s1], axis=1)
    if n2 == 1:
        mat = w_real.reshape(n1, 2 * n1).astype(BF16)
        return pl.pallas_call(
            functools.partial(_dft_kernel, width=width, scale=scale),
            out_shape=jax.ShapeDtypeStruct((b, n, width), BF16),
            grid=(b,),
            in_specs=[pl.BlockSpec((None, n, w2), lambda bi: (bi, 0, 0)),
                      pl.BlockSpec((n1, 2 * n1), lambda bi: (0, 0))],
            out_specs=pl.BlockSpec((None, n, width), lambda bi: (bi, 0, 0)),
            compiler_params=_cparams(("parallel",)),
            name="dft_ctx",
        )(y3, mat)

    g = BF16_SUBLANES
    eye = jnp.eye(g, dtype=F32)
    c2, s2 = _cos_sin(n2, n2, n2)
    w_cplx = jnp.stack([jnp.stack([c2, -s2], axis=1), jnp.stack([s2, c2], axis=1)], axis=0)
    mat1 = jnp.einsum('akbn,ji->ajkbni', w_cplx, eye).reshape(2 * g * n2, 2 * n2 * g).astype(BF16)
    mat2 = jnp.einsum('kbn,ji->kjbni', w_real, eye).reshape(n1 * g, 2 * n1 * g).astype(BF16)
    tc, ts = _cos_sin(n1, n2, n)
    tc = jnp.broadcast_to(tc[:, :, None], (n1, n2, LANES))
    ts = jnp.broadcast_to(ts[:, :, None], (n1, n2, LANES))
    t = pl.pallas_call(
        functools.partial(_dft_kernel, width=width, scale=None),
        out_shape=jax.ShapeDtypeStruct((b, n1, n2, w2), BF16),
        grid=(b, n1 // g),
        in_specs=[pl.BlockSpec((None, n2, g, w2), lambda bi, j: (bi, 0, j, 0)),
                  pl.BlockSpec(mat1.shape, lambda bi, j: (0, 0), pipeline_mode=pl.Buffered(1)),
                  pl.BlockSpec((g, n2, LANES), lambda bi, j: (j, 0, 0)),
                  pl.BlockSpec((g, n2, LANES), lambda bi, j: (j, 0, 0))],
        out_specs=pl.BlockSpec((None, g, n2, w2), lambda bi, j: (bi, j, 0, 0)),
        compiler_params=_cparams(("parallel", "parallel")),
        name="dft_stage1",
    )(y3.reshape(b, n2, n1, w2), mat1, tc, ts)
    out = pl.pallas_call(
        functools.partial(_dft_kernel, width=width, scale=scale),
        out_shape=jax.ShapeDtypeStruct((b, n1, n2, width), BF16),
        grid=(b, n2 // g),
        in_specs=[pl.BlockSpec((None, n1, g, w2), lambda bi, j: (bi, 0, j, 0)),
                  pl.BlockSpec(mat2.shape, lambda bi, j: (0, 0), pipeline_mode=pl.Buffered(1))],
        out_specs=pl.BlockSpec((None, n1, g, width), lambda bi, j: (bi, 0, j, 0)),
        compiler_params=_cparams(("parallel", "parallel")),
        name="dft_stage2",
    )(t, mat2)
    return out.reshape(b, n, width)


def _out_kernel(a_ref, b_ref, f_ref, w_ref, x_ref, mod_ref, g_ref, x1_ref, h2_ref):
    ca = a_ref.shape[1]
    cb = b_ref.shape[1]
    acc = jnp.dot(a_ref[...], w_ref[0:ca, :], preferred_element_type=F32)
    acc += jnp.dot(b_ref[...], w_ref[ca:ca + cb, :], preferred_element_type=F32)
    acc += jnp.dot(f_ref[...], w_ref[ca + cb:, :], preferred_element_type=F32)
    m = mod_ref[...]
    x1 = x_ref[...] + m[2:3] * acc
    x1_ref[...] = x1
    h = _rms(x1, g_ref[...])
    h2_ref[...] = (h * (1.0 + m[4:5]) + m[3:4]).astype(BF16)


def _out_call(att_a, att_b, four, w_out_b, x2d, mod_l, cond_row_fn, norm_g, tm):
    t, d = x2d.shape
    ca, cb, cf = att_a.shape[1], att_b.shape[1], four.shape[1]
    return pl.pallas_call(
        _out_kernel,
        out_shape=[jax.ShapeDtypeStruct((t, d), F32), jax.ShapeDtypeStruct((t, d), BF16)],
        grid=(t // tm,),
        in_specs=[
            pl.BlockSpec((tm, ca), lambda i: (i, 0)),
            pl.BlockSpec((tm, cb), lambda i: (i, 0)),
            pl.BlockSpec((tm, cf), lambda i: (i, 0)),
            pl.BlockSpec((ca + cb + cf, d), lambda i: (0, 0), pipeline_mode=pl.Buffered(1)),
            pl.BlockSpec((tm, d), lambda i: (i, 0)),
            pl.BlockSpec((None, N_MOD, d), lambda i: (cond_row_fn(i), 0, 0)),
            pl.BlockSpec((1, d), lambda i: (0, 0)),
        ],
        out_specs=[pl.BlockSpec((tm, d), lambda i: (i, 0)), pl.BlockSpec((tm, d), lambda i: (i, 0))],
        compiler_params=_cparams(("parallel",)),
        name="out_proj",
    )(att_a, att_b, four, w_out_b, x2d, mod_l, norm_g)


HALO = 16


def _ffn_kernel(h_ref, hp_ref, hn_ref, wg_ref, wu_ref, cw_ref, cb_ref, wd_ref, x1_ref, mod_ref, gf_ref,
                o_ref, hext_ref, *, tm, seq_len, final):
    i = pl.program_id(0)
    j = pl.program_id(1)

    @pl.when(j == 0)
    def _():
        hext_ref[0:HALO, :] = hp_ref[...]
        hext_ref[HALO:HALO + tm, :] = h_ref[...]
        hext_ref[HALO + tm:, :] = hn_ref[...]
        o_ref[...] = jnp.zeros_like(o_ref)

    g = jnp.dot(hext_ref[...], wg_ref[...], preferred_element_type=F32)
    u = jnp.dot(h_ref[...], wu_ref[...], preferred_element_type=F32)
    ext = tm + 2 * HALO
    pos = (i * tm + lax.broadcasted_iota(jnp.int32, (tm, 1), 0)) % seq_len
    g_prev = jnp.where(pos == 0, 0.0, pltpu.roll(g, 1, 0)[HALO:HALO + tm])
    g_next = jnp.where(pos == seq_len - 1, 0.0, pltpu.roll(g, ext - 1, 0)[HALO:HALO + tm])
    cw = cw_ref[...]
    gc = g_prev * cw[0:1] + g[HALO:HALO + tm] * cw[1:2] + g_next * cw[2:3] + cb_ref[...]
    act = (gc * jax.nn.sigmoid(gc)) * u
    o_ref[...] += jnp.dot(act.astype(BF16), wd_ref[...], preferred_element_type=F32)

    @pl.when(j == pl.num_programs(1) - 1)
    def _():
        m = mod_ref[...]
        x2 = x1_ref[...] + m[5:6] * o_ref[...]
        if final:
            x2 = _rms(x2, gf_ref[...])
        o_ref[...] = x2


def _ffn_call(h2, x1, w_gate_b, w_up_b, conv_w, conv_b, w_down_b, mod_l, cond_row_fn, final_g, seq_len, final, tm, tf):
    t, d = x1.shape
    f = w_gate_b.shape[1]
    hb = tm // HALO
    last = t // HALO - 1
    return pl.pallas_call(
        functools.partial(_ffn_kernel, tm=tm, seq_len=seq_len, final=final),
        out_shape=jax.ShapeDtypeStruct((t, d), F32),
        grid=(t // tm, f // tf),
        in_specs=[
            pl.BlockSpec((tm, d), lambda i, j: (i, 0)),
            pl.BlockSpec((HALO, d), lambda i, j: (jnp.maximum(i * hb - 1, 0), 0)),
            pl.BlockSpec((HALO, d), lambda i, j: (jnp.minimum((i + 1) * hb, last), 0)),
            pl.BlockSpec((d, tf), lambda i, j: (0, j)),
            pl.BlockSpec((d, tf), lambda i, j: (0, j)),
            pl.BlockSpec((3, tf), lambda i, j: (0, j)),
            pl.BlockSpec((1, tf), lambda i, j: (0, j)),
            pl.BlockSpec((tf, d), lambda i, j: (j, 0)),
            pl.BlockSpec((tm, d), lambda i, j: (i, 0)),
            pl.BlockSpec((None, N_MOD, d), lambda i, j: (cond_row_fn(i), 0, 0)),
            pl.BlockSpec((1, d), lambda i, j: (0, 0)),
        ],
        out_specs=pl.BlockSpec((tm, d), lambda i, j: (i, 0)),
        scratch_shapes=[pltpu.VMEM((tm + 2 * HALO, d), BF16)],
        compiler_params=_cparams(("parallel", "arbitrary")),
        name="conv_ffn",
    )(h2, h2, h2, w_gate_b, w_up_b, conv_w, conv_b, w_down_b, x1, mod_l, final_g)


def _rope_tables(n, head_dim):
    rows = n // GRID_W
    t_row = jnp.repeat(jnp.arange(rows, dtype=F32), GRID_W)
    t_col = jnp.tile(jnp.arange(GRID_W, dtype=F32), rows)
    axis_dim = head_dim // 2
    inv = jnp.power(ROPE_BASE, -jnp.arange(0, axis_dim, 2, dtype=F32) / axis_dim)
    ar = t_row[:, None] * inv[None, :]
    ac = t_col[:, None] * inv[None, :]
    ang = jnp.concatenate([ar, ar, ac, ac], axis=-1)
    reps = LANES // head_dim
    quarter = head_dim // 4
    sign = jnp.where((jnp.arange(head_dim) % (2 * quarter)) < quarter, -1.0, 1.0).astype(F32)
    cos = jnp.tile(jnp.cos(ang), (1, reps))
    sin_signed = jnp.tile(jnp.sin(ang) * sign[None, :], (1, reps))
    return cos, sin_signed


def kernel(x_prompt, x_sample, cache_attn_k, cache_attn_v, cache_diff_k, cache_diff_v, c, c_ctx, norm1_g, norm2_g, w_ada, b_ada, w_in, attn_q_norm_g, attn_k_norm_g, diff_lambda_q1, diff_lambda_k1, diff_lambda_q2, diff_lambda_k2, diff_subnorm_g, w_out, ffn_w_gate, ffn_w_up, ffn_conv_w, ffn_conv_b, ffn_w_down, final_norm_g):
    depth = w_in.shape[0]
    bc, lc, d = x_prompt.shape
    bl, ll, _ = x_sample.shape
    past = cache_attn_k.shape[2]

    w_in_b = [w_in[l].astype(BF16) for l in range(depth)]
    w_out_b = [w_out[l].astype(BF16) for l in range(depth)]
    w_gate_b = [ffn_w_gate[l].astype(BF16) for l in range(depth)]
    w_up_b = [ffn_w_up[l].astype(BF16) for l in range(depth)]
    w_down_b = [ffn_w_down[l].astype(BF16) for l in range(depth)]

    n_rows = 8 * ((1 + bl + 7) // 8)
    cvec = jnp.concatenate([c_ctx[None, :], c, jnp.zeros((n_rows - 1 - bl, d), F32)], axis=0)
    mod = _ada_call(cvec, w_ada, b_ada).reshape(depth, n_rows, N_MOD, d)

    dft_c = jnp.concatenate(_cos_sin(FOURIER_GROUP_DIM, FOURIER_GROUP_DIM, FOURIER_GROUP_DIM), axis=1).astype(BF16)
    rope_tabs = _rope_tables(ll, HEAD_DIM_A) + _rope_tables(ll, DK_B)
    caches = (cache_attn_k.reshape(bl, depth, past, N_KV_HEADS_A * HEAD_DIM_A),
              cache_attn_v.reshape(bl, depth, past, N_KV_HEADS_A * HEAD_DIM_A),
              cache_diff_k.reshape(bl, depth, past, N_HEADS_B * 2 * DK_B),
              cache_diff_v.reshape(bl, depth, past, N_HEADS_B * DV_B))
    lam_all = jnp.stack([diff_lambda_q1, diff_lambda_k1, diff_lambda_q2, diff_lambda_k2], axis=1)

    def run_pass(x3, is_ctx):
        b, n, _ = x3.shape
        t = b * n
        tm = 512
        x = x3.reshape(t, d)
        if is_ctx:
            cond_row_fn = lambda i: 0
        else:
            cond_row_fn = lambda i: 1 + (i * tm) // n
        new_kv = None
        for l in range(depth):
            lam_init = 0.8 - 0.6 * math.exp(-0.3 * l)
            res = _in_call(x, mod[l], cond_row_fn, norm1_g[l][None, :], w_in_b[l],
                           attn_q_norm_g[l][None, :], attn_k_norm_g[l][None, :], dft_c,
                           None if is_ctx else rope_tabs, n, (depth, l, new_kv) if is_ctx else None, tm)
            qkv, y12 = res[0], res[1]
            if is_ctx:
                new_kv = res[2:]
            qkv3 = qkv.reshape(b, n, qkv.shape[1])
            att_a, att_b = _attn_calls(qkv3, None if is_ctx else caches, l, lam_all[l],
                                       diff_subnorm_g[l][None, :], lam_init,
                                       tq_a=min(256, n), tq_b=min(512, n), kc=1024,
                                       ha=N_KV_HEADS_A if is_ctx else 1, hb=N_HEADS_B if is_ctx else 1)
            if is_ctx:
                four = _fourier_call(y12.reshape(b, n, y12.shape[1]), n, 1)
            else:
                four = _fourier_call(y12.reshape(b, n, y12.shape[1]), GRID_W, n // GRID_W)
            x1, h2 = _out_call(att_a.reshape(t, -1), att_b.reshape(t, -1), four.reshape(t, -1), w_out_b[l],
                               x, mod[l], cond_row_fn, norm2_g[l][None, :], tm)
            x = _ffn_call(h2, x1, w_gate_b[l], w_up_b[l], ffn_conv_w[l], ffn_conv_b[l][None, :], w_down_b[l],
                          mod[l], cond_row_fn, final_norm_g[None, :], n, l == depth - 1, tm, 512)
        return x.reshape(b, n, d), new_kv

    y_prompt, kvs = run_pass(x_prompt, True)
    new_attn_k = kvs[0].reshape(bc, depth, lc, N_KV_HEADS_A, HEAD_DIM_A)
    new_attn_v = kvs[1].reshape(bc, depth, lc, N_KV_HEADS_A, HEAD_DIM_A)
    new_diff_k = kvs[2].reshape(bc, depth, lc, N_HEADS_B, 2 * DK_B)
    new_diff_v = kvs[3].reshape(bc, depth, lc, N_HEADS_B, DV_B)

    y_sample, _ = run_pass(x_sample, False)
    return (y_prompt, y_sample, new_attn_k, new_attn_v, new_diff_k, new_diff_v)
```

```python
import functools
import math

import jax
import jax.numpy as jnp
from jax import lax
from jax.experimental import pallas as pl
from jax.experimental.pallas import tpu as pltpu

F32 = jnp.float32
BF16 = jnp.bfloat16

GRID_W = 64
ROPE_BASE = 10000.0
NORM_EPS = 1e-6
HEAD_DIM_A = 128
N_KV_HEADS_A = 2
GQA_GROUP = 4
N_HEADS_B = 4
DK_B = 64
DV_B = 128
N_FOURIER_GROUPS = 4
FOURIER_GROUP_DIM = 128
N_MOD = 6

LOG2E = math.log2(math.e)
LANES = 128
BF16_SUBLANES = 16
VMEM_LIMIT = 56 * 1024 * 1024


def _cparams(sem):
    return pltpu.CompilerParams(dimension_semantics=sem, vmem_limit_bytes=VMEM_LIMIT)


def _rms(x, g):
    return x * lax.rsqrt(jnp.mean(x * x, axis=-1, keepdims=True) + NORM_EPS) * g


def _ada_kernel(c_ref, w_ref, b_ref, o_ref):
    c = c_ref[...]
    s = (c * jax.nn.sigmoid(c)).astype(BF16)
    w = w_ref[...].astype(BF16)
    o_ref[...] = jnp.dot(s, w, preferred_element_type=F32) + b_ref[...]


def _ada_call(cvec, w_ada, b_ada):
    depth, d, n = w_ada.shape
    rows = cvec.shape[0]
    tn = 512
    return pl.pallas_call(
        _ada_kernel,
        out_shape=jax.ShapeDtypeStruct((depth, rows, n), F32),
        grid=(depth, n // tn),
        in_specs=[
            pl.BlockSpec((rows, d), lambda l, j: (0, 0)),
            pl.BlockSpec((None, d, tn), lambda l, j: (l, 0, j)),
            pl.BlockSpec((None, 1, tn), lambda l, j: (l, 0, j)),
        ],
        out_specs=pl.BlockSpec((None, rows, tn), lambda l, j: (l, 0, j)),
        compiler_params=_cparams(("parallel", "parallel")),
        name="ada_mod",
    )(cvec, w_ada, b_ada.reshape(depth, 1, n))


def _rope(xs, cos, sin_signed, shift):
    w = xs.shape[-1]
    lane = lax.broadcasted_iota(jnp.int32, xs.shape, 1)
    first = (lane % (2 * shift)) < shift
    rot = jnp.where(first, pltpu.roll(xs, w - shift, 1), pltpu.roll(xs, shift, 1))
    return xs * cos + rot * sin_signed


def _in_kernel(*refs, rope, ctx_out, n_alias, cols):
    it = iter(refs)
    x_ref, mod_ref, g_ref, w_ref, gq_ref, gk_ref, dft_ref = (next(it) for _ in range(7))
    if rope:
        cos_a, sin_a, cos_b, sin_b = (next(it)[...] for _ in range(4))
    for _ in range(n_alias):
        next(it)
    qkv_ref, y_ref = next(it), next(it)
    if ctx_out:
        ka_ref, va_ref, kb_ref, vb_ref = (next(it) for _ in range(4))
    c_qa, c_ka, c_va, c_qb, c_kb, c_vb, c_f, c_end = cols

    def put_cache(ref, off, v):
        head = off // LANES
        seqs, rows, _ = ref.shape
        heads = rows * seqs // v.shape[0]
        n = rows // heads
        for s in range(seqs):
            ref[s, pl.ds(head, n, stride=heads), :] = v[s * n:(s + 1) * n, :]

    m = mod_ref[...]
    h = _rms(x_ref[...], g_ref[...])
    hb = (h * (1.0 + m[1:2]) + m[0:1]).astype(BF16)
    gq = gq_ref[...]
    gk = gk_ref[...]
    scale_a = HEAD_DIM_A ** -0.5 * LOG2E
    scale_b = DK_B ** -0.5 * LOG2E
    chunk = 4 * LANES

    for c0 in range(0, c_end, chunk):
        acc = jnp.dot(hb, w_ref[:, c0:c0 + chunk], preferred_element_type=F32)
        for s in range(chunk // LANES):
            col = c0 + s * LANES
            v = acc[:, s * LANES:(s + 1) * LANES]
            if col < c_ka:
                v = _rms(v, gq)
                if rope:
                    v = _rope(v, cos_a, sin_a, HEAD_DIM_A // 4)
                qkv_ref[:, col:col + LANES] = (v * scale_a).astype(BF16)
            elif col < c_va:
                v = _rms(v, gk)
                if ctx_out:
                    put_cache(ka_ref, col - c_ka, v)
                if rope:
                    v = _rope(v, cos_a, sin_a, HEAD_DIM_A // 4)
                qkv_ref[:, col:col + LANES] = v.astype(BF16)
            elif col < c_qb:
                if ctx_out:
                    put_cache(va_ref, col - c_va, v)
                qkv_ref[:, col:col + LANES] = v.astype(BF16)
            elif col < c_kb:
                if rope:
                    v = _rope(v, cos_b, sin_b, DK_B // 4)
                qkv_ref[:, col:col + LANES] = (v * scale_b).astype(BF16)
            elif col < c_vb:
                if ctx_out:
                    put_cache(kb_ref, col - c_kb, v)
                if rope:
                    v = _rope(v, cos_b, sin_b, DK_B // 4)
                qkv_ref[:, col:col + LANES] = v.astype(BF16)
            elif col < c_f:
                if ctx_out:
                    put_cache(vb_ref, col - c_vb, v)
                qkv_ref[:, col:col + LANES] = v.astype(BF16)
            else:
                yy = jnp.dot(v.astype(BF16), dft_ref[...], preferred_element_type=F32)
                gcol = col - c_f
                half = c_end - c_f
                y_ref[:, gcol:gcol + LANES] = yy[:, :LANES].astype(BF16)
                y_ref[:, half + gcol:half + gcol + LANES] = yy[:, LANES:].astype(BF16)


def _in_call(x2d, mod_l, cond_row_fn, norm_g, w_in_b, gq, gk, dft_c, rope_tabs, seq_len, new_cache, tm):
    ctx_out = new_cache is not None
    t, d = x2d.shape
    w_in_b, w_layer = w_in_b
    d_in = w_in_b.shape[2]
    c_qa = 0
    c_ka = N_KV_HEADS_A * GQA_GROUP * HEAD_DIM_A
    c_va = c_ka + N_KV_HEADS_A * HEAD_DIM_A
    c_qb = c_va + N_KV_HEADS_A * HEAD_DIM_A
    c_kb = c_qb + N_HEADS_B * 2 * DK_B
    c_vb = c_kb + N_HEADS_B * 2 * DK_B
    c_f = c_vb + N_HEADS_B * DV_B
    c_end = c_f + N_FOURIER_GROUPS * FOURIER_GROUP_DIM
    assert c_end == d_in
    cols = (c_qa, c_ka, c_va, c_qb, c_kb, c_vb, c_f, c_end)
    rope = rope_tabs is not None
    n_f = c_end - c_f

    in_specs = [
        pl.BlockSpec((tm, d), lambda i: (i, 0)),
        pl.BlockSpec((None, N_MOD, d), lambda i: (cond_row_fn(i), 0, 0)),
        pl.BlockSpec((1, d), lambda i: (0, 0)),
        pl.BlockSpec((None, d, d_in), lambda i: (w_layer, 0, 0), pipeline_mode=pl.Buffered(1)),
        pl.BlockSpec((1, HEAD_DIM_A), lambda i: (0, 0)),
        pl.BlockSpec((1, HEAD_DIM_A), lambda i: (0, 0)),
        pl.BlockSpec((FOURIER_GROUP_DIM, 2 * FOURIER_GROUP_DIM), lambda i: (0, 0)),
    ]
    args = [x2d, mod_l, norm_g, w_in_b, gq, gk, dft_c]
    if rope:
        nblk = seq_len // tm
        for tab in rope_tabs:
            in_specs.append(pl.BlockSpec((tm, LANES), lambda i: (i % nblk, 0)))
            args.append(tab)
    out_shape = [jax.ShapeDtypeStruct((t, c_f), BF16), jax.ShapeDtypeStruct((t, 2 * n_f), BF16)]
    out_specs = [pl.BlockSpec((tm, c_f), lambda i: (i, 0)), pl.BlockSpec((tm, 2 * n_f), lambda i: (i, 0))]
    aliases = {}
    n_alias = 0
    if ctx_out:
        depth, layer, prev = new_cache
        nb = t // seq_len
        spt = tm // seq_len
        for width in (c_va - c_ka, c_qb - c_va, c_vb - c_kb, c_f - c_vb):
            rows = seq_len * (width // LANES)
            out_shape.append(jax.ShapeDtypeStruct((nb, depth, rows, LANES), F32))
            out_specs.append(pl.BlockSpec((spt, None, rows, LANES), lambda i: (i, layer, 0, 0)))
        if prev is not None:
            n_alias = len(prev)
            for a, arr in enumerate(prev):
                aliases[len(args)] = 2 + a
                in_specs.append(pl.BlockSpec(memory_space=pl.ANY))
                args.append(arr)
    return pl.pallas_call(
        functools.partial(_in_kernel, rope=rope, ctx_out=ctx_out, n_alias=n_alias, cols=cols),
        out_shape=out_shape,
        grid=(t // tm,),
        in_specs=in_specs,
        out_specs=out_specs,
        input_output_aliases=aliases,
        compiler_params=_cparams(("parallel",)),
        name="in_proj_ctx" if ctx_out else "in_proj_lat",
    )(*args)


def _softmax_pv(q, srcs):
    chunks = []
    for k_ref, v_ref, n_keys, kc, col in srcs:
        for c in range(n_keys // kc):
            chunks.append((k_ref, v_ref, c * kc, kc, col))

    def scores(ch):
        k_ref, _, off, kc, col = ch
        k = k_ref[off:off + kc, col:col + LANES].astype(BF16)
        return lax.dot_general(q, k, (((1,), (1,)), ((), ())), preferred_element_type=F32)

    def lane_fold(x, op):
        out = x[:, 0:LANES]
        for t in range(1, x.shape[1] // LANES):
            out = op(out, x[:, t * LANES:(t + 1) * LANES])
        return out

    m = l_part = acc = None
    s_next = scores(chunks[0])
    for i, ch in enumerate(chunks):
        s = s_next
        if i + 1 < len(chunks):
            s_next = scores(chunks[i + 1])
        _, v_ref, off, kc, col = ch
        row_max = jnp.max(lane_fold(s, jnp.maximum), axis=-1, keepdims=True)
        m_new = row_max if m is None else jnp.maximum(m, row_max)
        p = jnp.exp2(s - m_new)
        p_sum = lane_fold(p, jnp.add)
        v = v_ref[off:off + kc, col:col + LANES].astype(BF16)
        pv = jnp.dot(p.astype(BF16), v, preferred_element_type=F32)
        if m is None:
            l_part, acc = p_sum, pv
        else:
            alpha = jnp.exp2(m - m_new)
            l_part = alpha * l_part + p_sum
            acc = alpha * acc + pv
        m = m_new
    return acc / jnp.sum(l_part, axis=-1, keepdims=True)


def _srcs(cache_refs, k_ref, v_ref, kc, head):
    srcs = []
    col = head * LANES
    if cache_refs is not None:
        ck, cv = cache_refs
        srcs.append((ck, cv, ck.shape[0], ck.shape[0], col))
    n = k_ref.shape[0]
    srcs.append((k_ref, v_ref, n, min(kc, n), col))
    return srcs


def _attn_a_kernel(*refs, has_cache, tq, kc, heads):
    if has_cache:
        q_ref, ck_ref, cv_ref, k_ref, v_ref, o_ref = refs
        cache = (ck_ref, cv_ref)
    else:
        q_ref, k_ref, v_ref, o_ref = refs
        cache = None
    for kh in range(heads):
        base = kh * GQA_GROUP * LANES
        q4 = jnp.concatenate([q_ref[:, base + h * LANES:base + (h + 1) * LANES] for h in range(GQA_GROUP)], axis=0)
        o = _softmax_pv(q4, _srcs(cache, k_ref, v_ref, kc, kh))
        for h in range(GQA_GROUP):
            o_ref[:, base + h * LANES:base + (h + 1) * LANES] = o[h * tq:(h + 1) * tq].astype(BF16)


def _attn_b_kernel(*refs, has_cache, tq, kc, heads, lam_init):
    if has_cache:
        lam_ref, gs_ref, q_ref, ck_ref, cv_ref, k_ref, v_ref, o_ref = refs
        cache = (ck_ref, cv_ref)
    else:
        lam_ref, gs_ref, q_ref, k_ref, v_ref, o_ref = refs
        cache = None
    lp = lam_ref[...]
    lam = (jnp.exp(jnp.sum(lp[0:1] * lp[1:2], axis=-1, keepdims=True))
           - jnp.exp(jnp.sum(lp[2:3] * lp[3:4], axis=-1, keepdims=True)) + lam_init)
    for hd in range(heads):
        q = q_ref[:, hd * LANES:(hd + 1) * LANES]
        lane = lax.broadcasted_iota(jnp.int32, q.shape, 1)
        zero = jnp.zeros_like(q)
        qz = jnp.concatenate([jnp.where(lane < DK_B, q, zero), jnp.where(lane >= DK_B, q, zero)], axis=0)
        o = _softmax_pv(qz, _srcs(cache, k_ref, v_ref, kc, hd))
        dlt = o[:tq] - lam * o[tq:]
        o_ref[:, hd * LANES:(hd + 1) * LANES] = (_rms(dlt, gs_ref[...]) * (1.0 - lam_init)).astype(BF16)


def _attn_calls(qkv3, caches, layer, lam_params, g_sub, lam_init, tq_a, tq_b, kc, ha, hb):
    b, n, _ = qkv3.shape
    has_cache = caches is not None
    qa_blk =GQA_GROUP * HEAD_DIM_A // LANES
    k_a0 = N_KV_HEADS_A * qa_blk
    v_a0 = k_a0 + N_KV_HEADS_A
    q_b0 = v_a0 + N_KV_HEADS_A
    k_b0 = q_b0 + N_HEADS_B
    v_b0 = k_b0 + N_HEADS_B
    assert all(x % ha == 0 for x in (N_KV_HEADS_A, k_a0, v_a0)) and all(x % hb == 0 for x in (N_HEADS_B, q_b0, k_b0, v_b0))

    qw, kw = ha * GQA_GROUP * LANES, ha * LANES
    in_specs = [pl.BlockSpec((None, tq_a, qw), lambda bi, h, i: (bi, i, h))]
    args = [qkv3]
    if has_cache:
        ck, cv = caches[0], caches[1]
        p = ck.shape[2]
        in_specs += [pl.BlockSpec((None, None, p, kw), lambda bi, h, i: (bi, layer, 0, h))] * 2
        args += [ck, cv]
    in_specs += [pl.BlockSpec((None, n, kw), lambda bi, h, i: (bi, 0, k_a0 // ha + h)),
                 pl.BlockSpec((None, n, kw), lambda bi, h, i: (bi, 0, v_a0 // ha + h))]
    args += [qkv3, qkv3]
    att_a = pl.pallas_call(
        functools.partial(_attn_a_kernel, has_cache=has_cache, tq=tq_a, kc=kc, heads=ha),
        out_shape=jax.ShapeDtypeStruct((b, n, N_KV_HEADS_A * GQA_GROUP * HEAD_DIM_A), BF16),
        grid=(b, N_KV_HEADS_A // ha, n // tq_a),
        in_specs=in_specs,
        out_specs=pl.BlockSpec((None, tq_a, qw), lambda bi, h, i: (bi, i, h)),
        compiler_params=_cparams(("parallel", "parallel", "arbitrary")),
        name="attn_a_lat" if has_cache else "attn_a_ctx",
    )(*args)

    bw = hb * LANES
    in_specs = [pl.BlockSpec((4, DK_B), lambda bi, h, i: (0, 0)),
                pl.BlockSpec((1, DV_B), lambda bi, h, i: (0, 0)),
                pl.BlockSpec((None, tq_b, bw), lambda bi, h, i: (bi, i, q_b0 // hb + h))]
    args = [lam_params, g_sub, qkv3]
    if has_cache:
        ck, cv = caches[2], caches[3]
        p = ck.shape[2]
        in_specs += [pl.BlockSpec((None, None, p, bw), lambda bi, h, i: (bi, layer, 0, h))] * 2
        args += [ck, cv]
    in_specs += [pl.BlockSpec((None, n, bw), lambda bi, h, i: (bi, 0, k_b0 // hb + h)),
                 pl.BlockSpec((None, n, bw), lambda bi, h, i: (bi, 0, v_b0 // hb + h))]
    args += [qkv3, qkv3]
    att_b = pl.pallas_call(
        functools.partial(_attn_b_kernel, has_cache=has_cache, tq=tq_b, kc=kc, heads=hb, lam_init=lam_init),
        out_shape=jax.ShapeDtypeStruct((b, n, N_HEADS_B * DV_B), BF16),
        grid=(b, N_HEADS_B // hb, n // tq_b),
        in_specs=in_specs,
        out_specs=pl.BlockSpec((None, tq_b, bw), lambda bi, h, i: (bi, i, h)),
        compiler_params=_cparams(("parallel", "parallel", "arbitrary")),
        name="attn_b_lat" if has_cache else "attn_b_ctx",
    )(*args)
    return att_a, att_b


def _dft_kernel(x_ref, mat_ref, *rest, width, scale):
    x = x_ref[...]
    if x.ndim == 3:
        x = x.reshape(x.shape[0] * x.shape[1], x.shape[2])
    xs = jnp.concatenate([x[:, :width], x[:, width:]], axis=0)
    u = jnp.dot(mat_ref[...], xs, preferred_element_type=F32)
    if len(rest) == 3:
        tc_ref, ts_ref, o_ref = rest
        half = u.shape[0] // 2
        ur, ui = u[:half], u[half:]
        reps = width // LANES
        tc = jnp.concatenate([tc_ref[...].reshape(half, LANES)] * reps, axis=1)
        ts = jnp.concatenate([ts_ref[...].reshape(half, LANES)] * reps, axis=1)
        o_ref[:, :, :width] = (ur * tc - ui * ts).astype(BF16).reshape(o_ref.shape[0], o_ref.shape[1], width)
        o_ref[:, :, width:] = (ur * ts + ui * tc).astype(BF16).reshape(o_ref.shape[0], o_ref.shape[1], width)
    else:
        (o_ref,) = rest
        o_ref[...] = (u * scale).astype(BF16).reshape(o_ref.shape)


def _cos_sin(n_rows, n_cols, period):
    a = jnp.arange(n_rows, dtype=jnp.int32)[:, None]
    b = jnp.arange(n_cols, dtype=jnp.int32)[None, :]
    ang = ((a * b) % period).astype(F32) * (2.0 * math.pi / period)
    return jnp.cos(ang), jnp.sin(ang)


def _fourier_call(y3, n1, n2):
    b, n, w2 = y3.shape
    width = w2 // 2
    scale = 1.0 / math.sqrt(n * FOURIER_GROUP_DIM)
    c1, s1 = _cos_sin(n1, n1, n1)
    w_real = jnp.stack([c1, -s1], axis=1)
    if n2 == 1:
        mat = w_real.reshape(n1, 2 * n1).astype(BF16)
        return pl.pallas_call(
            functools.partial(_dft_kernel, width=width, scale=scale),
            out_shape=jax.ShapeDtypeStruct((b, n, width), BF16),
            grid=(b,),
            in_specs=[pl.BlockSpec((None, n, w2), lambda bi: (bi, 0, 0)),
                      pl.BlockSpec((n1, 2 * n1), lambda bi: (0, 0))],
            out_specs=pl.BlockSpec((None, n, width), lambda bi: (bi, 0, 0)),
            compiler_params=_cparams(("parallel",)),
            name="dft_ctx",
        )(y3, mat)

    g = BF16_SUBLANES
    c2, s2 = _cos_sin(n2, n2, n2)
    w_cplx = jnp.stack([jnp.stack([c2, -s2], axis=1), jnp.stack([s2, c2], axis=1)], axis=0)

    def kron_cols(base, row_j):
        rows, cols = base.shape
        col = lax.broadcasted_iota(jnp.int32, (cols, cols * g), 1)
        expand = (col // g == lax.broadcasted_iota(jnp.int32, (cols, cols * g), 0)).astype(BF16)
        wide = jnp.dot(base.astype(BF16), expand, preferred_element_type=F32)
        keep = row_j[:, None] == (lax.broadcasted_iota(jnp.int32, (rows, cols * g), 1) % g)
        return jnp.where(keep, wide, 0.0).astype(BF16)

    base1 = jnp.broadcast_to(w_cplx.reshape(2, 1, n2, 2 * n2), (2, g, n2, 2 * n2)).reshape(2 * g * n2, 2 * n2)
    mat1 = kron_cols(base1, (jnp.arange(2 * g * n2, dtype=jnp.int32) // n2) % g)
    base2 = jnp.broadcast_to(w_real.reshape(n1, 1, 2 * n1), (n1, g, 2 * n1)).reshape(n1 * g, 2 * n1)
    mat2 = kron_cols(base2, jnp.arange(n1 * g, dtype=jnp.int32) % g)
    tc, ts = _cos_sin(n1, n2, n)
    tc = jnp.broadcast_to(tc[:, :, None], (n1, n2, LANES))
    ts = jnp.broadcast_to(ts[:, :, None], (n1, n2, LANES))
    t = pl.pallas_call(
        functools.partial(_dft_kernel, width=width, scale=None),
        out_shape=jax.ShapeDtypeStruct((b, n1, n2, w2), BF16),
        grid=(b, n1 // g),
        in_specs=[pl.BlockSpec((None, n2, g, w2), lambda bi, j: (bi, 0, j, 0)),
                  pl.BlockSpec(mat1.shape, lambda bi, j: (0, 0), pipeline_mode=pl.Buffered(1)),
                  pl.BlockSpec((g, n2, LANES), lambda bi, j: (j, 0, 0)),
                  pl.BlockSpec((g, n2, LANES), lambda bi, j: (j, 0, 0))],
        out_specs=pl.BlockSpec((None, g, n2, w2), lambda bi, j: (bi, j, 0, 0)),
        compiler_params=_cparams(("parallel", "parallel")),
        name="dft_stage1",
    )(y3.reshape(b, n2, n1, w2), mat1, tc, ts)
    out = pl.pallas_call(
        functools.partial(_dft_kernel, width=width, scale=scale),
        out_shape=jax.ShapeDtypeStruct((b, n1, n2, width), BF16),
        grid=(b, n2 // g),
        in_specs=[pl.BlockSpec((None, n1, g, w2), lambda bi, j: (bi, 0, j, 0)),
                  pl.BlockSpec(mat2.shape, lambda bi, j: (0, 0), pipeline_mode=pl.Buffered(1))],
        out_specs=pl.BlockSpec((None, n1, g, width), lambda bi, j: (bi, 0, j, 0)),
        compiler_params=_cparams(("parallel", "parallel")),
        name="dft_stage2",
    )(t, mat2)
    return out.reshape(b, n, width)


def _out_kernel(a_ref, b_ref, f_ref, w_ref, x_ref, mod_ref, g_ref, x1_ref, h2_ref):
    ca = a_ref.shape[1]
    cb = b_ref.shape[1]
    acc = jnp.dot(a_ref[...], w_ref[0:ca, :], preferred_element_type=F32)
    acc += jnp.dot(b_ref[...], w_ref[ca:ca + cb, :], preferred_element_type=F32)
    acc += jnp.dot(f_ref[...], w_ref[ca + cb:, :], preferred_element_type=F32)
    m = mod_ref[...]
    x1 = x_ref[...] + m[2:3] * acc
    x1_ref[...] = x1
    h = _rms(x1, g_ref[...])
    h2_ref[...] = (h * (1.0 + m[4:5]) + m[3:4]).astype(BF16)


def _out_call(att_a, att_b, four, w_out_b, x2d, mod_l, cond_row_fn, norm_g, tm):
    t, d = x2d.shape
    ca, cb, cf = att_a.shape[1], att_b.shape[1], four.shape[1]
    w_out_b, w_layer = w_out_b
    return pl.pallas_call(
        _out_kernel,
        out_shape=[jax.ShapeDtypeStruct((t, d), F32), jax.ShapeDtypeStruct((t, d), BF16)],
        grid=(t // tm,),
        in_specs=[
            pl.BlockSpec((tm, ca), lambda i: (i, 0)),
            pl.BlockSpec((tm, cb), lambda i: (i, 0)),
            pl.BlockSpec((tm, cf), lambda i: (i, 0)),
            pl.BlockSpec((None, ca + cb + cf, d), lambda i: (w_layer, 0, 0), pipeline_mode=pl.Buffered(1)),
            pl.BlockSpec((tm, d), lambda i: (i, 0)),
            pl.BlockSpec((None, N_MOD, d), lambda i: (cond_row_fn(i), 0, 0)),
            pl.BlockSpec((1, d), lambda i: (0, 0)),
        ],
        out_specs=[pl.BlockSpec((tm, d), lambda i: (i, 0)), pl.BlockSpec((tm, d), lambda i: (i, 0))],
        compiler_params=_cparams(("parallel",)),
        name="out_proj",
    )(att_a, att_b, four, w_out_b, x2d, mod_l, norm_g)


HALO = 16


def _ffn_kernel(h_ref, hp_ref, hn_ref, wg_ref, wu_ref, cw_ref, cb_ref, wd_ref, x1_ref, mod_ref, gf_ref,
                o_ref, hext_ref, *, tm, seq_len, final):
    i = pl.program_id(0)
    j = pl.program_id(1)

    @pl.when(j == 0)
    def _():
        hext_ref[0:HALO, :] = hp_ref[...]
        hext_ref[HALO:HALO + tm, :] = h_ref[...]
        hext_ref[HALO + tm:, :] = hn_ref[...]
        o_ref[...] = jnp.zeros_like(o_ref)

    g = jnp.dot(hext_ref[...], wg_ref[...], preferred_element_type=F32)
    u = jnp.dot(h_ref[...], wu_ref[...], preferred_element_type=F32)
    ext = tm + 2 * HALO
    pos = (i * tm + lax.broadcasted_iota(jnp.int32, (tm, 1), 0)) % seq_len
    g_prev = jnp.where(pos == 0, 0.0, pltpu.roll(g, 1, 0)[HALO:HALO + tm])
    g_next = jnp.where(pos == seq_len - 1, 0.0, pltpu.roll(g, ext - 1, 0)[HALO:HALO + tm])
    cw = cw_ref[...]
    gc = g_prev * cw[0:1] + g[HALO:HALO + tm] * cw[1:2] + g_next * cw[2:3] + cb_ref[...]
    act = (gc * jax.nn.sigmoid(gc)) * u
    o_ref[...] += jnp.dot(act.astype(BF16), wd_ref[...], preferred_element_type=F32)

    @pl.when(j == pl.num_programs(1) - 1)
    def _():
        m = mod_ref[...]
        x2 = x1_ref[...] + m[5:6] * o_ref[...]
        if final:
            x2 = _rms(x2, gf_ref[...])
        o_ref[...] = x2


def _ffn_call(h2, x1, weights, conv_w, conv_b, mod_l, cond_row_fn, final_g, seq_len, final, tm, tf):
    t, d = x1.shape
    w_gate_b, w_up_b, w_down_b, w_layer = weights
    f = w_gate_b.shape[2]
    hb = tm // HALO
    last = t // HALO - 1
    return pl.pallas_call(
        functools.partial(_ffn_kernel, tm=tm, seq_len=seq_len, final=final),
        out_shape=jax.ShapeDtypeStruct((t, d), F32),
        grid=(t // tm, f // tf),
        in_specs=[
            pl.BlockSpec((tm, d), lambda i, j: (i, 0)),
            pl.BlockSpec((HALO, d), lambda i, j: (jnp.maximum(i * hb - 1, 0), 0)),
            pl.BlockSpec((HALO, d), lambda i, j: (jnp.minimum((i + 1) * hb, last), 0)),
            pl.BlockSpec((None, d, tf), lambda i, j: (w_layer, 0, j)),
            pl.BlockSpec((None, d, tf), lambda i, j: (w_layer, 0, j)),
            pl.BlockSpec((3, tf), lambda i, j: (0, j)),
            pl.BlockSpec((1, tf), lambda i, j: (0, j)),
            pl.BlockSpec((None, tf, d), lambda i, j: (w_layer, j, 0)),
            pl.BlockSpec((tm, d), lambda i, j: (i, 0)),
            pl.BlockSpec((None, N_MOD, d), lambda i, j: (cond_row_fn(i), 0, 0)),
            pl.BlockSpec((1, d), lambda i, j: (0, 0)),
        ],
        out_specs=pl.BlockSpec((tm, d), lambda i, j: (i, 0)),
        scratch_shapes=[pltpu.VMEM((tm + 2 * HALO, d), BF16)],
        compiler_params=_cparams(("parallel", "arbitrary")),
        name="conv_ffn",
    )(h2, h2, h2, w_gate_b, w_up_b, conv_w, conv_b, w_down_b, x1, mod_l, final_g)


def _rope_tables(n, head_dim):
    rows = n // GRID_W
    t_row = jnp.repeat(jnp.arange(rows, dtype=F32), GRID_W)
    t_col = jnp.tile(jnp.arange(GRID_W, dtype=F32), rows)
    axis_dim = head_dim // 2
    inv = jnp.power(ROPE_BASE, -jnp.arange(0, axis_dim, 2, dtype=F32) / axis_dim)
    ar = t_row[:, None] * inv[None, :]
    ac = t_col[:, None] * inv[None, :]
    ang = jnp.concatenate([ar, ar, ac, ac], axis=-1)
    reps = LANES // head_dim
    quarter = head_dim // 4
    sign = jnp.where((jnp.arange(head_dim) % (2 * quarter)) < quarter, -1.0, 1.0).astype(F32)
    cos = jnp.tile(jnp.cos(ang), (1, reps))
    sin_signed = jnp.tile(jnp.sin(ang) * sign[None, :], (1, reps))
    return cos, sin_signed


def kernel(x_prompt, x_sample, cache_attn_k, cache_attn_v, cache_diff_k, cache_diff_v, c, c_ctx, norm1_g, norm2_g, w_ada, b_ada, w_in, attn_q_norm_g, attn_k_norm_g, diff_lambda_q1, diff_lambda_k1, diff_lambda_q2, diff_lambda_k2, diff_subnorm_g, w_out, ffn_w_gate, ffn_w_up, ffn_conv_w, ffn_conv_b, ffn_w_down, final_norm_g):
    depth = w_in.shape[0]
    bc, lc, d = x_prompt.shape
    bl, ll, _ = x_sample.shape
    past = cache_attn_k.shape[2]

    w_in_b = w_in.astype(BF16)
    w_out_b = w_out.astype(BF16)
    w_gate_b = ffn_w_gate.astype(BF16)
    w_up_b = ffn_w_up.astype(BF16)
    w_down_b = ffn_w_down.astype(BF16)

    n_rows = 8 * ((1 + bl + 7) // 8)
    cvec = jnp.concatenate([c_ctx[None, :], c, jnp.zeros((n_rows - 1 - bl, d), F32)], axis=0)
    mod = _ada_call(cvec, w_ada, b_ada).reshape(depth, n_rows, N_MOD, d)

    dft_c = jnp.concatenate(_cos_sin(FOURIER_GROUP_DIM, FOURIER_GROUP_DIM, FOURIER_GROUP_DIM), axis=1).astype(BF16)
    rope_tabs = _rope_tables(ll, HEAD_DIM_A) + _rope_tables(ll, DK_B)
    caches = (cache_attn_k.reshape(bl, depth, past, N_KV_HEADS_A * HEAD_DIM_A),
              cache_attn_v.reshape(bl, depth, past, N_KV_HEADS_A * HEAD_DIM_A),
              cache_diff_k.reshape(bl, depth, past, N_HEADS_B * 2 * DK_B),
              cache_diff_v.reshape(bl, depth, past, N_HEADS_B * DV_B))
    lam_all = jnp.stack([diff_lambda_q1, diff_lambda_k1, diff_lambda_q2, diff_lambda_k2], axis=1)

    def run_pass(x3, is_ctx):
        b, n, _ = x3.shape
        t = b * n
        tm = 512
        x = x3.reshape(t, d)
        if is_ctx:
            cond_row_fn = lambda i: 0
        else:
            cond_row_fn = lambda i: 1 + (i * tm) // n
        new_kv = None
        for l in range(depth):
            lam_init = 0.8 - 0.6 * math.exp(-0.3 * l)
            res = _in_call(x, mod[l], cond_row_fn, norm1_g[l][None, :], (w_in_b, l),
                           attn_q_norm_g[l][None, :], attn_k_norm_g[l][None, :], dft_c,
                           None if is_ctx else rope_tabs, n, (depth, l, new_kv) if is_ctx else None, tm)
            qkv, y12 = res[0], res[1]
            if is_ctx:
                new_kv = res[2:]
            qkv3 = qkv.reshape(b, n, qkv.shape[1])
            att_a, att_b = _attn_calls(qkv3, None if is_ctx else caches, l, lam_all[l],
                                       diff_subnorm_g[l][None, :], lam_init,
                                       tq_a=min(256, n), tq_b=min(512, n), kc=1024,
                                       ha=N_KV_HEADS_A if is_ctx else 1, hb=N_HEADS_B if is_ctx else 1)
            if is_ctx:
                four = _fourier_call(y12.reshape(b, n, y12.shape[1]), n, 1)
            else:
                four = _fourier_call(y12.reshape(b, n, y12.shape[1]), GRID_W, n // GRID_W)
            x1, h2 = _out_call(att_a.reshape(t, -1), att_b.reshape(t, -1), four.reshape(t, -1), (w_out_b, l),
                               x, mod[l], cond_row_fn, norm2_g[l][None, :], tm)
            x = _ffn_call(h2, x1, (w_gate_b, w_up_b, w_down_b, l), ffn_conv_w[l], ffn_conv_b[l][None, :],
                          mod[l], cond_row_fn, final_norm_g[None, :], n, l == depth - 1, tm, 512)
        return x.reshape(b, n, d), new_kv

    y_prompt, kvs = run_pass(x_prompt, True)
    new_attn_k = kvs[0].reshape(bc, depth, lc, N_KV_HEADS_A, HEAD_DIM_A)
    new_attn_v = kvs[1].reshape(bc, depth, lc, N_KV_HEADS_A, HEAD_DIM_A)
    new_diff_k = kvs[2].reshape(bc, depth, lc, N_HEADS_B, 2 * DK_B)
    new_diff_v = kvs[3].reshape(bc, depth, lc, N_HEADS_B, DV_B)

    y_sample, _ = run_pass(x_sample, False)
    return (y_prompt, y_sample, new_attn_k, new_attn_v, new_diff_k, new_diff_v)
```

```python
import functools
import math

import jax
import jax.numpy as jnp
from jax import lax
from jax.experimental import pallas as pl
from jax.experimental.pallas import tpu as pltpu

F32 = jnp.float32
BF16 = jnp.bfloat16

GRID_W = 64
ROPE_BASE = 10000.0
NORM_EPS = 1e-6
HEAD_DIM_A = 128
N_KV_HEADS_A = 2
GQA_GROUP = 4
N_HEADS_B = 4
DK_B = 64
DV_B = 128
N_FOURIER_GROUPS = 4
FOURIER_GROUP_DIM = 128
N_MOD = 6

LOG2E = math.log2(math.e)
LANES = 128
BF16_SUBLANES = 16
VMEM_LIMIT = 56 * 1024 * 1024


def _cparams(sem, flags=None):
    return pltpu.CompilerParams(dimension_semantics=sem, vmem_limit_bytes=VMEM_LIMIT, flags=flags)


_ATTN_FLAGS = None


def _rms(x, g):
    return x * lax.rsqrt(jnp.mean(x * x, axis=-1, keepdims=True) + NORM_EPS) * g


def _ada_kernel(c_ref, w_ref, b_ref, o_ref):
    c = c_ref[...]
    s = (c * jax.nn.sigmoid(c)).astype(BF16)
    w = w_ref[...].astype(BF16)
    o_ref[...] = jnp.dot(s, w, preferred_element_type=F32) + b_ref[...]


def _ada_call(cvec, w_ada, b_ada):
    depth, d, n = w_ada.shape
    rows = cvec.shape[0]
    tn = 512
    return pl.pallas_call(
        _ada_kernel,
        out_shape=jax.ShapeDtypeStruct((depth, rows, n), F32),
        grid=(depth, n // tn),
        in_specs=[
            pl.BlockSpec((rows, d), lambda l, j: (0, 0)),
            pl.BlockSpec((None, d, tn), lambda l, j: (l, 0, j)),
            pl.BlockSpec((None, 1, tn), lambda l, j: (l, 0, j)),
        ],
        out_specs=pl.BlockSpec((None, rows, tn), lambda l, j: (l, 0, j)),
        compiler_params=_cparams(("parallel", "parallel")),
        name="ada_mod",
    )(cvec, w_ada, b_ada.reshape(depth, 1, n))


def _rope(xs, cos, sin_signed, shift):
    w = xs.shape[-1]
    lane = lax.broadcasted_iota(jnp.int32, xs.shape, 1)
    first = (lane % (2 * shift)) < shift
    rot = jnp.where(first, pltpu.roll(xs, w - shift, 1), pltpu.roll(xs, shift, 1))
    return xs * cos + rot * sin_signed


def _in_kernel(*refs, rope, ctx_out, n_alias, cols):
    it = iter(refs)
    x_ref, mod_ref, g_ref, w_ref, gq_ref, gk_ref, dft_ref = (next(it) for _ in range(7))
    if rope:
        cos_a, sin_a, cos_b, sin_b = (next(it)[...] for _ in range(4))
    for _ in range(n_alias):
        next(it)
    qkv_ref, y_ref = next(it), next(it)
    if ctx_out:
        ka_ref, va_ref, kb_ref, vb_ref = (next(it) for _ in range(4))
    c_qa, c_ka, c_va, c_qb, c_kb, c_vb, c_f, c_end = cols

    def put_cache(ref, off, v):
        head = off // LANES
        seqs, rows, _ = ref.shape
        heads = rows * seqs // v.shape[0]
        n = rows // heads
        for s in range(seqs):
            ref[s, pl.ds(head, n, stride=heads), :] = v[s * n:(s + 1) * n, :]

    m = mod_ref[...]
    h = _rms(x_ref[...], g_ref[...])
    hb = (h * (1.0 + m[1:2]) + m[0:1]).astype(BF16)
    gq = gq_ref[...]
    gk = gk_ref[...]
    scale_a = HEAD_DIM_A ** -0.5 * LOG2E
    scale_b = DK_B ** -0.5 * LOG2E
    chunk = 4 * LANES

    for c0 in range(0, c_end, chunk):
        acc = jnp.dot(hb, w_ref[:, c0:c0 + chunk], preferred_element_type=F32)
        for s in range(chunk // LANES):
            col = c0 + s * LANES
            v = acc[:, s * LANES:(s + 1) * LANES]
            if col < c_ka:
                v = _rms(v, gq)
                if rope:
                    v = _rope(v, cos_a, sin_a, HEAD_DIM_A // 4)
                qkv_ref[:, col:col + LANES] = (v * scale_a).astype(BF16)
            elif col < c_va:
                v = _rms(v, gk)
                if ctx_out:
                    put_cache(ka_ref, col - c_ka, v)
                if rope:
                    v = _rope(v, cos_a, sin_a, HEAD_DIM_A // 4)
                qkv_ref[:, col:col + LANES] = v.astype(BF16)
            elif col < c_qb:
                if ctx_out:
                    put_cache(va_ref, col - c_va, v)
                qkv_ref[:, col:col + LANES] = v.astype(BF16)
            elif col < c_kb:
                if rope:
                    v = _rope(v, cos_b, sin_b, DK_B // 4)
                qkv_ref[:, col:col + LANES] = (v * scale_b).astype(BF16)
            elif col < c_vb:
                if ctx_out:
                    put_cache(kb_ref, col - c_kb, v)
                if rope:
                    v = _rope(v, cos_b, sin_b, DK_B // 4)
                qkv_ref[:, col:col + LANES] = v.astype(BF16)
            elif col < c_f:
                if ctx_out:
                    put_cache(vb_ref, col - c_vb, v)
                qkv_ref[:, col:col + LANES] = v.astype(BF16)
            else:
                yy = jnp.dot(v.astype(BF16), dft_ref[...], preferred_element_type=F32)
                gcol = col - c_f
                half = c_end - c_f
                y_ref[:, gcol:gcol + LANES] = yy[:, :LANES].astype(BF16)
                y_ref[:, half + gcol:half + gcol + LANES] = yy[:, LANES:].astype(BF16)


def _in_call(x2d, mod_l, cond_row_fn, norm_g, w_in_b, gq, gk, dft_c, rope_tabs, seq_len, new_cache, tm):
    ctx_out = new_cache is not None
    t, d = x2d.shape
    w_in_b, w_layer = w_in_b
    d_in = w_in_b.shape[2]
    c_qa = 0
    c_ka = N_KV_HEADS_A * GQA_GROUP * HEAD_DIM_A
    c_va = c_ka + N_KV_HEADS_A * HEAD_DIM_A
    c_qb = c_va + N_KV_HEADS_A * HEAD_DIM_A
    c_kb = c_qb + N_HEADS_B * 2 * DK_B
    c_vb = c_kb + N_HEADS_B * 2 * DK_B
    c_f = c_vb + N_HEADS_B * DV_B
    c_end = c_f + N_FOURIER_GROUPS * FOURIER_GROUP_DIM
    assert c_end == d_in
    cols = (c_qa, c_ka, c_va, c_qb, c_kb, c_vb, c_f, c_end)
    rope = rope_tabs is not None
    n_f = c_end - c_f

    in_specs = [
        pl.BlockSpec((tm, d), lambda i: (i, 0)),
        pl.BlockSpec((None, N_MOD, d), lambda i: (cond_row_fn(i), 0, 0)),
        pl.BlockSpec((1, d), lambda i: (0, 0)),
        pl.BlockSpec((None, d, d_in), lambda i: (w_layer, 0, 0), pipeline_mode=pl.Buffered(1)),
        pl.BlockSpec((1, HEAD_DIM_A), lambda i: (0, 0)),
        pl.BlockSpec((1, HEAD_DIM_A), lambda i: (0, 0)),
        pl.BlockSpec((FOURIER_GROUP_DIM, 2 * FOURIER_GROUP_DIM), lambda i: (0, 0)),
    ]
    args = [x2d, mod_l, norm_g, w_in_b, gq, gk, dft_c]
    if rope:
        nblk = seq_len // tm
        for tab in rope_tabs:
            in_specs.append(pl.BlockSpec((tm, LANES), lambda i: (i % nblk, 0)))
            args.append(tab)
    out_shape = [jax.ShapeDtypeStruct((t, c_f), BF16), jax.ShapeDtypeStruct((t, 2 * n_f), BF16)]
    out_specs = [pl.BlockSpec((tm, c_f), lambda i: (i, 0)), pl.BlockSpec((tm, 2 * n_f), lambda i: (i, 0))]
    aliases = {}
    n_alias = 0
    if ctx_out:
        depth, layer, prev = new_cache
        nb = t // seq_len
        spt = tm // seq_len
        for width in (c_va - c_ka, c_qb - c_va, c_vb - c_kb, c_f - c_vb):
            rows = seq_len * (width // LANES)
            out_shape.append(jax.ShapeDtypeStruct((nb, depth, rows, LANES), F32))
            out_specs.append(pl.BlockSpec((spt, None, rows, LANES), lambda i: (i, layer, 0, 0)))
        if prev is not None:
            n_alias = len(prev)
            for a, arr in enumerate(prev):
                aliases[len(args)] = 2 + a
                in_specs.append(pl.BlockSpec(memory_space=pl.ANY))
                args.append(arr)
    return pl.pallas_call(
        functools.partial(_in_kernel, rope=rope, ctx_out=ctx_out, n_alias=n_alias, cols=cols),
        out_shape=out_shape,
        grid=(t // tm,),
        in_specs=in_specs,
        out_specs=out_specs,
        input_output_aliases=aliases,
        compiler_params=_cparams(("parallel",)),
        name="in_proj_ctx" if ctx_out else "in_proj_lat",
    )(*args)


def _softmax_pv(q, srcs):
    chunks = []
    for k_ref, v_ref, n_keys, kc, col in srcs:
        for c in range(n_keys // kc):
            chunks.append((k_ref, v_ref, c * kc, kc, col))

    def scores(ch):
        k_ref, _, off, kc, col = ch
        k = k_ref[off:off + kc, col:col + LANES].astype(BF16)
        return lax.dot_general(q, k, (((1,), (1,)), ((), ())), preferred_element_type=F32)

    def lane_fold(x, op):
        out = x[:, 0:LANES]
        for t in range(1, x.shape[1] // LANES):
            out = op(out, x[:, t * LANES:(t + 1) * LANES])
        return out

    m = l_part = acc = None
    s_next = scores(chunks[0])
    for i, ch in enumerate(chunks):
        s = s_next
        if i + 1 < len(chunks):
            s_next = scores(chunks[i + 1])
        _, v_ref, off, kc, col = ch
        row_max = jnp.max(lane_fold(s, jnp.maximum), axis=-1, keepdims=True)
        m_new = row_max if m is None else jnp.maximum(m, row_max)
        p = jnp.exp2(s - m_new)
        p_sum = lane_fold(p, jnp.add)
        v = v_ref[off:off + kc, col:col + LANES].astype(BF16)
        pv = jnp.dot(p.astype(BF16), v, preferred_element_type=F32)
        if m is None:
            l_part, acc = p_sum, pv
        else:
            alpha = jnp.exp2(m - m_new)
            l_part = alpha * l_part + p_sum
            acc = alpha * acc + pv
        m = m_new
    return acc / jnp.sum(l_part, axis=-1, keepdims=True)


def _srcs(cache_refs, k_ref, v_ref, kc, head):
    srcs = []
    col = head * LANES
    if cache_refs is not None:
        ck, cv = cache_refs
        srcs.append((ck, cv, ck.shape[0], ck.shape[0], col))
    n = k_ref.shape[0]
    srcs.append((k_ref, v_ref, n, min(kc, n), col))
    return srcs


def _attn_a_kernel(*refs, has_cache, tq, kc, heads, split):
    if has_cache:
        q_ref, ck_ref, cv_ref, k_ref, v_ref, o_ref = refs
        cache = (ck_ref, cv_ref)
    else:
        q_ref, k_ref, v_ref, o_ref = refs
        cache = None
    per = GQA_GROUP // split
    for kh in range(heads):
        for part in range(split):
            cols = [(kh * GQA_GROUP + part * per + h) * LANES for h in range(per)]
            qs = jnp.concatenate([q_ref[:, c:c + LANES] for c in cols], axis=0)
            o = _softmax_pv(qs, _srcs(cache, k_ref, v_ref, kc, kh))
            for h, c in enumerate(cols):
                o_ref[:, c:c + LANES] = o[h * tq:(h + 1) * tq].astype(BF16)


def _attn_b_kernel(*refs, has_cache, tq, kc, heads, split, lam_init):
    if has_cache:
        lam_ref, gs_ref, q_ref, ck_ref, cv_ref, k_ref, v_ref, o_ref = refs
        cache = (ck_ref, cv_ref)
    else:
        lam_ref, gs_ref, q_ref, k_ref, v_ref, o_ref = refs
        cache = None
    lp = lam_ref[...]
    lam = (jnp.exp(jnp.sum(lp[0:1] * lp[1:2], axis=-1, keepdims=True))
           - jnp.exp(jnp.sum(lp[2:3] * lp[3:4], axis=-1, keepdims=True)) + lam_init)
    rows = tq // split
    for hd in range(heads):
        for part in range(split):
            q = q_ref[part * rows:(part + 1) * rows, hd * LANES:(hd + 1) * LANES]
            lane = lax.broadcasted_iota(jnp.int32, q.shape, 1)
            zero = jnp.zeros_like(q)
            qz = jnp.concatenate([jnp.where(lane < DK_B, q, zero), jnp.where(lane >= DK_B, q, zero)], axis=0)
            o = _softmax_pv(qz, _srcs(cache, k_ref, v_ref, kc, hd))
            dlt = o[:rows] - lam * o[rows:]
            o_ref[part * rows:(part + 1) * rows, hd * LANES:(hd + 1) * LANES] = (
                _rms(dlt, gs_ref[...]) * (1.0 - lam_init)).astype(BF16)


def _attn_calls(qkv3, caches, layer, lam_params, g_sub, lam_init, tq_a, tq_b, kc, ha, hb, split_a, split_b):
    b, n, _ = qkv3.shape
    has_cache = caches is not None
    qa_blk =GQA_GROUP * HEAD_DIM_A // LANES
    k_a0 = N_KV_HEADS_A * qa_blk
    v_a0 = k_a0 + N_KV_HEADS_A
    q_b0 = v_a0 + N_KV_HEADS_A
    k_b0 = q_b0 + N_HEADS_B
    v_b0 = k_b0 + N_HEADS_B
    assert all(x % ha == 0 for x in (N_KV_HEADS_A, k_a0, v_a0)) and all(x % hb == 0 for x in (N_HEADS_B, q_b0, k_b0, v_b0))

    qw, kw = ha * GQA_GROUP * LANES, ha * LANES
    in_specs = [pl.BlockSpec((None, tq_a, qw), lambda bi, h, i: (bi, i, h))]
    args = [qkv3]
    if has_cache:
        ck, cv = caches[0], caches[1]
        p = ck.shape[2]
        in_specs += [pl.BlockSpec((None, None, p, kw), lambda bi, h, i: (bi, layer, 0, h))] * 2
        args += [ck, cv]
    in_specs += [pl.BlockSpec((None, n, kw), lambda bi, h, i: (bi, 0, k_a0 // ha + h)),
                 pl.BlockSpec((None, n, kw), lambda bi, h, i: (bi, 0, v_a0 // ha + h))]
    args += [qkv3, qkv3]
    att_a = pl.pallas_call(
        functools.partial(_attn_a_kernel, has_cache=has_cache, tq=tq_a, kc=kc, heads=ha, split=split_a),
        out_shape=jax.ShapeDtypeStruct((b, n, N_KV_HEADS_A * GQA_GROUP * HEAD_DIM_A), BF16),
        grid=(b, N_KV_HEADS_A // ha, n // tq_a),
        in_specs=in_specs,
        out_specs=pl.BlockSpec((None, tq_a, qw), lambda bi, h, i: (bi, i, h)),
        compiler_params=_cparams(("parallel", "parallel", "arbitrary"), _ATTN_FLAGS),
        name="attn_a_lat" if has_cache else "attn_a_ctx",
    )(*args)

    bw = hb * LANES
    in_specs = [pl.BlockSpec((4, DK_B), lambda bi, h, i: (0, 0)),
                pl.BlockSpec((1, DV_B), lambda bi, h, i: (0, 0)),
                pl.BlockSpec((None, tq_b, bw), lambda bi, h, i: (bi, i, q_b0 // hb + h))]
    args = [lam_params, g_sub, qkv3]
    if has_cache:
        ck, cv = caches[2], caches[3]
        p = ck.shape[2]
        in_specs += [pl.BlockSpec((None, None, p, bw), lambda bi, h, i: (bi, layer, 0, h))] * 2
        args += [ck, cv]
    in_specs += [pl.BlockSpec((None, n, bw), lambda bi, h, i: (bi, 0, k_b0 // hb + h)),
                 pl.BlockSpec((None, n, bw), lambda bi, h, i: (bi, 0, v_b0 // hb + h))]
    args += [qkv3, qkv3]
    att_b = pl.pallas_call(
        functools.partial(_attn_b_kernel, has_cache=has_cache, tq=tq_b, kc=kc, heads=hb, split=split_b, lam_init=lam_init),
        out_shape=jax.ShapeDtypeStruct((b, n, N_HEADS_B * DV_B), BF16),
        grid=(b, N_HEADS_B // hb, n // tq_b),
        in_specs=in_specs,
        out_specs=pl.BlockSpec((None, tq_b, bw), lambda bi, h, i: (bi, i, h)),
        compiler_params=_cparams(("parallel", "parallel", "arbitrary"), _ATTN_FLAGS),
        name="attn_b_lat" if has_cache else "attn_b_ctx",
    )(*args)
    return att_a, att_b


def _dft_kernel(x_ref, mat_ref, *rest, width, scale):
    x = x_ref[...]
    if x.ndim == 3:
        x = x.reshape(x.shape[0] * x.shape[1], x.shape[2])
    xs = jnp.concatenate([x[:, :width], x[:, width:]], axis=0)
    u = jnp.dot(mat_ref[...], xs, preferred_element_type=F32)
    if len(rest) == 3:
        tc_ref, ts_ref, o_ref = rest
        half = u.shape[0] // 2
        ur, ui = u[:half], u[half:]
        reps = width // LANES
        tc = jnp.concatenate([tc_ref[...].reshape(half, LANES)] * reps, axis=1)
        ts = jnp.concatenate([ts_ref[...].reshape(half, LANES)] * reps, axis=1)
        o_ref[:, :, :width] = (ur * tc - ui * ts).astype(BF16).reshape(o_ref.shape[0], o_ref.shape[1], width)
        o_ref[:, :, width:] = (ur * ts + ui * tc).astype(BF16).reshape(o_ref.shape[0], o_ref.shape[1], width)
    else:
        (o_ref,) = rest
        o_ref[...] = (u * scale).astype(BF16).reshape(o_ref.shape)


def _cos_sin(n_rows, n_cols, period):
    a = jnp.arange(n_rows, dtype=jnp.int32)[:, None]
    b = jnp.arange(n_cols, dtype=jnp.int32)[None, :]
    ang = ((a * b) % period).astype(F32) * (2.0 * math.pi / period)
    return jnp.cos(ang), jnp.sin(ang)


def _fourier_call(y3, n1, n2):
    b, n, w2 = y3.shape
    width = w2 // 2
    scale = 1.0 / math.sqrt(n * FOURIER_GROUP_DIM)
    c1, s1 = _cos_sin(n1, n1, n1)
    w_real = jnp.stack([c1, -s1], axis=1)
    if n2 == 1:
        mat = w_real.reshape(n1, 2 * n1).astype(BF16)
        return pl.pallas_call(
            functools.partial(_dft_kernel, width=width, scale=scale),
            out_shape=jax.ShapeDtypeStruct((b, n, width), BF16),
            grid=(b,),
            in_specs=[pl.BlockSpec((None, n, w2), lambda bi: (bi, 0, 0)),
                      pl.BlockSpec((n1, 2 * n1), lambda bi: (0, 0))],
            out_specs=pl.BlockSpec((None, n, width), lambda bi: (bi, 0, 0)),
            compiler_params=_cparams(("parallel",)),
            name="dft_ctx",
        )(y3, mat)

    g = BF16_SUBLANES
    c2, s2 = _cos_sin(n2, n2, n2)
    w_cplx = jnp.stack([jnp.stack([c2, -s2], axis=1), jnp.stack([s2, c2], axis=1)], axis=0)

    def kron_cols(base, row_j):
        rows, cols = base.shape
        col = lax.broadcasted_iota(jnp.int32, (cols, cols * g), 1)
        expand = (col // g == lax.broadcasted_iota(jnp.int32, (cols, cols * g), 0)).astype(BF16)
        wide = jnp.dot(base.astype(BF16), expand, preferred_element_type=F32)
        keep = row_j[:, None] == (lax.broadcasted_iota(jnp.int32, (rows, cols * g), 1) % g)
        return jnp.where(keep, wide, 0.0).astype(BF16)

    base1 = jnp.broadcast_to(w_cplx.reshape(2, 1, n2, 2 * n2), (2, g, n2, 2 * n2)).reshape(2 * g * n2, 2 * n2)
    mat1 = kron_cols(base1, (jnp.arange(2 * g * n2, dtype=jnp.int32) // n2) % g)
    base2 = jnp.broadcast_to(w_real.reshape(n1, 1, 2 * n1), (n1, g, 2 * n1)).reshape(n1 * g, 2 * n1)
    mat2 = kron_cols(base2, jnp.arange(n1 * g, dtype=jnp.int32) % g)
    tc, ts = _cos_sin(n1, n2, n)
    tc = jnp.broadcast_to(tc[:, :, None], (n1, n2, LANES))
    ts = jnp.broadcast_to(ts[:, :, None], (n1, n2, LANES))
    t = pl.pallas_call(
        functools.partial(_dft_kernel, width=width, scale=None),
        out_shape=jax.ShapeDtypeStruct((b, n1, n2, w2), BF16),
        grid=(b, n1 // g),
        in_specs=[pl.BlockSpec((None, n2, g, w2), lambda bi, j: (bi, 0, j, 0)),
                  pl.BlockSpec(mat1.shape, lambda bi, j: (0, 0), pipeline_mode=pl.Buffered(1)),
                  pl.BlockSpec((g, n2, LANES), lambda bi, j: (j, 0, 0)),
                  pl.BlockSpec((g, n2, LANES), lambda bi, j: (j, 0, 0))],
        out_specs=pl.BlockSpec((None, g, n2, w2), lambda bi, j: (bi, j, 0, 0)),
        compiler_params=_cparams(("parallel", "parallel")),
        name="dft_stage1",
    )(y3.reshape(b, n2, n1, w2), mat1, tc, ts)
    out = pl.pallas_call(
        functools.partial(_dft_kernel, width=width, scale=scale),
        out_shape=jax.ShapeDtypeStruct((b, n1, n2, width), BF16),
        grid=(b, n2 // g),
        in_specs=[pl.BlockSpec((None, n1, g, w2), lambda bi, j: (bi, 0, j, 0)),
                  pl.BlockSpec(mat2.shape, lambda bi, j: (0, 0), pipeline_mode=pl.Buffered(1))],
        out_specs=pl.BlockSpec((None, n1, g, width), lambda bi, j: (bi, 0, j, 0)),
        compiler_params=_cparams(("parallel", "parallel")),
        name="dft_stage2",
    )(t, mat2)
    return out.reshape(b, n, width)


def _out_kernel(a_ref, b_ref, f_ref, w_ref, x_ref, mod_ref, g_ref, x1_ref, h2_ref):
    ca = a_ref.shape[1]
    cb = b_ref.shape[1]
    acc = jnp.dot(a_ref[...], w_ref[0:ca, :], preferred_element_type=F32)
    acc += jnp.dot(b_ref[...], w_ref[ca:ca + cb, :], preferred_element_type=F32)
    acc += jnp.dot(f_ref[...], w_ref[ca + cb:, :], preferred_element_type=F32)
    m = mod_ref[...]
    x1 = x_ref[...] + m[2:3] * acc
    x1_ref[...] = x1
    h = _rms(x1, g_ref[...])
    h2_ref[...] = (h * (1.0 + m[4:5]) + m[3:4]).astype(BF16)


def _out_call(att_a, att_b, four, w_out_b, x2d, mod_l, cond_row_fn, norm_g, tm):
    t, d = x2d.shape
    ca, cb, cf = att_a.shape[1], att_b.shape[1], four.shape[1]
    w_out_b, w_layer = w_out_b
    return pl.pallas_call(
        _out_kernel,
        out_shape=[jax.ShapeDtypeStruct((t, d), F32), jax.ShapeDtypeStruct((t, d), BF16)],
        grid=(t // tm,),
        in_specs=[
            pl.BlockSpec((tm, ca), lambda i: (i, 0)),
            pl.BlockSpec((tm, cb), lambda i: (i, 0)),
            pl.BlockSpec((tm, cf), lambda i: (i, 0)),
            pl.BlockSpec((None, ca + cb + cf, d), lambda i: (w_layer, 0, 0), pipeline_mode=pl.Buffered(1)),
            pl.BlockSpec((tm, d), lambda i: (i, 0)),
            pl.BlockSpec((None, N_MOD, d), lambda i: (cond_row_fn(i), 0, 0)),
            pl.BlockSpec((1, d), lambda i: (0, 0)),
        ],
        out_specs=[pl.BlockSpec((tm, d), lambda i: (i, 0)), pl.BlockSpec((tm, d), lambda i: (i, 0))],
        compiler_params=_cparams(("parallel",)),
        name="out_proj",
    )(att_a, att_b, four, w_out_b, x2d, mod_l, norm_g)


HALO = 16


def _ffn_kernel(h_ref, hp_ref, hn_ref, wg_ref, wu_ref, cw_ref, cb_ref, wd_ref, x1_ref, mod_ref, gf_ref,
                o_ref, hext_ref, *, tm, seq_len, final):
    i = pl.program_id(0)
    j = pl.program_id(1)

    @pl.when(j == 0)
    def _():
        hext_ref[0:HALO, :] = hp_ref[...]
        hext_ref[HALO:HALO + tm, :] = h_ref[...]
        hext_ref[HALO + tm:, :] = hn_ref[...]
        o_ref[...] = jnp.zeros_like(o_ref)

    g = jnp.dot(hext_ref[...], wg_ref[...], preferred_element_type=F32)
    u = jnp.dot(h_ref[...], wu_ref[...], preferred_element_type=F32)
    ext = tm + 2 * HALO
    pos = (i * tm + lax.broadcasted_iota(jnp.int32, (tm, 1), 0)) % seq_len
    g_prev = jnp.where(pos == 0, 0.0, pltpu.roll(g, 1, 0)[HALO:HALO + tm])
    g_next = jnp.where(pos == seq_len - 1, 0.0, pltpu.roll(g, ext - 1, 0)[HALO:HALO + tm])
    cw = cw_ref[...]
    gc = g_prev * cw[0:1] + g[HALO:HALO + tm] * cw[1:2] + g_next * cw[2:3] + cb_ref[...]
    act = (gc * jax.nn.sigmoid(gc)) * u
    o_ref[...] += jnp.dot(act.astype(BF16), wd_ref[...], preferred_element_type=F32)

    @pl.when(j == pl.num_programs(1) - 1)
    def _():
        m = mod_ref[...]
        x2 = x1_ref[...] + m[5:6] * o_ref[...]
        if final:
            x2 = _rms(x2, gf_ref[...])
        o_ref[...] = x2


def _ffn_call(h2, x1, weights, conv_w, conv_b, mod_l, cond_row_fn, final_g, seq_len, final, tm, tf):
    t, d = x1.shape
    w_gate_b, w_up_b, w_down_b, w_layer = weights
    f = w_gate_b.shape[2]
    hb = tm // HALO
    last = t // HALO - 1
    return pl.pallas_call(
        functools.partial(_ffn_kernel, tm=tm, seq_len=seq_len, final=final),
        out_shape=jax.ShapeDtypeStruct((t, d), F32),
        grid=(t // tm, f // tf),
        in_specs=[
            pl.BlockSpec((tm, d), lambda i, j: (i, 0)),
            pl.BlockSpec((HALO, d), lambda i, j: (jnp.maximum(i * hb - 1, 0), 0)),
            pl.BlockSpec((HALO, d), lambda i, j: (jnp.minimum((i + 1) * hb, last), 0)),
            pl.BlockSpec((None, d, tf), lambda i, j: (w_layer, 0, j)),
            pl.BlockSpec((None, d, tf), lambda i, j: (w_layer, 0, j)),
            pl.BlockSpec((3, tf), lambda i, j: (0, j)),
            pl.BlockSpec((1, tf), lambda i, j: (0, j)),
            pl.BlockSpec((None, tf, d), lambda i, j: (w_layer, j, 0)),
            pl.BlockSpec((tm, d), lambda i, j: (i, 0)),
            pl.BlockSpec((None, N_MOD, d), lambda i, j: (cond_row_fn(i), 0, 0)),
            pl.BlockSpec((1, d), lambda i, j: (0, 0)),
        ],
        out_specs=pl.BlockSpec((tm, d), lambda i, j: (i, 0)),
        scratch_shapes=[pltpu.VMEM((tm + 2 * HALO, d), BF16)],
        compiler_params=_cparams(("parallel", "arbitrary")),
        name="conv_ffn",
    )(h2, h2, h2, w_gate_b, w_up_b, conv_w, conv_b, w_down_b, x1, mod_l, final_g)


def _rope_tables(n, head_dim):
    rows = n // GRID_W
    t_row = jnp.repeat(jnp.arange(rows, dtype=F32), GRID_W)
    t_col = jnp.tile(jnp.arange(GRID_W, dtype=F32), rows)
    axis_dim = head_dim // 2
    inv = jnp.power(ROPE_BASE, -jnp.arange(0, axis_dim, 2, dtype=F32) / axis_dim)
    ar = t_row[:, None] * inv[None, :]
    ac = t_col[:, None] * inv[None, :]
    ang = jnp.concatenate([ar, ar, ac, ac], axis=-1)
    reps = LANES // head_dim
    quarter = head_dim // 4
    sign = jnp.where((jnp.arange(head_dim) % (2 * quarter)) < quarter, -1.0, 1.0).astype(F32)
    cos = jnp.tile(jnp.cos(ang), (1, reps))
    sin_signed = jnp.tile(jnp.sin(ang) * sign[None, :], (1, reps))
    return cos, sin_signed


def kernel(x_prompt, x_sample, cache_attn_k, cache_attn_v, cache_diff_k, cache_diff_v, c, c_ctx, norm1_g, norm2_g, w_ada, b_ada, w_in, attn_q_norm_g, attn_k_norm_g, diff_lambda_q1, diff_lambda_k1, diff_lambda_q2, diff_lambda_k2, diff_subnorm_g, w_out, ffn_w_gate, ffn_w_up, ffn_conv_w, ffn_conv_b, ffn_w_down, final_norm_g):
    depth = w_in.shape[0]
    bc, lc, d = x_prompt.shape
    bl, ll, _ = x_sample.shape
    past = cache_attn_k.shape[2]

    w_in_b = w_in.astype(BF16)
    w_out_b = w_out.astype(BF16)
    w_gate_b = ffn_w_gate.astype(BF16)
    w_up_b = ffn_w_up.astype(BF16)
    w_down_b = ffn_w_down.astype(BF16)

    n_rows = 8 * ((1 + bl + 7) // 8)
    cvec = jnp.concatenate([c_ctx[None, :], c, jnp.zeros((n_rows - 1 - bl, d), F32)], axis=0)
    mod = _ada_call(cvec, w_ada, b_ada).reshape(depth, n_rows, N_MOD, d)

    dft_c = jnp.concatenate(_cos_sin(FOURIER_GROUP_DIM, FOURIER_GROUP_DIM, FOURIER_GROUP_DIM), axis=1).astype(BF16)
    rope_tabs = _rope_tables(ll, HEAD_DIM_A) + _rope_tables(ll, DK_B)
    caches = (cache_attn_k.reshape(bl, depth, past, N_KV_HEADS_A * HEAD_DIM_A),
              cache_attn_v.reshape(bl, depth, past, N_KV_HEADS_A * HEAD_DIM_A),
              cache_diff_k.reshape(bl, depth, past, N_HEADS_B * 2 * DK_B),
              cache_diff_v.reshape(bl, depth, past, N_HEADS_B * DV_B))
    lam_all = jnp.stack([diff_lambda_q1, diff_lambda_k1, diff_lambda_q2, diff_lambda_k2], axis=1)

    def run_pass(x3, is_ctx):
        b, n, _ = x3.shape
        t = b * n
        tm = 512
        x = x3.reshape(t, d)
        if is_ctx:
            cond_row_fn = lambda i: 0
        else:
            cond_row_fn = lambda i: 1 + (i * tm) // n
        new_kv = None
        for l in range(depth):
            lam_init = 0.8 - 0.6 * math.exp(-0.3 * l)
            res = _in_call(x, mod[l], cond_row_fn, norm1_g[l][None, :], (w_in_b, l),
                           attn_q_norm_g[l][None, :], attn_k_norm_g[l][None, :], dft_c,
                           None if is_ctx else rope_tabs, n, (depth, l, new_kv) if is_ctx else None, tm)
            qkv, y12 = res[0], res[1]
            if is_ctx:
                new_kv = res[2:]
            qkv3 = qkv.reshape(b, n, qkv.shape[1])
            att_a, att_b = _attn_calls(qkv3, None if is_ctx else caches, l, lam_all[l],
                                       diff_subnorm_g[l][None, :], lam_init,
                                       tq_a=min(512, n), tq_b=min(1024, n), kc=1024,
                                       ha=N_KV_HEADS_A if is_ctx else 1, hb=N_HEADS_B if is_ctx else 1,
                                       split_a=1 if is_ctx else 4, split_b=1 if is_ctx else 4)
            if is_ctx:
                four = _fourier_call(y12.reshape(b, n, y12.shape[1]), n, 1)
            else:
                four = _fourier_call(y12.reshape(b, n, y12.shape[1]), GRID_W, n // GRID_W)
            x1, h2 = _out_call(att_a.reshape(t, -1), att_b.reshape(t, -1), four.reshape(t, -1), (w_out_b, l),
                               x, mod[l], cond_row_fn, norm2_g[l][None, :], tm)
            x = _ffn_call(h2, x1, (w_gate_b, w_up_b, w_down_b, l), ffn_conv_w[l], ffn_conv_b[l][None, :],
                          mod[l], cond_row_fn, final_norm_g[None, :], n, l == depth - 1, tm, 512)
        return x.reshape(b, n, d), new_kv

    y_prompt, kvs = run_pass(x_prompt, True)
    new_attn_k = kvs[0].reshape(bc, depth, lc, N_KV_HEADS_A, HEAD_DIM_A)
    new_attn_v = kvs[1].reshape(bc, depth, lc, N_KV_HEADS_A, HEAD_DIM_A)
    new_diff_k = kvs[2].reshape(bc, depth, lc, N_HEADS_B, 2 * DK_B)
    new_diff_v = kvs[3].reshape(bc, depth, lc, N_HEADS_B, DV_B)

    y_sample, _ = run_pass(x_sample, False)
    return (y_prompt, y_sample, new_attn_k, new_attn_v, new_diff_k, new_diff_v)
```

```python
import functools
import math

import jax
import jax.numpy as jnp
from jax import lax
from jax.experimental import pallas as pl
from jax.experimental.pallas import tpu as pltpu

F32 = jnp.float32
BF16 = jnp.bfloat16

GRID_W = 64
ROPE_BASE = 10000.0
NORM_EPS = 1e-6
HEAD_DIM_A = 128
N_KV_HEADS_A = 2
GQA_GROUP = 4
N_HEADS_B = 4
DK_B = 64
DV_B = 128
N_FOURIER_GROUPS = 4
FOURIER_GROUP_DIM = 128
N_MOD = 6

LOG2E = math.log2(math.e)
LANES = 128
BF16_SUBLANES = 16
VMEM_LIMIT = 56 * 1024 * 1024


def _cparams(sem, flags=None):
    return pltpu.CompilerParams(dimension_semantics=sem, vmem_limit_bytes=VMEM_LIMIT, flags=flags)


_ATTN_FLAGS = None


def _rms(x, g):
    return x * lax.rsqrt(jnp.mean(x * x, axis=-1, keepdims=True) + NORM_EPS) * g


def _ada_kernel(c_ref, w_ref, b_ref, o_ref):
    c = c_ref[...]
    s = (c * jax.nn.sigmoid(c)).astype(BF16)
    w = w_ref[...].astype(BF16)
    o_ref[...] = jnp.dot(s, w, preferred_element_type=F32) + b_ref[...]


def _ada_call(cvec, w_ada, b_ada):
    depth, d, n = w_ada.shape
    rows = cvec.shape[0]
    tn = 512
    return pl.pallas_call(
        _ada_kernel,
        out_shape=jax.ShapeDtypeStruct((depth, rows, n), F32),
        grid=(depth, n // tn),
        in_specs=[
            pl.BlockSpec((rows, d), lambda l, j: (0, 0)),
            pl.BlockSpec((None, d, tn), lambda l, j: (l, 0, j)),
            pl.BlockSpec((None, 1, tn), lambda l, j: (l, 0, j)),
        ],
        out_specs=pl.BlockSpec((None, rows, tn), lambda l, j: (l, 0, j)),
        compiler_params=_cparams(("parallel", "parallel")),
        name="ada_mod",
    )(cvec, w_ada, b_ada.reshape(depth, 1, n))


def _rope(xs, cos, sin_signed, shift):
    w = xs.shape[-1]
    lane = lax.broadcasted_iota(jnp.int32, xs.shape, 1)
    first = (lane % (2 * shift)) < shift
    rot = jnp.where(first, pltpu.roll(xs, w - shift, 1), pltpu.roll(xs, shift, 1))
    return xs * cos + rot * sin_signed


def _in_kernel(*refs, rope, ctx_out, n_alias, cols):
    it = iter(refs)
    x_ref, mod_ref, g_ref, w_ref, gq_ref, gk_ref, dft_ref = (next(it) for _ in range(7))
    if rope:
        cos_a, sin_a, cos_b, sin_b = (next(it)[...] for _ in range(4))
    for _ in range(n_alias):
        next(it)
    qkv_ref, y_ref = next(it), next(it)
    if ctx_out:
        ka_ref, va_ref, kb_ref, vb_ref = (next(it) for _ in range(4))
    c_qa, c_ka, c_va, c_qb, c_kb, c_vb, c_f, c_end = cols

    def put_cache(ref, off, v):
        head = off // LANES
        seqs, rows, _ = ref.shape
        heads = rows * seqs // v.shape[0]
        n = rows // heads
        for s in range(seqs):
            ref[s, pl.ds(head, n, stride=heads), :] = v[s * n:(s + 1) * n, :]

    m = mod_ref[...]
    h = _rms(x_ref[...], g_ref[...])
    hb = (h * (1.0 + m[1:2]) + m[0:1]).astype(BF16)
    gq = gq_ref[...]
    gk = gk_ref[...]
    scale_a = HEAD_DIM_A ** -0.5 * LOG2E
    scale_b = DK_B ** -0.5 * LOG2E
    chunk = 4 * LANES

    for c0 in range(0, c_end, chunk):
        acc = jnp.dot(hb, w_ref[:, c0:c0 + chunk], preferred_element_type=F32)
        for s in range(chunk // LANES):
            col = c0 + s * LANES
            v = acc[:, s * LANES:(s + 1) * LANES]
            if col < c_ka:
                v = _rms(v, gq)
                if rope:
                    v = _rope(v, cos_a, sin_a, HEAD_DIM_A // 4)
                qkv_ref[:, col:col + LANES] = (v * scale_a).astype(BF16)
            elif col < c_va:
                v = _rms(v, gk)
                if ctx_out:
                    put_cache(ka_ref, col - c_ka, v)
                if rope:
                    v = _rope(v, cos_a, sin_a, HEAD_DIM_A // 4)
                qkv_ref[:, col:col + LANES] = v.astype(BF16)
            elif col < c_qb:
                if ctx_out:
                    put_cache(va_ref, col - c_va, v)
                qkv_ref[:, col:col + LANES] = v.astype(BF16)
            elif col < c_kb:
                if rope:
                    v = _rope(v, cos_b, sin_b, DK_B // 4)
                qkv_ref[:, col:col + LANES] = (v * scale_b).astype(BF16)
            elif col < c_vb:
                if ctx_out:
                    put_cache(kb_ref, col - c_kb, v)
                if rope:
                    v = _rope(v, cos_b, sin_b, DK_B // 4)
                qkv_ref[:, col:col + LANES] = v.astype(BF16)
            elif col < c_f:
                if ctx_out:
                    put_cache(vb_ref, col - c_vb, v)
                qkv_ref[:, col:col + LANES] = v.astype(BF16)
            else:
                yy = jnp.dot(v.astype(BF16), dft_ref[...], preferred_element_type=F32)
                gcol = col - c_f
                half = c_end - c_f
                y_ref[:, gcol:gcol + LANES] = yy[:, :LANES].astype(BF16)
                y_ref[:, half + gcol:half + gcol + LANES] = yy[:, LANES:].astype(BF16)


def _in_call(x2d, mod_l, cond_row_fn, norm_g, w_in_b, gq, gk, dft_c, rope_tabs, seq_len, new_cache, tm):
    ctx_out = new_cache is not None
    t, d = x2d.shape
    w_in_b, w_layer = w_in_b
    d_in = w_in_b.shape[2]
    c_qa = 0
    c_ka = N_KV_HEADS_A * GQA_GROUP * HEAD_DIM_A
    c_va = c_ka + N_KV_HEADS_A * HEAD_DIM_A
    c_qb = c_va + N_KV_HEADS_A * HEAD_DIM_A
    c_kb = c_qb + N_HEADS_B * 2 * DK_B
    c_vb = c_kb + N_HEADS_B * 2 * DK_B
    c_f = c_vb + N_HEADS_B * DV_B
    c_end = c_f + N_FOURIER_GROUPS * FOURIER_GROUP_DIM
    assert c_end == d_in
    cols = (c_qa, c_ka, c_va, c_qb, c_kb, c_vb, c_f, c_end)
    rope = rope_tabs is not None
    n_f = c_end - c_f

    in_specs = [
        pl.BlockSpec((tm, d), lambda i: (i, 0)),
        pl.BlockSpec((None, N_MOD, d), lambda i: (cond_row_fn(i), 0, 0)),
        pl.BlockSpec((1, d), lambda i: (0, 0)),
        pl.BlockSpec((None, d, d_in), lambda i: (w_layer, 0, 0), pipeline_mode=pl.Buffered(1)),
        pl.BlockSpec((1, HEAD_DIM_A), lambda i: (0, 0)),
        pl.BlockSpec((1, HEAD_DIM_A), lambda i: (0, 0)),
        pl.BlockSpec((FOURIER_GROUP_DIM, 2 * FOURIER_GROUP_DIM), lambda i: (0, 0)),
    ]
    args = [x2d, mod_l, norm_g, w_in_b, gq, gk, dft_c]
    if rope:
        nblk = seq_len // tm
        for tab in rope_tabs:
            in_specs.append(pl.BlockSpec((tm, LANES), lambda i: (i % nblk, 0)))
            args.append(tab)
    out_shape = [jax.ShapeDtypeStruct((t, c_f), BF16), jax.ShapeDtypeStruct((t, 2 * n_f), BF16)]
    out_specs = [pl.BlockSpec((tm, c_f), lambda i: (i, 0)), pl.BlockSpec((tm, 2 * n_f), lambda i: (i, 0))]
    aliases = {}
    n_alias = 0
    if ctx_out:
        depth, layer, prev = new_cache
        nb = t // seq_len
        spt = tm // seq_len
        shapes = []
        for width in (c_va - c_ka, c_qb - c_va, c_vb - c_kb, c_f - c_vb):
            rows = seq_len * (width // LANES)
            shapes.append((nb, depth, rows, LANES))
            out_shape.append(jax.ShapeDtypeStruct(shapes[-1], F32))
            out_specs.append(pl.BlockSpec((spt, None, rows, LANES), lambda i: (i, layer, 0, 0)))
        if prev is None:
            prev = [jnp.zeros(s, F32) for s in shapes]
        n_alias = len(prev)
        for a, arr in enumerate(prev):
            aliases[len(args)] = 2 + a
            in_specs.append(pl.BlockSpec(memory_space=pl.ANY))
            args.append(arr)
    return pl.pallas_call(
        functools.partial(_in_kernel, rope=rope, ctx_out=ctx_out, n_alias=n_alias, cols=cols),
        out_shape=out_shape,
        grid=(t // tm,),
        in_specs=in_specs,
        out_specs=out_specs,
        input_output_aliases=aliases,
        compiler_params=_cparams(("parallel",)),
        name="in_proj_ctx" if ctx_out else "in_proj_lat",
    )(*args)


def _softmax_pv(q, srcs):
    chunks = []
    for k_ref, v_ref, n_keys, kc, col in srcs:
        for c in range(n_keys // kc):
            chunks.append((k_ref, v_ref, c * kc, kc, col))

    def scores(ch):
        k_ref, _, off, kc, col = ch
        k = k_ref[off:off + kc, col:col + LANES].astype(BF16)
        return lax.dot_general(q, k, (((1,), (1,)), ((), ())), preferred_element_type=F32)

    def lane_fold(x, op):
        out = x[:, 0:LANES]
        for t in range(1, x.shape[1] // LANES):
            out = op(out, x[:, t * LANES:(t + 1) * LANES])
        return out

    m = l_part = acc = None
    s_next = scores(chunks[0])
    for i, ch in enumerate(chunks):
        s = s_next
        if i + 1 < len(chunks):
            s_next = scores(chunks[i + 1])
        _, v_ref, off, kc, col = ch
        row_max = jnp.max(lane_fold(s, jnp.maximum), axis=-1, keepdims=True)
        m_new = row_max if m is None else jnp.maximum(m, row_max)
        p = jnp.exp2(s - m_new)
        p_sum = lane_fold(p, jnp.add)
        v = v_ref[off:off + kc, col:col + LANES].astype(BF16)
        pv = jnp.dot(p.astype(BF16), v, preferred_element_type=F32)
        if m is None:
            l_part, acc = p_sum, pv
        else:
            alpha = jnp.exp2(m - m_new)
            l_part = alpha * l_part + p_sum
            acc = alpha * acc + pv
        m = m_new
    return acc / jnp.sum(l_part, axis=-1, keepdims=True)


def _srcs(cache_refs, k_ref, v_ref, kc, head):
    srcs = []
    col = head * LANES
    if cache_refs is not None:
        ck, cv = cache_refs
        srcs.append((ck, cv, ck.shape[0], ck.shape[0], col))
    n = k_ref.shape[0]
    srcs.append((k_ref, v_ref, n, min(kc, n), col))
    return srcs


def _attn_a_kernel(*refs, has_cache, tq, kc, heads, split):
    if has_cache:
        q_ref, ck_ref, cv_ref, k_ref, v_ref, o_ref = refs
        cache = (ck_ref, cv_ref)
    else:
        q_ref, k_ref, v_ref, o_ref = refs
        cache = None
    per = GQA_GROUP // split
    for kh in range(heads):
        for part in range(split):
            cols = [(kh * GQA_GROUP + part * per + h) * LANES for h in range(per)]
            qs = jnp.concatenate([q_ref[:, c:c + LANES] for c in cols], axis=0)
            o = _softmax_pv(qs, _srcs(cache, k_ref, v_ref, kc, kh))
            for h, c in enumerate(cols):
                o_ref[:, c:c + LANES] = o[h * tq:(h + 1) * tq].astype(BF16)


def _attn_b_kernel(*refs, has_cache, tq, kc, heads, split, lam_init):
    if has_cache:
        lam_ref, gs_ref, q_ref, ck_ref, cv_ref, k_ref, v_ref, o_ref = refs
        cache = (ck_ref, cv_ref)
    else:
        lam_ref, gs_ref, q_ref, k_ref, v_ref, o_ref = refs
        cache = None
    lp = lam_ref[...]
    lam = (jnp.exp(jnp.sum(lp[0:1] * lp[1:2], axis=-1, keepdims=True))
           - jnp.exp(jnp.sum(lp[2:3] * lp[3:4], axis=-1, keepdims=True)) + lam_init)
    rows = tq // split
    for hd in range(heads):
        for part in range(split):
            q = q_ref[part * rows:(part + 1) * rows, hd * LANES:(hd + 1) * LANES]
            lane = lax.broadcasted_iota(jnp.int32, q.shape, 1)
            zero = jnp.zeros_like(q)
            qz = jnp.concatenate([jnp.where(lane < DK_B, q, zero), jnp.where(lane >= DK_B, q, zero)], axis=0)
            o = _softmax_pv(qz, _srcs(cache, k_ref, v_ref, kc, hd))
            dlt = o[:rows] - lam * o[rows:]
            o_ref[part * rows:(part + 1) * rows, hd * LANES:(hd + 1) * LANES] = (
                _rms(dlt, gs_ref[...]) * (1.0 - lam_init)).astype(BF16)


def _attn_calls(qkv3, caches, layer, lam_params, g_sub, lam_init, tq_a, tq_b, kc, ha, hb, split_a, split_b):
    b, n, _ = qkv3.shape
    has_cache = caches is not None
    qa_blk =GQA_GROUP * HEAD_DIM_A // LANES
    k_a0 = N_KV_HEADS_A * qa_blk
    v_a0 = k_a0 + N_KV_HEADS_A
    q_b0 = v_a0 + N_KV_HEADS_A
    k_b0 = q_b0 + N_HEADS_B
    v_b0 = k_b0 + N_HEADS_B
    assert all(x % ha == 0 for x in (N_KV_HEADS_A, k_a0, v_a0)) and all(x % hb == 0 for x in (N_HEADS_B, q_b0, k_b0, v_b0))

    qw, kw = ha * GQA_GROUP * LANES, ha * LANES
    in_specs = [pl.BlockSpec((None, tq_a, qw), lambda bi, h, i: (bi, i, h))]
    args = [qkv3]
    if has_cache:
        ck, cv = caches[0], caches[1]
        p = ck.shape[2]
        in_specs += [pl.BlockSpec((None, None, p, kw), lambda bi, h, i: (bi, layer, 0, h))] * 2
        args += [ck, cv]
    in_specs += [pl.BlockSpec((None, n, kw), lambda bi, h, i: (bi, 0, k_a0 // ha + h)),
                 pl.BlockSpec((None, n, kw), lambda bi, h, i: (bi, 0, v_a0 // ha + h))]
    args += [qkv3, qkv3]
    att_a = pl.pallas_call(
        functools.partial(_attn_a_kernel, has_cache=has_cache, tq=tq_a, kc=kc, heads=ha, split=split_a),
        out_shape=jax.ShapeDtypeStruct((b, n, N_KV_HEADS_A * GQA_GROUP * HEAD_DIM_A), BF16),
        grid=(b, N_KV_HEADS_A // ha, n // tq_a),
        in_specs=in_specs,
        out_specs=pl.BlockSpec((None, tq_a, qw), lambda bi, h, i: (bi, i, h)),
        compiler_params=_cparams(("parallel", "parallel", "arbitrary"), _ATTN_FLAGS),
        name="attn_a_lat" if has_cache else "attn_a_ctx",
    )(*args)

    bw = hb * LANES
    in_specs = [pl.BlockSpec((4, DK_B), lambda bi, h, i: (0, 0)),
                pl.BlockSpec((1, DV_B), lambda bi, h, i: (0, 0)),
                pl.BlockSpec((None, tq_b, bw), lambda bi, h, i: (bi, i, q_b0 // hb + h))]
    args = [lam_params, g_sub, qkv3]
    if has_cache:
        ck, cv = caches[2], caches[3]
        p = ck.shape[2]
        in_specs += [pl.BlockSpec((None, None, p, bw), lambda bi, h, i: (bi, layer, 0, h))] * 2
        args += [ck, cv]
    in_specs += [pl.BlockSpec((None, n, bw), lambda bi, h, i: (bi, 0, k_b0 // hb + h)),
                 pl.BlockSpec((None, n, bw), lambda bi, h, i: (bi, 0, v_b0 // hb + h))]
    args += [qkv3, qkv3]
    att_b = pl.pallas_call(
        functools.partial(_attn_b_kernel, has_cache=has_cache, tq=tq_b, kc=kc, heads=hb, split=split_b, lam_init=lam_init),
        out_shape=jax.ShapeDtypeStruct((b, n, N_HEADS_B * DV_B), BF16),
        grid=(b, N_HEADS_B // hb, n // tq_b),
        in_specs=in_specs,
        out_specs=pl.BlockSpec((None, tq_b, bw), lambda bi, h, i: (bi, i, h)),
        compiler_params=_cparams(("parallel", "parallel", "arbitrary"), _ATTN_FLAGS),
        name="attn_b_lat" if has_cache else "attn_b_ctx",
    )(*args)
    return att_a, att_b


def _dft_kernel(x_ref, mat_ref, *rest, width, scale):
    x = x_ref[...]
    if x.ndim == 3:
        x = x.reshape(x.shape[0] * x.shape[1], x.shape[2])
    xs = jnp.concatenate([x[:, :width], x[:, width:]], axis=0)
    u = jnp.dot(mat_ref[...], xs, preferred_element_type=F32)
    if len(rest) == 3:
        tc_ref, ts_ref, o_ref = rest
        half = u.shape[0] // 2
        ur, ui = u[:half], u[half:]
        reps = width // LANES
        tc = jnp.concatenate([tc_ref[...].reshape(half, LANES)] * reps, axis=1)
        ts = jnp.concatenate([ts_ref[...].reshape(half, LANES)] * reps, axis=1)
        o_ref[:, :, :width] = (ur * tc - ui * ts).astype(BF16).reshape(o_ref.shape[0], o_ref.shape[1], width)
        o_ref[:, :, width:] = (ur * ts + ui * tc).astype(BF16).reshape(o_ref.shape[0], o_ref.shape[1], width)
    else:
        (o_ref,) = rest
        o_ref[...] = (u * scale).astype(BF16).reshape(o_ref.shape)


def _cos_sin(n_rows, n_cols, period):
    a = jnp.arange(n_rows, dtype=jnp.int32)[:, None]
    b = jnp.arange(n_cols, dtype=jnp.int32)[None, :]
    ang = ((a * b) % period).astype(F32) * (2.0 * math.pi / period)
    return jnp.cos(ang), jnp.sin(ang)


def _fourier_call(y3, n1, n2):
    b, n, w2 = y3.shape
    width = w2 // 2
    scale = 1.0 / math.sqrt(n * FOURIER_GROUP_DIM)
    c1, s1 = _cos_sin(n1, n1, n1)
    w_real = jnp.stack([c1, -s1], axis=1)
    if n2 == 1:
        mat = w_real.reshape(n1, 2 * n1).astype(BF16)
        return pl.pallas_call(
            functools.partial(_dft_kernel, width=width, scale=scale),
            out_shape=jax.ShapeDtypeStruct((b, n, width), BF16),
            grid=(b,),
            in_specs=[pl.BlockSpec((None, n, w2), lambda bi: (bi, 0, 0)),
                      pl.BlockSpec((n1, 2 * n1), lambda bi: (0, 0))],
            out_specs=pl.BlockSpec((None, n, width), lambda bi: (bi, 0, 0)),
            compiler_params=_cparams(("parallel",)),
            name="dft_ctx",
        )(y3, mat)

    g = BF16_SUBLANES
    c2, s2 = _cos_sin(n2, n2, n2)
    w_cplx = jnp.stack([jnp.stack([c2, -s2], axis=1), jnp.stack([s2, c2], axis=1)], axis=0)

    def kron_cols(base, row_j):
        rows, cols = base.shape
        col = lax.broadcasted_iota(jnp.int32, (cols, cols * g), 1)
        expand = (col // g == lax.broadcasted_iota(jnp.int32, (cols, cols * g), 0)).astype(BF16)
        wide = jnp.dot(base.astype(BF16), expand, preferred_element_type=F32)
        keep = row_j[:, None] == (lax.broadcasted_iota(jnp.int32, (rows, cols * g), 1) % g)
        return jnp.where(keep, wide, 0.0).astype(BF16)

    base1 = jnp.broadcast_to(w_cplx.reshape(2, 1, n2, 2 * n2), (2, g, n2, 2 * n2)).reshape(2 * g * n2, 2 * n2)
    mat1 = kron_cols(base1, (jnp.arange(2 * g * n2, dtype=jnp.int32) // n2) % g)
    base2 = jnp.broadcast_to(w_real.reshape(n1, 1, 2 * n1), (n1, g, 2 * n1)).reshape(n1 * g, 2 * n1)
    mat2 = kron_cols(base2, jnp.arange(n1 * g, dtype=jnp.int32) % g)
    tc, ts = _cos_sin(n1, n2, n)
    tc = jnp.broadcast_to(tc[:, :, None], (n1, n2, LANES))
    ts = jnp.broadcast_to(ts[:, :, None], (n1, n2, LANES))
    t = pl.pallas_call(
        functools.partial(_dft_kernel, width=width, scale=None),
        out_shape=jax.ShapeDtypeStruct((b, n1, n2, w2), BF16),
        grid=(b, n1 // g),
        in_specs=[pl.BlockSpec((None, n2, g, w2), lambda bi, j: (bi, 0, j, 0)),
                  pl.BlockSpec(mat1.shape, lambda bi, j: (0, 0), pipeline_mode=pl.Buffered(1)),
                  pl.BlockSpec((g, n2, LANES), lambda bi, j: (j, 0, 0)),
                  pl.BlockSpec((g, n2, LANES), lambda bi, j: (j, 0, 0))],
        out_specs=pl.BlockSpec((None, g, n2, w2), lambda bi, j: (bi, j, 0, 0)),
        compiler_params=_cparams(("parallel", "parallel")),
        name="dft_stage1",
    )(y3.reshape(b, n2, n1, w2), mat1, tc, ts)
    out = pl.pallas_call(
        functools.partial(_dft_kernel, width=width, scale=scale),
        out_shape=jax.ShapeDtypeStruct((b, n1, n2, width), BF16),
        grid=(b, n2 // g),
        in_specs=[pl.BlockSpec((None, n1, g, w2), lambda bi, j: (bi, 0, j, 0)),
                  pl.BlockSpec(mat2.shape, lambda bi, j: (0, 0), pipeline_mode=pl.Buffered(1))],
        out_specs=pl.BlockSpec((None, n1, g, width), lambda bi, j: (bi, 0, j, 0)),
        compiler_params=_cparams(("parallel", "parallel")),
        name="dft_stage2",
    )(t, mat2)
    return out.reshape(b, n, width)


def _out_kernel(a_ref, b_ref, f_ref, w_ref, x_ref, mod_ref, g_ref, x1_ref, h2_ref):
    ca = a_ref.shape[1]
    cb = b_ref.shape[1]
    m = mod_ref[...]
    half = a_ref.shape[0] // 2
    for r0 in (0, half):
        rows = slice(r0, r0 + half)
        acc = jnp.dot(a_ref[rows, :], w_ref[0:ca, :], preferred_element_type=F32)
        acc += jnp.dot(b_ref[rows, :], w_ref[ca:ca + cb, :], preferred_element_type=F32)
        acc += jnp.dot(f_ref[rows, :], w_ref[ca + cb:, :], preferred_element_type=F32)
        x1 = x_ref[rows, :] + m[2:3] * acc
        x1_ref[rows, :] = x1
        h = _rms(x1, g_ref[...])
        h2_ref[rows, :] = (h * (1.0 + m[4:5]) + m[3:4]).astype(BF16)


def _out_call(att_a, att_b, four, w_out_b, x2d, mod_l, cond_row_fn, norm_g, tm):
    t, d = x2d.shape
    ca, cb, cf = att_a.shape[1], att_b.shape[1], four.shape[1]
    w_out_b, w_layer = w_out_b
    return pl.pallas_call(
        _out_kernel,
        out_shape=[jax.ShapeDtypeStruct((t, d), F32), jax.ShapeDtypeStruct((t, d), BF16)],
        grid=(t // tm,),
        in_specs=[
            pl.BlockSpec((tm, ca), lambda i: (i, 0)),
            pl.BlockSpec((tm, cb), lambda i: (i, 0)),
            pl.BlockSpec((tm, cf), lambda i: (i, 0)),
            pl.BlockSpec((None, ca + cb + cf, d), lambda i: (w_layer, 0, 0), pipeline_mode=pl.Buffered(1)),
            pl.BlockSpec((tm, d), lambda i: (i, 0)),
            pl.BlockSpec((None, N_MOD, d), lambda i: (cond_row_fn(i), 0, 0)),
            pl.BlockSpec((1, d), lambda i: (0, 0)),
        ],
        out_specs=[pl.BlockSpec((tm, d), lambda i: (i, 0)), pl.BlockSpec((tm, d), lambda i: (i, 0))],
        compiler_params=_cparams(("parallel",)),
        name="out_proj",
    )(att_a, att_b, four, w_out_b, x2d, mod_l, norm_g)


HALO = 16


def _ffn_kernel(h_ref, hp_ref, hn_ref, wg_ref, wu_ref, cw_ref, cb_ref, wd_ref, x1_ref, mod_ref, gf_ref,
                o_ref, hext_ref, *, tm, seq_len, final):
    i = pl.program_id(0)
    j = pl.program_id(1)

    @pl.when(j == 0)
    def _():
        hext_ref[0:HALO, :] = hp_ref[...]
        hext_ref[HALO:HALO + tm, :] = h_ref[...]
        hext_ref[HALO + tm:, :] = hn_ref[...]
        o_ref[...] = jnp.zeros_like(o_ref)

    g = jnp.dot(hext_ref[...], wg_ref[...], preferred_element_type=F32)
    u = jnp.dot(h_ref[...], wu_ref[...], preferred_element_type=F32)
    ext = tm + 2 * HALO
    pos = (i * tm + lax.broadcasted_iota(jnp.int32, (tm, 1), 0)) % seq_len
    g_prev = jnp.where(pos == 0, 0.0, pltpu.roll(g, 1, 0)[HALO:HALO + tm])
    g_next = jnp.where(pos == seq_len - 1, 0.0, pltpu.roll(g, ext - 1, 0)[HALO:HALO + tm])
    cw = cw_ref[...]
    gc = g_prev * cw[0:1] + g[HALO:HALO + tm] * cw[1:2] + g_next * cw[2:3] + cb_ref[...]
    act = (gc * jax.nn.sigmoid(gc)) * u
    o_ref[...] += jnp.dot(act.astype(BF16), wd_ref[...], preferred_element_type=F32)

    @pl.when(j == pl.num_programs(1) - 1)
    def _():
        m = mod_ref[...]
        x2 = x1_ref[...] + m[5:6] * o_ref[...]
        if final:
            x2 = _rms(x2, gf_ref[...])
        o_ref[...] = x2


def _ffn_call(h2, x1, weights, conv_w, conv_b, mod_l, cond_row_fn, final_g, seq_len, final, tm, tf):
    t, d = x1.shape
    w_gate_b, w_up_b, w_down_b, w_layer = weights
    f = w_gate_b.shape[2]
    hb = tm // HALO
    last = t // HALO - 1
    return pl.pallas_call(
        functools.partial(_ffn_kernel, tm=tm, seq_len=seq_len, final=final),
        out_shape=jax.ShapeDtypeStruct((t, d), F32),
        grid=(t // tm, f // tf),
        in_specs=[
            pl.BlockSpec((tm, d), lambda i, j: (i, 0)),
            pl.BlockSpec((HALO, d), lambda i, j: (jnp.maximum(i * hb - 1, 0), 0)),
            pl.BlockSpec((HALO, d), lambda i, j: (jnp.minimum((i + 1) * hb, last), 0)),
            pl.BlockSpec((None, d, tf), lambda i, j: (w_layer, 0, j)),
            pl.BlockSpec((None, d, tf), lambda i, j: (w_layer, 0, j)),
            pl.BlockSpec((3, tf), lambda i, j: (0, j)),
            pl.BlockSpec((1, tf), lambda i, j: (0, j)),
            pl.BlockSpec((None, tf, d), lambda i, j: (w_layer, j, 0)),
            pl.BlockSpec((tm, d), lambda i, j: (i, 0)),
            pl.BlockSpec((None, N_MOD, d), lambda i, j: (cond_row_fn(i), 0, 0)),
            pl.BlockSpec((1, d), lambda i, j: (0, 0)),
        ],
        out_specs=pl.BlockSpec((tm, d), lambda i, j: (i, 0)),
        scratch_shapes=[pltpu.VMEM((tm + 2 * HALO, d), BF16)],
        compiler_params=_cparams(("parallel", "arbitrary")),
        name="conv_ffn",
    )(h2, h2, h2, w_gate_b, w_up_b, conv_w, conv_b, w_down_b, x1, mod_l, final_g)


def _rope_tables(n, head_dim):
    rows = n // GRID_W
    t_row = jnp.repeat(jnp.arange(rows, dtype=F32), GRID_W)
    t_col = jnp.tile(jnp.arange(GRID_W, dtype=F32), rows)
    axis_dim = head_dim // 2
    inv = jnp.power(ROPE_BASE, -jnp.arange(0, axis_dim, 2, dtype=F32) / axis_dim)
    ar = t_row[:, None] * inv[None, :]
    ac = t_col[:, None] * inv[None, :]
    ang = jnp.concatenate([ar, ar, ac, ac], axis=-1)
    reps = LANES // head_dim
    quarter = head_dim // 4
    sign = jnp.where((jnp.arange(head_dim) % (2 * quarter)) < quarter, -1.0, 1.0).astype(F32)
    cos = jnp.tile(jnp.cos(ang), (1, reps))
    sin_signed = jnp.tile(jnp.sin(ang) * sign[None, :], (1, reps))
    return cos, sin_signed


def kernel(x_prompt, x_sample, cache_attn_k, cache_attn_v, cache_diff_k, cache_diff_v, c, c_ctx, norm1_g, norm2_g, w_ada, b_ada, w_in, attn_q_norm_g, attn_k_norm_g, diff_lambda_q1, diff_lambda_k1, diff_lambda_q2, diff_lambda_k2, diff_subnorm_g, w_out, ffn_w_gate, ffn_w_up, ffn_conv_w, ffn_conv_b, ffn_w_down, final_norm_g):
    depth = w_in.shape[0]
    bc, lc, d = x_prompt.shape
    bl, ll, _ = x_sample.shape
    past = cache_attn_k.shape[2]

    w_in_b = w_in.astype(BF16)
    w_out_b = w_out.astype(BF16)
    w_gate_b = ffn_w_gate.astype(BF16)
    w_up_b = ffn_w_up.astype(BF16)
    w_down_b = ffn_w_down.astype(BF16)

    n_rows = 8 * ((1 + bl + 7) // 8)
    cvec = jnp.concatenate([c_ctx[None, :], c, jnp.zeros((n_rows - 1 - bl, d), F32)], axis=0)
    mod = _ada_call(cvec, w_ada, b_ada).reshape(depth, n_rows, N_MOD, d)

    dft_c = jnp.concatenate(_cos_sin(FOURIER_GROUP_DIM, FOURIER_GROUP_DIM, FOURIER_GROUP_DIM), axis=1).astype(BF16)
    rope_tabs = _rope_tables(ll, HEAD_DIM_A) + _rope_tables(ll, DK_B)
    caches = (cache_attn_k.reshape(bl, depth, past, N_KV_HEADS_A * HEAD_DIM_A),
              cache_attn_v.reshape(bl, depth, past, N_KV_HEADS_A * HEAD_DIM_A),
              cache_diff_k.reshape(bl, depth, past, N_HEADS_B * 2 * DK_B),
              cache_diff_v.reshape(bl, depth, past, N_HEADS_B * DV_B))
    lam_all = jnp.stack([diff_lambda_q1, diff_lambda_k1, diff_lambda_q2, diff_lambda_k2], axis=1)

    def run_pass(x3, is_ctx):
        b, n, _ = x3.shape
        t = b * n
        tm = 512
        x = x3.reshape(t, d)
        if is_ctx:
            cond_row_fn = lambda i: 0
        else:
            cond_row_fn = lambda i: 1 + (i * tm) // n
        new_kv = None
        for l in range(depth):
            lam_init = 0.8 - 0.6 * math.exp(-0.3 * l)
            res = _in_call(x, mod[l], cond_row_fn, norm1_g[l][None, :], (w_in_b, l),
                           attn_q_norm_g[l][None, :], attn_k_norm_g[l][None, :], dft_c,
                           None if is_ctx else rope_tabs, n, (depth, l, new_kv) if is_ctx else None, tm)
            qkv, y12 = res[0], res[1]
            if is_ctx:
                new_kv = res[2:]
            qkv3 = qkv.reshape(b, n, qkv.shape[1])
            att_a, att_b = _attn_calls(qkv3, None if is_ctx else caches, l, lam_all[l],
                                       diff_subnorm_g[l][None, :], lam_init,
                                       tq_a=min(512, n), tq_b=min(1024, n), kc=1024,
                                       ha=N_KV_HEADS_A if is_ctx else 1, hb=N_HEADS_B if is_ctx else 1,
                                       split_a=1 if is_ctx else 4, split_b=1 if is_ctx else 4)
            if is_ctx:
                four = _fourier_call(y12.reshape(b, n, y12.shape[1]), n, 1)
            else:
                four = _fourier_call(y12.reshape(b, n, y12.shape[1]), GRID_W, n // GRID_W)
            x1, h2 = _out_call(att_a.reshape(t, -1), att_b.reshape(t, -1), four.reshape(t, -1), (w_out_b, l),
                               x, mod[l], cond_row_fn, norm2_g[l][None, :], tm)
            x = _ffn_call(h2, x1, (w_gate_b, w_up_b, w_down_b, l), ffn_conv_w[l], ffn_conv_b[l][None, :],
                          mod[l], cond_row_fn, final_norm_g[None, :], n, l == depth - 1, tm, 512)
        return x.reshape(b, n, d), new_kv

    y_prompt, kvs = run_pass(x_prompt, True)
    new_attn_k = kvs[0].reshape(bc, depth, lc, N_KV_HEADS_A, HEAD_DIM_A)
    new_attn_v = kvs[1].reshape(bc, depth, lc, N_KV_HEADS_A, HEAD_DIM_A)
    new_diff_k = kvs[2].reshape(bc, depth, lc, N_HEADS_B, 2 * DK_B)
    new_diff_v = kvs[3].reshape(bc, depth, lc, N_HEADS_B, DV_B)

    y_sample, _ = run_pass(x_sample, False)
    return (y_prompt, y_sample, new_attn_k, new_attn_v, new_diff_k, new_diff_v)
```

```python
import functools
import math

import jax
import jax.numpy as jnp
from jax import lax
from jax.experimental import pallas as pl
from jax.experimental.pallas import tpu as pltpu

F32 = jnp.float32
BF16 = jnp.bfloat16

GRID_W = 64
ROPE_BASE = 10000.0
NORM_EPS = 1e-6
HEAD_DIM_A = 128
N_KV_HEADS_A = 2
GQA_GROUP = 4
N_HEADS_B = 4
DK_B = 64
DV_B = 128
N_FOURIER_GROUPS = 4
FOURIER_GROUP_DIM = 128
N_MOD = 6

LOG2E = math.log2(math.e)
LANES = 128
BF16_SUBLANES = 16
VMEM_LIMIT = 56 * 1024 * 1024


def _cparams(sem, flags=None):
    return pltpu.CompilerParams(dimension_semantics=sem, vmem_limit_bytes=VMEM_LIMIT, flags=flags)


_ATTN_FLAGS = None


def _rms(x, g):
    return x * lax.rsqrt(jnp.mean(x * x, axis=-1, keepdims=True) + NORM_EPS) * g


def _ada_kernel(c_ref, w_ref, b_ref, o_ref):
    c = c_ref[...]
    s = (c * jax.nn.sigmoid(c)).astype(BF16)
    w = w_ref[...].astype(BF16)
    o_ref[...] = jnp.dot(s, w, preferred_element_type=F32) + b_ref[...]


def _ada_call(cvec, w_ada, b_ada):
    depth, d, n = w_ada.shape
    rows = cvec.shape[0]
    tn = 512
    return pl.pallas_call(
        _ada_kernel,
        out_shape=jax.ShapeDtypeStruct((depth, rows, n), F32),
        grid=(depth, n // tn),
        in_specs=[
            pl.BlockSpec((rows, d), lambda l, j: (0, 0)),
            pl.BlockSpec((None, d, tn), lambda l, j: (l, 0, j)),
            pl.BlockSpec((None, 1, tn), lambda l, j: (l, 0, j)),
        ],
        out_specs=pl.BlockSpec((None, rows, tn), lambda l, j: (l, 0, j)),
        compiler_params=_cparams(("parallel", "parallel")),
        name="ada_mod",
    )(cvec, w_ada, b_ada.reshape(depth, 1, n))


def _rope(xs, cos, sin_signed, shift):
    w = xs.shape[-1]
    lane = lax.broadcasted_iota(jnp.int32, xs.shape, 1)
    first = (lane % (2 * shift)) < shift
    rot = jnp.where(first, pltpu.roll(xs, w - shift, 1), pltpu.roll(xs, shift, 1))
    return xs * cos + rot * sin_signed


def _in_kernel(*refs, rope, ctx_out, n_alias, cols):
    it = iter(refs)
    x_ref, mod_ref, g_ref, w_ref, gq_ref, gk_ref, dft_ref = (next(it) for _ in range(7))
    if rope:
        cos_a, sin_a, cos_b, sin_b = (next(it)[...] for _ in range(4))
    for _ in range(n_alias):
        next(it)
    qkv_ref, y_ref = next(it), next(it)
    if ctx_out:
        ka_ref, va_ref, kb_ref, vb_ref = (next(it) for _ in range(4))
    c_qa, c_ka, c_va, c_qb, c_kb, c_vb, c_f, c_end = cols

    def put_cache(ref, off, v):
        head = off // LANES
        seqs, rows = ref.shape[0], ref.shape[-2]
        heads = rows * seqs // v.shape[0]
        n = rows // heads
        for s in range(seqs):
            vs = v[s * n:(s + 1) * n, :]
            if len(ref.shape) == 4:
                for dd in range(ref.shape[1]):
                    ref[s, dd, pl.ds(head, n, stride=heads), :] = vs
            else:
                ref[s, pl.ds(head, n, stride=heads), :] = vs

    m = mod_ref[...]
    h = _rms(x_ref[...], g_ref[...])
    hb = (h * (1.0 + m[1:2]) + m[0:1]).astype(BF16)
    gq = gq_ref[...]
    gk = gk_ref[...]
    scale_a = HEAD_DIM_A ** -0.5 * LOG2E
    scale_b = DK_B ** -0.5 * LOG2E
    chunk = 4 * LANES

    for c0 in range(0, c_end, chunk):
        acc = jnp.dot(hb, w_ref[:, c0:c0 + chunk], preferred_element_type=F32)
        for s in range(chunk // LANES):
            col = c0 + s * LANES
            v = acc[:, s * LANES:(s + 1) * LANES]
            if col < c_ka:
                v = _rms(v, gq)
                if rope:
                    v = _rope(v, cos_a, sin_a, HEAD_DIM_A // 4)
                qkv_ref[:, col:col + LANES] = (v * scale_a).astype(BF16)
            elif col < c_va:
                v = _rms(v, gk)
                if ctx_out:
                    put_cache(ka_ref, col - c_ka, v)
                if rope:
                    v = _rope(v, cos_a, sin_a, HEAD_DIM_A // 4)
                qkv_ref[:, col:col + LANES] = v.astype(BF16)
            elif col < c_qb:
                if ctx_out:
                    put_cache(va_ref, col - c_va, v)
                qkv_ref[:, col:col + LANES] = v.astype(BF16)
            elif col < c_kb:
                if rope:
                    v = _rope(v, cos_b, sin_b, DK_B // 4)
                qkv_ref[:, col:col + LANES] = (v * scale_b).astype(BF16)
            elif col < c_vb:
                if ctx_out:
                    put_cache(kb_ref, col - c_kb, v)
                if rope:
                    v = _rope(v, cos_b, sin_b, DK_B // 4)
                qkv_ref[:, col:col + LANES] = v.astype(BF16)
            elif col < c_f:
                if ctx_out:
                    put_cache(vb_ref, col - c_vb, v)
                qkv_ref[:, col:col + LANES] = v.astype(BF16)
            else:
                yy = jnp.dot(v.astype(BF16), dft_ref[...], preferred_element_type=F32)
                gcol = col - c_f
                half = c_end - c_f
                y_ref[:, gcol:gcol + LANES] = yy[:, :LANES].astype(BF16)
                y_ref[:, half + gcol:half + gcol + LANES] = yy[:, LANES:].astype(BF16)


def _in_call(x2d, mod_l, cond_row_fn, norm_g, w_in_b, gq, gk, dft_c, rope_tabs, seq_len, new_cache, tm):
    ctx_out = new_cache is not None
    t, d = x2d.shape
    w_in_b, w_layer = w_in_b
    d_in = w_in_b.shape[2]
    c_qa = 0
    c_ka = N_KV_HEADS_A * GQA_GROUP * HEAD_DIM_A
    c_va = c_ka + N_KV_HEADS_A * HEAD_DIM_A
    c_qb = c_va + N_KV_HEADS_A * HEAD_DIM_A
    c_kb = c_qb + N_HEADS_B * 2 * DK_B
    c_vb = c_kb + N_HEADS_B * 2 * DK_B
    c_f = c_vb + N_HEADS_B * DV_B
    c_end = c_f + N_FOURIER_GROUPS * FOURIER_GROUP_DIM
    assert c_end == d_in
    cols = (c_qa, c_ka, c_va, c_qb, c_kb, c_vb, c_f, c_end)
    rope = rope_tabs is not None
    n_f = c_end - c_f

    in_specs = [
        pl.BlockSpec((tm, d), lambda i: (i, 0)),
        pl.BlockSpec((None, N_MOD, d), lambda i: (cond_row_fn(i), 0, 0)),
        pl.BlockSpec((1, d), lambda i: (0, 0)),
        pl.BlockSpec((None, d, d_in), lambda i: (w_layer, 0, 0), pipeline_mode=pl.Buffered(1)),
        pl.BlockSpec((1, HEAD_DIM_A), lambda i: (0, 0)),
        pl.BlockSpec((1, HEAD_DIM_A), lambda i: (0, 0)),
        pl.BlockSpec((FOURIER_GROUP_DIM, 2 * FOURIER_GROUP_DIM), lambda i: (0, 0)),
    ]
    args = [x2d, mod_l, norm_g, w_in_b, gq, gk, dft_c]
    if rope:
        nblk = seq_len // tm
        for tab in rope_tabs:
            in_specs.append(pl.BlockSpec((tm, LANES), lambda i: (i % nblk, 0)))
            args.append(tab)
    out_shape = [jax.ShapeDtypeStruct((t, c_f), BF16), jax.ShapeDtypeStruct((t, 2 * n_f), BF16)]
    out_specs = [pl.BlockSpec((tm, c_f), lambda i: (i, 0)), pl.BlockSpec((tm, 2 * n_f), lambda i: (i, 0))]
    aliases = {}
    n_alias = 0
    if ctx_out:
        depth, layer, prev = new_cache
        nb = t // seq_len
        spt = tm // seq_len
        for width in (c_va - c_ka, c_qb - c_va, c_vb - c_kb, c_f - c_vb):
            rows = seq_len * (width // LANES)
            out_shape.append(jax.ShapeDtypeStruct((nb, depth, rows, LANES), F32))
            if prev is None:
                out_specs.append(pl.BlockSpec((spt, depth, rows, LANES), lambda i: (i, 0, 0, 0)))
            else:
                out_specs.append(pl.BlockSpec((spt, None, rows, LANES), lambda i: (i, layer, 0, 0)))
        if prev is not None:
            n_alias = len(prev)
            for a, arr in enumerate(prev):
                aliases[len(args)] = 2 + a
                in_specs.append(pl.BlockSpec(memory_space=pl.ANY))
                args.append(arr)
    return pl.pallas_call(
        functools.partial(_in_kernel, rope=rope, ctx_out=ctx_out, n_alias=n_alias, cols=cols),
        out_shape=out_shape,
        grid=(t // tm,),
        in_specs=in_specs,
        out_specs=out_specs,
        input_output_aliases=aliases,
        compiler_params=_cparams(("parallel",)),
        name="in_proj_ctx" if ctx_out else "in_proj_lat",
    )(*args)


def _softmax_pv(q, srcs):
    chunks = []
    for k_ref, v_ref, n_keys, kc, col in srcs:
        for c in range(n_keys // kc):
            chunks.append((k_ref, v_ref, c * kc, kc, col))

    def scores(ch):
        k_ref, _, off, kc, col = ch
        k = k_ref[off:off + kc, col:col + LANES].astype(BF16)
        return lax.dot_general(q, k, (((1,), (1,)), ((), ())), preferred_element_type=F32)

    def lane_fold(x, op):
        out = x[:, 0:LANES]
        for t in range(1, x.shape[1] // LANES):
            out = op(out, x[:, t * LANES:(t + 1) * LANES])
        return out

    m = l_part = acc = None
    s_next = scores(chunks[0])
    for i, ch in enumerate(chunks):
        s = s_next
        if i + 1 < len(chunks):
            s_next = scores(chunks[i + 1])
        _, v_ref, off, kc, col = ch
        row_max = jnp.max(lane_fold(s, jnp.maximum), axis=-1, keepdims=True)
        m_new = row_max if m is None else jnp.maximum(m, row_max)
        p = jnp.exp2(s - m_new)
        p_sum = lane_fold(p, jnp.add)
        v = v_ref[off:off + kc, col:col + LANES].astype(BF16)
        pv = jnp.dot(p.astype(BF16), v, preferred_element_type=F32)
        if m is None:
            l_part, acc = p_sum, pv
        else:
            alpha = jnp.exp2(m - m_new)
            l_part = alpha * l_part + p_sum
            acc = alpha * acc + pv
        m = m_new
    return acc / jnp.sum(l_part, axis=-1, keepdims=True)


def _srcs(cache_refs, k_ref, v_ref, kc, head):
    srcs = []
    col = head * LANES
    if cache_refs is not None:
        ck, cv = cache_refs
        srcs.append((ck, cv, ck.shape[0], ck.shape[0], col))
    n = k_ref.shape[0]
    srcs.append((k_ref, v_ref, n, min(kc, n), col))
    return srcs


def _attn_a_kernel(*refs, has_cache, tq, kc, heads, split):
    if has_cache:
        q_ref, ck_ref, cv_ref, k_ref, v_ref, o_ref = refs
        cache = (ck_ref, cv_ref)
    else:
        q_ref, k_ref, v_ref, o_ref = refs
        cache = None
    per = GQA_GROUP // split
    for kh in range(heads):
        for part in range(split):
            cols = [(kh * GQA_GROUP + part * per + h) * LANES for h in range(per)]
            qs = jnp.concatenate([q_ref[:, c:c + LANES] for c in cols], axis=0)
            o = _softmax_pv(qs, _srcs(cache, k_ref, v_ref, kc, kh))
            for h, c in enumerate(cols):
                o_ref[:, c:c + LANES] = o[h * tq:(h + 1) * tq].astype(BF16)


def _attn_b_kernel(*refs, has_cache, tq, kc, heads, split, lam_init):
    if has_cache:
        lam_ref, gs_ref, q_ref, ck_ref, cv_ref, k_ref, v_ref, o_ref = refs
        cache = (ck_ref, cv_ref)
    else:
        lam_ref, gs_ref, q_ref, k_ref, v_ref, o_ref = refs
        cache = None
    lp = lam_ref[...]
    lam = (jnp.exp(jnp.sum(lp[0:1] * lp[1:2], axis=-1, keepdims=True))
           - jnp.exp(jnp.sum(lp[2:3] * lp[3:4], axis=-1, keepdims=True)) + lam_init)
    rows = tq // split
    for hd in range(heads):
        for part in range(split):
            q = q_ref[part * rows:(part + 1) * rows, hd * LANES:(hd + 1) * LANES]
            lane = lax.broadcasted_iota(jnp.int32, q.shape, 1)
            zero = jnp.zeros_like(q)
            qz = jnp.concatenate([jnp.where(lane < DK_B, q, zero), jnp.where(lane >= DK_B, q, zero)], axis=0)
            o = _softmax_pv(qz, _srcs(cache, k_ref, v_ref, kc, hd))
            dlt = o[:rows] - lam * o[rows:]
            o_ref[part * rows:(part + 1) * rows, hd * LANES:(hd + 1) * LANES] = (
                _rms(dlt, gs_ref[...]) * (1.0 - lam_init)).astype(BF16)


def _attn_calls(qkv3, caches, layer, lam_params, g_sub, lam_init, tq_a, tq_b, kc, ha, hb, split_a, split_b):
    b, n, _ = qkv3.shape
    has_cache = caches is not None
    qa_blk =GQA_GROUP * HEAD_DIM_A // LANES
    k_a0 = N_KV_HEADS_A * qa_blk
    v_a0 = k_a0 + N_KV_HEADS_A
    q_b0 = v_a0 + N_KV_HEADS_A
    k_b0 = q_b0 + N_HEADS_B
    v_b0 = k_b0 + N_HEADS_B
    assert all(x % ha == 0 for x in (N_KV_HEADS_A, k_a0, v_a0)) and all(x % hb == 0 for x in (N_HEADS_B, q_b0, k_b0, v_b0))

    qw, kw = ha * GQA_GROUP * LANES, ha * LANES
    in_specs = [pl.BlockSpec((None, tq_a, qw), lambda bi, h, i: (bi, i, h))]
    args = [qkv3]
    if has_cache:
        ck, cv = caches[0], caches[1]
        p = ck.shape[2]
        in_specs += [pl.BlockSpec((None, None, p, kw), lambda bi, h, i: (bi, layer, 0, h))] * 2
        args += [ck, cv]
    in_specs += [pl.BlockSpec((None, n, kw), lambda bi, h, i: (bi, 0, k_a0 // ha + h)),
                 pl.BlockSpec((None, n, kw), lambda bi, h, i: (bi, 0, v_a0 // ha + h))]
    args += [qkv3, qkv3]
    att_a = pl.pallas_call(
        functools.partial(_attn_a_kernel, has_cache=has_cache, tq=tq_a, kc=kc, heads=ha, split=split_a),
        out_shape=jax.ShapeDtypeStruct((b, n, N_KV_HEADS_A * GQA_GROUP * HEAD_DIM_A), BF16),
        grid=(b, N_KV_HEADS_A // ha, n // tq_a),
        in_specs=in_specs,
        out_specs=pl.BlockSpec((None, tq_a, qw), lambda bi, h, i: (bi, i, h)),
        compiler_params=_cparams(("parallel", "parallel", "arbitrary"), _ATTN_FLAGS),
        name="attn_a_lat" if has_cache else "attn_a_ctx",
    )(*args)

    bw = hb * LANES
    in_specs = [pl.BlockSpec((4, DK_B), lambda bi, h, i: (0, 0)),
                pl.BlockSpec((1, DV_B), lambda bi, h, i: (0, 0)),
                pl.BlockSpec((None, tq_b, bw), lambda bi, h, i: (bi, i, q_b0 // hb + h))]
    args = [lam_params, g_sub, qkv3]
    if has_cache:
        ck, cv = caches[2], caches[3]
        p = ck.shape[2]
        in_specs += [pl.BlockSpec((None, None, p, bw), lambda bi, h, i: (bi, layer, 0, h))] * 2
        args += [ck, cv]
    in_specs += [pl.BlockSpec((None, n, bw), lambda bi, h, i: (bi, 0, k_b0 // hb + h)),
                 pl.BlockSpec((None, n, bw), lambda bi, h, i: (bi, 0, v_b0 // hb + h))]
    args += [qkv3, qkv3]
    att_b = pl.pallas_call(
        functools.partial(_attn_b_kernel, has_cache=has_cache, tq=tq_b, kc=kc, heads=hb, split=split_b, lam_init=lam_init),
        out_shape=jax.ShapeDtypeStruct((b, n, N_HEADS_B * DV_B), BF16),
        grid=(b, N_HEADS_B // hb, n // tq_b),
        in_specs=in_specs,
        out_specs=pl.BlockSpec((None, tq_b, bw), lambda bi, h, i: (bi, i, h)),
        compiler_params=_cparams(("parallel", "parallel", "arbitrary"), _ATTN_FLAGS),
        name="attn_b_lat" if has_cache else "attn_b_ctx",
    )(*args)
    return att_a, att_b


def _dft_kernel(x_ref, mat_ref, *rest, width, scale):
    x = x_ref[...]
    if x.ndim == 3:
        x = x.reshape(x.shape[0] * x.shape[1], x.shape[2])
    xs = jnp.concatenate([x[:, :width], x[:, width:]], axis=0)
    u = jnp.dot(mat_ref[...], xs, preferred_element_type=F32)
    if len(rest) == 3:
        tc_ref, ts_ref, o_ref = rest
        half = u.shape[0] // 2
        ur, ui = u[:half], u[half:]
        reps = width // LANES
        tc = jnp.concatenate([tc_ref[...].reshape(half, LANES)] * reps, axis=1)
        ts = jnp.concatenate([ts_ref[...].reshape(half, LANES)] * reps, axis=1)
        o_ref[:, :, :width] = (ur * tc - ui * ts).astype(BF16).reshape(o_ref.shape[0], o_ref.shape[1], width)
        o_ref[:, :, width:] = (ur * ts + ui * tc).astype(BF16).reshape(o_ref.shape[0], o_ref.shape[1], width)
    else:
        (o_ref,) = rest
        o_ref[...] = (u * scale).astype(BF16).reshape(o_ref.shape)


def _cos_sin(n_rows, n_cols, period):
    a = jnp.arange(n_rows, dtype=jnp.int32)[:, None]
    b = jnp.arange(n_cols, dtype=jnp.int32)[None, :]
    ang = ((a * b) % period).astype(F32) * (2.0 * math.pi / period)
    return jnp.cos(ang), jnp.sin(ang)


def _fourier_call(y3, n1, n2):
    b, n, w2 = y3.shape
    width = w2 // 2
    scale = 1.0 / math.sqrt(n * FOURIER_GROUP_DIM)
    c1, s1 = _cos_sin(n1, n1, n1)
    w_real = jnp.stack([c1, -s1], axis=1)
    if n2 == 1:
        mat = w_real.reshape(n1, 2 * n1).astype(BF16)
        return pl.pallas_call(
            functools.partial(_dft_kernel, width=width, scale=scale),
            out_shape=jax.ShapeDtypeStruct((b, n, width), BF16),
            grid=(b,),
            in_specs=[pl.BlockSpec((None, n, w2), lambda bi: (bi, 0, 0)),
                      pl.BlockSpec((n1, 2 * n1), lambda bi: (0, 0))],
            out_specs=pl.BlockSpec((None, n, width), lambda bi: (bi, 0, 0)),
            compiler_params=_cparams(("parallel",)),
            name="dft_ctx",
        )(y3, mat)

    g = BF16_SUBLANES
    c2, s2 = _cos_sin(n2, n2, n2)
    w_cplx = jnp.stack([jnp.stack([c2, -s2], axis=1), jnp.stack([s2, c2], axis=1)], axis=0)

    def kron_cols(base, row_j):
        rows, cols = base.shape
        col = lax.broadcasted_iota(jnp.int32, (cols, cols * g), 1)
        expand = (col // g == lax.broadcasted_iota(jnp.int32, (cols, cols * g), 0)).astype(BF16)
        wide = jnp.dot(base.astype(BF16), expand, preferred_element_type=F32)
        keep = row_j[:, None] == (lax.broadcasted_iota(jnp.int32, (rows, cols * g), 1) % g)
        return jnp.where(keep, wide, 0.0).astype(BF16)

    base1 = jnp.broadcast_to(w_cplx.reshape(2, 1, n2, 2 * n2), (2, g, n2, 2 * n2)).reshape(2 * g * n2, 2 * n2)
    mat1 = kron_cols(base1, (jnp.arange(2 * g * n2, dtype=jnp.int32) // n2) % g)
    base2 = jnp.broadcast_to(w_real.reshape(n1, 1, 2 * n1), (n1, g, 2 * n1)).reshape(n1 * g, 2 * n1)
    mat2 = kron_cols(base2, jnp.arange(n1 * g, dtype=jnp.int32) % g)
    tc, ts = _cos_sin(n1, n2, n)
    tc = jnp.broadcast_to(tc[:, :, None], (n1, n2, LANES))
    ts = jnp.broadcast_to(ts[:, :, None], (n1, n2, LANES))
    t = pl.pallas_call(
        functools.partial(_dft_kernel, width=width, scale=None),
        out_shape=jax.ShapeDtypeStruct((b, n1, n2, w2), BF16),
        grid=(b, n1 // g),
        in_specs=[pl.BlockSpec((None, n2, g, w2), lambda bi, j: (bi, 0, j, 0)),
                  pl.BlockSpec(mat1.shape, lambda bi, j: (0, 0), pipeline_mode=pl.Buffered(1)),
                  pl.BlockSpec((g, n2, LANES), lambda bi, j: (j, 0, 0)),
                  pl.BlockSpec((g, n2, LANES), lambda bi, j: (j, 0, 0))],
        out_specs=pl.BlockSpec((None, g, n2, w2), lambda bi, j: (bi, j, 0, 0)),
        compiler_params=_cparams(("parallel", "parallel")),
        name="dft_stage1",
    )(y3.reshape(b, n2, n1, w2), mat1, tc, ts)
    out = pl.pallas_call(
        functools.partial(_dft_kernel, width=width, scale=scale),
        out_shape=jax.ShapeDtypeStruct((b, n1, n2, width), BF16),
        grid=(b, n2 // g),
        in_specs=[pl.BlockSpec((None, n1, g, w2), lambda bi, j: (bi, 0, j, 0)),
                  pl.BlockSpec(mat2.shape, lambda bi, j: (0, 0), pipeline_mode=pl.Buffered(1))],
        out_specs=pl.BlockSpec((None, n1, g, width), lambda bi, j: (bi, 0, j, 0)),
        compiler_params=_cparams(("parallel", "parallel")),
        name="dft_stage2",
    )(t, mat2)
    return out.reshape(b, n, width)


def _out_kernel(a_ref, b_ref, f_ref, w_ref, x_ref, mod_ref, g_ref, x1_ref, h2_ref):
    ca = a_ref.shape[1]
    cb = b_ref.shape[1]
    m = mod_ref[...]
    half = a_ref.shape[0] // 2
    for r0 in (0, half):
        rows = slice(r0, r0 + half)
        acc = jnp.dot(a_ref[rows, :], w_ref[0:ca, :], preferred_element_type=F32)
        acc += jnp.dot(b_ref[rows, :], w_ref[ca:ca + cb, :], preferred_element_type=F32)
        acc += jnp.dot(f_ref[rows, :], w_ref[ca + cb:, :], preferred_element_type=F32)
        x1 = x_ref[rows, :] + m[2:3] * acc
        x1_ref[rows, :] = x1
        h = _rms(x1, g_ref[...])
        h2_ref[rows, :] = (h * (1.0 + m[4:5]) + m[3:4]).astype(BF16)


def _out_call(att_a, att_b, four, w_out_b, x2d, mod_l, cond_row_fn, norm_g, tm):
    t, d = x2d.shape
    ca, cb, cf = att_a.shape[1], att_b.shape[1], four.shape[1]
    w_out_b, w_layer = w_out_b
    return pl.pallas_call(
        _out_kernel,
        out_shape=[jax.ShapeDtypeStruct((t, d), F32), jax.ShapeDtypeStruct((t, d), BF16)],
        grid=(t // tm,),
        in_specs=[
            pl.BlockSpec((tm, ca), lambda i: (i, 0)),
            pl.BlockSpec((tm, cb), lambda i: (i, 0)),
            pl.BlockSpec((tm, cf), lambda i: (i, 0)),
            pl.BlockSpec((None, ca + cb + cf, d), lambda i: (w_layer, 0, 0), pipeline_mode=pl.Buffered(1)),
            pl.BlockSpec((tm, d), lambda i: (i, 0)),
            pl.BlockSpec((None, N_MOD, d), lambda i: (cond_row_fn(i), 0, 0)),
            pl.BlockSpec((1, d), lambda i: (0, 0)),
        ],
        out_specs=[pl.BlockSpec((tm, d), lambda i: (i, 0)), pl.BlockSpec((tm, d), lambda i: (i, 0))],
        compiler_params=_cparams(("parallel",)),
        name="out_proj",
    )(att_a, att_b, four, w_out_b, x2d, mod_l, norm_g)


HALO = 16


def _ffn_kernel(h_ref, hp_ref, hn_ref, cw_ref, cb_ref, x1_ref, mod_ref, gf_ref, wg_hbm, wu_hbm, wd_hbm,
                o_ref, hext_ref, wg_buf, wu_buf, wd_buf, sem, *, tm, tf, nj, layer, seq_len, final):
    i = pl.program_id(0)
    n_i = pl.num_programs(0)

    def weight_copies(j, slot):
        cols = pl.ds(pl.multiple_of(j * tf, tf), tf)
        return (pltpu.make_async_copy(wg_hbm.at[layer, :, cols], wg_buf.at[slot], sem.at[0, slot]),
                pltpu.make_async_copy(wu_hbm.at[layer, :, cols], wu_buf.at[slot], sem.at[1, slot]),
                pltpu.make_async_copy(wd_hbm.at[layer, cols, :], wd_buf.at[slot], sem.at[2, slot]))

    @pl.when(i == 0)
    def _():
        for cp in weight_copies(0, 0):
            cp.start()

    hext_ref[0:HALO, :] = hp_ref[...]
    hext_ref[HALO:HALO + tm, :] = h_ref[...]
    hext_ref[HALO + tm:, :] = hn_ref[...]
    o_ref[...] = jnp.zeros_like(o_ref)
    ext = tm + 2 * HALO
    pos = (i * tm + lax.broadcasted_iota(jnp.int32, (tm, 1), 0)) % seq_len

    def chunk(j, carry):
        step = i * nj + j
        slot = step % 2
        for cp in weight_copies(j, slot):
            cp.wait()

        @pl.when(step + 1 < n_i * nj)
        def _():
            for cp in weight_copies((j + 1) % nj, 1 - slot):
                cp.start()

        g = jnp.dot(hext_ref[...], wg_buf[slot], preferred_element_type=F32)
        u = jnp.dot(h_ref[...], wu_buf[slot], preferred_element_type=F32)
        g_prev = jnp.where(pos == 0, 0.0, pltpu.roll(g, 1, 0)[HALO:HALO + tm])
        g_next = jnp.where(pos == seq_len - 1, 0.0, pltpu.roll(g, ext - 1, 0)[HALO:HALO + tm])
        cw = cw_ref[j]
        gc = g_prev * cw[0:1] + g[HALO:HALO + tm] * cw[1:2] + g_next * cw[2:3] + cb_ref[j]
        act = (gc * jax.nn.sigmoid(gc)) * u
        o_ref[...] += jnp.dot(act.astype(BF16), wd_buf[slot], preferred_element_type=F32)
        return carry

    lax.fori_loop(0, nj, chunk, 0)

    m = mod_ref[...]
    x2 = x1_ref[...] + m[5:6] * o_ref[...]
    if final:
        x2 = _rms(x2, gf_ref[...])
    o_ref[...] = x2


def _ffn_call(h2, x1, weights, conv_w, conv_b, mod_l, cond_row_fn, final_g, seq_len, final, tm, tf):
    t, d = x1.shape
    w_gate_b, w_up_b, w_down_b, w_layer = weights
    f = w_gate_b.shape[2]
    nj = f // tf
    hb = tm // HALO
    last = t // HALO - 1
    cw3 = conv_w.reshape(3, nj, tf).transpose(1, 0, 2)
    cb3 = conv_b.reshape(nj, 1, tf)
    return pl.pallas_call(
        functools.partial(_ffn_kernel, tm=tm, tf=tf, nj=nj, layer=w_layer, seq_len=seq_len, final=final),
        out_shape=jax.ShapeDtypeStruct((t, d), F32),
        grid=(t // tm,),
        in_specs=[
            pl.BlockSpec((tm, d), lambda i: (i, 0)),
            pl.BlockSpec((HALO, d), lambda i: (jnp.maximum(i * hb - 1, 0), 0)),
            pl.BlockSpec((HALO, d), lambda i: (jnp.minimum((i + 1) * hb, last), 0)),
            pl.BlockSpec((nj, 3, tf), lambda i: (0, 0, 0)),
            pl.BlockSpec((nj, 1, tf), lambda i: (0, 0, 0)),
            pl.BlockSpec((tm, d), lambda i: (i, 0)),
            pl.BlockSpec((None, N_MOD, d), lambda i: (cond_row_fn(i), 0, 0)),
            pl.BlockSpec((1, d), lambda i: (0, 0)),
            pl.BlockSpec(memory_space=pl.ANY),
            pl.BlockSpec(memory_space=pl.ANY),
            pl.BlockSpec(memory_space=pl.ANY),
        ],
        out_specs=pl.BlockSpec((tm, d), lambda i: (i, 0)),
        scratch_shapes=[pltpu.VMEM((tm + 2 * HALO, d), BF16),
                        pltpu.VMEM((2, d, tf), BF16), pltpu.VMEM((2, d, tf), BF16), pltpu.VMEM((2, tf, d), BF16),
                        pltpu.SemaphoreType.DMA((3, 2))],
        compiler_params=_cparams(("arbitrary",)),
        name="conv_ffn",
    )(h2, h2, h2, cw3, cb3, x1, mod_l, final_g, w_gate_b, w_up_b, w_down_b)


def _rope_tables(n, head_dim):
    rows = n // GRID_W
    t_row = jnp.repeat(jnp.arange(rows, dtype=F32), GRID_W)
    t_col = jnp.tile(jnp.arange(GRID_W, dtype=F32), rows)
    axis_dim = head_dim // 2
    inv = jnp.power(ROPE_BASE, -jnp.arange(0, axis_dim, 2, dtype=F32) / axis_dim)
    ar = t_row[:, None] * inv[None, :]
    ac = t_col[:, None] * inv[None, :]
    ang = jnp.concatenate([ar, ar, ac, ac], axis=-1)
    reps = LANES // head_dim
    quarter = head_dim // 4
    sign = jnp.where((jnp.arange(head_dim) % (2 * quarter)) < quarter, -1.0, 1.0).astype(F32)
    cos = jnp.tile(jnp.cos(ang), (1, reps))
    sin_signed = jnp.tile(jnp.sin(ang) * sign[None, :], (1, reps))
    return cos, sin_signed


def kernel(x_prompt, x_sample, cache_attn_k, cache_attn_v, cache_diff_k, cache_diff_v, c, c_ctx, norm1_g, norm2_g, w_ada, b_ada, w_in, attn_q_norm_g, attn_k_norm_g, diff_lambda_q1, diff_lambda_k1, diff_lambda_q2, diff_lambda_k2, diff_subnorm_g, w_out, ffn_w_gate, ffn_w_up, ffn_conv_w, ffn_conv_b, ffn_w_down, final_norm_g):
    depth = w_in.shape[0]
    bc, lc, d = x_prompt.shape
    bl, ll, _ = x_sample.shape
    past = cache_attn_k.shape[2]

    w_in_b = w_in.astype(BF16)
    w_out_b = w_out.astype(BF16)
    w_gate_b = ffn_w_gate.astype(BF16)
    w_up_b = ffn_w_up.astype(BF16)
    w_down_b = ffn_w_down.astype(BF16)

    n_rows = 8 * ((1 + bl + 7) // 8)
    cvec = jnp.concatenate([c_ctx[None, :], c, jnp.zeros((n_rows - 1 - bl, d), F32)], axis=0)
    mod = _ada_call(cvec, w_ada, b_ada).reshape(depth, n_rows, N_MOD, d)

    dft_c = jnp.concatenate(_cos_sin(FOURIER_GROUP_DIM, FOURIER_GROUP_DIM, FOURIER_GROUP_DIM), axis=1).astype(BF16)
    rope_tabs = _rope_tables(ll, HEAD_DIM_A) + _rope_tables(ll, DK_B)
    caches = (cache_attn_k.reshape(bl, depth, past, N_KV_HEADS_A * HEAD_DIM_A),
              cache_attn_v.reshape(bl, depth, past, N_KV_HEADS_A * HEAD_DIM_A),
              cache_diff_k.reshape(bl, depth, past, N_HEADS_B * 2 * DK_B),
              cache_diff_v.reshape(bl, depth, past, N_HEADS_B * DV_B))
    lam_all = jnp.stack([diff_lambda_q1, diff_lambda_k1, diff_lambda_q2, diff_lambda_k2], axis=1)

    def run_pass(x3, is_ctx):
        b, n, _ = x3.shape
        t = b * n
        tm = 512
        x = x3.reshape(t, d)
        if is_ctx:
            cond_row_fn = lambda i: 0
        else:
            cond_row_fn = lambda i: 1 + (i * tm) // n
        new_kv = None
        for l in range(depth):
            lam_init = 0.8 - 0.6 * math.exp(-0.3 * l)
            res = _in_call(x, mod[l], cond_row_fn, norm1_g[l][None, :], (w_in_b, l),
                           attn_q_norm_g[l][None, :], attn_k_norm_g[l][None, :], dft_c,
                           None if is_ctx else rope_tabs, n, (depth, l, new_kv) if is_ctx else None, tm)
            qkv, y12 = res[0], res[1]
            if is_ctx:
                new_kv = res[2:]
            qkv3 = qkv.reshape(b, n, qkv.shape[1])
            att_a, att_b = _attn_calls(qkv3, None if is_ctx else caches, l, lam_all[l],
                                       diff_subnorm_g[l][None, :], lam_init,
                                       tq_a=min(512, n), tq_b=min(1024, n), kc=1024,
                                       ha=N_KV_HEADS_A if is_ctx else 1, hb=N_HEADS_B if is_ctx else 1,
                                       split_a=1 if is_ctx else 4, split_b=1 if is_ctx else 4)
            if is_ctx:
                four = _fourier_call(y12.reshape(b, n, y12.shape[1]), n, 1)
            else:
                four = _fourier_call(y12.reshape(b, n, y12.shape[1]), GRID_W, n // GRID_W)
            x1, h2 = _out_call(att_a.reshape(t, -1), att_b.reshape(t, -1), four.reshape(t, -1), (w_out_b, l),
                               x, mod[l], cond_row_fn, norm2_g[l][None, :], tm)
            x = _ffn_call(h2, x1, (w_gate_b, w_up_b, w_down_b, l), ffn_conv_w[l], ffn_conv_b[l][None, :],
                          mod[l], cond_row_fn, final_norm_g[None, :], n, l == depth - 1, tm, 512)
        return x.reshape(b, n, d), new_kv

    y_prompt, kvs = run_pass(x_prompt, True)
    new_attn_k = kvs[0].reshape(bc, depth, lc, N_KV_HEADS_A, HEAD_DIM_A)
    new_attn_v = kvs[1].reshape(bc, depth, lc, N_KV_HEADS_A, HEAD_DIM_A)
    new_diff_k = kvs[2].reshape(bc, depth, lc, N_HEADS_B, 2 * DK_B)
    new_diff_v = kvs[3].reshape(bc, depth, lc, N_HEADS_B, DV_B)

    y_sample, _ = run_pass(x_sample, False)
    return (y_prompt, y_sample, new_attn_k, new_attn_v, new_diff_k, new_diff_v)
```

```python
import functools
import math

import jax
import jax.numpy as jnp
from jax import lax
from jax.experimental import pallas as pl
from jax.experimental.pallas import tpu as pltpu

F32 = jnp.float32
BF16 = jnp.bfloat16

GRID_W = 64
ROPE_BASE = 10000.0
NORM_EPS = 1e-6
HEAD_DIM_A = 128
N_KV_HEADS_A = 2
GQA_GROUP = 4
N_HEADS_B = 4
DK_B = 64
DV_B = 128
N_FOURIER_GROUPS = 4
FOURIER_GROUP_DIM = 128
N_MOD = 6

LOG2E = math.log2(math.e)
LANES = 128
BF16_SUBLANES = 16
VMEM_LIMIT = 58 * 1024 * 1024


def _cparams(sem, flags=None):
    return pltpu.CompilerParams(dimension_semantics=sem, vmem_limit_bytes=VMEM_LIMIT, flags=flags)


_ATTN_FLAGS = None


def _rms(x, g):
    return x * lax.rsqrt(jnp.mean(x * x, axis=-1, keepdims=True) + NORM_EPS) * g


def _ada_kernel(c_ref, w_ref, b_ref, o_ref):
    c = c_ref[...]
    s = (c * jax.nn.sigmoid(c)).astype(BF16)
    w = w_ref[...].astype(BF16)
    o_ref[...] = jnp.dot(s, w, preferred_element_type=F32) + b_ref[...]


def _ada_call(cvec, w_ada, b_ada):
    depth, d, n = w_ada.shape
    rows = cvec.shape[0]
    tn = 512
    return pl.pallas_call(
        _ada_kernel,
        out_shape=jax.ShapeDtypeStruct((depth, rows, n), F32),
        grid=(depth, n // tn),
        in_specs=[
            pl.BlockSpec((rows, d), lambda l, j: (0, 0)),
            pl.BlockSpec((None, d, tn), lambda l, j: (l, 0, j)),
            pl.BlockSpec((None, 1, tn), lambda l, j: (l, 0, j)),
        ],
        out_specs=pl.BlockSpec((None, rows, tn), lambda l, j: (l, 0, j)),
        compiler_params=_cparams(("parallel", "parallel")),
        name="ada_mod",
    )(cvec, w_ada, b_ada.reshape(depth, 1, n))


def _rope(xs, cos, sin_signed, shift):
    w = xs.shape[-1]
    lane = lax.broadcasted_iota(jnp.int32, xs.shape, 1)
    first = (lane % (2 * shift)) < shift
    rot = jnp.where(first, pltpu.roll(xs, w - shift, 1), pltpu.roll(xs, shift, 1))
    return xs * cos + rot * sin_signed


def _in_kernel(*refs, rope, ctx_out, n_alias, cols):
    it = iter(refs)
    x_ref, mod_ref, g_ref, w_ref, gq_ref, gk_ref, dft_ref = (next(it) for _ in range(7))
    if rope:
        cos_a, sin_a, cos_b, sin_b = (next(it)[...] for _ in range(4))
    for _ in range(n_alias):
        next(it)
    qkv_ref, y_ref = next(it), next(it)
    if ctx_out:
        ka_ref, va_ref, kb_ref, vb_ref = (next(it) for _ in range(4))
    c_qa, c_ka, c_va, c_qb, c_kb, c_vb, c_f, c_end = cols

    def put_cache(ref, off, v):
        head = off // LANES
        seqs, rows = ref.shape[0], ref.shape[-2]
        heads = rows * seqs // v.shape[0]
        n = rows // heads
        for s in range(seqs):
            vs = v[s * n:(s + 1) * n, :]
            if len(ref.shape) == 4:
                for dd in range(ref.shape[1]):
                    ref[s, dd, pl.ds(head, n, stride=heads), :] = vs
            else:
                ref[s, pl.ds(head, n, stride=heads), :] = vs

    m = mod_ref[...]
    h = _rms(x_ref[...], g_ref[...])
    hb = (h * (1.0 + m[1:2]) + m[0:1]).astype(BF16)
    gq = gq_ref[...]
    gk = gk_ref[...]
    scale_a = HEAD_DIM_A ** -0.5 * LOG2E
    scale_b = DK_B ** -0.5 * LOG2E
    chunk = 4 * LANES

    for c0 in range(0, c_end, chunk):
        acc = jnp.dot(hb, w_ref[:, c0:c0 + chunk], preferred_element_type=F32)
        for s in range(chunk // LANES):
            col = c0 + s * LANES
            v = acc[:, s * LANES:(s + 1) * LANES]
            if col < c_ka:
                v = _rms(v, gq)
                if rope:
                    v = _rope(v, cos_a, sin_a, HEAD_DIM_A // 4)
                qkv_ref[:, col:col + LANES] = (v * scale_a).astype(BF16)
            elif col < c_va:
                v = _rms(v, gk)
                if ctx_out:
                    put_cache(ka_ref, col - c_ka, v)
                if rope:
                    v = _rope(v, cos_a, sin_a, HEAD_DIM_A // 4)
                qkv_ref[:, col:col + LANES] = v.astype(BF16)
            elif col < c_qb:
                if ctx_out:
                    put_cache(va_ref, col - c_va, v)
                qkv_ref[:, col:col + LANES] = v.astype(BF16)
            elif col < c_kb:
                if rope:
                    v = _rope(v, cos_b, sin_b, DK_B // 4)
                qkv_ref[:, col:col + LANES] = (v * scale_b).astype(BF16)
            elif col < c_vb:
                if ctx_out:
                    put_cache(kb_ref, col - c_kb, v)
                if rope:
                    v = _rope(v, cos_b, sin_b, DK_B // 4)
                qkv_ref[:, col:col + LANES] = v.astype(BF16)
            elif col < c_f:
                if ctx_out:
                    put_cache(vb_ref, col - c_vb, v)
                qkv_ref[:, col:col + LANES] = v.astype(BF16)
            else:
                yy = jnp.dot(v.astype(BF16), dft_ref[...], preferred_element_type=F32)
                gcol = col - c_f
                half = c_end - c_f
                y_ref[:, gcol:gcol + LANES] = yy[:, :LANES].astype(BF16)
                y_ref[:, half + gcol:half + gcol + LANES] = yy[:, LANES:].astype(BF16)


def _in_call(x2d, mod_l, cond_row_fn, norm_g, w_in_b, gq, gk, dft_c, rope_tabs, seq_len, new_cache, tm):
    ctx_out = new_cache is not None
    t, d = x2d.shape
    w_in_b, w_layer = w_in_b
    d_in = w_in_b.shape[2]
    c_qa = 0
    c_ka = N_KV_HEADS_A * GQA_GROUP * HEAD_DIM_A
    c_va = c_ka + N_KV_HEADS_A * HEAD_DIM_A
    c_qb = c_va + N_KV_HEADS_A * HEAD_DIM_A
    c_kb = c_qb + N_HEADS_B * 2 * DK_B
    c_vb = c_kb + N_HEADS_B * 2 * DK_B
    c_f = c_vb + N_HEADS_B * DV_B
    c_end = c_f + N_FOURIER_GROUPS * FOURIER_GROUP_DIM
    assert c_end == d_in
    cols = (c_qa, c_ka, c_va, c_qb, c_kb, c_vb, c_f, c_end)
    rope = rope_tabs is not None
    n_f = c_end - c_f

    in_specs = [
        pl.BlockSpec((tm, d), lambda i: (i, 0)),
        pl.BlockSpec((None, N_MOD, d), lambda i: (cond_row_fn(i), 0, 0)),
        pl.BlockSpec((1, d), lambda i: (0, 0)),
        pl.BlockSpec((None, d, d_in), lambda i: (w_layer, 0, 0), pipeline_mode=pl.Buffered(1)),
        pl.BlockSpec((1, HEAD_DIM_A), lambda i: (0, 0)),
        pl.BlockSpec((1, HEAD_DIM_A), lambda i: (0, 0)),
        pl.BlockSpec((FOURIER_GROUP_DIM, 2 * FOURIER_GROUP_DIM), lambda i: (0, 0)),
    ]
    args = [x2d, mod_l, norm_g, w_in_b, gq, gk, dft_c]
    if rope:
        nblk = seq_len // tm
        for tab in rope_tabs:
            in_specs.append(pl.BlockSpec((tm, LANES), lambda i: (i % nblk, 0)))
            args.append(tab)
    out_shape = [jax.ShapeDtypeStruct((t, c_f), BF16), jax.ShapeDtypeStruct((t, 2 * n_f), BF16)]
    out_specs = [pl.BlockSpec((tm, c_f), lambda i: (i, 0)), pl.BlockSpec((tm, 2 * n_f), lambda i: (i, 0))]
    aliases = {}
    n_alias = 0
    if ctx_out:
        depth, layer, prev = new_cache
        nb = t // seq_len
        spt = tm // seq_len
        for width in (c_va - c_ka, c_qb - c_va, c_vb - c_kb, c_f - c_vb):
            rows = seq_len * (width // LANES)
            out_shape.append(jax.ShapeDtypeStruct((nb, depth, rows, LANES), F32))
            if prev is None:
                out_specs.append(pl.BlockSpec((spt, depth, rows, LANES), lambda i: (i, 0, 0, 0)))
            else:
                out_specs.append(pl.BlockSpec((spt, None, rows, LANES), lambda i: (i, layer, 0, 0)))
        if prev is not None:
            n_alias = len(prev)
            for a, arr in enumerate(prev):
                aliases[len(args)] = 2 + a
                in_specs.append(pl.BlockSpec(memory_space=pl.ANY))
                args.append(arr)
    return pl.pallas_call(
        functools.partial(_in_kernel, rope=rope, ctx_out=ctx_out, n_alias=n_alias, cols=cols),
        out_shape=out_shape,
        grid=(t // tm,),
        in_specs=in_specs,
        out_specs=out_specs,
        input_output_aliases=aliases,
        compiler_params=_cparams(("parallel",)),
        name="in_proj_ctx" if ctx_out else "in_proj_lat",
    )(*args)


def _softmax_pv(q, srcs):
    chunks = []
    for k_ref, v_ref, n_keys, kc, col in srcs:
        for c in range(n_keys // kc):
            chunks.append((k_ref, v_ref, c * kc, kc, col))

    def scores(ch):
        k_ref, _, off, kc, col = ch
        k = k_ref[off:off + kc, col:col + LANES].astype(BF16)
        return lax.dot_general(q, k, (((1,), (1,)), ((), ())), preferred_element_type=F32)

    def lane_fold(x, op):
        out = x[:, 0:LANES]
        for t in range(1, x.shape[1] // LANES):
            out = op(out, x[:, t * LANES:(t + 1) * LANES])
        return out

    m = l_part = acc = None
    s_next = scores(chunks[0])
    for i, ch in enumerate(chunks):
        s = s_next
        if i + 1 < len(chunks):
            s_next = scores(chunks[i + 1])
        _, v_ref, off, kc, col = ch
        row_max = jnp.max(lane_fold(s, jnp.maximum), axis=-1, keepdims=True)
        m_new = row_max if m is None else jnp.maximum(m, row_max)
        p = jnp.exp2(s - m_new)
        p_sum = lane_fold(p, jnp.add)
        v = v_ref[off:off + kc, col:col + LANES].astype(BF16)
        pv = jnp.dot(p.astype(BF16), v, preferred_element_type=F32)
        if m is None:
            l_part, acc = p_sum, pv
        else:
            alpha = jnp.exp2(m - m_new)
            l_part = alpha * l_part + p_sum
            acc = alpha * acc + pv
        m = m_new
    return acc / jnp.sum(l_part, axis=-1, keepdims=True)


def _srcs(cache_refs, k_ref, v_ref, kc, head):
    srcs = []
    col = head * LANES
    if cache_refs is not None:
        ck, cv = cache_refs
        srcs.append((ck, cv, ck.shape[0], ck.shape[0], col))
    n = k_ref.shape[0]
    srcs.append((k_ref, v_ref, n, min(kc, n), col))
    return srcs


def _attn_a_kernel(*refs, has_cache, tq, kc, heads, split):
    if has_cache:
        q_ref, ck_ref, cv_ref, k_ref, v_ref, o_ref = refs
        cache = (ck_ref, cv_ref)
    else:
        q_ref, k_ref, v_ref, o_ref = refs
        cache = None
    per = GQA_GROUP // split
    for kh in range(heads):
        for part in range(split):
            cols = [(kh * GQA_GROUP + part * per + h) * LANES for h in range(per)]
            qs = jnp.concatenate([q_ref[:, c:c + LANES] for c in cols], axis=0)
            o = _softmax_pv(qs, _srcs(cache, k_ref, v_ref, kc, kh))
            for h, c in enumerate(cols):
                o_ref[:, c:c + LANES] = o[h * tq:(h + 1) * tq].astype(BF16)


def _attn_b_kernel(*refs, has_cache, tq, kc, heads, split, lam_init):
    if has_cache:
        lam_ref, gs_ref, q_ref, ck_ref, cv_ref, k_ref, v_ref, o_ref = refs
        cache = (ck_ref, cv_ref)
    else:
        lam_ref, gs_ref, q_ref, k_ref, v_ref, o_ref = refs
        cache = None
    lp = lam_ref[...]
    lam = (jnp.exp(jnp.sum(lp[0:1] * lp[1:2], axis=-1, keepdims=True))
           - jnp.exp(jnp.sum(lp[2:3] * lp[3:4], axis=-1, keepdims=True)) + lam_init)
    rows = tq // split
    for hd in range(heads):
        for part in range(split):
            q = q_ref[part * rows:(part + 1) * rows, hd * LANES:(hd + 1) * LANES]
            lane = lax.broadcasted_iota(jnp.int32, q.shape, 1)
            zero = jnp.zeros_like(q)
            qz = jnp.concatenate([jnp.where(lane < DK_B, q, zero), jnp.where(lane >= DK_B, q, zero)], axis=0)
            o = _softmax_pv(qz, _srcs(cache, k_ref, v_ref, kc, hd))
            dlt = o[:rows] - lam * o[rows:]
            o_ref[part * rows:(part + 1) * rows, hd * LANES:(hd + 1) * LANES] = (
                _rms(dlt, gs_ref[...]) * (1.0 - lam_init)).astype(BF16)


def _attn_calls(qkv3, caches, layer, lam_params, g_sub, lam_init, tq_a, tq_b, kc, ha, hb, split_a, split_b):
    b, n, _ = qkv3.shape
    has_cache = caches is not None
    qa_blk =GQA_GROUP * HEAD_DIM_A // LANES
    k_a0 = N_KV_HEADS_A * qa_blk
    v_a0 = k_a0 + N_KV_HEADS_A
    q_b0 = v_a0 + N_KV_HEADS_A
    k_b0 = q_b0 + N_HEADS_B
    v_b0 = k_b0 + N_HEADS_B
    assert all(x % ha == 0 for x in (N_KV_HEADS_A, k_a0, v_a0)) and all(x % hb == 0 for x in (N_HEADS_B, q_b0, k_b0, v_b0))

    qw, kw = ha * GQA_GROUP * LANES, ha * LANES
    in_specs = [pl.BlockSpec((None, tq_a, qw), lambda bi, h, i: (bi, i, h))]
    args = [qkv3]
    if has_cache:
        ck, cv = caches[0], caches[1]
        p = ck.shape[2]
        in_specs += [pl.BlockSpec((None, None, p, kw), lambda bi, h, i: (bi, layer, 0, h))] * 2
        args += [ck, cv]
    in_specs += [pl.BlockSpec((None, n, kw), lambda bi, h, i: (bi, 0, k_a0 // ha + h)),
                 pl.BlockSpec((None, n, kw), lambda bi, h, i: (bi, 0, v_a0 // ha + h))]
    args += [qkv3, qkv3]
    att_a = pl.pallas_call(
        functools.partial(_attn_a_kernel, has_cache=has_cache, tq=tq_a, kc=kc, heads=ha, split=split_a),
        out_shape=jax.ShapeDtypeStruct((b, n, N_KV_HEADS_A * GQA_GROUP * HEAD_DIM_A), BF16),
        grid=(b, N_KV_HEADS_A // ha, n // tq_a),
        in_specs=in_specs,
        out_specs=pl.BlockSpec((None, tq_a, qw), lambda bi, h, i: (bi, i, h)),
        compiler_params=_cparams(("parallel", "parallel", "arbitrary"), _ATTN_FLAGS),
        name="attn_a_lat" if has_cache else "attn_a_ctx",
    )(*args)

    bw = hb * LANES
    in_specs = [pl.BlockSpec((4, DK_B), lambda bi, h, i: (0, 0)),
                pl.BlockSpec((1, DV_B), lambda bi, h, i: (0, 0)),
                pl.BlockSpec((None, tq_b, bw), lambda bi, h, i: (bi, i, q_b0 // hb + h))]
    args = [lam_params, g_sub, qkv3]
    if has_cache:
        ck, cv = caches[2], caches[3]
        p = ck.shape[2]
        in_specs += [pl.BlockSpec((None, None, p, bw), lambda bi, h, i: (bi, layer, 0, h))] * 2
        args += [ck, cv]
    in_specs += [pl.BlockSpec((None, n, bw), lambda bi, h, i: (bi, 0, k_b0 // hb + h)),
                 pl.BlockSpec((None, n, bw), lambda bi, h, i: (bi, 0, v_b0 // hb + h))]
    args += [qkv3, qkv3]
    att_b = pl.pallas_call(
        functools.partial(_attn_b_kernel, has_cache=has_cache, tq=tq_b, kc=kc, heads=hb, split=split_b, lam_init=lam_init),
        out_shape=jax.ShapeDtypeStruct((b, n, N_HEADS_B * DV_B), BF16),
        grid=(b, N_HEADS_B // hb, n // tq_b),
        in_specs=in_specs,
        out_specs=pl.BlockSpec((None, tq_b, bw), lambda bi, h, i: (bi, i, h)),
        compiler_params=_cparams(("parallel", "parallel", "arbitrary"), _ATTN_FLAGS),
        name="attn_b_lat" if has_cache else "attn_b_ctx",
    )(*args)
    return att_a, att_b


def _dft_kernel(x_ref, mat_ref, *rest, width, scale):
    x = x_ref[...]
    if x.ndim == 3:
        x = x.reshape(x.shape[0] * x.shape[1], x.shape[2])
    xs = jnp.concatenate([x[:, :width], x[:, width:]], axis=0)
    u = jnp.dot(mat_ref[...], xs, preferred_element_type=F32)
    if len(rest) == 3:
        tc_ref, ts_ref, o_ref = rest
        half = u.shape[0] // 2
        ur, ui = u[:half], u[half:]
        reps = width // LANES
        tc = jnp.concatenate([tc_ref[...].reshape(half, LANES)] * reps, axis=1)
        ts = jnp.concatenate([ts_ref[...].reshape(half, LANES)] * reps, axis=1)
        o_ref[:, :, :width] = (ur * tc - ui * ts).astype(BF16).reshape(o_ref.shape[0], o_ref.shape[1], width)
        o_ref[:, :, width:] = (ur * ts + ui * tc).astype(BF16).reshape(o_ref.shape[0], o_ref.shape[1], width)
    else:
        (o_ref,) = rest
        o_ref[...] = (u * scale).astype(BF16).reshape(o_ref.shape)


def _cos_sin(n_rows, n_cols, period):
    a = jnp.arange(n_rows, dtype=jnp.int32)[:, None]
    b = jnp.arange(n_cols, dtype=jnp.int32)[None, :]
    ang = ((a * b) % period).astype(F32) * (2.0 * math.pi / period)
    return jnp.cos(ang), jnp.sin(ang)


def _fourier_call(y3, n1, n2):
    b, n, w2 = y3.shape
    width = w2 // 2
    scale = 1.0 / math.sqrt(n * FOURIER_GROUP_DIM)
    c1, s1 = _cos_sin(n1, n1, n1)
    w_real = jnp.stack([c1, -s1], axis=1)
    if n2 == 1:
        mat = w_real.reshape(n1, 2 * n1).astype(BF16)
        return pl.pallas_call(
            functools.partial(_dft_kernel, width=width, scale=scale),
            out_shape=jax.ShapeDtypeStruct((b, n, width), BF16),
            grid=(b,),
            in_specs=[pl.BlockSpec((None, n, w2), lambda bi: (bi, 0, 0)),
                      pl.BlockSpec((n1, 2 * n1), lambda bi: (0, 0))],
            out_specs=pl.BlockSpec((None, n, width), lambda bi: (bi, 0, 0)),
            compiler_params=_cparams(("parallel",)),
            name="dft_ctx",
        )(y3, mat)

    g = BF16_SUBLANES
    c2, s2 = _cos_sin(n2, n2, n2)
    w_cplx = jnp.stack([jnp.stack([c2, -s2], axis=1), jnp.stack([s2, c2], axis=1)], axis=0)

    def kron_cols(base, row_j):
        rows, cols = base.shape
        col = lax.broadcasted_iota(jnp.int32, (cols, cols * g), 1)
        expand = (col // g == lax.broadcasted_iota(jnp.int32, (cols, cols * g), 0)).astype(BF16)
        wide = jnp.dot(base.astype(BF16), expand, preferred_element_type=F32)
        keep = row_j[:, None] == (lax.broadcasted_iota(jnp.int32, (rows, cols * g), 1) % g)
        return jnp.where(keep, wide, 0.0).astype(BF16)

    base1 = jnp.broadcast_to(w_cplx.reshape(2, 1, n2, 2 * n2), (2, g, n2, 2 * n2)).reshape(2 * g * n2, 2 * n2)
    mat1 = kron_cols(base1, (jnp.arange(2 * g * n2, dtype=jnp.int32) // n2) % g)
    base2 = jnp.broadcast_to(w_real.reshape(n1, 1, 2 * n1), (n1, g, 2 * n1)).reshape(n1 * g, 2 * n1)
    mat2 = kron_cols(base2, jnp.arange(n1 * g, dtype=jnp.int32) % g)
    tc, ts = _cos_sin(n1, n2, n)
    tc = jnp.broadcast_to(tc[:, :, None], (n1, n2, LANES))
    ts = jnp.broadcast_to(ts[:, :, None], (n1, n2, LANES))
    t = pl.pallas_call(
        functools.partial(_dft_kernel, width=width, scale=None),
        out_shape=jax.ShapeDtypeStruct((b, n1, n2, w2), BF16),
        grid=(b, n1 // g),
        in_specs=[pl.BlockSpec((None, n2, g, w2), lambda bi, j: (bi, 0, j, 0)),
                  pl.BlockSpec(mat1.shape, lambda bi, j: (0, 0), pipeline_mode=pl.Buffered(1)),
                  pl.BlockSpec((g, n2, LANES), lambda bi, j: (j, 0, 0)),
                  pl.BlockSpec((g, n2, LANES), lambda bi, j: (j, 0, 0))],
        out_specs=pl.BlockSpec((None, g, n2, w2), lambda bi, j: (bi, j, 0, 0)),
        compiler_params=_cparams(("parallel", "parallel")),
        name="dft_stage1",
    )(y3.reshape(b, n2, n1, w2), mat1, tc, ts)
    out = pl.pallas_call(
        functools.partial(_dft_kernel, width=width, scale=scale),
        out_shape=jax.ShapeDtypeStruct((b, n1, n2, width), BF16),
        grid=(b, n2 // g),
        in_specs=[pl.BlockSpec((None, n1, g, w2), lambda bi, j: (bi, 0, j, 0)),
                  pl.BlockSpec(mat2.shape, lambda bi, j: (0, 0), pipeline_mode=pl.Buffered(1))],
        out_specs=pl.BlockSpec((None, n1, g, width), lambda bi, j: (bi, 0, j, 0)),
        compiler_params=_cparams(("parallel", "parallel")),
        name="dft_stage2",
    )(t, mat2)
    return out.reshape(b, n, width)


def _out_kernel(a_ref, b_ref, f_ref, w_ref, x_ref, mod_ref, g_ref, x1_ref, h2_ref):
    ca = a_ref.shape[1]
    cb = b_ref.shape[1]
    m = mod_ref[...]
    half = a_ref.shape[0] // 2
    for r0 in (0, half):
        rows = slice(r0, r0 + half)
        acc = jnp.dot(a_ref[rows, :], w_ref[0:ca, :], preferred_element_type=F32)
        acc += jnp.dot(b_ref[rows, :], w_ref[ca:ca + cb, :], preferred_element_type=F32)
        acc += jnp.dot(f_ref[rows, :], w_ref[ca + cb:, :], preferred_element_type=F32)
        x1 = x_ref[rows, :] + m[2:3] * acc
        x1_ref[rows, :] = x1
        h = _rms(x1, g_ref[...])
        h2_ref[rows, :] = (h * (1.0 + m[4:5]) + m[3:4]).astype(BF16)


def _out_call(att_a, att_b, four, w_out_b, x2d, mod_l, cond_row_fn, norm_g, tm):
    t, d = x2d.shape
    ca, cb, cf = att_a.shape[1], att_b.shape[1], four.shape[1]
    w_out_b, w_layer = w_out_b
    return pl.pallas_call(
        _out_kernel,
        out_shape=[jax.ShapeDtypeStruct((t, d), F32), jax.ShapeDtypeStruct((t, d), BF16)],
        grid=(t // tm,),
        in_specs=[
            pl.BlockSpec((tm, ca), lambda i: (i, 0)),
            pl.BlockSpec((tm, cb), lambda i: (i, 0)),
            pl.BlockSpec((tm, cf), lambda i: (i, 0)),
            pl.BlockSpec((None, ca + cb + cf, d), lambda i: (w_layer, 0, 0), pipeline_mode=pl.Buffered(1)),
            pl.BlockSpec((tm, d), lambda i: (i, 0)),
            pl.BlockSpec((None, N_MOD, d), lambda i: (cond_row_fn(i), 0, 0)),
            pl.BlockSpec((1, d), lambda i: (0, 0)),
        ],
        out_specs=[pl.BlockSpec((tm, d), lambda i: (i, 0)), pl.BlockSpec((tm, d), lambda i: (i, 0))],
        compiler_params=_cparams(("parallel",)),
        name="out_proj",
    )(att_a, att_b, four, w_out_b, x2d, mod_l, norm_g)


HALO = 16
FFN_TF = 512


def _ffn_kernel(cw_ref, cb_ref, mod_ref, gf_ref, h_hbm, x1_hbm, wg_hbm, wu_hbm, wd_hbm,
                o_ref, hext_ref, x1_buf, wg_buf, wu_buf, wd_buf, sem, *, tm, tf, nj, layer, seq_len, final):
    i = pl.program_id(0)
    n_i = pl.num_programs(0)
    row0 = pl.multiple_of(i * tm, tm)

    def weight_copies(j, slot):
        cols = pl.ds(pl.multiple_of(j * tf, tf), tf)
        return (pltpu.make_async_copy(wg_hbm.at[layer, :, cols], wg_buf.at[slot], sem.at[0, slot]),
                pltpu.make_async_copy(wu_hbm.at[layer, :, cols], wu_buf.at[slot], sem.at[1, slot]),
                pltpu.make_async_copy(wd_hbm.at[layer, cols, :], wd_buf.at[slot], sem.at[2, slot]))

    before = pl.multiple_of(jnp.maximum(row0 - HALO, 0), HALO)
    after = pl.multiple_of(jnp.minimum(row0 + tm, n_i * tm - HALO), HALO)
    tile_copies = (
        pltpu.make_async_copy(h_hbm.at[pl.ds(before, HALO), :], hext_ref.at[pl.ds(0, HALO), :], sem.at[3, 0]),
        pltpu.make_async_copy(h_hbm.at[pl.ds(row0, tm), :], hext_ref.at[pl.ds(HALO, tm), :], sem.at[3, 1]),
        pltpu.make_async_copy(h_hbm.at[pl.ds(after, HALO), :], hext_ref.at[pl.ds(HALO + tm, HALO), :], sem.at[4, 0]),
    )
    x1_copy = pltpu.make_async_copy(x1_hbm.at[pl.ds(row0, tm), :], x1_buf, sem.at[4, 1])

    for cp in tile_copies:
        cp.start()
    x1_copy.start()

    @pl.when(i == 0)
    def _():
        for cp in weight_copies(0, 0):
            cp.start()

    o_ref[...] = jnp.zeros_like(o_ref)
    ext = tm + 2 * HALO
    pos = (i * tm + lax.broadcasted_iota(jnp.int32, (tm, 1), 0)) % seq_len
    for cp in tile_copies:
        cp.wait()

    def chunk(j, carry):
        step = i * nj + j
        slot = step % 2
        for cp in weight_copies(j, slot):
            cp.wait()

        @pl.when(step + 1 < n_i * nj)
        def _():
            for cp in weight_copies((j + 1) % nj, 1 - slot):
                cp.start()

        g = jnp.dot(hext_ref[...], wg_buf[slot], preferred_element_type=F32)
        u = jnp.dot(hext_ref[HALO:HALO + tm, :], wu_buf[slot], preferred_element_type=F32)
        g_prev = jnp.where(pos == 0, 0.0, pltpu.roll(g, 1, 0)[HALO:HALO + tm])
        g_next = jnp.where(pos == seq_len - 1, 0.0, pltpu.roll(g, ext - 1, 0)[HALO:HALO + tm])
        cw = cw_ref[j]
        gc = g_prev * cw[0:1] + g[HALO:HALO + tm] * cw[1:2] + g_next * cw[2:3] + cb_ref[j]
        act = (gc * jax.nn.sigmoid(gc)) * u
        o_ref[...] += jnp.dot(act.astype(BF16), wd_buf[slot], preferred_element_type=F32)
        return carry

    lax.fori_loop(0, nj, chunk, 0)

    x1_copy.wait()
    m = mod_ref[...]
    x2 = x1_buf[...] + m[5:6] * o_ref[...]
    if final:
        x2 = _rms(x2, gf_ref[...])
    o_ref[...] = x2


def _ffn_call(h2, x1, weights, conv_w, conv_b, mod_l, cond_row_fn, final_g, seq_len, final, tm, tf):
    t, d = x1.shape
    w_gate_b, w_up_b, w_down_b, w_layer = weights
    f = w_gate_b.shape[2]
    nj = f // tf
    cw3 = conv_w.reshape(3, nj, tf).transpose(1, 0, 2)
    cb3 = conv_b.reshape(nj, 1, tf)
    hbm = pl.BlockSpec(memory_space=pl.ANY)
    return pl.pallas_call(
        functools.partial(_ffn_kernel, tm=tm, tf=tf, nj=nj, layer=w_layer, seq_len=seq_len, final=final),
        out_shape=jax.ShapeDtypeStruct((t, d), F32),
        grid=(t // tm,),
        in_specs=[
            pl.BlockSpec((nj, 3, tf), lambda i: (0, 0, 0)),
            pl.BlockSpec((nj, 1, tf), lambda i: (0, 0, 0)),
            pl.BlockSpec((None, N_MOD, d), lambda i: (cond_row_fn(i), 0, 0)),
            pl.BlockSpec((1, d), lambda i: (0, 0)),
            hbm, hbm, hbm, hbm, hbm,
        ],
        out_specs=pl.BlockSpec((tm, d), lambda i: (i, 0)),
        scratch_shapes=[pltpu.VMEM((tm + 2 * HALO, d), BF16), pltpu.VMEM((tm, d), F32),
                        pltpu.VMEM((2, d, tf), BF16), pltpu.VMEM((2, d, tf), BF16), pltpu.VMEM((2, tf, d), BF16),
                        pltpu.SemaphoreType.DMA((5, 2))],
        compiler_params=_cparams(("arbitrary",)),
        name="conv_ffn",
    )(cw3, cb3, mod_l, final_g, h2, x1, w_gate_b, w_up_b, w_down_b)


def _rope_tables(n, head_dim):
    rows = n // GRID_W
    t_row = jnp.repeat(jnp.arange(rows, dtype=F32), GRID_W)
    t_col = jnp.tile(jnp.arange(GRID_W, dtype=F32), rows)
    axis_dim = head_dim // 2
    inv = jnp.power(ROPE_BASE, -jnp.arange(0, axis_dim, 2, dtype=F32) / axis_dim)
    ar = t_row[:, None] * inv[None, :]
    ac = t_col[:, None] * inv[None, :]
    ang = jnp.concatenate([ar, ar, ac, ac], axis=-1)
    reps = LANES // head_dim
    quarter = head_dim // 4
    sign = jnp.where((jnp.arange(head_dim) % (2 * quarter)) < quarter, -1.0, 1.0).astype(F32)
    cos = jnp.tile(jnp.cos(ang), (1, reps))
    sin_signed = jnp.tile(jnp.sin(ang) * sign[None, :], (1, reps))
    return cos, sin_signed


def kernel(x_prompt, x_sample, cache_attn_k, cache_attn_v, cache_diff_k, cache_diff_v, c, c_ctx, norm1_g, norm2_g, w_ada, b_ada, w_in, attn_q_norm_g, attn_k_norm_g, diff_lambda_q1, diff_lambda_k1, diff_lambda_q2, diff_lambda_k2, diff_subnorm_g, w_out, ffn_w_gate, ffn_w_up, ffn_conv_w, ffn_conv_b, ffn_w_down, final_norm_g):
    depth = w_in.shape[0]
    bc, lc, d = x_prompt.shape
    bl, ll, _ = x_sample.shape
    past = cache_attn_k.shape[2]

    w_in_b = w_in.astype(BF16)
    w_out_b = w_out.astype(BF16)
    w_gate_b = ffn_w_gate.astype(BF16)
    w_up_b = ffn_w_up.astype(BF16)
    w_down_b = ffn_w_down.astype(BF16)

    n_rows = 8 * ((1 + bl + 7) // 8)
    cvec = jnp.concatenate([c_ctx[None, :], c, jnp.zeros((n_rows - 1 - bl, d), F32)], axis=0)
    mod = _ada_call(cvec, w_ada, b_ada).reshape(depth, n_rows, N_MOD, d)

    dft_c = jnp.concatenate(_cos_sin(FOURIER_GROUP_DIM, FOURIER_GROUP_DIM, FOURIER_GROUP_DIM), axis=1).astype(BF16)
    rope_tabs = _rope_tables(ll, HEAD_DIM_A) + _rope_tables(ll, DK_B)
    caches = (cache_attn_k.reshape(bl, depth, past, N_KV_HEADS_A * HEAD_DIM_A),
              cache_attn_v.reshape(bl, depth, past, N_KV_HEADS_A * HEAD_DIM_A),
              cache_diff_k.reshape(bl, depth, past, N_HEADS_B * 2 * DK_B),
              cache_diff_v.reshape(bl, depth, past, N_HEADS_B * DV_B))
    lam_all = jnp.stack([diff_lambda_q1, diff_lambda_k1, diff_lambda_q2, diff_lambda_k2], axis=1)

    def run_pass(x3, is_ctx):
        b, n, _ = x3.shape
        t = b * n
        tm = 512
        tm_ffn = 1024
        x = x3.reshape(t, d)

        def cond_rows(tile):
            return (lambda i: 0) if is_ctx else (lambda i: 1 + (i * tile) // n)

        cond_row_fn = cond_rows(tm)
        new_kv = None
        for l in range(depth):
            lam_init = 0.8 - 0.6 * math.exp(-0.3 * l)
            res = _in_call(x, mod[l], cond_row_fn, norm1_g[l][None, :], (w_in_b, l),
                           attn_q_norm_g[l][None, :], attn_k_norm_g[l][None, :], dft_c,
                           None if is_ctx else rope_tabs, n, (depth, l, new_kv) if is_ctx else None, tm)
            qkv, y12 = res[0], res[1]
            if is_ctx:
                new_kv = res[2:]
            qkv3 = qkv.reshape(b, n, qkv.shape[1])
            att_a, att_b = _attn_calls(qkv3, None if is_ctx else caches, l, lam_all[l],
                                       diff_subnorm_g[l][None, :], lam_init,
                                       tq_a=min(512, n), tq_b=min(1024, n), kc=1024,
                                       ha=N_KV_HEADS_A if is_ctx else 1, hb=N_HEADS_B if is_ctx else 1,
                                       split_a=1 if is_ctx else 4, split_b=1 if is_ctx else 4)
            if is_ctx:
                four = _fourier_call(y12.reshape(b, n, y12.shape[1]), n, 1)
            else:
                four = _fourier_call(y12.reshape(b, n, y12.shape[1]), GRID_W, n // GRID_W)
            x1, h2 = _out_call(att_a.reshape(t, -1), att_b.reshape(t, -1), four.reshape(t, -1), (w_out_b, l),
                               x, mod[l], cond_row_fn, norm2_g[l][None, :], tm)
            x = _ffn_call(h2, x1, (w_gate_b, w_up_b, w_down_b, l), ffn_conv_w[l], ffn_conv_b[l][None, :],
                          mod[l], cond_rows(tm_ffn), final_norm_g[None, :], n, l == depth - 1, tm_ffn, FFN_TF)
        return x.reshape(b, n, d), new_kv

    y_prompt, kvs = run_pass(x_prompt, True)
    new_attn_k = kvs[0].reshape(bc, depth, lc, N_KV_HEADS_A, HEAD_DIM_A)
    new_attn_v = kvs[1].reshape(bc, depth, lc, N_KV_HEADS_A, HEAD_DIM_A)
    new_diff_k = kvs[2].reshape(bc, depth, lc, N_HEADS_B, 2 * DK_B)
    new_diff_v = kvs[3].reshape(bc, depth, lc, N_HEADS_B, DV_B)

    y_sample, _ = run_pass(x_sample, False)
    return (y_prompt, y_sample, new_attn_k, new_attn_v, new_diff_k, new_diff_v)
```

```python
import functools
import math

import jax
import jax.numpy as jnp
from jax import lax
from jax.experimental import pallas as pl
from jax.experimental.pallas import tpu as pltpu

F32 = jnp.float32
BF16 = jnp.bfloat16

GRID_W = 64
ROPE_BASE = 10000.0
NORM_EPS = 1e-6
HEAD_DIM_A = 128
N_KV_HEADS_A = 2
GQA_GROUP = 4
N_HEADS_B = 4
DK_B = 64
DV_B = 128
N_FOURIER_GROUPS = 4
FOURIER_GROUP_DIM = 128
N_MOD = 6

LOG2E = math.log2(math.e)
LANES = 128
BF16_SUBLANES = 16
VMEM_LIMIT = 58 * 1024 * 1024


def _cparams(sem, flags=None):
    return pltpu.CompilerParams(dimension_semantics=sem, vmem_limit_bytes=VMEM_LIMIT, flags=flags)


_ATTN_FLAGS = None


def _rms(x, g):
    return x * lax.rsqrt(jnp.mean(x * x, axis=-1, keepdims=True) + NORM_EPS) * g


def _ada_kernel(c_ref, w_ref, b_ref, o_ref):
    c = c_ref[...]
    s = (c * jax.nn.sigmoid(c)).astype(BF16)
    w = w_ref[...].astype(BF16)
    o_ref[...] = jnp.dot(s, w, preferred_element_type=F32) + b_ref[...]


def _ada_call(cvec, w_ada, b_ada):
    depth, d, n = w_ada.shape
    rows = cvec.shape[0]
    tn = 512
    return pl.pallas_call(
        _ada_kernel,
        out_shape=jax.ShapeDtypeStruct((depth, rows, n), F32),
        grid=(depth, n // tn),
        in_specs=[
            pl.BlockSpec((rows, d), lambda l, j: (0, 0)),
            pl.BlockSpec((None, d, tn), lambda l, j: (l, 0, j)),
            pl.BlockSpec((None, 1, tn), lambda l, j: (l, 0, j)),
        ],
        out_specs=pl.BlockSpec((None, rows, tn), lambda l, j: (l, 0, j)),
        compiler_params=_cparams(("parallel", "parallel")),
        name="ada_mod",
    )(cvec, w_ada, b_ada.reshape(depth, 1, n))


def _rope(xs, cos, sin_signed, shift):
    w = xs.shape[-1]
    lane = lax.broadcasted_iota(jnp.int32, xs.shape, 1)
    first = (lane % (2 * shift)) < shift
    rot = jnp.where(first, pltpu.roll(xs, w - shift, 1), pltpu.roll(xs, shift, 1))
    return xs * cos + rot * sin_signed


def _in_kernel(*refs, rope, ctx_out, n_alias, cols):
    it = iter(refs)
    x_ref, mod_ref, g_ref, w_ref, gq_ref, gk_ref, dft_ref = (next(it) for _ in range(7))
    if rope:
        cos_a, sin_a, cos_b, sin_b = (next(it)[...] for _ in range(4))
    for _ in range(n_alias):
        next(it)
    qkv_ref, y_ref = next(it), next(it)
    if ctx_out:
        ka_ref, va_ref, kb_ref, vb_ref = (next(it) for _ in range(4))
    c_qa, c_ka, c_va, c_qb, c_kb, c_vb, c_f, c_end = cols

    def put_cache(ref, off, v):
        head = off // LANES
        seqs, rows = ref.shape[0], ref.shape[-2]
        heads = rows * seqs // v.shape[0]
        n = rows // heads
        for s in range(seqs):
            vs = v[s * n:(s + 1) * n, :]
            if len(ref.shape) == 4:
                for dd in range(ref.shape[1]):
                    ref[s, dd, pl.ds(head, n, stride=heads), :] = vs
            else:
                ref[s, pl.ds(head, n, stride=heads), :] = vs

    m = mod_ref[...]
    h = _rms(x_ref[...], g_ref[...])
    hb = (h * (1.0 + m[1:2]) + m[0:1]).astype(BF16)
    gq = gq_ref[...]
    gk = gk_ref[...]
    scale_a = HEAD_DIM_A ** -0.5 * LOG2E
    scale_b = DK_B ** -0.5 * LOG2E
    chunk = 4 * LANES

    for c0 in range(0, c_end, chunk):
        acc = jnp.dot(hb, w_ref[:, c0:c0 + chunk], preferred_element_type=F32)
        for s in range(chunk // LANES):
            col = c0 + s * LANES
            v = acc[:, s * LANES:(s + 1) * LANES]
            if col < c_ka:
                v = _rms(v, gq)
                if rope:
                    v = _rope(v, cos_a, sin_a, HEAD_DIM_A // 4)
                qkv_ref[:, col:col + LANES] = (v * scale_a).astype(BF16)
            elif col < c_va:
                v = _rms(v, gk)
                if ctx_out:
                    put_cache(ka_ref, col - c_ka, v)
                if rope:
                    v = _rope(v, cos_a, sin_a, HEAD_DIM_A // 4)
                qkv_ref[:, col:col + LANES] = v.astype(BF16)
            elif col < c_qb:
                if ctx_out:
                    put_cache(va_ref, col - c_va, v)
                qkv_ref[:, col:col + LANES] = v.astype(BF16)
            elif col < c_kb:
                if rope:
                    v = _rope(v, cos_b, sin_b, DK_B // 4)
                qkv_ref[:, col:col + LANES] = (v * scale_b).astype(BF16)
            elif col < c_vb:
                if ctx_out:
                    put_cache(kb_ref, col - c_kb, v)
                if rope:
                    v = _rope(v, cos_b, sin_b, DK_B // 4)
                qkv_ref[:, col:col + LANES] = v.astype(BF16)
            elif col < c_f:
                if ctx_out:
                    put_cache(vb_ref, col - c_vb, v)
                qkv_ref[:, col:col + LANES] = v.astype(BF16)
            else:
                yy = jnp.dot(v.astype(BF16), dft_ref[...], preferred_element_type=F32)
                gcol = col - c_f
                half = c_end - c_f
                y_ref[:, gcol:gcol + LANES] = yy[:, :LANES].astype(BF16)
                y_ref[:, half + gcol:half + gcol + LANES] = yy[:, LANES:].astype(BF16)


def _in_call(x2d, mod_l, cond_row_fn, norm_g, w_in_b, gq, gk, dft_c, rope_tabs, seq_len, new_cache, tm):
    ctx_out = new_cache is not None
    t, d = x2d.shape
    w_in_b, w_layer = w_in_b
    d_in = w_in_b.shape[2]
    c_qa = 0
    c_ka = N_KV_HEADS_A * GQA_GROUP * HEAD_DIM_A
    c_va = c_ka + N_KV_HEADS_A * HEAD_DIM_A
    c_qb = c_va + N_KV_HEADS_A * HEAD_DIM_A
    c_kb = c_qb + N_HEADS_B * 2 * DK_B
    c_vb = c_kb + N_HEADS_B * 2 * DK_B
    c_f = c_vb + N_HEADS_B * DV_B
    c_end = c_f + N_FOURIER_GROUPS * FOURIER_GROUP_DIM
    assert c_end == d_in
    cols = (c_qa, c_ka, c_va, c_qb, c_kb, c_vb, c_f, c_end)
    rope = rope_tabs is not None
    n_f = c_end - c_f

    in_specs = [
        pl.BlockSpec((tm, d), lambda i: (i, 0)),
        pl.BlockSpec((None, N_MOD, d), lambda i: (cond_row_fn(i), 0, 0)),
        pl.BlockSpec((1, d), lambda i: (0, 0)),
        pl.BlockSpec((None, d, d_in), lambda i: (w_layer, 0, 0), pipeline_mode=pl.Buffered(1)),
        pl.BlockSpec((1, HEAD_DIM_A), lambda i: (0, 0)),
        pl.BlockSpec((1, HEAD_DIM_A), lambda i: (0, 0)),
        pl.BlockSpec((FOURIER_GROUP_DIM, 2 * FOURIER_GROUP_DIM), lambda i: (0, 0)),
    ]
    args = [x2d, mod_l, norm_g, w_in_b, gq, gk, dft_c]
    if rope:
        nblk = seq_len // tm
        for tab in rope_tabs:
            in_specs.append(pl.BlockSpec((tm, LANES), lambda i: (i % nblk, 0)))
            args.append(tab)
    out_shape = [jax.ShapeDtypeStruct((t, c_f), BF16), jax.ShapeDtypeStruct((t, 2 * n_f), BF16)]
    out_specs = [pl.BlockSpec((tm, c_f), lambda i: (i, 0)), pl.BlockSpec((tm, 2 * n_f), lambda i: (i, 0))]
    aliases = {}
    n_alias = 0
    if ctx_out:
        depth, layer, prev = new_cache
        nb = t // seq_len
        spt = tm // seq_len
        for width in (c_va - c_ka, c_qb - c_va, c_vb - c_kb, c_f - c_vb):
            rows = seq_len * (width // LANES)
            out_shape.append(jax.ShapeDtypeStruct((nb, depth, rows, LANES), F32))
            if prev is None:
                out_specs.append(pl.BlockSpec((spt, depth, rows, LANES), lambda i: (i, 0, 0, 0)))
            else:
                out_specs.append(pl.BlockSpec((spt, None, rows, LANES), lambda i: (i, layer, 0, 0)))
        if prev is not None:
            n_alias = len(prev)
            for a, arr in enumerate(prev):
                aliases[len(args)] = 2 + a
                in_specs.append(pl.BlockSpec(memory_space=pl.ANY))
                args.append(arr)
    return pl.pallas_call(
        functools.partial(_in_kernel, rope=rope, ctx_out=ctx_out, n_alias=n_alias, cols=cols),
        out_shape=out_shape,
        grid=(t // tm,),
        in_specs=in_specs,
        out_specs=out_specs,
        input_output_aliases=aliases,
        compiler_params=_cparams(("parallel",)),
        name="in_proj_ctx" if ctx_out else "in_proj_lat",
    )(*args)


def _softmax_pv(q, srcs):
    chunks = []
    for k_ref, v_ref, n_keys, kc, col in srcs:
        for c in range(n_keys // kc):
            chunks.append((k_ref, v_ref, c * kc, kc, col))

    def scores(ch):
        k_ref, _, off, kc, col = ch
        k = k_ref[off:off + kc, col:col + LANES].astype(BF16)
        return lax.dot_general(q, k, (((1,), (1,)), ((), ())), preferred_element_type=F32)

    def lane_fold(x, op):
        out = x[:, 0:LANES]
        for t in range(1, x.shape[1] // LANES):
            out = op(out, x[:, t * LANES:(t + 1) * LANES])
        return out

    m = l_part = acc = None
    s_next = scores(chunks[0])
    for i, ch in enumerate(chunks):
        s = s_next
        if i + 1 < len(chunks):
            s_next = scores(chunks[i + 1])
        _, v_ref, off, kc, col = ch
        row_max = jnp.max(lane_fold(s, jnp.maximum), axis=-1, keepdims=True)
        m_new = row_max if m is None else jnp.maximum(m, row_max)
        p = jnp.exp2(s - m_new)
        p_sum = lane_fold(p, jnp.add)
        v = v_ref[off:off + kc, col:col + LANES].astype(BF16)
        pv = jnp.dot(p.astype(BF16), v, preferred_element_type=F32)
        if m is None:
            l_part, acc = p_sum, pv
        else:
            alpha = jnp.exp2(m - m_new)
            l_part = alpha * l_part + p_sum
            acc = alpha * acc + pv
        m = m_new
    return acc / jnp.sum(l_part, axis=-1, keepdims=True)


def _srcs(cache_refs, k_ref, v_ref, kc, head):
    srcs = []
    col = head * LANES
    if cache_refs is not None:
        ck, cv = cache_refs
        srcs.append((ck, cv, ck.shape[0], ck.shape[0], col))
    n = k_ref.shape[0]
    srcs.append((k_ref, v_ref, n, min(kc, n), col))
    return srcs


def _attn_a_kernel(*refs, has_cache, tq, kc, heads, split):
    if has_cache:
        q_ref, ck_ref, cv_ref, k_ref, v_ref, o_ref = refs
        cache = (ck_ref, cv_ref)
    else:
        q_ref, k_ref, v_ref, o_ref = refs
        cache = None
    per = GQA_GROUP // split
    for bb in range(q_ref.shape[0]):
        qb, kb, vb, ob = q_ref.at[bb], k_ref.at[bb], v_ref.at[bb], o_ref.at[bb]
        for kh in range(heads):
            for part in range(split):
                cols = [(kh * GQA_GROUP + part * per + h) * LANES for h in range(per)]
                qs = jnp.concatenate([qb[:, c:c + LANES] for c in cols], axis=0)
                o = _softmax_pv(qs, _srcs(cache, kb, vb, kc, kh))
                for h, c in enumerate(cols):
                    ob[:, c:c + LANES] = o[h * tq:(h + 1) * tq].astype(BF16)


def _attn_b_kernel(*refs, has_cache, tq, kc, heads, split, lam_init):
    if has_cache:
        lam_ref, gs_ref, q_ref, ck_ref, cv_ref, k_ref, v_ref, o_ref = refs
        cache = (ck_ref, cv_ref)
    else:
        lam_ref, gs_ref, q_ref, k_ref, v_ref, o_ref = refs
        cache = None
    lp = lam_ref[...]
    lam = (jnp.exp(jnp.sum(lp[0:1] * lp[1:2], axis=-1, keepdims=True))
           - jnp.exp(jnp.sum(lp[2:3] * lp[3:4], axis=-1, keepdims=True)) + lam_init)
    rows = tq // split
    for bb in range(q_ref.shape[0]):
        qb, kb, vb, ob = q_ref.at[bb], k_ref.at[bb], v_ref.at[bb], o_ref.at[bb]
        for hd in range(heads):
            for part in range(split):
                q = qb[part * rows:(part + 1) * rows, hd * LANES:(hd + 1) * LANES]
                lane = lax.broadcasted_iota(jnp.int32, q.shape, 1)
                zero = jnp.zeros_like(q)
                qz = jnp.concatenate([jnp.where(lane < DK_B, q, zero), jnp.where(lane >= DK_B, q, zero)], axis=0)
                o = _softmax_pv(qz, _srcs(cache, kb, vb, kc, hd))
                dlt = o[:rows] - lam * o[rows:]
                ob[part * rows:(part + 1) * rows, hd * LANES:(hd + 1) * LANES] = (
                    _rms(dlt, gs_ref[...]) * (1.0 - lam_init)).astype(BF16)


def _attn_calls(qkv3, caches, layer, lam_params, g_sub, lam_init, tq_a, tq_b, kc, ha, hb, split_a, split_b, nb):
    b, n, _ = qkv3.shape
    has_cache = caches is not None
    assert b % nb == 0 and (nb == 1 or not has_cache)
    qa_blk =GQA_GROUP * HEAD_DIM_A // LANES
    k_a0 = N_KV_HEADS_A * qa_blk
    v_a0 = k_a0 + N_KV_HEADS_A
    q_b0 = v_a0 + N_KV_HEADS_A
    k_b0 = q_b0 + N_HEADS_B
    v_b0 = k_b0 + N_HEADS_B
    assert all(x % ha == 0 for x in (N_KV_HEADS_A, k_a0, v_a0)) and all(x % hb == 0 for x in (N_HEADS_B, q_b0, k_b0, v_b0))

    qw, kw = ha * GQA_GROUP * LANES, ha * LANES
    in_specs = [pl.BlockSpec((nb, tq_a, qw), lambda bi, h, i: (bi, i, h))]
    args = [qkv3]
    if has_cache:
        ck, cv = caches[0], caches[1]
        p = ck.shape[2]
        in_specs += [pl.BlockSpec((None, None, p, kw), lambda bi, h, i: (bi, layer, 0, h))] * 2
        args += [ck, cv]
    in_specs += [pl.BlockSpec((nb, n, kw), lambda bi, h, i: (bi, 0, k_a0 // ha + h)),
                 pl.BlockSpec((nb, n, kw), lambda bi, h, i: (bi, 0, v_a0 // ha + h))]
    args += [qkv3, qkv3]
    att_a = pl.pallas_call(
        functools.partial(_attn_a_kernel, has_cache=has_cache, tq=tq_a, kc=kc, heads=ha, split=split_a),
        out_shape=jax.ShapeDtypeStruct((b, n, N_KV_HEADS_A * GQA_GROUP * HEAD_DIM_A), BF16),
        grid=(b // nb, N_KV_HEADS_A // ha, n // tq_a),
        in_specs=in_specs,
        out_specs=pl.BlockSpec((nb, tq_a, qw), lambda bi, h, i: (bi, i, h)),
        compiler_params=_cparams(("parallel", "parallel", "arbitrary"), _ATTN_FLAGS),
        name="attn_a_lat" if has_cache else "attn_a_ctx",
    )(*args)

    bw = hb * LANES
    in_specs = [pl.BlockSpec((4, DK_B), lambda bi, h, i: (0, 0)),
                pl.BlockSpec((1, DV_B), lambda bi, h, i: (0, 0)),
                pl.BlockSpec((nb, tq_b, bw), lambda bi, h, i: (bi, i, q_b0 // hb + h))]
    args = [lam_params, g_sub, qkv3]
    if has_cache:
        ck, cv = caches[2], caches[3]
        p = ck.shape[2]
        in_specs += [pl.BlockSpec((None, None, p, bw), lambda bi, h, i: (bi, layer, 0, h))] * 2
        args += [ck, cv]
    in_specs += [pl.BlockSpec((nb, n, bw), lambda bi, h, i: (bi, 0, k_b0 // hb + h)),
                 pl.BlockSpec((nb, n, bw), lambda bi, h, i: (bi, 0, v_b0 // hb + h))]
    args += [qkv3, qkv3]
    att_b = pl.pallas_call(
        functools.partial(_attn_b_kernel, has_cache=has_cache, tq=tq_b, kc=kc, heads=hb, split=split_b, lam_init=lam_init),
        out_shape=jax.ShapeDtypeStruct((b, n, N_HEADS_B * DV_B), BF16),
        grid=(b // nb, N_HEADS_B // hb, n // tq_b),
        in_specs=in_specs,
        out_specs=pl.BlockSpec((nb, tq_b, bw), lambda bi, h, i: (bi, i, h)),
        compiler_params=_cparams(("parallel", "parallel", "arbitrary"), _ATTN_FLAGS),
        name="attn_b_lat" if has_cache else "attn_b_ctx",
    )(*args)
    return att_a, att_b


def _dft_kernel(x_ref, mat_ref, *rest, width, scale):
    p, g, w2 = x_ref.shape
    x = x_ref[...].reshape(p * g, w2)
    xs = jnp.concatenate([x[:, :width], x[:, width:]], axis=0)
    u = jnp.dot(mat_ref[...], xs, preferred_element_type=F32)
    if len(rest) == 3:
        tc_ref, ts_ref, o_ref = rest
        half = u.shape[0] // 2
        ur, ui = u[:half], u[half:]
        reps = width // LANES
        tc = jnp.concatenate([tc_ref[...].reshape(half, LANES)] * reps, axis=1)
        ts = jnp.concatenate([ts_ref[...].reshape(half, LANES)] * reps, axis=1)
        o_ref[:, :, :width] = (ur * tc - ui * ts).astype(BF16).reshape(o_ref.shape[0], o_ref.shape[1], width)
        o_ref[:, :, width:] = (ur * ts + ui * tc).astype(BF16).reshape(o_ref.shape[0], o_ref.shape[1], width)
    else:
        (o_ref,) = rest
        o_ref[...] = (u * scale).astype(BF16).reshape(o_ref.shape)


def _dft_seq_kernel(x_ref, mat_ref, o_ref, *, width, scale):
    nb = x_ref.shape[0]
    xs = jnp.concatenate(
        [jnp.concatenate([x_ref[bb, :, :width], x_ref[bb, :, width:]], axis=0) for bb in range(nb)], axis=1)
    u = jnp.dot(mat_ref[...], xs, preferred_element_type=F32)
    for bb in range(nb):
        o_ref[bb] = (u[:, bb * width:(bb + 1) * width] * scale).astype(BF16)


def _cos_sin(n_rows, n_cols, period):
    a = jnp.arange(n_rows, dtype=jnp.int32)[:, None]
    b = jnp.arange(n_cols, dtype=jnp.int32)[None, :]
    ang = ((a * b) % period).astype(F32) * (2.0 * math.pi / period)
    return jnp.cos(ang), jnp.sin(ang)


def _fourier_call(y3, n1, n2):
    b, n, w2 = y3.shape
    width = w2 // 2
    scale = 1.0 / math.sqrt(n * FOURIER_GROUP_DIM)
    c1, s1 = _cos_sin(n1, n1, n1)
    w_real = jnp.stack([c1, -s1], axis=1)
    if n2 == 1:
        mat = w_real.reshape(n1, 2 * n1).astype(BF16)
        nb = math.gcd(b, 4)
        return pl.pallas_call(
            functools.partial(_dft_seq_kernel, width=width, scale=scale),
            out_shape=jax.ShapeDtypeStruct((b, n, width), BF16),
            grid=(b // nb,),
            in_specs=[pl.BlockSpec((nb, n, w2), lambda bi: (bi, 0, 0)),
                      pl.BlockSpec((n1, 2 * n1), lambda bi: (0, 0))],
            out_specs=pl.BlockSpec((nb, n, width), lambda bi: (bi, 0, 0)),
            compiler_params=_cparams(("parallel",)),
            name="dft_ctx",
        )(y3, mat)

    g = BF16_SUBLANES
    c2, s2 = _cos_sin(n2, n2, n2)
    w_cplx = jnp.stack([jnp.stack([c2, -s2], axis=1), jnp.stack([s2, c2], axis=1)], axis=0)

    def kron_cols(base, row_j):
        rows, cols = base.shape
        col = lax.broadcasted_iota(jnp.int32, (cols, cols * g), 1)
        expand = (col // g == lax.broadcasted_iota(jnp.int32, (cols, cols * g), 0)).astype(BF16)
        wide = jnp.dot(base.astype(BF16), expand, preferred_element_type=F32)
        keep = row_j[:, None] == (lax.broadcasted_iota(jnp.int32, (rows, cols * g), 1) % g)
        return jnp.where(keep, wide, 0.0).astype(BF16)

    base1 = jnp.broadcast_to(w_cplx.reshape(2, 1, n2, 2 * n2), (2, g, n2, 2 * n2)).reshape(2 * g * n2, 2 * n2)
    mat1 = kron_cols(base1, (jnp.arange(2 * g * n2, dtype=jnp.int32) // n2) % g)
    base2 = jnp.broadcast_to(w_real.reshape(n1, 1, 2 * n1), (n1, g, 2 * n1)).reshape(n1 * g, 2 * n1)
    mat2 = kron_cols(base2, jnp.arange(n1 * g, dtype=jnp.int32) % g)
    tc, ts = _cos_sin(n1, n2, n)
    tc = jnp.broadcast_to(tc[:, :, None], (n1, n2, LANES))
    ts = jnp.broadcast_to(ts[:, :, None], (n1, n2, LANES))
    t = pl.pallas_call(
        functools.partial(_dft_kernel, width=width, scale=None),
        out_shape=jax.ShapeDtypeStruct((b, n1, n2, w2), BF16),
        grid=(b, n1 // g),
        in_specs=[pl.BlockSpec((None, n2, g, w2), lambda bi, j: (bi, 0, j, 0)),
                  pl.BlockSpec(mat1.shape, lambda bi, j: (0, 0), pipeline_mode=pl.Buffered(1)),
                  pl.BlockSpec((g, n2, LANES), lambda bi, j: (j, 0, 0)),
                  pl.BlockSpec((g, n2, LANES), lambda bi, j: (j, 0, 0))],
        out_specs=pl.BlockSpec((None, g, n2, w2), lambda bi, j: (bi, j, 0, 0)),
        compiler_params=_cparams(("parallel", "parallel")),
        name="dft_stage1",
    )(y3.reshape(b, n2, n1, w2), mat1, tc, ts)
    out = pl.pallas_call(
        functools.partial(_dft_kernel, width=width, scale=scale),
        out_shape=jax.ShapeDtypeStruct((b, n1, n2, width), BF16),
        grid=(b, n2 // g),
        in_specs=[pl.BlockSpec((None, n1, g, w2), lambda bi, j: (bi, 0, j, 0)),
                  pl.BlockSpec(mat2.shape, lambda bi, j: (0, 0), pipeline_mode=pl.Buffered(1))],
        out_specs=pl.BlockSpec((None, n1, g, width), lambda bi, j: (bi, 0, j, 0)),
        compiler_params=_cparams(("parallel", "parallel")),
        name="dft_stage2",
    )(t, mat2)
    return out.reshape(b, n, width)


def _out_kernel(a_ref, b_ref, f_ref, w_ref, x_ref, mod_ref, g_ref, x1_ref, h2_ref):
    ca = a_ref.shape[1]
    cb = b_ref.shape[1]
    m = mod_ref[...]
    half = a_ref.shape[0] // 2
    for r0 in (0, half):
        rows = slice(r0, r0 + half)
        acc = jnp.dot(a_ref[rows, :], w_ref[0:ca, :], preferred_element_type=F32)
        acc += jnp.dot(b_ref[rows, :], w_ref[ca:ca + cb, :], preferred_element_type=F32)
        acc += jnp.dot(f_ref[rows, :], w_ref[ca + cb:, :], preferred_element_type=F32)
        x1 = x_ref[rows, :] + m[2:3] * acc
        x1_ref[rows, :] = x1
        h = _rms(x1, g_ref[...])
        h2_ref[rows, :] = (h * (1.0 + m[4:5]) + m[3:4]).astype(BF16)


def _out_call(att_a, att_b, four, w_out_b, x2d, mod_l, cond_row_fn, norm_g, tm):
    t, d = x2d.shape
    ca, cb, cf = att_a.shape[1], att_b.shape[1], four.shape[1]
    w_out_b, w_layer = w_out_b
    return pl.pallas_call(
        _out_kernel,
        out_shape=[jax.ShapeDtypeStruct((t, d), F32), jax.ShapeDtypeStruct((t, d), BF16)],
        grid=(t // tm,),
        in_specs=[
            pl.BlockSpec((tm, ca), lambda i: (i, 0)),
            pl.BlockSpec((tm, cb), lambda i: (i, 0)),
            pl.BlockSpec((tm, cf), lambda i: (i, 0)),
            pl.BlockSpec((None, ca + cb + cf, d), lambda i: (w_layer, 0, 0), pipeline_mode=pl.Buffered(1)),
            pl.BlockSpec((tm, d), lambda i: (i, 0)),
            pl.BlockSpec((None, N_MOD, d), lambda i: (cond_row_fn(i), 0, 0)),
            pl.BlockSpec((1, d), lambda i: (0, 0)),
        ],
        out_specs=[pl.BlockSpec((tm, d), lambda i: (i, 0)), pl.BlockSpec((tm, d), lambda i: (i, 0))],
        compiler_params=_cparams(("parallel",)),
        name="out_proj",
    )(att_a, att_b, four, w_out_b, x2d, mod_l, norm_g)


HALO = 16
FFN_TF = 512


def _ffn_kernel(cw_ref, cb_ref, mod_ref, gf_ref, h_hbm, x1_hbm, wg_hbm, wu_hbm, wd_hbm,
                o_ref, hext_ref, x1_buf, wg_buf, wu_buf, wd_buf, sem, *, tm, tf, nj, layer, seq_len, final):
    i = pl.program_id(0)
    n_i = pl.num_programs(0)
    row0 = pl.multiple_of(i * tm, tm)

    def weight_copies(j, slot):
        cols = pl.ds(pl.multiple_of(j * tf, tf), tf)
        return (pltpu.make_async_copy(wg_hbm.at[layer, :, cols], wg_buf.at[slot], sem.at[0, slot]),
                pltpu.make_async_copy(wu_hbm.at[layer, :, cols], wu_buf.at[slot], sem.at[1, slot]),
                pltpu.make_async_copy(wd_hbm.at[layer, cols, :], wd_buf.at[slot], sem.at[2, slot]))

    before = pl.multiple_of(jnp.maximum(row0 - HALO, 0), HALO)
    after = pl.multiple_of(jnp.minimum(row0 + tm, n_i * tm - HALO), HALO)
    tile_copies = (
        pltpu.make_async_copy(h_hbm.at[pl.ds(before, HALO), :], hext_ref.at[pl.ds(0, HALO), :], sem.at[3, 0]),
        pltpu.make_async_copy(h_hbm.at[pl.ds(row0, tm), :], hext_ref.at[pl.ds(HALO, tm), :], sem.at[3, 1]),
        pltpu.make_async_copy(h_hbm.at[pl.ds(after, HALO), :], hext_ref.at[pl.ds(HALO + tm, HALO), :], sem.at[4, 0]),
    )
    x1_copy = pltpu.make_async_copy(x1_hbm.at[pl.ds(row0, tm), :], x1_buf, sem.at[4, 1])

    for cp in tile_copies:
        cp.start()
    x1_copy.start()

    @pl.when(i == 0)
    def _():
        for cp in weight_copies(0, 0):
            cp.start()

    o_ref[...] = jnp.zeros_like(o_ref)
    ext = tm + 2 * HALO
    pos = (i * tm + lax.broadcasted_iota(jnp.int32, (tm, 1), 0)) % seq_len
    for cp in tile_copies:
        cp.wait()

    def chunk(j, carry):
        step = i * nj + j
        slot = step % 2
        for cp in weight_copies(j, slot):
            cp.wait()

        @pl.when(step + 1 < n_i * nj)
        def _():
            for cp in weight_copies((j + 1) % nj, 1 - slot):
                cp.start()

        g = jnp.dot(hext_ref[...], wg_buf[slot], preferred_element_type=F32)
        u = jnp.dot(hext_ref[HALO:HALO + tm, :], wu_buf[slot], preferred_element_type=F32)
        g_prev = jnp.where(pos == 0, 0.0, pltpu.roll(g, 1, 0)[HALO:HALO + tm])
        g_next = jnp.where(pos == seq_len - 1, 0.0, pltpu.roll(g, ext - 1, 0)[HALO:HALO + tm])
        cw = cw_ref[j]
        gc = g_prev * cw[0:1] + g[HALO:HALO + tm] * cw[1:2] + g_next * cw[2:3] + cb_ref[j]
        act = (gc * jax.nn.sigmoid(gc)) * u
        o_ref[...] += jnp.dot(act.astype(BF16), wd_buf[slot], preferred_element_type=F32)
        return carry

    lax.fori_loop(0, nj, chunk, 0)

    x1_copy.wait()
    m = mod_ref[...]
    x2 = x1_buf[...] + m[5:6] * o_ref[...]
    if final:
        x2 = _rms(x2, gf_ref[...])
    o_ref[...] = x2


def _ffn_call(h2, x1, weights, conv_w, conv_b, mod_l, cond_row_fn, final_g, seq_len, final, tm, tf):
    t, d = x1.shape
    w_gate_b, w_up_b, w_down_b, w_layer = weights
    f = w_gate_b.shape[2]
    nj = f // tf
    cw3 = conv_w.reshape(3, nj, tf).transpose(1, 0, 2)
    cb3 = conv_b.reshape(nj, 1, tf)
    hbm = pl.BlockSpec(memory_space=pl.ANY)
    return pl.pallas_call(
        functools.partial(_ffn_kernel, tm=tm, tf=tf, nj=nj, layer=w_layer, seq_len=seq_len, final=final),
        out_shape=jax.ShapeDtypeStruct((t, d), F32),
        grid=(t // tm,),
        in_specs=[
            pl.BlockSpec((nj, 3, tf), lambda i: (0, 0, 0)),
            pl.BlockSpec((nj, 1, tf), lambda i: (0, 0, 0)),
            pl.BlockSpec((None, N_MOD, d), lambda i: (cond_row_fn(i), 0, 0)),
            pl.BlockSpec((1, d), lambda i: (0, 0)),
            hbm, hbm, hbm, hbm, hbm,
        ],
        out_specs=pl.BlockSpec((tm, d), lambda i: (i, 0)),
        scratch_shapes=[pltpu.VMEM((tm + 2 * HALO, d), BF16), pltpu.VMEM((tm, d), F32),
                        pltpu.VMEM((2, d, tf), BF16), pltpu.VMEM((2, d, tf), BF16), pltpu.VMEM((2, tf, d), BF16),
                        pltpu.SemaphoreType.DMA((5, 2))],
        compiler_params=_cparams(("arbitrary",)),
        name="conv_ffn",
    )(cw3, cb3, mod_l, final_g, h2, x1, w_gate_b, w_up_b, w_down_b)


def _rope_tables(n, head_dim):
    rows = n // GRID_W
    t_row = jnp.repeat(jnp.arange(rows, dtype=F32), GRID_W)
    t_col = jnp.tile(jnp.arange(GRID_W, dtype=F32), rows)
    axis_dim = head_dim // 2
    inv = jnp.power(ROPE_BASE, -jnp.arange(0, axis_dim, 2, dtype=F32) / axis_dim)
    ar = t_row[:, None] * inv[None, :]
    ac = t_col[:, None] * inv[None, :]
    ang = jnp.concatenate([ar, ar, ac, ac], axis=-1)
    reps = LANES // head_dim
    quarter = head_dim // 4
    sign = jnp.where((jnp.arange(head_dim) % (2 * quarter)) < quarter, -1.0, 1.0).astype(F32)
    cos = jnp.tile(jnp.cos(ang), (1, reps))
    sin_signed = jnp.tile(jnp.sin(ang) * sign[None, :], (1, reps))
    return cos, sin_signed


def kernel(x_prompt, x_sample, cache_attn_k, cache_attn_v, cache_diff_k, cache_diff_v, c, c_ctx, norm1_g, norm2_g, w_ada, b_ada, w_in, attn_q_norm_g, attn_k_norm_g, diff_lambda_q1, diff_lambda_k1, diff_lambda_q2, diff_lambda_k2, diff_subnorm_g, w_out, ffn_w_gate, ffn_w_up, ffn_conv_w, ffn_conv_b, ffn_w_down, final_norm_g):
    depth = w_in.shape[0]
    bc, lc, d = x_prompt.shape
    bl, ll, _ = x_sample.shape
    past = cache_attn_k.shape[2]

    w_in_b = w_in.astype(BF16)
    w_out_b = w_out.astype(BF16)
    w_gate_b = ffn_w_gate.astype(BF16)
    w_up_b = ffn_w_up.astype(BF16)
    w_down_b = ffn_w_down.astype(BF16)

    n_rows = 8 * ((1 + bl + 7) // 8)
    cvec = jnp.concatenate([c_ctx[None, :], c, jnp.zeros((n_rows - 1 - bl, d), F32)], axis=0)
    mod = _ada_call(cvec, w_ada, b_ada).reshape(depth, n_rows, N_MOD, d)

    dft_c = jnp.concatenate(_cos_sin(FOURIER_GROUP_DIM, FOURIER_GROUP_DIM, FOURIER_GROUP_DIM), axis=1).astype(BF16)
    rope_tabs = _rope_tables(ll, HEAD_DIM_A) + _rope_tables(ll, DK_B)
    caches = (cache_attn_k.reshape(bl, depth, past, N_KV_HEADS_A * HEAD_DIM_A),
              cache_attn_v.reshape(bl, depth, past, N_KV_HEADS_A * HEAD_DIM_A),
              cache_diff_k.reshape(bl, depth, past, N_HEADS_B * 2 * DK_B),
              cache_diff_v.reshape(bl, depth, past, N_HEADS_B * DV_B))
    lam_all = jnp.stack([diff_lambda_q1, diff_lambda_k1, diff_lambda_q2, diff_lambda_k2], axis=1)

    def run_pass(x3, is_ctx):
        b, n, _ = x3.shape
        t = b * n
        tm = 512
        tm_ffn = 1024
        x = x3.reshape(t, d)

        def cond_rows(tile):
            return (lambda i: 0) if is_ctx else (lambda i: 1 + (i * tile) // n)

        cond_row_fn = cond_rows(tm)
        new_kv = None
        for l in range(depth):
            lam_init = 0.8 - 0.6 * math.exp(-0.3 * l)
            res = _in_call(x, mod[l], cond_row_fn, norm1_g[l][None, :], (w_in_b, l),
                           attn_q_norm_g[l][None, :], attn_k_norm_g[l][None, :], dft_c,
                           None if is_ctx else rope_tabs, n, (depth, l, new_kv) if is_ctx else None, tm)
            qkv, y12 = res[0], res[1]
            if is_ctx:
                new_kv = res[2:]
            qkv3 = qkv.reshape(b, n, qkv.shape[1])
            att_a, att_b = _attn_calls(qkv3, None if is_ctx else caches, l, lam_all[l],
                                       diff_subnorm_g[l][None, :], lam_init,
                                       tq_a=min(512, n), tq_b=min(1024, n), kc=1024,
                                       ha=N_KV_HEADS_A if is_ctx else 1, hb=N_HEADS_B if is_ctx else 1,
                                       split_a=1 if is_ctx else 4, split_b=1 if is_ctx else 4,
                                       nb=4 if is_ctx else 1)
            if is_ctx:
                four = _fourier_call(y12.reshape(b, n, y12.shape[1]), n, 1)
            else:
                four = _fourier_call(y12.reshape(b, n, y12.shape[1]), GRID_W, n // GRID_W)
            x1, h2 = _out_call(att_a.reshape(t, -1), att_b.reshape(t, -1), four.reshape(t, -1), (w_out_b, l),
                               x, mod[l], cond_row_fn, norm2_g[l][None, :], tm)
            x = _ffn_call(h2, x1, (w_gate_b, w_up_b, w_down_b, l), ffn_conv_w[l], ffn_conv_b[l][None, :],
                          mod[l], cond_rows(tm_ffn), final_norm_g[None, :], n, l == depth - 1, tm_ffn, FFN_TF)
        return x.reshape(b, n, d), new_kv

    y_prompt, kvs = run_pass(x_prompt, True)
    new_attn_k = kvs[0].reshape(bc, depth, lc, N_KV_HEADS_A, HEAD_DIM_A)
    new_attn_v = kvs[1].reshape(bc, depth, lc, N_KV_HEADS_A, HEAD_DIM_A)
    new_diff_k = kvs[2].reshape(bc, depth, lc, N_HEADS_B, 2 * DK_B)
    new_diff_v = kvs[3].reshape(bc, depth, lc, N_HEADS_B, DV_B)

    y_sample, _ = run_pass(x_sample, False)
    return (y_prompt, y_sample, new_attn_k, new_attn_v, new_diff_k, new_diff_v)
```

```python
import functools
import math

import jax
import jax.numpy as jnp
from jax import lax
from jax.experimental import pallas as pl
from jax.experimental.pallas import tpu as pltpu

F32 = jnp.float32
BF16 = jnp.bfloat16

GRID_W = 64
ROPE_BASE = 10000.0
NORM_EPS = 1e-6
HEAD_DIM_A = 128
N_KV_HEADS_A = 2
GQA_GROUP = 4
N_HEADS_B = 4
DK_B = 64
DV_B = 128
N_FOURIER_GROUPS = 4
FOURIER_GROUP_DIM = 128
N_MOD = 6

LOG2E = math.log2(math.e)
LANES = 128
BF16_SUBLANES = 16
VMEM_LIMIT = 58 * 1024 * 1024


def _cparams(sem, flags=None):
    return pltpu.CompilerParams(dimension_semantics=sem, vmem_limit_bytes=VMEM_LIMIT, flags=flags)


_ATTN_FLAGS = None


def _rms(x, g):
    return x * lax.rsqrt(jnp.mean(x * x, axis=-1, keepdims=True) + NORM_EPS) * g


def _ada_kernel(c_ref, w_ref, b_ref, o_ref):
    c = c_ref[...]
    s = (c * jax.nn.sigmoid(c)).astype(BF16)
    w = w_ref[...].astype(BF16)
    o_ref[...] = jnp.dot(s, w, preferred_element_type=F32) + b_ref[...]


def _ada_call(cvec, w_ada, b_ada):
    depth, d, n = w_ada.shape
    rows = cvec.shape[0]
    tn = 512
    return pl.pallas_call(
        _ada_kernel,
        out_shape=jax.ShapeDtypeStruct((depth, rows, n), F32),
        grid=(depth, n // tn),
        in_specs=[
            pl.BlockSpec((rows, d), lambda l, j: (0, 0)),
            pl.BlockSpec((None, d, tn), lambda l, j: (l, 0, j)),
            pl.BlockSpec((None, 1, tn), lambda l, j: (l, 0, j)),
        ],
        out_specs=pl.BlockSpec((None, rows, tn), lambda l, j: (l, 0, j)),
        compiler_params=_cparams(("parallel", "parallel")),
        name="ada_mod",
    )(cvec, w_ada, b_ada.reshape(depth, 1, n))


def _rope(xs, cos, sin_signed, shift):
    w = xs.shape[-1]
    lane = lax.broadcasted_iota(jnp.int32, xs.shape, 1)
    first = (lane % (2 * shift)) < shift
    rot = jnp.where(first, pltpu.roll(xs, w - shift, 1), pltpu.roll(xs, shift, 1))
    return xs * cos + rot * sin_signed


def _in_kernel(*refs, rope, ctx_out, n_alias, cols):
    it = iter(refs)
    x_ref, mod_ref, g_ref, w_ref, gq_ref, gk_ref, dft_ref = (next(it) for _ in range(7))
    if rope:
        cos_a, sin_a, cos_b, sin_b = (next(it)[...] for _ in range(4))
    for _ in range(n_alias):
        next(it)
    qkv_ref, y_ref = next(it), next(it)
    if ctx_out:
        ka_ref, va_ref, kb_ref, vb_ref = (next(it) for _ in range(4))
    c_qa, c_ka, c_va, c_qb, c_kb, c_vb, c_f, c_end = cols

    def put_cache(ref, off, v):
        head = off // LANES
        seqs, rows = ref.shape[0], ref.shape[-2]
        heads = rows * seqs // v.shape[0]
        n = rows // heads
        for s in range(seqs):
            vs = v[s * n:(s + 1) * n, :]
            if len(ref.shape) == 4:
                for dd in range(ref.shape[1]):
                    ref[s, dd, pl.ds(head, n, stride=heads), :] = vs
            else:
                ref[s, pl.ds(head, n, stride=heads), :] = vs

    m = mod_ref[...]
    h = _rms(x_ref[...], g_ref[...])
    hb = (h * (1.0 + m[1:2]) + m[0:1]).astype(BF16)
    gq = gq_ref[...]
    gk = gk_ref[...]
    scale_a = HEAD_DIM_A ** -0.5 * LOG2E
    scale_b = DK_B ** -0.5 * LOG2E
    chunk = 4 * LANES

    for c0 in range(0, c_end, chunk):
        acc = jnp.dot(hb, w_ref[:, c0:c0 + chunk], preferred_element_type=F32)
        for s in range(chunk // LANES):
            col = c0 + s * LANES
            v = acc[:, s * LANES:(s + 1) * LANES]
            if col < c_ka:
                v = _rms(v, gq)
                if rope:
                    v = _rope(v, cos_a, sin_a, HEAD_DIM_A // 4)
                qkv_ref[:, col:col + LANES] = (v * scale_a).astype(BF16)
            elif col < c_va:
                v = _rms(v, gk)
                if ctx_out:
                    put_cache(ka_ref, col - c_ka, v)
                if rope:
                    v = _rope(v, cos_a, sin_a, HEAD_DIM_A // 4)
                qkv_ref[:, col:col + LANES] = v.astype(BF16)
            elif col < c_qb:
                if ctx_out:
                    put_cache(va_ref, col - c_va, v)
                qkv_ref[:, col:col + LANES] = v.astype(BF16)
            elif col < c_kb:
                if rope:
                    v = _rope(v, cos_b, sin_b, DK_B // 4)
                qkv_ref[:, col:col + LANES] = (v * scale_b).astype(BF16)
            elif col < c_vb:
                if ctx_out:
                    put_cache(kb_ref, col - c_kb, v)
                if rope:
                    v = _rope(v, cos_b, sin_b, DK_B // 4)
                qkv_ref[:, col:col + LANES] = v.astype(BF16)
            elif col < c_f:
                if ctx_out:
                    put_cache(vb_ref, col - c_vb, v)
                qkv_ref[:, col:col + LANES] = v.astype(BF16)
            else:
                yy = jnp.dot(v.astype(BF16), dft_ref[...], preferred_element_type=F32)
                gcol = col - c_f
                half = c_end - c_f
                y_ref[:, gcol:gcol + LANES] = yy[:, :LANES].astype(BF16)
                y_ref[:, half + gcol:half + gcol + LANES] = yy[:, LANES:].astype(BF16)


def _in_call(x2d, mod_l, cond_row_fn, norm_g, w_in_b, gq, gk, dft_c, rope_tabs, seq_len, new_cache, tm):
    ctx_out = new_cache is not None
    t, d = x2d.shape
    w_in_b, w_layer = w_in_b
    d_in = w_in_b.shape[2]
    c_qa = 0
    c_ka = N_KV_HEADS_A * GQA_GROUP * HEAD_DIM_A
    c_va = c_ka + N_KV_HEADS_A * HEAD_DIM_A
    c_qb = c_va + N_KV_HEADS_A * HEAD_DIM_A
    c_kb = c_qb + N_HEADS_B * 2 * DK_B
    c_vb = c_kb + N_HEADS_B * 2 * DK_B
    c_f = c_vb + N_HEADS_B * DV_B
    c_end = c_f + N_FOURIER_GROUPS * FOURIER_GROUP_DIM
    assert c_end == d_in
    cols = (c_qa, c_ka, c_va, c_qb, c_kb, c_vb, c_f, c_end)
    rope = rope_tabs is not None
    n_f = c_end - c_f

    in_specs = [
        pl.BlockSpec((tm, d), lambda i: (i, 0)),
        pl.BlockSpec((None, N_MOD, d), lambda i: (cond_row_fn(i), 0, 0)),
        pl.BlockSpec((1, d), lambda i: (0, 0)),
        pl.BlockSpec((None, d, d_in), lambda i: (w_layer, 0, 0), pipeline_mode=pl.Buffered(1)),
        pl.BlockSpec((1, HEAD_DIM_A), lambda i: (0, 0)),
        pl.BlockSpec((1, HEAD_DIM_A), lambda i: (0, 0)),
        pl.BlockSpec((FOURIER_GROUP_DIM, 2 * FOURIER_GROUP_DIM), lambda i: (0, 0)),
    ]
    args = [x2d, mod_l, norm_g, w_in_b, gq, gk, dft_c]
    if rope:
        nblk = seq_len // tm
        for tab in rope_tabs:
            in_specs.append(pl.BlockSpec((tm, LANES), lambda i: (i % nblk, 0)))
            args.append(tab)
    out_shape = [jax.ShapeDtypeStruct((t, c_f), BF16), jax.ShapeDtypeStruct((t, 2 * n_f), BF16)]
    out_specs = [pl.BlockSpec((tm, c_f), lambda i: (i, 0)), pl.BlockSpec((tm, 2 * n_f), lambda i: (i, 0))]
    aliases = {}
    n_alias = 0
    if ctx_out:
        depth, layer, prev = new_cache
        nb = t // seq_len
        spt = tm // seq_len
        for width in (c_va - c_ka, c_qb - c_va, c_vb - c_kb, c_f - c_vb):
            rows = seq_len * (width // LANES)
            out_shape.append(jax.ShapeDtypeStruct((nb, depth, rows, LANES), F32))
            if prev is None:
                out_specs.append(pl.BlockSpec((spt, depth, rows, LANES), lambda i: (i, 0, 0, 0)))
            else:
                out_specs.append(pl.BlockSpec((spt, None, rows, LANES), lambda i: (i, layer, 0, 0)))
        if prev is not None:
            n_alias = len(prev)
            for a, arr in enumerate(prev):
                aliases[len(args)] = 2 + a
                in_specs.append(pl.BlockSpec(memory_space=pl.ANY))
                args.append(arr)
    return pl.pallas_call(
        functools.partial(_in_kernel, rope=rope, ctx_out=ctx_out, n_alias=n_alias, cols=cols),
        out_shape=out_shape,
        grid=(t // tm,),
        in_specs=in_specs,
        out_specs=out_specs,
        input_output_aliases=aliases,
        compiler_params=_cparams(("parallel",)),
        name="in_proj_ctx" if ctx_out else "in_proj_lat",
    )(*args)


def _softmax_pv(q, srcs):
    chunks = []
    for k_ref, v_ref, n_keys, kc, col in srcs:
        for c in range(n_keys // kc):
            chunks.append((k_ref, v_ref, c * kc, kc, col))

    def scores(ch):
        k_ref, _, off, kc, col = ch
        k = k_ref[off:off + kc, col:col + LANES].astype(BF16)
        return lax.dot_general(q, k, (((1,), (1,)), ((), ())), preferred_element_type=F32)

    def lane_fold(x, op):
        out = x[:, 0:LANES]
        for t in range(1, x.shape[1] // LANES):
            out = op(out, x[:, t * LANES:(t + 1) * LANES])
        return out

    m = l_part = acc = None
    s_next = scores(chunks[0])
    for i, ch in enumerate(chunks):
        s = s_next
        if i + 1 < len(chunks):
            s_next = scores(chunks[i + 1])
        _, v_ref, off, kc, col = ch
        row_max = jnp.max(lane_fold(s, jnp.maximum), axis=-1, keepdims=True)
        m_new = row_max if m is None else jnp.maximum(m, row_max)
        p = jnp.exp2(s - m_new)
        p_sum = lane_fold(p, jnp.add)
        v = v_ref[off:off + kc, col:col + LANES].astype(BF16)
        pv = jnp.dot(p.astype(BF16), v, preferred_element_type=F32)
        if m is None:
            l_part, acc = p_sum, pv
        else:
            alpha = jnp.exp2(m - m_new)
            l_part = alpha * l_part + p_sum
            acc = alpha * acc + pv
        m = m_new
    return acc / jnp.sum(l_part, axis=-1, keepdims=True)


def _srcs(cache_refs, k_ref, v_ref, kc, head):
    srcs = []
    col = head * LANES
    if cache_refs is not None:
        ck, cv = cache_refs
        srcs.append((ck, cv, ck.shape[0], ck.shape[0], col))
    n = k_ref.shape[0]
    srcs.append((k_ref, v_ref, n, min(kc, n), col))
    return srcs


def _attn_a_kernel(*refs, has_cache, tq, kc, heads, split):
    if has_cache:
        q_ref, ck_ref, cv_ref, k_ref, v_ref, o_ref = refs
        cache = (ck_ref, cv_ref)
    else:
        q_ref, k_ref, v_ref, o_ref = refs
        cache = None
    per = GQA_GROUP // split
    for bb in range(q_ref.shape[0]):
        qb, kb, vb, ob = q_ref.at[bb], k_ref.at[bb], v_ref.at[bb], o_ref.at[bb]
        for kh in range(heads):
            for part in range(split):
                cols = [(kh * GQA_GROUP + part * per + h) * LANES for h in range(per)]
                qs = jnp.concatenate([qb[:, c:c + LANES] for c in cols], axis=0)
                o = _softmax_pv(qs, _srcs(cache, kb, vb, kc, kh))
                for h, c in enumerate(cols):
                    ob[:, c:c + LANES] = o[h * tq:(h + 1) * tq].astype(BF16)


def _attn_b_kernel(*refs, has_cache, tq, kc, heads, split, lam_init):
    if has_cache:
        lam_ref, gs_ref, q_ref, ck_ref, cv_ref, k_ref, v_ref, o_ref = refs
        cache = (ck_ref, cv_ref)
    else:
        lam_ref, gs_ref, q_ref, k_ref, v_ref, o_ref = refs
        cache = None
    lp = lam_ref[...]
    lam = (jnp.exp(jnp.sum(lp[0:1] * lp[1:2], axis=-1, keepdims=True))
           - jnp.exp(jnp.sum(lp[2:3] * lp[3:4], axis=-1, keepdims=True)) + lam_init)
    rows = tq // split
    for bb in range(q_ref.shape[0]):
        qb, kb, vb, ob = q_ref.at[bb], k_ref.at[bb], v_ref.at[bb], o_ref.at[bb]
        for hd in range(heads):
            for part in range(split):
                q = qb[part * rows:(part + 1) * rows, hd * LANES:(hd + 1) * LANES]
                lane = lax.broadcasted_iota(jnp.int32, q.shape, 1)
                zero = jnp.zeros_like(q)
                qz = jnp.concatenate([jnp.where(lane < DK_B, q, zero), jnp.where(lane >= DK_B, q, zero)], axis=0)
                o = _softmax_pv(qz, _srcs(cache, kb, vb, kc, hd))
                dlt = o[:rows] - lam * o[rows:]
                ob[part * rows:(part + 1) * rows, hd * LANES:(hd + 1) * LANES] = (
                    _rms(dlt, gs_ref[...]) * (1.0 - lam_init)).astype(BF16)


def _attn_calls(qkv3, caches, layer, lam_params, g_sub, lam_init, tq_a, tq_b, kc, ha, hb, split_a, split_b, nb):
    b, n, _ = qkv3.shape
    has_cache = caches is not None
    assert b % nb == 0 and (nb == 1 or not has_cache)
    qa_blk =GQA_GROUP * HEAD_DIM_A // LANES
    k_a0 = N_KV_HEADS_A * qa_blk
    v_a0 = k_a0 + N_KV_HEADS_A
    q_b0 = v_a0 + N_KV_HEADS_A
    k_b0 = q_b0 + N_HEADS_B
    v_b0 = k_b0 + N_HEADS_B
    assert all(x % ha == 0 for x in (N_KV_HEADS_A, k_a0, v_a0)) and all(x % hb == 0 for x in (N_HEADS_B, q_b0, k_b0, v_b0))

    qw, kw = ha * GQA_GROUP * LANES, ha * LANES
    in_specs = [pl.BlockSpec((nb, tq_a, qw), lambda bi, h, i: (bi, i, h))]
    args = [qkv3]
    if has_cache:
        ck, cv = caches[0], caches[1]
        p = ck.shape[2]
        in_specs += [pl.BlockSpec((None, None, p, kw), lambda bi, h, i: (bi, layer, 0, h))] * 2
        args += [ck, cv]
    in_specs += [pl.BlockSpec((nb, n, kw), lambda bi, h, i: (bi, 0, k_a0 // ha + h)),
                 pl.BlockSpec((nb, n, kw), lambda bi, h, i: (bi, 0, v_a0 // ha + h))]
    args += [qkv3, qkv3]
    att_a = pl.pallas_call(
        functools.partial(_attn_a_kernel, has_cache=has_cache, tq=tq_a, kc=kc, heads=ha, split=split_a),
        out_shape=jax.ShapeDtypeStruct((b, n, N_KV_HEADS_A * GQA_GROUP * HEAD_DIM_A), BF16),
        grid=(b // nb, N_KV_HEADS_A // ha, n // tq_a),
        in_specs=in_specs,
        out_specs=pl.BlockSpec((nb, tq_a, qw), lambda bi, h, i: (bi, i, h)),
        compiler_params=_cparams(("parallel", "parallel", "arbitrary"), _ATTN_FLAGS),
        name="attn_a_lat" if has_cache else "attn_a_ctx",
    )(*args)

    bw = hb * LANES
    in_specs = [pl.BlockSpec((4, DK_B), lambda bi, h, i: (0, 0)),
                pl.BlockSpec((1, DV_B), lambda bi, h, i: (0, 0)),
                pl.BlockSpec((nb, tq_b, bw), lambda bi, h, i: (bi, i, q_b0 // hb + h))]
    args = [lam_params, g_sub, qkv3]
    if has_cache:
        ck, cv = caches[2], caches[3]
        p = ck.shape[2]
        in_specs += [pl.BlockSpec((None, None, p, bw), lambda bi, h, i: (bi, layer, 0, h))] * 2
        args += [ck, cv]
    in_specs += [pl.BlockSpec((nb, n, bw), lambda bi, h, i: (bi, 0, k_b0 // hb + h)),
                 pl.BlockSpec((nb, n, bw), lambda bi, h, i: (bi, 0, v_b0 // hb + h))]
    args += [qkv3, qkv3]
    att_b = pl.pallas_call(
        functools.partial(_attn_b_kernel, has_cache=has_cache, tq=tq_b, kc=kc, heads=hb, split=split_b, lam_init=lam_init),
        out_shape=jax.ShapeDtypeStruct((b, n, N_HEADS_B * DV_B), BF16),
        grid=(b // nb, N_HEADS_B // hb, n // tq_b),
        in_specs=in_specs,
        out_specs=pl.BlockSpec((nb, tq_b, bw), lambda bi, h, i: (bi, i, h)),
        compiler_params=_cparams(("parallel", "parallel", "arbitrary"), _ATTN_FLAGS),
        name="attn_b_lat" if has_cache else "attn_b_ctx",
    )(*args)
    return att_a, att_b


def _dft_kernel(x_ref, mat_ref, *rest, width, scale):
    p, g, w2 = x_ref.shape
    x = x_ref[...].reshape(p * g, w2)
    xs = jnp.concatenate([x[:, :width], x[:, width:]], axis=0)
    u = jnp.dot(mat_ref[...], xs, preferred_element_type=F32)
    if len(rest) == 3:
        tc_ref, ts_ref, o_ref = rest
        half = u.shape[0] // 2
        ur, ui = u[:half], u[half:]
        reps = width // LANES
        tc = jnp.concatenate([tc_ref[...].reshape(half, LANES)] * reps, axis=1)
        ts = jnp.concatenate([ts_ref[...].reshape(half, LANES)] * reps, axis=1)
        o_ref[:, :, :width] = (ur * tc - ui * ts).astype(BF16).reshape(o_ref.shape[0], o_ref.shape[1], width)
        o_ref[:, :, width:] = (ur * ts + ui * tc).astype(BF16).reshape(o_ref.shape[0], o_ref.shape[1], width)
    else:
        (o_ref,) = rest
        o_ref[...] = (u * scale).astype(BF16).reshape(o_ref.shape)


def _dft_seq_kernel(x_ref, mat_ref, o_ref, *, width, scale):
    nb = x_ref.shape[0]
    xs = jnp.concatenate(
        [jnp.concatenate([x_ref[bb, :, :width], x_ref[bb, :, width:]], axis=0) for bb in range(nb)], axis=1)
    u = jnp.dot(mat_ref[...], xs, preferred_element_type=F32)
    for bb in range(nb):
        o_ref[bb] = (u[:, bb * width:(bb + 1) * width] * scale).astype(BF16)


def _cos_sin(n_rows, n_cols, period):
    a = jnp.arange(n_rows, dtype=jnp.int32)[:, None]
    b = jnp.arange(n_cols, dtype=jnp.int32)[None, :]
    ang = ((a * b) % period).astype(F32) * (2.0 * math.pi / period)
    return jnp.cos(ang), jnp.sin(ang)


def _fourier_call(y3, n1, n2):
    b, n, w2 = y3.shape
    width = w2 // 2
    scale = 1.0 / math.sqrt(n * FOURIER_GROUP_DIM)
    c1, s1 = _cos_sin(n1, n1, n1)
    w_real = jnp.stack([c1, -s1], axis=1)
    if n2 == 1:
        mat = w_real.reshape(n1, 2 * n1).astype(BF16)
        nb = math.gcd(b, 4)
        return pl.pallas_call(
            functools.partial(_dft_seq_kernel, width=width, scale=scale),
            out_shape=jax.ShapeDtypeStruct((b, n, width), BF16),
            grid=(b // nb,),
            in_specs=[pl.BlockSpec((nb, n, w2), lambda bi: (bi, 0, 0)),
                      pl.BlockSpec((n1, 2 * n1), lambda bi: (0, 0))],
            out_specs=pl.BlockSpec((nb, n, width), lambda bi: (bi, 0, 0)),
            compiler_params=_cparams(("parallel",)),
            name="dft_ctx",
        )(y3, mat)

    g = BF16_SUBLANES
    c2, s2 = _cos_sin(n2, n2, n2)
    w_cplx = jnp.stack([jnp.stack([c2, -s2], axis=1), jnp.stack([s2, c2], axis=1)], axis=0)

    def kron_cols(base, row_j):
        rows, cols = base.shape
        col = lax.broadcasted_iota(jnp.int32, (cols, cols * g), 1)
        expand = (col // g == lax.broadcasted_iota(jnp.int32, (cols, cols * g), 0)).astype(BF16)
        wide = jnp.dot(base.astype(BF16), expand, preferred_element_type=F32)
        keep = row_j[:, None] == (lax.broadcasted_iota(jnp.int32, (rows, cols * g), 1) % g)
        return jnp.where(keep, wide, 0.0).astype(BF16)

    base1 = jnp.broadcast_to(w_cplx.reshape(2, 1, n2, 2 * n2), (2, g, n2, 2 * n2)).reshape(2 * g * n2, 2 * n2)
    mat1 = kron_cols(base1, (jnp.arange(2 * g * n2, dtype=jnp.int32) // n2) % g)
    base2 = jnp.broadcast_to(w_real.reshape(n1, 1, 2 * n1), (n1, g, 2 * n1)).reshape(n1 * g, 2 * n1)
    mat2 = kron_cols(base2, jnp.arange(n1 * g, dtype=jnp.int32) % g)
    tc, ts = _cos_sin(n1, n2, n)
    tc = jnp.broadcast_to(tc[:, :, None], (n1, n2, LANES))
    ts = jnp.broadcast_to(ts[:, :, None], (n1, n2, LANES))
    t = pl.pallas_call(
        functools.partial(_dft_kernel, width=width, scale=None),
        out_shape=jax.ShapeDtypeStruct((b, n1, n2, w2), BF16),
        grid=(b, n1 // g),
        in_specs=[pl.BlockSpec((None, n2, g, w2), lambda bi, j: (bi, 0, j, 0)),
                  pl.BlockSpec(mat1.shape, lambda bi, j: (0, 0), pipeline_mode=pl.Buffered(1)),
                  pl.BlockSpec((g, n2, LANES), lambda bi, j: (j, 0, 0)),
                  pl.BlockSpec((g, n2, LANES), lambda bi, j: (j, 0, 0))],
        out_specs=pl.BlockSpec((None, g, n2, w2), lambda bi, j: (bi, j, 0, 0)),
        compiler_params=_cparams(("parallel", "parallel")),
        name="dft_stage1",
    )(y3.reshape(b, n2, n1, w2), mat1, tc, ts)
    out = pl.pallas_call(
        functools.partial(_dft_kernel, width=width, scale=scale),
        out_shape=jax.ShapeDtypeStruct((b, n1, n2, width), BF16),
        grid=(b, n2 // g),
        in_specs=[pl.BlockSpec((None, n1, g, w2), lambda bi, j: (bi, 0, j, 0)),
                  pl.BlockSpec(mat2.shape, lambda bi, j: (0, 0), pipeline_mode=pl.Buffered(1))],
        out_specs=pl.BlockSpec((None, n1, g, width), lambda bi, j: (bi, 0, j, 0)),
        compiler_params=_cparams(("parallel", "parallel")),
        name="dft_stage2",
    )(t, mat2)
    return out.reshape(b, n, width)


def _out_kernel(a_ref, b_ref, f_ref, w_ref, x_ref, mod_ref, g_ref, x1_ref, h2_ref):
    ca = a_ref.shape[1]
    cb = b_ref.shape[1]
    m = mod_ref[...]
    half = a_ref.shape[0] // 2
    for r0 in (0, half):
        rows = slice(r0, r0 + half)
        acc = jnp.dot(a_ref[rows, :], w_ref[0:ca, :], preferred_element_type=F32)
        acc += jnp.dot(b_ref[rows, :], w_ref[ca:ca + cb, :], preferred_element_type=F32)
        acc += jnp.dot(f_ref[rows, :], w_ref[ca + cb:, :], preferred_element_type=F32)
        x1 = x_ref[rows, :] + m[2:3] * acc
        x1_ref[rows, :] = x1
        h = _rms(x1, g_ref[...])
        h2_ref[rows, :] = (h * (1.0 + m[4:5]) + m[3:4]).astype(BF16)


def _out_call(att_a, att_b, four, w_out_b, x2d, mod_l, cond_row_fn, norm_g, tm):
    t, d = x2d.shape
    ca, cb, cf = att_a.shape[1], att_b.shape[1], four.shape[1]
    w_out_b, w_layer = w_out_b
    return pl.pallas_call(
        _out_kernel,
        out_shape=[jax.ShapeDtypeStruct((t, d), F32), jax.ShapeDtypeStruct((t, d), BF16)],
        grid=(t // tm,),
        in_specs=[
            pl.BlockSpec((tm, ca), lambda i: (i, 0)),
            pl.BlockSpec((tm, cb), lambda i: (i, 0)),
            pl.BlockSpec((tm, cf), lambda i: (i, 0)),
            pl.BlockSpec((None, ca + cb + cf, d), lambda i: (w_layer, 0, 0), pipeline_mode=pl.Buffered(1)),
            pl.BlockSpec((tm, d), lambda i: (i, 0)),
            pl.BlockSpec((None, N_MOD, d), lambda i: (cond_row_fn(i), 0, 0)),
            pl.BlockSpec((1, d), lambda i: (0, 0)),
        ],
        out_specs=[pl.BlockSpec((tm, d), lambda i: (i, 0)), pl.BlockSpec((tm, d), lambda i: (i, 0))],
        compiler_params=_cparams(("parallel",)),
        name="out_proj",
    )(att_a, att_b, four, w_out_b, x2d, mod_l, norm_g)


HALO = 16
FFN_TF = 512


def _ffn_kernel(cw_ref, cb_ref, mod_ref, gf_ref, h_hbm, x1_hbm, wg_hbm, wu_hbm, wd_hbm,
                o_ref, hext_ref, x1_buf, wg_buf, wu_buf, wd_buf, sem, *, tm, tf, nj, layer, seq_len, final):
    i = pl.program_id(0)
    n_i = pl.num_programs(0)
    row0 = pl.multiple_of(i * tm, tm)

    def weight_copies(j, slot):
        cols = pl.ds(pl.multiple_of(j * tf, tf), tf)
        return (pltpu.make_async_copy(wg_hbm.at[layer, :, cols], wg_buf.at[slot], sem.at[0, slot]),
                pltpu.make_async_copy(wu_hbm.at[layer, :, cols], wu_buf.at[slot], sem.at[1, slot]),
                pltpu.make_async_copy(wd_hbm.at[layer, cols, :], wd_buf.at[slot], sem.at[2, slot]))

    def tile_copies(ti, slot):
        start = pl.multiple_of(ti * tm, tm)
        before = pl.multiple_of(jnp.maximum(start - HALO, 0), HALO)
        after = pl.multiple_of(jnp.minimum(start + tm, n_i * tm - HALO), HALO)
        dst = hext_ref.at[slot]
        return (pltpu.make_async_copy(h_hbm.at[pl.ds(before, HALO), :], dst.at[pl.ds(0, HALO), :], sem.at[3, slot]),
                pltpu.make_async_copy(h_hbm.at[pl.ds(start, tm), :], dst.at[pl.ds(HALO, tm), :], sem.at[4, slot]),
                pltpu.make_async_copy(h_hbm.at[pl.ds(after, HALO), :], dst.at[pl.ds(HALO + tm, HALO), :],
                                      sem.at[5, slot]))

    x1_copy = pltpu.make_async_copy(x1_hbm.at[pl.ds(row0, tm), :], x1_buf, sem.at[6, 0])
    hslot = i % 2

    @pl.when(i == 0)
    def _():
        for cp in tile_copies(0, 0) + weight_copies(0, 0):
            cp.start()

    x1_copy.start()

    @pl.when(i + 1 < n_i)
    def _():
        for cp in tile_copies(i + 1, 1 - hslot):
            cp.start()

    o_ref[...] = jnp.zeros_like(o_ref)
    ext = tm + 2 * HALO
    pos = (i * tm + lax.broadcasted_iota(jnp.int32, (tm, 1), 0)) % seq_len
    for cp in tile_copies(i, hslot):
        cp.wait()
    hext = hext_ref.at[hslot]

    def chunk(j, carry):
        step = i * nj + j
        slot = step % 2
        for cp in weight_copies(j, slot):
            cp.wait()

        @pl.when(step + 1 < n_i * nj)
        def _():
            for cp in weight_copies((j + 1) % nj, 1 - slot):
                cp.start()

        g = jnp.dot(hext[...], wg_buf[slot], preferred_element_type=F32)
        u = jnp.dot(hext[HALO:HALO + tm, :], wu_buf[slot], preferred_element_type=F32)
        g_prev = jnp.where(pos == 0, 0.0, pltpu.roll(g, 1, 0)[HALO:HALO + tm])
        g_next = jnp.where(pos == seq_len - 1, 0.0, pltpu.roll(g, ext - 1, 0)[HALO:HALO + tm])
        cw = cw_ref[j]
        gc = g_prev * cw[0:1] + g[HALO:HALO + tm] * cw[1:2] + g_next * cw[2:3] + cb_ref[j]
        act = (gc * jax.nn.sigmoid(gc)) * u
        o_ref[...] += jnp.dot(act.astype(BF16), wd_buf[slot], preferred_element_type=F32)
        return carry

    lax.fori_loop(0, nj, chunk, 0)

    x1_copy.wait()
    m = mod_ref[...]
    x2 = x1_buf[...] + m[5:6] * o_ref[...]
    if final:
        x2 = _rms(x2, gf_ref[...])
    o_ref[...] = x2


def _ffn_call(h2, x1, weights, conv_w, conv_b, mod_l, cond_row_fn, final_g, seq_len, final, tm, tf):
    t, d = x1.shape
    w_gate_b, w_up_b, w_down_b, w_layer = weights
    f = w_gate_b.shape[2]
    nj = f // tf
    cw3 = conv_w.reshape(3, nj, tf).transpose(1, 0, 2)
    cb3 = conv_b.reshape(nj, 1, tf)
    hbm = pl.BlockSpec(memory_space=pl.ANY)
    return pl.pallas_call(
        functools.partial(_ffn_kernel, tm=tm, tf=tf, nj=nj, layer=w_layer, seq_len=seq_len, final=final),
        out_shape=jax.ShapeDtypeStruct((t, d), F32),
        grid=(t // tm,),
        in_specs=[
            pl.BlockSpec((nj, 3, tf), lambda i: (0, 0, 0)),
            pl.BlockSpec((nj, 1, tf), lambda i: (0, 0, 0)),
            pl.BlockSpec((None, N_MOD, d), lambda i: (cond_row_fn(i), 0, 0)),
            pl.BlockSpec((1, d), lambda i: (0, 0)),
            hbm, hbm, hbm, hbm, hbm,
        ],
        out_specs=pl.BlockSpec((tm, d), lambda i: (i, 0)),
        scratch_shapes=[pltpu.VMEM((2, tm + 2 * HALO, d), BF16), pltpu.VMEM((tm, d), F32),
                        pltpu.VMEM((2, d, tf), BF16), pltpu.VMEM((2, d, tf), BF16), pltpu.VMEM((2, tf, d), BF16),
                        pltpu.SemaphoreType.DMA((7, 2))],
        compiler_params=_cparams(("arbitrary",)),
        name="conv_ffn",
    )(cw3, cb3, mod_l, final_g, h2, x1, w_gate_b, w_up_b, w_down_b)


def _rope_tables(n, head_dim):
    rows = n // GRID_W
    t_row = jnp.repeat(jnp.arange(rows, dtype=F32), GRID_W)
    t_col = jnp.tile(jnp.arange(GRID_W, dtype=F32), rows)
    axis_dim = head_dim // 2
    inv = jnp.power(ROPE_BASE, -jnp.arange(0, axis_dim, 2, dtype=F32) / axis_dim)
    ar = t_row[:, None] * inv[None, :]
    ac = t_col[:, None] * inv[None, :]
    ang = jnp.concatenate([ar, ar, ac, ac], axis=-1)
    reps = LANES // head_dim
    quarter = head_dim // 4
    sign = jnp.where((jnp.arange(head_dim) % (2 * quarter)) < quarter, -1.0, 1.0).astype(F32)
    cos = jnp.tile(jnp.cos(ang), (1, reps))
    sin_signed = jnp.tile(jnp.sin(ang) * sign[None, :], (1, reps))
    return cos, sin_signed


def kernel(x_prompt, x_sample, cache_attn_k, cache_attn_v, cache_diff_k, cache_diff_v, c, c_ctx, norm1_g, norm2_g, w_ada, b_ada, w_in, attn_q_norm_g, attn_k_norm_g, diff_lambda_q1, diff_lambda_k1, diff_lambda_q2, diff_lambda_k2, diff_subnorm_g, w_out, ffn_w_gate, ffn_w_up, ffn_conv_w, ffn_conv_b, ffn_w_down, final_norm_g):
    depth = w_in.shape[0]
    bc, lc, d = x_prompt.shape
    bl, ll, _ = x_sample.shape
    past = cache_attn_k.shape[2]

    w_in_b = w_in.astype(BF16)
    w_out_b = w_out.astype(BF16)
    w_gate_b = ffn_w_gate.astype(BF16)
    w_up_b = ffn_w_up.astype(BF16)
    w_down_b = ffn_w_down.astype(BF16)

    n_rows = 8 * ((1 + bl + 7) // 8)
    cvec = jnp.concatenate([c_ctx[None, :], c, jnp.zeros((n_rows - 1 - bl, d), F32)], axis=0)
    mod = _ada_call(cvec, w_ada, b_ada).reshape(depth, n_rows, N_MOD, d)

    dft_c = jnp.concatenate(_cos_sin(FOURIER_GROUP_DIM, FOURIER_GROUP_DIM, FOURIER_GROUP_DIM), axis=1).astype(BF16)
    rope_tabs = _rope_tables(ll, HEAD_DIM_A) + _rope_tables(ll, DK_B)
    caches = (cache_attn_k.reshape(bl, depth, past, N_KV_HEADS_A * HEAD_DIM_A),
              cache_attn_v.reshape(bl, depth, past, N_KV_HEADS_A * HEAD_DIM_A),
              cache_diff_k.reshape(bl, depth, past, N_HEADS_B * 2 * DK_B),
              cache_diff_v.reshape(bl, depth, past, N_HEADS_B * DV_B))
    lam_all = jnp.stack([diff_lambda_q1, diff_lambda_k1, diff_lambda_q2, diff_lambda_k2], axis=1)

    def run_pass(x3, is_ctx):
        b, n, _ = x3.shape
        t = b * n
        tm = 512
        tm_ffn = 1024
        x = x3.reshape(t, d)

        def cond_rows(tile):
            return (lambda i: 0) if is_ctx else (lambda i: 1 + (i * tile) // n)

        cond_row_fn = cond_rows(tm)
        new_kv = None
        for l in range(depth):
            lam_init = 0.8 - 0.6 * math.exp(-0.3 * l)
            res = _in_call(x, mod[l], cond_row_fn, norm1_g[l][None, :], (w_in_b, l),
                           attn_q_norm_g[l][None, :], attn_k_norm_g[l][None, :], dft_c,
                           None if is_ctx else rope_tabs, n, (depth, l, new_kv) if is_ctx else None, tm)
            qkv, y12 = res[0], res[1]
            if is_ctx:
                new_kv = res[2:]
            qkv3 = qkv.reshape(b, n, qkv.shape[1])
            att_a, att_b = _attn_calls(qkv3, None if is_ctx else caches, l, lam_all[l],
                                       diff_subnorm_g[l][None, :], lam_init,
                                       tq_a=min(512, n), tq_b=min(1024, n), kc=1024,
                                       ha=N_KV_HEADS_A if is_ctx else 1, hb=N_HEADS_B if is_ctx else 1,
                                       split_a=1 if is_ctx else 4, split_b=1 if is_ctx else 4,
                                       nb=4 if is_ctx else 1)
            if is_ctx:
                four = _fourier_call(y12.reshape(b, n, y12.shape[1]), n, 1)
            else:
                four = _fourier_call(y12.reshape(b, n, y12.shape[1]), GRID_W, n // GRID_W)
            x1, h2 = _out_call(att_a.reshape(t, -1), att_b.reshape(t, -1), four.reshape(t, -1), (w_out_b, l),
                               x, mod[l], cond_row_fn, norm2_g[l][None, :], tm)
            x = _ffn_call(h2, x1, (w_gate_b, w_up_b, w_down_b, l), ffn_conv_w[l], ffn_conv_b[l][None, :],
                          mod[l], cond_rows(tm_ffn), final_norm_g[None, :], n, l == depth - 1, tm_ffn, FFN_TF)
        return x.reshape(b, n, d), new_kv

    y_prompt, kvs = run_pass(x_prompt, True)
    new_attn_k = kvs[0].reshape(bc, depth, lc, N_KV_HEADS_A, HEAD_DIM_A)
    new_attn_v = kvs[1].reshape(bc, depth, lc, N_KV_HEADS_A, HEAD_DIM_A)
    new_diff_k = kvs[2].reshape(bc, depth, lc, N_HEADS_B, 2 * DK_B)
    new_diff_v = kvs[3].reshape(bc, depth, lc, N_HEADS_B, DV_B)

    y_sample, _ = run_pass(x_sample, False)
    return (y_prompt, y_sample, new_attn_k, new_attn_v, new_diff_k, new_diff_v)
```

```python
import functools
import math

import jax
import jax.numpy as jnp
from jax import lax
from jax.experimental import pallas as pl
from jax.experimental.pallas import tpu as pltpu

F32 = jnp.float32
BF16 = jnp.bfloat16

GRID_W = 64
ROPE_BASE = 10000.0
NORM_EPS = 1e-6
HEAD_DIM_A = 128
N_KV_HEADS_A = 2
GQA_GROUP = 4
N_HEADS_B = 4
DK_B = 64
DV_B = 128
N_FOURIER_GROUPS = 4
FOURIER_GROUP_DIM = 128
N_MOD = 6

LOG2E = math.log2(math.e)
LANES = 128
BF16_SUBLANES = 16
VMEM_LIMIT = 58 * 1024 * 1024


def _cparams(sem, flags=None):
    return pltpu.CompilerParams(dimension_semantics=sem, vmem_limit_bytes=VMEM_LIMIT, flags=flags)


_ATTN_FLAGS = None


def _rms(x, g):
    return x * lax.rsqrt(jnp.mean(x * x, axis=-1, keepdims=True) + NORM_EPS) * g


def _ada_kernel(c_ref, w_ref, b_ref, o_ref):
    c = c_ref[...]
    s = (c * jax.nn.sigmoid(c)).astype(BF16)
    w = w_ref[...].astype(BF16)
    o_ref[...] = jnp.dot(s, w, preferred_element_type=F32) + b_ref[...]


def _ada_call(cvec, w_ada, b_ada):
    depth, d, n = w_ada.shape
    rows = cvec.shape[0]
    tn = 512
    return pl.pallas_call(
        _ada_kernel,
        out_shape=jax.ShapeDtypeStruct((depth, rows, n), F32),
        grid=(depth, n // tn),
        in_specs=[
            pl.BlockSpec((rows, d), lambda l, j: (0, 0)),
            pl.BlockSpec((None, d, tn), lambda l, j: (l, 0, j)),
            pl.BlockSpec((None, 1, tn), lambda l, j: (l, 0, j)),
        ],
        out_specs=pl.BlockSpec((None, rows, tn), lambda l, j: (l, 0, j)),
        compiler_params=_cparams(("parallel", "parallel")),
        name="ada_mod",
    )(cvec, w_ada, b_ada.reshape(depth, 1, n))


def _rope(xs, cos, sin_signed, shift):
    w = xs.shape[-1]
    lane = lax.broadcasted_iota(jnp.int32, xs.shape, 1)
    first = (lane % (2 * shift)) < shift
    rot = jnp.where(first, pltpu.roll(xs, w - shift, 1), pltpu.roll(xs, shift, 1))
    return xs * cos + rot * sin_signed


def _in_kernel(*refs, rope, ctx_out, n_alias, cols):
    it = iter(refs)
    x_ref, mod_ref, g_ref, w_ref, gq_ref, gk_ref, dft_ref = (next(it) for _ in range(7))
    if rope:
        cos_a, sin_a, cos_b, sin_b = (next(it)[...] for _ in range(4))
        perm_ref = next(it)
    for _ in range(n_alias):
        next(it)
    qkv_ref, y_ref = next(it), next(it)
    if ctx_out:
        ka_ref, va_ref, kb_ref, vb_ref = (next(it) for _ in range(4))
    c_qa, c_ka, c_va, c_qb, c_kb, c_vb, c_f, c_end = cols

    def put_cache(ref, off, v):
        head = off // LANES
        seqs, rows = ref.shape[0], ref.shape[-2]
        heads = rows * seqs // v.shape[0]
        n = rows // heads
        for s in range(seqs):
            vs = v[s * n:(s + 1) * n, :]
            if len(ref.shape) == 4:
                for dd in range(ref.shape[1]):
                    ref[s, dd, pl.ds(head, n, stride=heads), :] = vs
            else:
                ref[s, pl.ds(head, n, stride=heads), :] = vs

    m = mod_ref[...]
    h = _rms(x_ref[...], g_ref[...])
    hb = (h * (1.0 + m[1:2]) + m[0:1]).astype(BF16)
    gq = gq_ref[...]
    gk = gk_ref[...]
    scale_a = HEAD_DIM_A ** -0.5 * LOG2E
    scale_b = DK_B ** -0.5 * LOG2E
    chunk = 4 * LANES

    for c0 in range(0, c_end, chunk):
        acc = jnp.dot(hb, w_ref[:, c0:c0 + chunk], preferred_element_type=F32)
        for s in range(chunk // LANES):
            col = c0 + s * LANES
            v = acc[:, s * LANES:(s + 1) * LANES]
            if col < c_ka:
                v = _rms(v, gq)
                if rope:
                    v = _rope(v, cos_a, sin_a, HEAD_DIM_A // 4)
                qkv_ref[:, col:col + LANES] = (v * scale_a).astype(BF16)
            elif col < c_va:
                v = _rms(v, gk)
                if ctx_out:
                    put_cache(ka_ref, col - c_ka, v)
                if rope:
                    v = _rope(v, cos_a, sin_a, HEAD_DIM_A // 4)
                qkv_ref[:, col:col + LANES] = v.astype(BF16)
            elif col < c_qb:
                if ctx_out:
                    put_cache(va_ref, col - c_va, v)
                qkv_ref[:, col:col + LANES] = v.astype(BF16)
            elif col < c_kb:
                if rope:
                    v = _rope(v, cos_b, sin_b, DK_B // 4)
                qkv_ref[:, col:col + LANES] = (v * scale_b).astype(BF16)
            elif col < c_vb:
                if ctx_out:
                    put_cache(kb_ref, col - c_kb, v)
                if rope:
                    v = _rope(v, cos_b, sin_b, DK_B // 4)
                qkv_ref[:, col:col + LANES] = v.astype(BF16)
            elif col < c_f:
                if ctx_out:
                    put_cache(vb_ref, col - c_vb, v)
                qkv_ref[:, col:col + LANES] = v.astype(BF16)
            else:
                yy = jnp.dot(v.astype(BF16), dft_ref[...], preferred_element_type=F32)
                gcol = col - c_f
                half = c_end - c_f
                y_ref[:, gcol:gcol + LANES] = yy[:, :LANES].astype(BF16)
                y_ref[:, half + gcol:half + gcol + LANES] = yy[:, LANES:].astype(BF16)
    if rope:
        y_ref[...] = jnp.dot(perm_ref[...], y_ref[...], preferred_element_type=F32).astype(BF16)


def _in_call(x2d, mod_l, cond_row_fn, norm_g, w_in_b, gq, gk, dft_c, rope_tabs, seq_len, new_cache, tm):
    ctx_out = new_cache is not None
    t, d = x2d.shape
    w_in_b, w_layer = w_in_b
    d_in = w_in_b.shape[2]
    c_qa = 0
    c_ka = N_KV_HEADS_A * GQA_GROUP * HEAD_DIM_A
    c_va = c_ka + N_KV_HEADS_A * HEAD_DIM_A
    c_qb = c_va + N_KV_HEADS_A * HEAD_DIM_A
    c_kb = c_qb + N_HEADS_B * 2 * DK_B
    c_vb = c_kb + N_HEADS_B * 2 * DK_B
    c_f = c_vb + N_HEADS_B * DV_B
    c_end = c_f + N_FOURIER_GROUPS * FOURIER_GROUP_DIM
    assert c_end == d_in
    cols = (c_qa, c_ka, c_va, c_qb, c_kb, c_vb, c_f, c_end)
    rope = rope_tabs is not None
    n_f = c_end - c_f

    in_specs = [
        pl.BlockSpec((tm, d), lambda i: (i, 0)),
        pl.BlockSpec((None, N_MOD, d), lambda i: (cond_row_fn(i), 0, 0)),
        pl.BlockSpec((1, d), lambda i: (0, 0)),
        pl.BlockSpec((None, d, d_in), lambda i: (w_layer, 0, 0), pipeline_mode=pl.Buffered(1)),
        pl.BlockSpec((1, HEAD_DIM_A), lambda i: (0, 0)),
        pl.BlockSpec((1, HEAD_DIM_A), lambda i: (0, 0)),
        pl.BlockSpec((FOURIER_GROUP_DIM, 2 * FOURIER_GROUP_DIM), lambda i: (0, 0)),
    ]
    args = [x2d, mod_l, norm_g, w_in_b, gq, gk, dft_c]
    if rope:
        nblk = seq_len // tm
        for tab in rope_tabs:
            in_specs.append(pl.BlockSpec((tm, LANES), lambda i: (i % nblk, 0)))
            args.append(tab)
        r_new = jnp.arange(tm, dtype=jnp.int32)
        r_old = (r_new % (tm // GRID_W)) * GRID_W + r_new // (tm // GRID_W)
        perm = (r_old[:, None] == jnp.arange(tm, dtype=jnp.int32)[None, :]).astype(BF16)
        in_specs.append(pl.BlockSpec((tm, tm), lambda i: (0, 0)))
        args.append(perm)
    out_shape = [jax.ShapeDtypeStruct((t, c_f), BF16), jax.ShapeDtypeStruct((t, 2 * n_f), BF16)]
    out_specs = [pl.BlockSpec((tm, c_f), lambda i: (i, 0)), pl.BlockSpec((tm, 2 * n_f), lambda i: (i, 0))]
    aliases = {}
    n_alias = 0
    if ctx_out:
        depth, layer, prev = new_cache
        nb = t // seq_len
        spt = tm // seq_len
        for width in (c_va - c_ka, c_qb - c_va, c_vb - c_kb, c_f - c_vb):
            rows = seq_len * (width // LANES)
            out_shape.append(jax.ShapeDtypeStruct((nb, depth, rows, LANES), F32))
            if prev is None:
                out_specs.append(pl.BlockSpec((spt, depth, rows, LANES), lambda i: (i, 0, 0, 0)))
            else:
                out_specs.append(pl.BlockSpec((spt, None, rows, LANES), lambda i: (i, layer, 0, 0)))
        if prev is not None:
            n_alias = len(prev)
            for a, arr in enumerate(prev):
                aliases[len(args)] = 2 + a
                in_specs.append(pl.BlockSpec(memory_space=pl.ANY))
                args.append(arr)
    return pl.pallas_call(
        functools.partial(_in_kernel, rope=rope, ctx_out=ctx_out, n_alias=n_alias, cols=cols),
        out_shape=out_shape,
        grid=(t // tm,),
        in_specs=in_specs,
        out_specs=out_specs,
        input_output_aliases=aliases,
        compiler_params=_cparams(("parallel",)),
        name="in_proj_ctx" if ctx_out else "in_proj_lat",
    )(*args)


def _softmax_pv(q, srcs):
    chunks = []
    for k_ref, v_ref, n_keys, kc, col in srcs:
        for c in range(n_keys // kc):
            chunks.append((k_ref, v_ref, c * kc, kc, col))

    def scores(ch):
        k_ref, _, off, kc, col = ch
        k = k_ref[off:off + kc, col:col + LANES].astype(BF16)
        return lax.dot_general(q, k, (((1,), (1,)), ((), ())), preferred_element_type=F32)

    def lane_fold(x, op):
        out = x[:, 0:LANES]
        for t in range(1, x.shape[1] // LANES):
            out = op(out, x[:, t * LANES:(t + 1) * LANES])
        return out

    m = l_part = acc = None
    s_next = scores(chunks[0])
    for i, ch in enumerate(chunks):
        s = s_next
        if i + 1 < len(chunks):
            s_next = scores(chunks[i + 1])
        _, v_ref, off, kc, col = ch
        row_max = jnp.max(lane_fold(s, jnp.maximum), axis=-1, keepdims=True)
        m_new = row_max if m is None else jnp.maximum(m, row_max)
        p = jnp.exp2(s - m_new)
        p_sum = lane_fold(p, jnp.add)
        v = v_ref[off:off + kc, col:col + LANES].astype(BF16)
        pv = jnp.dot(p.astype(BF16), v, preferred_element_type=F32)
        if m is None:
            l_part, acc = p_sum, pv
        else:
            alpha = jnp.exp2(m - m_new)
            l_part = alpha * l_part + p_sum
            acc = alpha * acc + pv
        m = m_new
    return acc / jnp.sum(l_part, axis=-1, keepdims=True)


def _srcs(cache_refs, k_ref, v_ref, kc, head):
    srcs = []
    col = head * LANES
    if cache_refs is not None:
        ck, cv = cache_refs
        srcs.append((ck, cv, ck.shape[0], ck.shape[0], col))
    n = k_ref.shape[0]
    srcs.append((k_ref, v_ref, n, min(kc, n), col))
    return srcs


def _attn_a_kernel(*refs, has_cache, tq, kc, heads, split):
    if has_cache:
        q_ref, ck_ref, cv_ref, k_ref, v_ref, o_ref = refs
        cache = (ck_ref, cv_ref)
    else:
        q_ref, k_ref, v_ref, o_ref = refs
        cache = None
    per = GQA_GROUP // split
    for bb in range(q_ref.shape[0]):
        qb, kb, vb, ob = q_ref.at[bb], k_ref.at[bb], v_ref.at[bb], o_ref.at[bb]
        for kh in range(heads):
            for part in range(split):
                cols = [(kh * GQA_GROUP + part * per + h) * LANES for h in range(per)]
                qs = jnp.concatenate([qb[:, c:c + LANES] for c in cols], axis=0)
                o = _softmax_pv(qs, _srcs(cache, kb, vb, kc, kh))
                for h, c in enumerate(cols):
                    ob[:, c:c + LANES] = o[h * tq:(h + 1) * tq].astype(BF16)


def _attn_b_kernel(*refs, has_cache, tq, kc, heads, split, lam_init):
    if has_cache:
        lam_ref, gs_ref, q_ref, ck_ref, cv_ref, k_ref, v_ref, o_ref = refs
        cache = (ck_ref, cv_ref)
    else:
        lam_ref, gs_ref, q_ref, k_ref, v_ref, o_ref = refs
        cache = None
    lp = lam_ref[...]
    lam = (jnp.exp(jnp.sum(lp[0:1] * lp[1:2], axis=-1, keepdims=True))
           - jnp.exp(jnp.sum(lp[2:3] * lp[3:4], axis=-1, keepdims=True)) + lam_init)
    rows = tq // split
    for bb in range(q_ref.shape[0]):
        qb, kb, vb, ob = q_ref.at[bb], k_ref.at[bb], v_ref.at[bb], o_ref.at[bb]
        for hd in range(heads):
            for part in range(split):
                q = qb[part * rows:(part + 1) * rows, hd * LANES:(hd + 1) * LANES]
                lane = lax.broadcasted_iota(jnp.int32, q.shape, 1)
                zero = jnp.zeros_like(q)
                qz = jnp.concatenate([jnp.where(lane < DK_B, q, zero), jnp.where(lane >= DK_B, q, zero)], axis=0)
                o = _softmax_pv(qz, _srcs(cache, kb, vb, kc, hd))
                dlt = o[:rows] - lam * o[rows:]
                ob[part * rows:(part + 1) * rows, hd * LANES:(hd + 1) * LANES] = (
                    _rms(dlt, gs_ref[...]) * (1.0 - lam_init)).astype(BF16)


def _attn_calls(qkv3, caches, layer, lam_params, g_sub, lam_init, tq_a, tq_b, kc, ha, hb, split_a, split_b, nb):
    b, n, _ = qkv3.shape
    has_cache = caches is not None
    assert b % nb == 0 and (nb == 1 or not has_cache)
    qa_blk =GQA_GROUP * HEAD_DIM_A // LANES
    k_a0 = N_KV_HEADS_A * qa_blk
    v_a0 = k_a0 + N_KV_HEADS_A
    q_b0 = v_a0 + N_KV_HEADS_A
    k_b0 = q_b0 + N_HEADS_B
    v_b0 = k_b0 + N_HEADS_B
    assert all(x % ha == 0 for x in (N_KV_HEADS_A, k_a0, v_a0)) and all(x % hb == 0 for x in (N_HEADS_B, q_b0, k_b0, v_b0))

    qw, kw = ha * GQA_GROUP * LANES, ha * LANES
    in_specs = [pl.BlockSpec((nb, tq_a, qw), lambda bi, h, i: (bi, i, h))]
    args = [qkv3]
    if has_cache:
        ck, cv = caches[0], caches[1]
        p = ck.shape[2]
        in_specs += [pl.BlockSpec((None, None, p, kw), lambda bi, h, i: (bi, layer, 0, h))] * 2
        args += [ck, cv]
    in_specs += [pl.BlockSpec((nb, n, kw), lambda bi, h, i: (bi, 0, k_a0 // ha + h)),
                 pl.BlockSpec((nb, n, kw), lambda bi, h, i: (bi, 0, v_a0 // ha + h))]
    args += [qkv3, qkv3]
    att_a = pl.pallas_call(
        functools.partial(_attn_a_kernel, has_cache=has_cache, tq=tq_a, kc=kc, heads=ha, split=split_a),
        out_shape=jax.ShapeDtypeStruct((b, n, N_KV_HEADS_A * GQA_GROUP * HEAD_DIM_A), BF16),
        grid=(b // nb, N_KV_HEADS_A // ha, n // tq_a),
        in_specs=in_specs,
        out_specs=pl.BlockSpec((nb, tq_a, qw), lambda bi, h, i: (bi, i, h)),
        compiler_params=_cparams(("parallel", "parallel", "arbitrary"), _ATTN_FLAGS),
        name="attn_a_lat" if has_cache else "attn_a_ctx",
    )(*args)

    bw = hb * LANES
    in_specs = [pl.BlockSpec((4, DK_B), lambda bi, h, i: (0, 0)),
                pl.BlockSpec((1, DV_B), lambda bi, h, i: (0, 0)),
                pl.BlockSpec((nb, tq_b, bw), lambda bi, h, i: (bi, i, q_b0 // hb + h))]
    args = [lam_params, g_sub, qkv3]
    if has_cache:
        ck, cv = caches[2], caches[3]
        p = ck.shape[2]
        in_specs += [pl.BlockSpec((None, None, p, bw), lambda bi, h, i: (bi, layer, 0, h))] * 2
        args += [ck, cv]
    in_specs += [pl.BlockSpec((nb, n, bw), lambda bi, h, i: (bi, 0, k_b0 // hb + h)),
                 pl.BlockSpec((nb, n, bw), lambda bi, h, i: (bi, 0, v_b0 // hb + h))]
    args += [qkv3, qkv3]
    att_b = pl.pallas_call(
        functools.partial(_attn_b_kernel, has_cache=has_cache, tq=tq_b, kc=kc, heads=hb, split=split_b, lam_init=lam_init),
        out_shape=jax.ShapeDtypeStruct((b, n, N_HEADS_B * DV_B), BF16),
        grid=(b // nb, N_HEADS_B // hb, n // tq_b),
        in_specs=in_specs,
        out_specs=pl.BlockSpec((nb, tq_b, bw), lambda bi, h, i: (bi, i, h)),
        compiler_params=_cparams(("parallel", "parallel", "arbitrary"), _ATTN_FLAGS),
        name="attn_b_lat" if has_cache else "attn_b_ctx",
    )(*args)
    return att_a, att_b


def _dft_cols_kernel(x_ref, mat_ref, o_ref, *, width, scale):
    p, g, w2 = x_ref.shape
    x = x_ref[...].reshape(p * g, w2)
    xs = jnp.concatenate([x[:, :width], x[:, width:]], axis=0)
    u = jnp.dot(mat_ref[...], xs, preferred_element_type=F32)
    o_ref[...] = (u * scale).astype(BF16).reshape(o_ref.shape)


def _dft_rows_kernel(x_ref, mat_ref, tc_ref, ts_ref, o_ref, *, width):
    tiles, rows, w2 = x_ref.shape
    pair = BF16_SUBLANES
    reps = width // LANES
    for p in range(rows // pair):
        x = x_ref[:, p * pair:(p + 1) * pair, :].reshape(tiles * pair, w2)
        xs = jnp.concatenate([x[:, :width], x[:, width:]], axis=0)
        u = jnp.dot(mat_ref[...], xs, preferred_element_type=F32)
        half = u.shape[0] // 2
        ur, ui = u[:half], u[half:]
        tc = jnp.concatenate([tc_ref[2 * p:2 * p + 2].reshape(half, LANES)] * reps, axis=1)
        ts = jnp.concatenate([ts_ref[2 * p:2 * p + 2].reshape(half, LANES)] * reps, axis=1)
        o_ref[2 * p:2 * p + 2, :, :width] = (ur * tc - ui * ts).astype(BF16).reshape(2, half // 2, width)
        o_ref[2 * p:2 * p + 2, :, width:] = (ur * ts + ui * tc).astype(BF16).reshape(2, half // 2, width)


def _dft_seq_kernel(x_ref, mat_ref, o_ref, *, width, scale):
    nb = x_ref.shape[0]
    xs = jnp.concatenate(
        [jnp.concatenate([x_ref[bb, :, :width], x_ref[bb, :, width:]], axis=0) for bb in range(nb)], axis=1)
    u = jnp.dot(mat_ref[...], xs, preferred_element_type=F32)
    for bb in range(nb):
        o_ref[bb] = (u[:, bb * width:(bb + 1) * width] * scale).astype(BF16)


def _cos_sin(n_rows, n_cols, period):
    a = jnp.arange(n_rows, dtype=jnp.int32)[:, None]
    b = jnp.arange(n_cols, dtype=jnp.int32)[None, :]
    ang = ((a * b) % period).astype(F32) * (2.0 * math.pi / period)
    return jnp.cos(ang), jnp.sin(ang)


def _fourier_call(y3, n1, n2, tile_n2=None):
    b, n, w2 = y3.shape
    width = w2 // 2
    scale = 1.0 / math.sqrt(n * FOURIER_GROUP_DIM)
    c1, s1 = _cos_sin(n1, n1, n1)
    w_real = jnp.stack([c1, -s1], axis=1)
    if n2 == 1:
        mat = w_real.reshape(n1, 2 * n1).astype(BF16)
        nb = math.gcd(b, 4)
        return pl.pallas_call(
            functools.partial(_dft_seq_kernel, width=width, scale=scale),
            out_shape=jax.ShapeDtypeStruct((b, n, width), BF16),
            grid=(b // nb,),
            in_specs=[pl.BlockSpec((nb, n, w2), lambda bi: (bi, 0, 0)),
                      pl.BlockSpec((n1, 2 * n1), lambda bi: (0, 0))],
            out_specs=pl.BlockSpec((nb, n, width), lambda bi: (bi, 0, 0)),
            compiler_params=_cparams(("parallel",)),
            name="dft_ctx",
        )(y3, mat)

    g = BF16_SUBLANES
    c2, s2 = _cos_sin(n2, n2, n2)
    w_cplx = jnp.stack([jnp.stack([c2, -s2], axis=1), jnp.stack([s2, c2], axis=1)], axis=0)

    def kron_cols(base, row_j):
        rows, cols = base.shape
        col = lax.broadcasted_iota(jnp.int32, (cols, cols * g), 1)
        expand = (col // g == lax.broadcasted_iota(jnp.int32, (cols, cols * g), 0)).astype(BF16)
        wide = jnp.dot(base.astype(BF16), expand, preferred_element_type=F32)
        keep = row_j[:, None] == (lax.broadcasted_iota(jnp.int32, (rows, cols * g), 1) % g)
        return jnp.where(keep, wide, 0.0).astype(BF16)

    base2 = jnp.broadcast_to(w_real.reshape(n1, 1, 2 * n1), (n1, g, 2 * n1)).reshape(n1 * g, 2 * n1)
    mat2 = kron_cols(base2, jnp.arange(n1 * g, dtype=jnp.int32) % g)
    tc, ts = _cos_sin(n1, n2, n)
    tc = jnp.broadcast_to(tc[:, :, None], (n1, n2, LANES))
    ts = jnp.broadcast_to(ts[:, :, None], (n1, n2, LANES))
    tiles = n2 // tile_n2
    tile_len = n1 * tile_n2
    mat1 = jnp.einsum('akbtl,ji->ajkbtil', w_cplx.reshape(2, n2, 2, tiles, tile_n2), jnp.eye(2, dtype=F32))
    mat1 = mat1.reshape(4 * n2, 4 * n2).astype(BF16)
    gp = 8
    t = pl.pallas_call(
        functools.partial(_dft_rows_kernel, width=width),
        out_shape=jax.ShapeDtypeStruct((b, n1, n2, w2), BF16),
        grid=(b, tile_len // (BF16_SUBLANES * gp)),
        in_specs=[pl.BlockSpec((None, tiles, BF16_SUBLANES * gp, w2), lambda bi, j: (bi, 0, j, 0)),
                  pl.BlockSpec(mat1.shape, lambda bi, j: (0, 0)),
                  pl.BlockSpec((2 * gp, n2, LANES), lambda bi, j: (j, 0, 0)),
                  pl.BlockSpec((2 * gp, n2, LANES), lambda bi, j: (j, 0, 0))],
        out_specs=pl.BlockSpec((None, 2 * gp, n2, w2), lambda bi, j: (bi, j, 0, 0)),
        compiler_params=_cparams(("parallel", "parallel")),
        name="dft_stage1",
    )(y3.reshape(b, tiles, tile_len, w2), mat1, tc, ts)
    out = pl.pallas_call(
        functools.partial(_dft_cols_kernel, width=width, scale=scale),
        out_shape=jax.ShapeDtypeStruct((b, n1, n2, width), BF16),
        grid=(b, n2 // g),
        in_specs=[pl.BlockSpec((None, n1, g, w2), lambda bi, j: (bi, 0, j, 0)),
                  pl.BlockSpec(mat2.shape, lambda bi, j: (0, 0), pipeline_mode=pl.Buffered(1))],
        out_specs=pl.BlockSpec((None, n1, g, width), lambda bi, j: (bi, 0, j, 0)),
        compiler_params=_cparams(("parallel", "parallel")),
        name="dft_stage2",
    )(t, mat2)
    return out.reshape(b, n, width)


def _out_kernel(a_ref, b_ref, f_ref, w_ref, x_ref, mod_ref, g_ref, x1_ref, h2_ref):
    ca = a_ref.shape[1]
    cb = b_ref.shape[1]
    m = mod_ref[...]
    half = a_ref.shape[0] // 2
    for r0 in (0, half):
        rows = slice(r0, r0 + half)
        acc = jnp.dot(a_ref[rows, :], w_ref[0:ca, :], preferred_element_type=F32)
        acc += jnp.dot(b_ref[rows, :], w_ref[ca:ca + cb, :], preferred_element_type=F32)
        acc += jnp.dot(f_ref[rows, :], w_ref[ca + cb:, :], preferred_element_type=F32)
        x1 = x_ref[rows, :] + m[2:3] * acc
        x1_ref[rows, :] = x1
        h = _rms(x1, g_ref[...])
        h2_ref[rows, :] = (h * (1.0 + m[4:5]) + m[3:4]).astype(BF16)


def _out_call(att_a, att_b, four, w_out_b, x2d, mod_l, cond_row_fn, norm_g, tm):
    t, d = x2d.shape
    ca, cb, cf = att_a.shape[1], att_b.shape[1], four.shape[1]
    w_out_b, w_layer = w_out_b
    return pl.pallas_call(
        _out_kernel,
        out_shape=[jax.ShapeDtypeStruct((t, d), F32), jax.ShapeDtypeStruct((t, d), BF16)],
        grid=(t // tm,),
        in_specs=[
            pl.BlockSpec((tm, ca), lambda i: (i, 0)),
            pl.BlockSpec((tm, cb), lambda i: (i, 0)),
            pl.BlockSpec((tm, cf), lambda i: (i, 0)),
            pl.BlockSpec((None, ca + cb + cf, d), lambda i: (w_layer, 0, 0), pipeline_mode=pl.Buffered(1)),
            pl.BlockSpec((tm, d), lambda i: (i, 0)),
            pl.BlockSpec((None, N_MOD, d), lambda i: (cond_row_fn(i), 0, 0)),
            pl.BlockSpec((1, d), lambda i: (0, 0)),
        ],
        out_specs=[pl.BlockSpec((tm, d), lambda i: (i, 0)), pl.BlockSpec((tm, d), lambda i: (i, 0))],
        compiler_params=_cparams(("parallel",)),
        name="out_proj",
    )(att_a, att_b, four, w_out_b, x2d, mod_l, norm_g)


HALO = 16
FFN_TF = 512


def _ffn_kernel(cw_ref, cb_ref, mod_ref, gf_ref, h_hbm, x1_hbm, wg_hbm, wu_hbm, wd_hbm,
                o_ref, hext_ref, x1_buf, wg_buf, wu_buf, wd_buf, sem, *, tm, tf, nj, layer, seq_len, final):
    i = pl.program_id(0)
    n_i = pl.num_programs(0)
    row0 = pl.multiple_of(i * tm, tm)

    def weight_copies(j, slot):
        cols = pl.ds(pl.multiple_of(j * tf, tf), tf)
        return (pltpu.make_async_copy(wg_hbm.at[layer, :, cols], wg_buf.at[slot], sem.at[0, slot]),
                pltpu.make_async_copy(wu_hbm.at[layer, :, cols], wu_buf.at[slot], sem.at[1, slot]),
                pltpu.make_async_copy(wd_hbm.at[layer, cols, :], wd_buf.at[slot], sem.at[2, slot]))

    def tile_copies(ti, slot):
        start = pl.multiple_of(ti * tm, tm)
        before = pl.multiple_of(jnp.maximum(start - HALO, 0), HALO)
        after = pl.multiple_of(jnp.minimum(start + tm, n_i * tm - HALO), HALO)
        dst = hext_ref.at[slot]
        return (pltpu.make_async_copy(h_hbm.at[pl.ds(before, HALO), :], dst.at[pl.ds(0, HALO), :], sem.at[3, slot]),
                pltpu.make_async_copy(h_hbm.at[pl.ds(start, tm), :], dst.at[pl.ds(HALO, tm), :], sem.at[4, slot]),
                pltpu.make_async_copy(h_hbm.at[pl.ds(after, HALO), :], dst.at[pl.ds(HALO + tm, HALO), :],
                                      sem.at[5, slot]))

    x1_copy = pltpu.make_async_copy(x1_hbm.at[pl.ds(row0, tm), :], x1_buf, sem.at[6, 0])
    hslot = i % 2

    @pl.when(i == 0)
    def _():
        for cp in tile_copies(0, 0) + weight_copies(0, 0):
            cp.start()

    x1_copy.start()

    @pl.when(i + 1 < n_i)
    def _():
        for cp in tile_copies(i + 1, 1 - hslot):
            cp.start()

    o_ref[...] = jnp.zeros_like(o_ref)
    ext = tm + 2 * HALO
    pos = (i * tm + lax.broadcasted_iota(jnp.int32, (tm, 1), 0)) % seq_len
    for cp in tile_copies(i, hslot):
        cp.wait()
    hext = hext_ref.at[hslot]

    def chunk(j, carry):
        step = i * nj + j
        slot = step % 2
        for cp in weight_copies(j, slot):
            cp.wait()

        @pl.when(step + 1 < n_i * nj)
        def _():
            for cp in weight_copies((j + 1) % nj, 1 - slot):
                cp.start()

        g = jnp.dot(hext[...], wg_buf[slot], preferred_element_type=F32)
        u = jnp.dot(hext[HALO:HALO + tm, :], wu_buf[slot], preferred_element_type=F32)
        g_prev = jnp.where(pos == 0, 0.0, pltpu.roll(g, 1, 0)[HALO:HALO + tm])
        g_next = jnp.where(pos == seq_len - 1, 0.0, pltpu.roll(g, ext - 1, 0)[HALO:HALO + tm])
        cw = cw_ref[j]
        gc = g_prev * cw[0:1] + g[HALO:HALO + tm] * cw[1:2] + g_next * cw[2:3] + cb_ref[j]
        act = (gc * jax.nn.sigmoid(gc)) * u
        o_ref[...] += jnp.dot(act.astype(BF16), wd_buf[slot], preferred_element_type=F32)
        return carry

    lax.fori_loop(0, nj, chunk, 0)

    x1_copy.wait()
    m = mod_ref[...]
    x2 = x1_buf[...] + m[5:6] * o_ref[...]
    if final:
        x2 = _rms(x2, gf_ref[...])
    o_ref[...] = x2


def _ffn_call(h2, x1, weights, conv_w, conv_b, mod_l, cond_row_fn, final_g, seq_len, final, tm, tf):
    t, d = x1.shape
    w_gate_b, w_up_b, w_down_b, w_layer = weights
    f = w_gate_b.shape[2]
    nj = f // tf
    cw3 = conv_w.reshape(3, nj, tf).transpose(1, 0, 2)
    cb3 = conv_b.reshape(nj, 1, tf)
    hbm = pl.BlockSpec(memory_space=pl.ANY)
    return pl.pallas_call(
        functools.partial(_ffn_kernel, tm=tm, tf=tf, nj=nj, layer=w_layer, seq_len=seq_len, final=final),
        out_shape=jax.ShapeDtypeStruct((t, d), F32),
        grid=(t // tm,),
        in_specs=[
            pl.BlockSpec((nj, 3, tf), lambda i: (0, 0, 0)),
            pl.BlockSpec((nj, 1, tf), lambda i: (0, 0, 0)),
            pl.BlockSpec((None, N_MOD, d), lambda i: (cond_row_fn(i), 0, 0)),
            pl.BlockSpec((1, d), lambda i: (0, 0)),
            hbm, hbm, hbm, hbm, hbm,
        ],
        out_specs=pl.BlockSpec((tm, d), lambda i: (i, 0)),
        scratch_shapes=[pltpu.VMEM((2, tm + 2 * HALO, d), BF16), pltpu.VMEM((tm, d), F32),
                        pltpu.VMEM((2, d, tf), BF16), pltpu.VMEM((2, d, tf), BF16), pltpu.VMEM((2, tf, d), BF16),
                        pltpu.SemaphoreType.DMA((7, 2))],
        compiler_params=_cparams(("arbitrary",)),
        name="conv_ffn",
    )(cw3, cb3, mod_l, final_g, h2, x1, w_gate_b, w_up_b, w_down_b)


def _rope_tables(n, head_dim):
    rows = n // GRID_W
    t_row = jnp.repeat(jnp.arange(rows, dtype=F32), GRID_W)
    t_col = jnp.tile(jnp.arange(GRID_W, dtype=F32), rows)
    axis_dim = head_dim // 2
    inv = jnp.power(ROPE_BASE, -jnp.arange(0, axis_dim, 2, dtype=F32) / axis_dim)
    ar = t_row[:, None] * inv[None, :]
    ac = t_col[:, None] * inv[None, :]
    ang = jnp.concatenate([ar, ar, ac, ac], axis=-1)
    reps = LANES // head_dim
    quarter = head_dim // 4
    sign = jnp.where((jnp.arange(head_dim) % (2 * quarter)) < quarter, -1.0, 1.0).astype(F32)
    cos = jnp.tile(jnp.cos(ang), (1, reps))
    sin_signed = jnp.tile(jnp.sin(ang) * sign[None, :], (1, reps))
    return cos, sin_signed


def kernel(x_prompt, x_sample, cache_attn_k, cache_attn_v, cache_diff_k, cache_diff_v, c, c_ctx, norm1_g, norm2_g, w_ada, b_ada, w_in, attn_q_norm_g, attn_k_norm_g, diff_lambda_q1, diff_lambda_k1, diff_lambda_q2, diff_lambda_k2, diff_subnorm_g, w_out, ffn_w_gate, ffn_w_up, ffn_conv_w, ffn_conv_b, ffn_w_down, final_norm_g):
    depth = w_in.shape[0]
    bc, lc, d = x_prompt.shape
    bl, ll, _ = x_sample.shape
    past = cache_attn_k.shape[2]

    w_in_b = w_in.astype(BF16)
    w_out_b = w_out.astype(BF16)
    w_gate_b = ffn_w_gate.astype(BF16)
    w_up_b = ffn_w_up.astype(BF16)
    w_down_b = ffn_w_down.astype(BF16)

    n_rows = 8 * ((1 + bl + 7) // 8)
    cvec = jnp.concatenate([c_ctx[None, :], c, jnp.zeros((n_rows - 1 - bl, d), F32)], axis=0)
    mod = _ada_call(cvec, w_ada, b_ada).reshape(depth, n_rows, N_MOD, d)

    dft_c = jnp.concatenate(_cos_sin(FOURIER_GROUP_DIM, FOURIER_GROUP_DIM, FOURIER_GROUP_DIM), axis=1).astype(BF16)
    rope_tabs = _rope_tables(ll, HEAD_DIM_A) + _rope_tables(ll, DK_B)
    caches = (cache_attn_k.reshape(bl, depth, past, N_KV_HEADS_A * HEAD_DIM_A),
              cache_attn_v.reshape(bl, depth, past, N_KV_HEADS_A * HEAD_DIM_A),
              cache_diff_k.reshape(bl, depth, past, N_HEADS_B * 2 * DK_B),
              cache_diff_v.reshape(bl, depth, past, N_HEADS_B * DV_B))
    lam_all = jnp.stack([diff_lambda_q1, diff_lambda_k1, diff_lambda_q2, diff_lambda_k2], axis=1)

    def run_pass(x3, is_ctx):
        b, n, _ = x3.shape
        t = b * n
        tm = 512
        tm_ffn = 1024
        x = x3.reshape(t, d)

        def cond_rows(tile):
            return (lambda i: 0) if is_ctx else (lambda i: 1 + (i * tile) // n)

        cond_row_fn = cond_rows(tm)
        new_kv = None
        for l in range(depth):
            lam_init = 0.8 - 0.6 * math.exp(-0.3 * l)
            res = _in_call(x, mod[l], cond_row_fn, norm1_g[l][None, :], (w_in_b, l),
                           attn_q_norm_g[l][None, :], attn_k_norm_g[l][None, :], dft_c,
                           None if is_ctx else rope_tabs, n, (depth, l, new_kv) if is_ctx else None, tm)
            qkv, y12 = res[0], res[1]
            if is_ctx:
                new_kv = res[2:]
            qkv3 = qkv.reshape(b, n, qkv.shape[1])
            att_a, att_b = _attn_calls(qkv3, None if is_ctx else caches, l, lam_all[l],
                                       diff_subnorm_g[l][None, :], lam_init,
                                       tq_a=min(512, n), tq_b=min(1024, n), kc=1024,
                                       ha=N_KV_HEADS_A if is_ctx else 1, hb=N_HEADS_B if is_ctx else 1,
                                       split_a=1 if is_ctx else 4, split_b=1 if is_ctx else 4,
                                       nb=4 if is_ctx else 1)
            if is_ctx:
                four = _fourier_call(y12.reshape(b, n, y12.shape[1]), n, 1)
            else:
                four = _fourier_call(y12.reshape(b, n, y12.shape[1]), GRID_W, n // GRID_W, tm // GRID_W)
            x1, h2 = _out_call(att_a.reshape(t, -1), att_b.reshape(t, -1), four.reshape(t, -1), (w_out_b, l),
                               x, mod[l], cond_row_fn, norm2_g[l][None, :], tm)
            x = _ffn_call(h2, x1, (w_gate_b, w_up_b, w_down_b, l), ffn_conv_w[l], ffn_conv_b[l][None, :],
                          mod[l], cond_rows(tm_ffn), final_norm_g[None, :], n, l == depth - 1, tm_ffn, FFN_TF)
        return x.reshape(b, n, d), new_kv

    y_prompt, kvs = run_pass(x_prompt, True)
    new_attn_k = kvs[0].reshape(bc, depth, lc, N_KV_HEADS_A, HEAD_DIM_A)
    new_attn_v = kvs[1].reshape(bc, depth, lc, N_KV_HEADS_A, HEAD_DIM_A)
    new_diff_k = kvs[2].reshape(bc, depth, lc, N_HEADS_B, 2 * DK_B)
    new_diff_v = kvs[3].reshape(bc, depth, lc, N_HEADS_B, DV_B)

    y_sample, _ = run_pass(x_sample, False)
    return (y_prompt, y_sample, new_attn_k, new_attn_v, new_diff_k, new_diff_v)
```

```python
import functools
import math

import jax
import jax.numpy as jnp
from jax import lax
from jax.experimental import pallas as pl
from jax.experimental.pallas import tpu as pltpu

F32 = jnp.float32
BF16 = jnp.bfloat16

GRID_W = 64
ROPE_BASE = 10000.0
NORM_EPS = 1e-6
HEAD_DIM_A = 128
N_KV_HEADS_A = 2
GQA_GROUP = 4
N_HEADS_B = 4
DK_B = 64
DV_B = 128
N_FOURIER_GROUPS = 4
FOURIER_GROUP_DIM = 128
N_MOD = 6

LOG2E = math.log2(math.e)
LANES = 128
BF16_SUBLANES = 16
VMEM_LIMIT = 58 * 1024 * 1024


def _cparams(sem, flags=None):
    return pltpu.CompilerParams(dimension_semantics=sem, vmem_limit_bytes=VMEM_LIMIT, flags=flags)


_ATTN_FLAGS = None


def _rms(x, g):
    return x * lax.rsqrt(jnp.mean(x * x, axis=-1, keepdims=True) + NORM_EPS) * g


def _ada_kernel(c_ref, w_ref, b_ref, o_ref):
    c = c_ref[...]
    s = (c * jax.nn.sigmoid(c)).astype(BF16)
    w = w_ref[...].astype(BF16)
    o_ref[...] = jnp.dot(s, w, preferred_element_type=F32) + b_ref[...]


def _ada_call(cvec, w_ada, b_ada):
    depth, d, n = w_ada.shape
    rows = cvec.shape[0]
    tn = 512
    return pl.pallas_call(
        _ada_kernel,
        out_shape=jax.ShapeDtypeStruct((depth, rows, n), F32),
        grid=(depth, n // tn),
        in_specs=[
            pl.BlockSpec((rows, d), lambda l, j: (0, 0)),
            pl.BlockSpec((None, d, tn), lambda l, j: (l, 0, j)),
            pl.BlockSpec((None, 1, tn), lambda l, j: (l, 0, j)),
        ],
        out_specs=pl.BlockSpec((None, rows, tn), lambda l, j: (l, 0, j)),
        compiler_params=_cparams(("parallel", "parallel")),
        name="ada_mod",
    )(cvec, w_ada, b_ada.reshape(depth, 1, n))


def _rope(xs, cos, sin_signed, shift):
    w = xs.shape[-1]
    lane = lax.broadcasted_iota(jnp.int32, xs.shape, 1)
    first = (lane % (2 * shift)) < shift
    rot = jnp.where(first, pltpu.roll(xs, w - shift, 1), pltpu.roll(xs, shift, 1))
    return xs * cos + rot * sin_signed


def _in_kernel(*refs, rope, ctx_out, n_alias, cols):
    it = iter(refs)
    x_ref, mod_ref, g_ref, w_ref, gq_ref, gk_ref, dft_ref = (next(it) for _ in range(7))
    if rope:
        cos_a, sin_a, cos_b, sin_b = (next(it)[...] for _ in range(4))
        perm_ref = next(it)
    for _ in range(n_alias):
        next(it)
    qkv_ref, y_ref = next(it), next(it)
    if ctx_out:
        ka_ref, va_ref, kb_ref, vb_ref = (next(it) for _ in range(4))
    c_qa, c_ka, c_va, c_qb, c_kb, c_vb, c_f, c_end = cols

    def put_cache(ref, off, v):
        head = off // LANES
        seqs, rows = ref.shape[0], ref.shape[-2]
        heads = rows * seqs // v.shape[0]
        n = rows // heads
        for s in range(seqs):
            vs = v[s * n:(s + 1) * n, :]
            if len(ref.shape) == 4:
                for dd in range(ref.shape[1]):
                    ref[s, dd, pl.ds(head, n, stride=heads), :] = vs
            else:
                ref[s, pl.ds(head, n, stride=heads), :] = vs

    m = mod_ref[...]
    h = _rms(x_ref[...], g_ref[...])
    hb = (h * (1.0 + m[1:2]) + m[0:1]).astype(BF16)
    gq = gq_ref[...]
    gk = gk_ref[...]
    scale_a = HEAD_DIM_A ** -0.5 * LOG2E
    scale_b = DK_B ** -0.5 * LOG2E
    chunk = 4 * LANES

    for c0 in range(0, c_end, chunk):
        acc = jnp.dot(hb, w_ref[:, c0:c0 + chunk], preferred_element_type=F32)
        for s in range(chunk // LANES):
            col = c0 + s * LANES
            v = acc[:, s * LANES:(s + 1) * LANES]
            if col < c_ka:
                v = _rms(v, gq)
                if rope:
                    v = _rope(v, cos_a, sin_a, HEAD_DIM_A // 4)
                qkv_ref[:, col:col + LANES] = (v * scale_a).astype(BF16)
            elif col < c_va:
                v = _rms(v, gk)
                if ctx_out:
                    put_cache(ka_ref, col - c_ka, v)
                if rope:
                    v = _rope(v, cos_a, sin_a, HEAD_DIM_A // 4)
                qkv_ref[:, col:col + LANES] = v.astype(BF16)
            elif col < c_qb:
                if ctx_out:
                    put_cache(va_ref, col - c_va, v)
                qkv_ref[:, col:col + LANES] = v.astype(BF16)
            elif col < c_kb:
                if rope:
                    v = _rope(v, cos_b, sin_b, DK_B // 4)
                qkv_ref[:, col:col + LANES] = (v * scale_b).astype(BF16)
            elif col < c_vb:
                if ctx_out:
                    put_cache(kb_ref, col - c_kb, v)
                if rope:
                    v = _rope(v, cos_b, sin_b, DK_B // 4)
                qkv_ref[:, col:col + LANES] = v.astype(BF16)
            elif col < c_f:
                if ctx_out:
                    put_cache(vb_ref, col - c_vb, v)
                qkv_ref[:, col:col + LANES] = v.astype(BF16)
            else:
                yy = jnp.dot(v.astype(BF16), dft_ref[...], preferred_element_type=F32)
                gcol = col - c_f
                half = c_end - c_f
                y_ref[:, gcol:gcol + LANES] = yy[:, :LANES].astype(BF16)
                y_ref[:, half + gcol:half + gcol + LANES] = yy[:, LANES:].astype(BF16)
    if rope:
        y_ref[...] = jnp.dot(perm_ref[...], y_ref[...], preferred_element_type=F32).astype(BF16)


def _in_call(x2d, mod_l, cond_row_fn, norm_g, w_in_b, gq, gk, dft_c, rope_tabs, seq_len, new_cache, tm):
    ctx_out = new_cache is not None
    t, d = x2d.shape
    w_in_b, w_layer = w_in_b
    d_in = w_in_b.shape[2]
    c_qa = 0
    c_ka = N_KV_HEADS_A * GQA_GROUP * HEAD_DIM_A
    c_va = c_ka + N_KV_HEADS_A * HEAD_DIM_A
    c_qb = c_va + N_KV_HEADS_A * HEAD_DIM_A
    c_kb = c_qb + N_HEADS_B * 2 * DK_B
    c_vb = c_kb + N_HEADS_B * 2 * DK_B
    c_f = c_vb + N_HEADS_B * DV_B
    c_end = c_f + N_FOURIER_GROUPS * FOURIER_GROUP_DIM
    assert c_end == d_in
    cols = (c_qa, c_ka, c_va, c_qb, c_kb, c_vb, c_f, c_end)
    rope = rope_tabs is not None
    n_f = c_end - c_f

    in_specs = [
        pl.BlockSpec((tm, d), lambda i: (i, 0)),
        pl.BlockSpec((None, N_MOD, d), lambda i: (cond_row_fn(i), 0, 0)),
        pl.BlockSpec((1, d), lambda i: (0, 0)),
        pl.BlockSpec((None, d, d_in), lambda i: (w_layer, 0, 0), pipeline_mode=pl.Buffered(1)),
        pl.BlockSpec((1, HEAD_DIM_A), lambda i: (0, 0)),
        pl.BlockSpec((1, HEAD_DIM_A), lambda i: (0, 0)),
        pl.BlockSpec((FOURIER_GROUP_DIM, 2 * FOURIER_GROUP_DIM), lambda i: (0, 0)),
    ]
    args = [x2d, mod_l, norm_g, w_in_b, gq, gk, dft_c]
    if rope:
        nblk = seq_len // tm
        for tab in rope_tabs:
            in_specs.append(pl.BlockSpec((tm, LANES), lambda i: (i % nblk, 0)))
            args.append(tab)
        r_new = jnp.arange(tm, dtype=jnp.int32)
        r_old = (r_new % (tm // GRID_W)) * GRID_W + r_new // (tm // GRID_W)
        perm = (r_old[:, None] == jnp.arange(tm, dtype=jnp.int32)[None, :]).astype(BF16)
        in_specs.append(pl.BlockSpec((tm, tm), lambda i: (0, 0)))
        args.append(perm)
    out_shape = [jax.ShapeDtypeStruct((t, c_f), BF16), jax.ShapeDtypeStruct((t, 2 * n_f), BF16)]
    out_specs = [pl.BlockSpec((tm, c_f), lambda i: (i, 0)), pl.BlockSpec((tm, 2 * n_f), lambda i: (i, 0))]
    aliases = {}
    n_alias = 0
    if ctx_out:
        depth, layer, prev = new_cache
        nb = t // seq_len
        spt = tm // seq_len
        for width in (c_va - c_ka, c_qb - c_va, c_vb - c_kb, c_f - c_vb):
            rows = seq_len * (width // LANES)
            out_shape.append(jax.ShapeDtypeStruct((nb, depth, rows, LANES), F32))
            if prev is None:
                out_specs.append(pl.BlockSpec((spt, depth, rows, LANES), lambda i: (i, 0, 0, 0)))
            else:
                out_specs.append(pl.BlockSpec((spt, None, rows, LANES), lambda i: (i, layer, 0, 0)))
        if prev is not None:
            n_alias = len(prev)
            for a, arr in enumerate(prev):
                aliases[len(args)] = 2 + a
                in_specs.append(pl.BlockSpec(memory_space=pl.ANY))
                args.append(arr)
    return pl.pallas_call(
        functools.partial(_in_kernel, rope=rope, ctx_out=ctx_out, n_alias=n_alias, cols=cols),
        out_shape=out_shape,
        grid=(t // tm,),
        in_specs=in_specs,
        out_specs=out_specs,
        input_output_aliases=aliases,
        compiler_params=_cparams(("parallel",)),
        name="in_proj_ctx" if ctx_out else "in_proj_lat",
    )(*args)


def _softmax_pv(q, srcs):
    chunks = []
    for k_ref, v_ref, n_keys, kc, col in srcs:
        for c in range(n_keys // kc):
            chunks.append((k_ref, v_ref, c * kc, kc, col))

    def scores(ch):
        k_ref, _, off, kc, col = ch
        k = k_ref[off:off + kc, col:col + LANES].astype(BF16)
        return lax.dot_general(q, k, (((1,), (1,)), ((), ())), preferred_element_type=F32)

    def lane_fold(x, op):
        out = x[:, 0:LANES]
        for t in range(1, x.shape[1] // LANES):
            out = op(out, x[:, t * LANES:(t + 1) * LANES])
        return out

    mxu_sum = len(chunks) > 1
    m = acc = l_part = None
    s_next = scores(chunks[0])
    for i, ch in enumerate(chunks):
        s = s_next
        if i + 1 < len(chunks):
            s_next = scores(chunks[i + 1])
        _, v_ref, off, kc, col = ch
        row_max = jnp.max(lane_fold(s, jnp.maximum), axis=-1, keepdims=True)
        m_new = row_max if m is None else jnp.maximum(m, row_max)
        p = jnp.exp2(s - m_new)
        v = v_ref[off:off + kc, col:col + LANES].astype(BF16)
        if mxu_sum:
            ones_col = (lax.broadcasted_iota(jnp.int32, (kc, LANES), 1) == 0).astype(BF16)
            v = jnp.concatenate([v, ones_col], axis=1)
        else:
            l_part = lane_fold(p, jnp.add)
        pv = jnp.dot(p.astype(BF16), v, preferred_element_type=F32)
        acc = pv if m is None else jnp.exp2(m - m_new) * acc + pv
        m = m_new
    if mxu_sum:
        return acc[:, :LANES] / acc[:, LANES:LANES + 1]
    return acc / jnp.sum(l_part, axis=-1, keepdims=True)


def _srcs(cache_refs, k_ref, v_ref, kc, head):
    srcs = []
    col = head * LANES
    if cache_refs is not None:
        ck, cv = cache_refs
        srcs.append((ck, cv, ck.shape[0], ck.shape[0], col))
    n = k_ref.shape[0]
    srcs.append((k_ref, v_ref, n, min(kc, n), col))
    return srcs


def _attn_a_kernel(*refs, has_cache, tq, kc, heads, split):
    if has_cache:
        q_ref, ck_ref, cv_ref, k_ref, v_ref, o_ref = refs
        cache = (ck_ref, cv_ref)
    else:
        q_ref, k_ref, v_ref, o_ref = refs
        cache = None
    per = GQA_GROUP // split
    for bb in range(q_ref.shape[0]):
        qb, kb, vb, ob = q_ref.at[bb], k_ref.at[bb], v_ref.at[bb], o_ref.at[bb]
        for kh in range(heads):
            for part in range(split):
                cols = [(kh * GQA_GROUP + part * per + h) * LANES for h in range(per)]
                qs = jnp.concatenate([qb[:, c:c + LANES] for c in cols], axis=0)
                o = _softmax_pv(qs, _srcs(cache, kb, vb, kc, kh))
                for h, c in enumerate(cols):
                    ob[:, c:c + LANES] = o[h * tq:(h + 1) * tq].astype(BF16)


def _attn_b_kernel(*refs, has_cache, tq, kc, heads, split, lam_init):
    if has_cache:
        lam_ref, gs_ref, q_ref, ck_ref, cv_ref, k_ref, v_ref, o_ref = refs
        cache = (ck_ref, cv_ref)
    else:
        lam_ref, gs_ref, q_ref, k_ref, v_ref, o_ref = refs
        cache = None
    lp = lam_ref[...]
    lam = (jnp.exp(jnp.sum(lp[0:1] * lp[1:2], axis=-1, keepdims=True))
           - jnp.exp(jnp.sum(lp[2:3] * lp[3:4], axis=-1, keepdims=True)) + lam_init)
    rows = tq // split
    for bb in range(q_ref.shape[0]):
        qb, kb, vb, ob = q_ref.at[bb], k_ref.at[bb], v_ref.at[bb], o_ref.at[bb]
        for hd in range(heads):
            for part in range(split):
                q = qb[part * rows:(part + 1) * rows, hd * LANES:(hd + 1) * LANES]
                lane = lax.broadcasted_iota(jnp.int32, q.shape, 1)
                zero = jnp.zeros_like(q)
                qz = jnp.concatenate([jnp.where(lane < DK_B, q, zero), jnp.where(lane >= DK_B, q, zero)], axis=0)
                o = _softmax_pv(qz, _srcs(cache, kb, vb, kc, hd))
                dlt = o[:rows] - lam * o[rows:]
                ob[part * rows:(part + 1) * rows, hd * LANES:(hd + 1) * LANES] = (
                    _rms(dlt, gs_ref[...]) * (1.0 - lam_init)).astype(BF16)


def _attn_calls(qkv3, caches, layer, lam_params, g_sub, lam_init, tq_a, tq_b, kc, ha, hb, split_a, split_b, nb):
    b, n, _ = qkv3.shape
    has_cache = caches is not None
    assert b % nb == 0 and (nb == 1 or not has_cache)
    qa_blk =GQA_GROUP * HEAD_DIM_A // LANES
    k_a0 = N_KV_HEADS_A * qa_blk
    v_a0 = k_a0 + N_KV_HEADS_A
    q_b0 = v_a0 + N_KV_HEADS_A
    k_b0 = q_b0 + N_HEADS_B
    v_b0 = k_b0 + N_HEADS_B
    assert all(x % ha == 0 for x in (N_KV_HEADS_A, k_a0, v_a0)) and all(x % hb == 0 for x in (N_HEADS_B, q_b0, k_b0, v_b0))

    qw, kw = ha * GQA_GROUP * LANES, ha * LANES
    in_specs = [pl.BlockSpec((nb, tq_a, qw), lambda bi, h, i: (bi, i, h))]
    args = [qkv3]
    if has_cache:
        ck, cv = caches[0], caches[1]
        p = ck.shape[2]
        in_specs += [pl.BlockSpec((None, None, p, kw), lambda bi, h, i: (bi, layer, 0, h))] * 2
        args += [ck, cv]
    in_specs += [pl.BlockSpec((nb, n, kw), lambda bi, h, i: (bi, 0, k_a0 // ha + h)),
                 pl.BlockSpec((nb, n, kw), lambda bi, h, i: (bi, 0, v_a0 // ha + h))]
    args += [qkv3, qkv3]
    att_a = pl.pallas_call(
        functools.partial(_attn_a_kernel, has_cache=has_cache, tq=tq_a, kc=kc, heads=ha, split=split_a),
        out_shape=jax.ShapeDtypeStruct((b, n, N_KV_HEADS_A * GQA_GROUP * HEAD_DIM_A), BF16),
        grid=(b // nb, N_KV_HEADS_A // ha, n // tq_a),
        in_specs=in_specs,
        out_specs=pl.BlockSpec((nb, tq_a, qw), lambda bi, h, i: (bi, i, h)),
        compiler_params=_cparams(("parallel", "parallel", "arbitrary"), _ATTN_FLAGS),
        name="attn_a_lat" if has_cache else "attn_a_ctx",
    )(*args)

    bw = hb * LANES
    in_specs = [pl.BlockSpec((4, DK_B), lambda bi, h, i: (0, 0)),
                pl.BlockSpec((1, DV_B), lambda bi, h, i: (0, 0)),
                pl.BlockSpec((nb, tq_b, bw), lambda bi, h, i: (bi, i, q_b0 // hb + h))]
    args = [lam_params, g_sub, qkv3]
    if has_cache:
        ck, cv = caches[2], caches[3]
        p = ck.shape[2]
        in_specs += [pl.BlockSpec((None, None, p, bw), lambda bi, h, i: (bi, layer, 0, h))] * 2
        args += [ck, cv]
    in_specs += [pl.BlockSpec((nb, n, bw), lambda bi, h, i: (bi, 0, k_b0 // hb + h)),
                 pl.BlockSpec((nb, n, bw), lambda bi, h, i: (bi, 0, v_b0 // hb + h))]
    args += [qkv3, qkv3]
    att_b = pl.pallas_call(
        functools.partial(_attn_b_kernel, has_cache=has_cache, tq=tq_b, kc=kc, heads=hb, split=split_b, lam_init=lam_init),
        out_shape=jax.ShapeDtypeStruct((b, n, N_HEADS_B * DV_B), BF16),
        grid=(b // nb, N_HEADS_B // hb, n // tq_b),
        in_specs=in_specs,
        out_specs=pl.BlockSpec((nb, tq_b, bw), lambda bi, h, i: (bi, i, h)),
        compiler_params=_cparams(("parallel", "parallel", "arbitrary"), _ATTN_FLAGS),
        name="attn_b_lat" if has_cache else "attn_b_ctx",
    )(*args)
    return att_a, att_b


def _dft_cols_kernel(x_ref, mat_ref, o_ref, *, width, scale):
    p, g, w2 = x_ref.shape
    x = x_ref[...].reshape(p * g, w2)
    xs = jnp.concatenate([x[:, :width], x[:, width:]], axis=0)
    u = jnp.dot(mat_ref[...], xs, preferred_element_type=F32)
    o_ref[...] = (u * scale).astype(BF16).reshape(o_ref.shape)


def _dft_rows_kernel(x_ref, mat_ref, tc_ref, ts_ref, o_ref, *, width):
    tiles, rows, w2 = x_ref.shape
    pair = BF16_SUBLANES
    reps = width // LANES
    for p in range(rows // pair):
        x = x_ref[:, p * pair:(p + 1) * pair, :].reshape(tiles * pair, w2)
        xs = jnp.concatenate([x[:, :width], x[:, width:]], axis=0)
        u = jnp.dot(mat_ref[...], xs, preferred_element_type=F32)
        half = u.shape[0] // 2
        ur, ui = u[:half], u[half:]
        tc = jnp.concatenate([tc_ref[2 * p:2 * p + 2].reshape(half, LANES)] * reps, axis=1)
        ts = jnp.concatenate([ts_ref[2 * p:2 * p + 2].reshape(half, LANES)] * reps, axis=1)
        o_ref[2 * p:2 * p + 2, :, :width] = (ur * tc - ui * ts).astype(BF16).reshape(2, half // 2, width)
        o_ref[2 * p:2 * p + 2, :, width:] = (ur * ts + ui * tc).astype(BF16).reshape(2, half // 2, width)


def _dft_seq_kernel(x_ref, mat_ref, o_ref, *, width, scale):
    nb = x_ref.shape[0]
    xs = jnp.concatenate(
        [jnp.concatenate([x_ref[bb, :, :width], x_ref[bb, :, width:]], axis=0) for bb in range(nb)], axis=1)
    u = jnp.dot(mat_ref[...], xs, preferred_element_type=F32)
    for bb in range(nb):
        o_ref[bb] = (u[:, bb * width:(bb + 1) * width] * scale).astype(BF16)


def _cos_sin(n_rows, n_cols, period):
    a = jnp.arange(n_rows, dtype=jnp.int32)[:, None]
    b = jnp.arange(n_cols, dtype=jnp.int32)[None, :]
    ang = ((a * b) % period).astype(F32) * (2.0 * math.pi / period)
    return jnp.cos(ang), jnp.sin(ang)


def _fourier_call(y3, n1, n2, tile_n2=None):
    b, n, w2 = y3.shape
    width = w2 // 2
    scale = 1.0 / math.sqrt(n * FOURIER_GROUP_DIM)
    c1, s1 = _cos_sin(n1, n1, n1)
    w_real = jnp.stack([c1, -s1], axis=1)
    if n2 == 1:
        mat = w_real.reshape(n1, 2 * n1).astype(BF16)
        nb = math.gcd(b, 4)
        return pl.pallas_call(
            functools.partial(_dft_seq_kernel, width=width, scale=scale),
            out_shape=jax.ShapeDtypeStruct((b, n, width), BF16),
            grid=(b // nb,),
            in_specs=[pl.BlockSpec((nb, n, w2), lambda bi: (bi, 0, 0)),
                      pl.BlockSpec((n1, 2 * n1), lambda bi: (0, 0))],
            out_specs=pl.BlockSpec((nb, n, width), lambda bi: (bi, 0, 0)),
            compiler_params=_cparams(("parallel",)),
            name="dft_ctx",
        )(y3, mat)

    g = BF16_SUBLANES
    c2, s2 = _cos_sin(n2, n2, n2)
    w_cplx = jnp.stack([jnp.stack([c2, -s2], axis=1), jnp.stack([s2, c2], axis=1)], axis=0)

    def kron_cols(base, row_j):
        rows, cols = base.shape
        col = lax.broadcasted_iota(jnp.int32, (cols, cols * g), 1)
        expand = (col // g == lax.broadcasted_iota(jnp.int32, (cols, cols * g), 0)).astype(BF16)
        wide = jnp.dot(base.astype(BF16), expand, preferred_element_type=F32)
        keep = row_j[:, None] == (lax.broadcasted_iota(jnp.int32, (rows, cols * g), 1) % g)
        return jnp.where(keep, wide, 0.0).astype(BF16)

    base2 = jnp.broadcast_to(w_real.reshape(n1, 1, 2 * n1), (n1, g, 2 * n1)).reshape(n1 * g, 2 * n1)
    mat2 = kron_cols(base2, jnp.arange(n1 * g, dtype=jnp.int32) % g)
    tc, ts = _cos_sin(n1, n2, n)
    tc = jnp.broadcast_to(tc[:, :, None], (n1, n2, LANES))
    ts = jnp.broadcast_to(ts[:, :, None], (n1, n2, LANES))
    tiles = n2 // tile_n2
    tile_len = n1 * tile_n2
    mat1 = jnp.einsum('akbtl,ji->ajkbtil', w_cplx.reshape(2, n2, 2, tiles, tile_n2), jnp.eye(2, dtype=F32))
    mat1 = mat1.reshape(4 * n2, 4 * n2).astype(BF16)
    gp = 8
    t = pl.pallas_call(
        functools.partial(_dft_rows_kernel, width=width),
        out_shape=jax.ShapeDtypeStruct((b, n1, n2, w2), BF16),
        grid=(b, tile_len // (BF16_SUBLANES * gp)),
        in_specs=[pl.BlockSpec((None, tiles, BF16_SUBLANES * gp, w2), lambda bi, j: (bi, 0, j, 0)),
                  pl.BlockSpec(mat1.shape, lambda bi, j: (0, 0)),
                  pl.BlockSpec((2 * gp, n2, LANES), lambda bi, j: (j, 0, 0)),
                  pl.BlockSpec((2 * gp, n2, LANES), lambda bi, j: (j, 0, 0))],
        out_specs=pl.BlockSpec((None, 2 * gp, n2, w2), lambda bi, j: (bi, j, 0, 0)),
        compiler_params=_cparams(("parallel", "parallel")),
        name="dft_stage1",
    )(y3.reshape(b, tiles, tile_len, w2), mat1, tc, ts)
    out = pl.pallas_call(
        functools.partial(_dft_cols_kernel, width=width, scale=scale),
        out_shape=jax.ShapeDtypeStruct((b, n1, n2, width), BF16),
        grid=(b, n2 // g),
        in_specs=[pl.BlockSpec((None, n1, g, w2), lambda bi, j: (bi, 0, j, 0)),
                  pl.BlockSpec(mat2.shape, lambda bi, j: (0, 0), pipeline_mode=pl.Buffered(1))],
        out_specs=pl.BlockSpec((None, n1, g, width), lambda bi, j: (bi, 0, j, 0)),
        compiler_params=_cparams(("parallel", "parallel")),
        name="dft_stage2",
    )(t, mat2)
    return out.reshape(b, n, width)


def _out_kernel(a_ref, b_ref, f_ref, w_ref, x_ref, mod_ref, g_ref, x1_ref, h2_ref):
    ca = a_ref.shape[1]
    cb = b_ref.shape[1]
    m = mod_ref[...]
    half = a_ref.shape[0] // 2
    for r0 in (0, half):
        rows = slice(r0, r0 + half)
        acc = jnp.dot(a_ref[rows, :], w_ref[0:ca, :], preferred_element_type=F32)
        acc += jnp.dot(b_ref[rows, :], w_ref[ca:ca + cb, :], preferred_element_type=F32)
        acc += jnp.dot(f_ref[rows, :], w_ref[ca + cb:, :], preferred_element_type=F32)
        x1 = x_ref[rows, :] + m[2:3] * acc
        x1_ref[rows, :] = x1
        h = _rms(x1, g_ref[...])
        h2_ref[rows, :] = (h * (1.0 + m[4:5]) + m[3:4]).astype(BF16)


def _out_call(att_a, att_b, four, w_out_b, x2d, mod_l, cond_row_fn, norm_g, tm):
    t, d = x2d.shape
    ca, cb, cf = att_a.shape[1], att_b.shape[1], four.shape[1]
    w_out_b, w_layer = w_out_b
    return pl.pallas_call(
        _out_kernel,
        out_shape=[jax.ShapeDtypeStruct((t, d), F32), jax.ShapeDtypeStruct((t, d), BF16)],
        grid=(t // tm,),
        in_specs=[
            pl.BlockSpec((tm, ca), lambda i: (i, 0)),
            pl.BlockSpec((tm, cb), lambda i: (i, 0)),
            pl.BlockSpec((tm, cf), lambda i: (i, 0)),
            pl.BlockSpec((None, ca + cb + cf, d), lambda i: (w_layer, 0, 0), pipeline_mode=pl.Buffered(1)),
            pl.BlockSpec((tm, d), lambda i: (i, 0)),
            pl.BlockSpec((None, N_MOD, d), lambda i: (cond_row_fn(i), 0, 0)),
            pl.BlockSpec((1, d), lambda i: (0, 0)),
        ],
        out_specs=[pl.BlockSpec((tm, d), lambda i: (i, 0)), pl.BlockSpec((tm, d), lambda i: (i, 0))],
        compiler_params=_cparams(("parallel",)),
        name="out_proj",
    )(att_a, att_b, four, w_out_b, x2d, mod_l, norm_g)


HALO = 16
FFN_TF = 512


def _ffn_kernel(cw_ref, cb_ref, mod_ref, gf_ref, h_hbm, x1_hbm, wg_hbm, wu_hbm, wd_hbm,
                o_ref, hext_ref, x1_buf, wg_buf, wu_buf, wd_buf, sem, *, tm, tf, nj, layer, seq_len, final):
    i = pl.program_id(0)
    n_i = pl.num_programs(0)
    row0 = pl.multiple_of(i * tm, tm)

    def weight_copies(j, slot):
        cols = pl.ds(pl.multiple_of(j * tf, tf), tf)
        return (pltpu.make_async_copy(wg_hbm.at[layer, :, cols], wg_buf.at[slot], sem.at[0, slot]),
                pltpu.make_async_copy(wu_hbm.at[layer, :, cols], wu_buf.at[slot], sem.at[1, slot]),
                pltpu.make_async_copy(wd_hbm.at[layer, cols, :], wd_buf.at[slot], sem.at[2, slot]))

    def tile_copies(ti, slot):
        start = pl.multiple_of(ti * tm, tm)
        before = pl.multiple_of(jnp.maximum(start - HALO, 0), HALO)
        after = pl.multiple_of(jnp.minimum(start + tm, n_i * tm - HALO), HALO)
        dst = hext_ref.at[slot]
        return (pltpu.make_async_copy(h_hbm.at[pl.ds(before, HALO), :], dst.at[pl.ds(0, HALO), :], sem.at[3, slot]),
                pltpu.make_async_copy(h_hbm.at[pl.ds(start, tm), :], dst.at[pl.ds(HALO, tm), :], sem.at[4, slot]),
                pltpu.make_async_copy(h_hbm.at[pl.ds(after, HALO), :], dst.at[pl.ds(HALO + tm, HALO), :],
                                      sem.at[5, slot]))

    x1_copy = pltpu.make_async_copy(x1_hbm.at[pl.ds(row0, tm), :], x1_buf, sem.at[6, 0])
    hslot = i % 2

    @pl.when(i == 0)
    def _():
        for cp in tile_copies(0, 0) + weight_copies(0, 0):
            cp.start()

    x1_copy.start()

    @pl.when(i + 1 < n_i)
    def _():
        for cp in tile_copies(i + 1, 1 - hslot):
            cp.start()

    o_ref[...] = jnp.zeros_like(o_ref)
    ext = tm + 2 * HALO
    pos = (i * tm + lax.broadcasted_iota(jnp.int32, (tm, 1), 0)) % seq_len
    for cp in tile_copies(i, hslot):
        cp.wait()
    hext = hext_ref.at[hslot]

    def chunk(j, carry):
        step = i * nj + j
        slot = step % 2
        for cp in weight_copies(j, slot):
            cp.wait()

        @pl.when(step + 1 < n_i * nj)
        def _():
            for cp in weight_copies((j + 1) % nj, 1 - slot):
                cp.start()

        g = jnp.dot(hext[...], wg_buf[slot], preferred_element_type=F32)
        u = jnp.dot(hext[HALO:HALO + tm, :], wu_buf[slot], preferred_element_type=F32)
        g_prev = jnp.where(pos == 0, 0.0, pltpu.roll(g, 1, 0)[HALO:HALO + tm])
        g_next = jnp.where(pos == seq_len - 1, 0.0, pltpu.roll(g, ext - 1, 0)[HALO:HALO + tm])
        cw = cw_ref[j]
        gc = g_prev * cw[0:1] + g[HALO:HALO + tm] * cw[1:2] + g_next * cw[2:3] + cb_ref[j]
        act = (gc * jax.nn.sigmoid(gc)) * u
        o_ref[...] += jnp.dot(act.astype(BF16), wd_buf[slot], preferred_element_type=F32)
        return carry

    lax.fori_loop(0, nj, chunk, 0)

    x1_copy.wait()
    m = mod_ref[...]
    x2 = x1_buf[...] + m[5:6] * o_ref[...]
    if final:
        x2 = _rms(x2, gf_ref[...])
    o_ref[...] = x2


def _ffn_call(h2, x1, weights, conv_w, conv_b, mod_l, cond_row_fn, final_g, seq_len, final, tm, tf):
    t, d = x1.shape
    w_gate_b, w_up_b, w_down_b, w_layer = weights
    f = w_gate_b.shape[2]
    nj = f // tf
    cw3 = conv_w.reshape(3, nj, tf).transpose(1, 0, 2)
    cb3 = conv_b.reshape(nj, 1, tf)
    hbm = pl.BlockSpec(memory_space=pl.ANY)
    return pl.pallas_call(
        functools.partial(_ffn_kernel, tm=tm, tf=tf, nj=nj, layer=w_layer, seq_len=seq_len, final=final),
        out_shape=jax.ShapeDtypeStruct((t, d), F32),
        grid=(t // tm,),
        in_specs=[
            pl.BlockSpec((nj, 3, tf), lambda i: (0, 0, 0)),
            pl.BlockSpec((nj, 1, tf), lambda i: (0, 0, 0)),
            pl.BlockSpec((None, N_MOD, d), lambda i: (cond_row_fn(i), 0, 0)),
            pl.BlockSpec((1, d), lambda i: (0, 0)),
            hbm, hbm, hbm, hbm, hbm,
        ],
        out_specs=pl.BlockSpec((tm, d), lambda i: (i, 0)),
        scratch_shapes=[pltpu.VMEM((2, tm + 2 * HALO, d), BF16), pltpu.VMEM((tm, d), F32),
                        pltpu.VMEM((2, d, tf), BF16), pltpu.VMEM((2, d, tf), BF16), pltpu.VMEM((2, tf, d), BF16),
                        pltpu.SemaphoreType.DMA((7, 2))],
        compiler_params=_cparams(("arbitrary",)),
        name="conv_ffn",
    )(cw3, cb3, mod_l, final_g, h2, x1, w_gate_b, w_up_b, w_down_b)


def _rope_tables(n, head_dim):
    rows = n // GRID_W
    t_row = jnp.repeat(jnp.arange(rows, dtype=F32), GRID_W)
    t_col = jnp.tile(jnp.arange(GRID_W, dtype=F32), rows)
    axis_dim = head_dim // 2
    inv = jnp.power(ROPE_BASE, -jnp.arange(0, axis_dim, 2, dtype=F32) / axis_dim)
    ar = t_row[:, None] * inv[None, :]
    ac = t_col[:, None] * inv[None, :]
    ang = jnp.concatenate([ar, ar, ac, ac], axis=-1)
    reps = LANES // head_dim
    quarter = head_dim // 4
    sign = jnp.where((jnp.arange(head_dim) % (2 * quarter)) < quarter, -1.0, 1.0).astype(F32)
    cos = jnp.tile(jnp.cos(ang), (1, reps))
    sin_signed = jnp.tile(jnp.sin(ang) * sign[None, :], (1, reps))
    return cos, sin_signed


def kernel(x_prompt, x_sample, cache_attn_k, cache_attn_v, cache_diff_k, cache_diff_v, c, c_ctx, norm1_g, norm2_g, w_ada, b_ada, w_in, attn_q_norm_g, attn_k_norm_g, diff_lambda_q1, diff_lambda_k1, diff_lambda_q2, diff_lambda_k2, diff_subnorm_g, w_out, ffn_w_gate, ffn_w_up, ffn_conv_w, ffn_conv_b, ffn_w_down, final_norm_g):
    depth = w_in.shape[0]
    bc, lc, d = x_prompt.shape
    bl, ll, _ = x_sample.shape
    past = cache_attn_k.shape[2]

    w_in_b = w_in.astype(BF16)
    w_out_b = w_out.astype(BF16)
    w_gate_b = ffn_w_gate.astype(BF16)
    w_up_b = ffn_w_up.astype(BF16)
    w_down_b = ffn_w_down.astype(BF16)

    n_rows = 8 * ((1 + bl + 7) // 8)
    cvec = jnp.concatenate([c_ctx[None, :], c, jnp.zeros((n_rows - 1 - bl, d), F32)], axis=0)
    mod = _ada_call(cvec, w_ada, b_ada).reshape(depth, n_rows, N_MOD, d)

    dft_c = jnp.concatenate(_cos_sin(FOURIER_GROUP_DIM, FOURIER_GROUP_DIM, FOURIER_GROUP_DIM), axis=1).astype(BF16)
    rope_tabs = _rope_tables(ll, HEAD_DIM_A) + _rope_tables(ll, DK_B)
    caches = (cache_attn_k.reshape(bl, depth, past, N_KV_HEADS_A * HEAD_DIM_A),
              cache_attn_v.reshape(bl, depth, past, N_KV_HEADS_A * HEAD_DIM_A),
              cache_diff_k.reshape(bl, depth, past, N_HEADS_B * 2 * DK_B),
              cache_diff_v.reshape(bl, depth, past, N_HEADS_B * DV_B))
    lam_all = jnp.stack([diff_lambda_q1, diff_lambda_k1, diff_lambda_q2, diff_lambda_k2], axis=1)

    def run_pass(x3, is_ctx):
        b, n, _ = x3.shape
        t = b * n
        tm = 512
        tm_ffn = 1024
        x = x3.reshape(t, d)

        def cond_rows(tile):
            return (lambda i: 0) if is_ctx else (lambda i: 1 + (i * tile) // n)

        cond_row_fn = cond_rows(tm)
        new_kv = None
        for l in range(depth):
            lam_init = 0.8 - 0.6 * math.exp(-0.3 * l)
            res = _in_call(x, mod[l], cond_row_fn, norm1_g[l][None, :], (w_in_b, l),
                           attn_q_norm_g[l][None, :], attn_k_norm_g[l][None, :], dft_c,
                           None if is_ctx else rope_tabs, n, (depth, l, new_kv) if is_ctx else None, tm)
            qkv, y12 = res[0], res[1]
            if is_ctx:
                new_kv = res[2:]
            qkv3 = qkv.reshape(b, n, qkv.shape[1])
            att_a, att_b = _attn_calls(qkv3, None if is_ctx else caches, l, lam_all[l],
                                       diff_subnorm_g[l][None, :], lam_init,
                                       tq_a=min(512, n), tq_b=min(1024, n), kc=1024,
                                       ha=N_KV_HEADS_A if is_ctx else 1, hb=N_HEADS_B if is_ctx else 1,
                                       split_a=1 if is_ctx else 4, split_b=1 if is_ctx else 4,
                                       nb=4 if is_ctx else 1)
            if is_ctx:
                four = _fourier_call(y12.reshape(b, n, y12.shape[1]), n, 1)
            else:
                four = _fourier_call(y12.reshape(b, n, y12.shape[1]), GRID_W, n // GRID_W, tm // GRID_W)
            x1, h2 = _out_call(att_a.reshape(t, -1), att_b.reshape(t, -1), four.reshape(t, -1), (w_out_b, l),
                               x, mod[l], cond_row_fn, norm2_g[l][None, :], tm)
            x = _ffn_call(h2, x1, (w_gate_b, w_up_b, w_down_b, l), ffn_conv_w[l], ffn_conv_b[l][None, :],
                          mod[l], cond_rows(tm_ffn), final_norm_g[None, :], n, l == depth - 1, tm_ffn, FFN_TF)
        return x.reshape(b, n, d), new_kv

    y_prompt, kvs = run_pass(x_prompt, True)
    new_attn_k = kvs[0].reshape(bc, depth, lc, N_KV_HEADS_A, HEAD_DIM_A)
    new_attn_v = kvs[1].reshape(bc, depth, lc, N_KV_HEADS_A, HEAD_DIM_A)
    new_diff_k = kvs[2].reshape(bc, depth, lc, N_HEADS_B, 2 * DK_B)
    new_diff_v = kvs[3].reshape(bc, depth, lc, N_HEADS_B, DV_B)

    y_sample, _ = run_pass(x_sample, False)
    return (y_prompt, y_sample, new_attn_k, new_attn_v, new_diff_k, new_diff_v)
```

```python
import functools
import math

import jax
import jax.numpy as jnp
from jax import lax
from jax.experimental import pallas as pl
from jax.experimental.pallas import tpu as pltpu

F32 = jnp.float32
BF16 = jnp.bfloat16

GRID_W = 64
ROPE_BASE = 10000.0
NORM_EPS = 1e-6
HEAD_DIM_A = 128
N_KV_HEADS_A = 2
GQA_GROUP = 4
N_HEADS_B = 4
DK_B = 64
DV_B = 128
N_FOURIER_GROUPS = 4
FOURIER_GROUP_DIM = 128
N_MOD = 6

LOG2E = math.log2(math.e)
LANES = 128
BF16_SUBLANES = 16
VMEM_LIMIT = 58 * 1024 * 1024

ROW_TILE = 512
FFN_ROW_TILE = 1024
FFN_TF = 512
ADA_TN = 1024
ATTN_TQ_A = 512
ATTN_TQ_B = 1024
ATTN_KC = 1024
ATTN_STREAMS = 4
CTX_SEQS = 4


def _cparams(sem):
    return pltpu.CompilerParams(dimension_semantics=sem, vmem_limit_bytes=VMEM_LIMIT)


def _rms(x, g):
    return x * lax.rsqrt(jnp.mean(x * x, axis=-1, keepdims=True) + NORM_EPS) * g


def _ada_kernel(c_ref, w_ref, b_ref, o_ref):
    c = c_ref[...]
    s = (c * jax.nn.sigmoid(c)).astype(BF16)
    w = w_ref[...].astype(BF16)
    o_ref[...] = jnp.dot(s, w, preferred_element_type=F32) + b_ref[...]


def _ada_call(cvec, w_ada, b_ada):
    depth, d, n = w_ada.shape
    rows = cvec.shape[0]
    tn = ADA_TN
    return pl.pallas_call(
        _ada_kernel,
        out_shape=jax.ShapeDtypeStruct((depth, rows, n), F32),
        grid=(depth, n // tn),
        in_specs=[
            pl.BlockSpec((rows, d), lambda l, j: (0, 0)),
            pl.BlockSpec((None, d, tn), lambda l, j: (l, 0, j)),
            pl.BlockSpec((None, 1, tn), lambda l, j: (l, 0, j)),
        ],
        out_specs=pl.BlockSpec((None, rows, tn), lambda l, j: (l, 0, j)),
        compiler_params=_cparams(("parallel", "parallel")),
        name="ada_mod",
    )(cvec, w_ada, b_ada.reshape(depth, 1, n))


def _rope(xs, cos, sin_signed, shift):
    w = xs.shape[-1]
    lane = lax.broadcasted_iota(jnp.int32, xs.shape, 1)
    first = (lane % (2 * shift)) < shift
    rot = jnp.where(first, pltpu.roll(xs, w - shift, 1), pltpu.roll(xs, shift, 1))
    return xs * cos + rot * sin_signed


def _in_kernel(*refs, rope, ctx_out, n_alias, cols):
    it = iter(refs)
    x_ref, mod_ref, g_ref, w_ref, gq_ref, gk_ref, dft_ref = (next(it) for _ in range(7))
    if rope:
        cos_a, sin_a, cos_b, sin_b = (next(it)[...] for _ in range(4))
        perm_ref = next(it)
    for _ in range(n_alias):
        next(it)
    qkv_ref, y_ref = next(it), next(it)
    if ctx_out:
        ka_ref, va_ref, kb_ref, vb_ref = (next(it) for _ in range(4))
    c_qa, c_ka, c_va, c_qb, c_kb, c_vb, c_f, c_end = cols

    def put_cache(ref, off, v):
        head = off // LANES
        seqs, rows = ref.shape[0], ref.shape[-2]
        heads = rows * seqs // v.shape[0]
        n = rows // heads
        for s in range(seqs):
            vs = v[s * n:(s + 1) * n, :]
            if len(ref.shape) == 4:
                for dd in range(ref.shape[1]):
                    ref[s, dd, pl.ds(head, n, stride=heads), :] = vs
            else:
                ref[s, pl.ds(head, n, stride=heads), :] = vs

    m = mod_ref[...]
    h = _rms(x_ref[...], g_ref[...])
    hb = (h * (1.0 + m[1:2]) + m[0:1]).astype(BF16)
    gq = gq_ref[...]
    gk = gk_ref[...]
    scale_a = HEAD_DIM_A ** -0.5 * LOG2E
    scale_b = DK_B ** -0.5 * LOG2E
    chunk = 4 * LANES

    for c0 in range(0, c_end, chunk):
        acc = jnp.dot(hb, w_ref[:, c0:c0 + chunk], preferred_element_type=F32)
        for s in range(chunk // LANES):
            col = c0 + s * LANES
            v = acc[:, s * LANES:(s + 1) * LANES]
            if col < c_ka:
                v = _rms(v, gq)
                if rope:
                    v = _rope(v, cos_a, sin_a, HEAD_DIM_A // 4)
                qkv_ref[:, col:col + LANES] = (v * scale_a).astype(BF16)
            elif col < c_va:
                v = _rms(v, gk)
                if ctx_out:
                    put_cache(ka_ref, col - c_ka, v)
                if rope:
                    v = _rope(v, cos_a, sin_a, HEAD_DIM_A // 4)
                qkv_ref[:, col:col + LANES] = v.astype(BF16)
            elif col < c_qb:
                if ctx_out:
                    put_cache(va_ref, col - c_va, v)
                qkv_ref[:, col:col + LANES] = v.astype(BF16)
            elif col < c_kb:
                if rope:
                    v = _rope(v, cos_b, sin_b, DK_B // 4)
                qkv_ref[:, col:col + LANES] = (v * scale_b).astype(BF16)
            elif col < c_vb:
                if ctx_out:
                    put_cache(kb_ref, col - c_kb, v)
                if rope:
                    v = _rope(v, cos_b, sin_b, DK_B // 4)
                qkv_ref[:, col:col + LANES] = v.astype(BF16)
            elif col < c_f:
                if ctx_out:
                    put_cache(vb_ref, col - c_vb, v)
                qkv_ref[:, col:col + LANES] = v.astype(BF16)
            else:
                yy = jnp.dot(v.astype(BF16), dft_ref[...], preferred_element_type=F32)
                gcol = col - c_f
                half = c_end - c_f
                y_ref[:, gcol:gcol + LANES] = yy[:, :LANES].astype(BF16)
                y_ref[:, half + gcol:half + gcol + LANES] = yy[:, LANES:].astype(BF16)
    if rope:
        y_ref[...] = jnp.dot(perm_ref[...], y_ref[...], preferred_element_type=F32).astype(BF16)


def _in_call(x2d, mod_l, cond_row_fn, norm_g, w_in_b, gq, gk, dft_c, rope_tabs, seq_len, new_cache, tm):
    ctx_out = new_cache is not None
    t, d = x2d.shape
    w_in_b, w_layer = w_in_b
    d_in = w_in_b.shape[2]
    c_qa = 0
    c_ka = N_KV_HEADS_A * GQA_GROUP * HEAD_DIM_A
    c_va = c_ka + N_KV_HEADS_A * HEAD_DIM_A
    c_qb = c_va + N_KV_HEADS_A * HEAD_DIM_A
    c_kb = c_qb + N_HEADS_B * 2 * DK_B
    c_vb = c_kb + N_HEADS_B * 2 * DK_B
    c_f = c_vb + N_HEADS_B * DV_B
    c_end = c_f + N_FOURIER_GROUPS * FOURIER_GROUP_DIM
    assert c_end == d_in
    cols = (c_qa, c_ka, c_va, c_qb, c_kb, c_vb, c_f, c_end)
    rope = rope_tabs is not None
    n_f = c_end - c_f

    in_specs = [
        pl.BlockSpec((tm, d), lambda i: (i, 0)),
        pl.BlockSpec((None, N_MOD, d), lambda i: (cond_row_fn(i), 0, 0)),
        pl.BlockSpec((1, d), lambda i: (0, 0)),
        pl.BlockSpec((None, d, d_in), lambda i: (w_layer, 0, 0), pipeline_mode=pl.Buffered(1)),
        pl.BlockSpec((1, HEAD_DIM_A), lambda i: (0, 0)),
        pl.BlockSpec((1, HEAD_DIM_A), lambda i: (0, 0)),
        pl.BlockSpec((FOURIER_GROUP_DIM, 2 * FOURIER_GROUP_DIM), lambda i: (0, 0)),
    ]
    args = [x2d, mod_l, norm_g, w_in_b, gq, gk, dft_c]
    if rope:
        nblk = seq_len // tm
        for tab in rope_tabs:
            in_specs.append(pl.BlockSpec((tm, LANES), lambda i: (i % nblk, 0)))
            args.append(tab)
        r_new = jnp.arange(tm, dtype=jnp.int32)
        r_old = (r_new % (tm // GRID_W)) * GRID_W + r_new // (tm // GRID_W)
        perm = (r_old[:, None] == jnp.arange(tm, dtype=jnp.int32)[None, :]).astype(BF16)
        in_specs.append(pl.BlockSpec((tm, tm), lambda i: (0, 0)))
        args.append(perm)
    out_shape = [jax.ShapeDtypeStruct((t, c_f), BF16), jax.ShapeDtypeStruct((t, 2 * n_f), BF16)]
    out_specs = [pl.BlockSpec((tm, c_f), lambda i: (i, 0)), pl.BlockSpec((tm, 2 * n_f), lambda i: (i, 0))]
    aliases = {}
    n_alias = 0
    if ctx_out:
        depth, layer, prev = new_cache
        nb = t // seq_len
        spt = tm // seq_len
        for width in (c_va - c_ka, c_qb - c_va, c_vb - c_kb, c_f - c_vb):
            rows = seq_len * (width // LANES)
            out_shape.append(jax.ShapeDtypeStruct((nb, depth, rows, LANES), F32))
            if prev is None:
                out_specs.append(pl.BlockSpec((spt, depth, rows, LANES), lambda i: (i, 0, 0, 0)))
            else:
                out_specs.append(pl.BlockSpec((spt, None, rows, LANES), lambda i: (i, layer, 0, 0)))
        if prev is not None:
            n_alias = len(prev)
            for a, arr in enumerate(prev):
                aliases[len(args)] = 2 + a
                in_specs.append(pl.BlockSpec(memory_space=pl.ANY))
                args.append(arr)
    return pl.pallas_call(
        functools.partial(_in_kernel, rope=rope, ctx_out=ctx_out, n_alias=n_alias, cols=cols),
        out_shape=out_shape,
        grid=(t // tm,),
        in_specs=in_specs,
        out_specs=out_specs,
        input_output_aliases=aliases,
        compiler_params=_cparams(("parallel",)),
        name="in_proj_ctx" if ctx_out else "in_proj_lat",
    )(*args)


def _softmax_pv(q, srcs):
    chunks = []
    for k_ref, v_ref, n_keys, kc, col in srcs:
        for c in range(n_keys // kc):
            chunks.append((k_ref, v_ref, c * kc, kc, col))

    def scores(ch):
        k_ref, _, off, kc, col = ch
        k = k_ref[off:off + kc, col:col + LANES].astype(BF16)
        return lax.dot_general(q, k, (((1,), (1,)), ((), ())), preferred_element_type=F32)

    def lane_fold(x, op):
        out = x[:, 0:LANES]
        for t in range(1, x.shape[1] // LANES):
            out = op(out, x[:, t * LANES:(t + 1) * LANES])
        return out

    mxu_sum = len(chunks) > 1
    m = acc = l_part = None
    s_next = scores(chunks[0])
    for i, ch in enumerate(chunks):
        s = s_next
        if i + 1 < len(chunks):
            s_next = scores(chunks[i + 1])
        _, v_ref, off, kc, col = ch
        row_max = jnp.max(lane_fold(s, jnp.maximum), axis=-1, keepdims=True)
        m_new = row_max if m is None else jnp.maximum(m, row_max)
        p = jnp.exp2(s - m_new)
        v = v_ref[off:off + kc, col:col + LANES].astype(BF16)
        if mxu_sum:
            ones_col = (lax.broadcasted_iota(jnp.int32, (kc, LANES), 1) == 0).astype(BF16)
            v = jnp.concatenate([v, ones_col], axis=1)
        else:
            l_part = lane_fold(p, jnp.add)
        pv = jnp.dot(p.astype(BF16), v, preferred_element_type=F32)
        acc = pv if m is None else jnp.exp2(m - m_new) * acc + pv
        m = m_new
    if mxu_sum:
        return acc[:, :LANES] / acc[:, LANES:LANES + 1]
    return acc / jnp.sum(l_part, axis=-1, keepdims=True)


def _srcs(cache_refs, k_ref, v_ref, kc, head):
    srcs = []
    col = head * LANES
    if cache_refs is not None:
        ck, cv = cache_refs
        srcs.append((ck, cv, ck.shape[0], ck.shape[0], col))
    n = k_ref.shape[0]
    srcs.append((k_ref, v_ref, n, min(kc, n), col))
    return srcs


def _attn_a_kernel(*refs, has_cache, tq, kc, heads, split):
    if has_cache:
        q_ref, ck_ref, cv_ref, k_ref, v_ref, o_ref = refs
        cache = (ck_ref, cv_ref)
    else:
        q_ref, k_ref, v_ref, o_ref = refs
        cache = None
    per = GQA_GROUP // split
    for bb in range(q_ref.shape[0]):
        qb, kb, vb, ob = q_ref.at[bb], k_ref.at[bb], v_ref.at[bb], o_ref.at[bb]
        for kh in range(heads):
            for part in range(split):
                cols = [(kh * GQA_GROUP + part * per + h) * LANES for h in range(per)]
                qs = jnp.concatenate([qb[:, c:c + LANES] for c in cols], axis=0)
                o = _softmax_pv(qs, _srcs(cache, kb, vb, kc, kh))
                for h, c in enumerate(cols):
                    ob[:, c:c + LANES] = o[h * tq:(h + 1) * tq].astype(BF16)


def _attn_b_kernel(*refs, has_cache, tq, kc, heads, split, lam_init):
    if has_cache:
        lam_ref, gs_ref, q_ref, ck_ref, cv_ref, k_ref, v_ref, o_ref = refs
        cache = (ck_ref, cv_ref)
    else:
        lam_ref, gs_ref, q_ref, k_ref, v_ref, o_ref = refs
        cache = None
    lp = lam_ref[...]
    lam = (jnp.exp(jnp.sum(lp[0:1] * lp[1:2], axis=-1, keepdims=True))
           - jnp.exp(jnp.sum(lp[2:3] * lp[3:4], axis=-1, keepdims=True)) + lam_init)
    rows = tq // split
    for bb in range(q_ref.shape[0]):
        qb, kb, vb, ob = q_ref.at[bb], k_ref.at[bb], v_ref.at[bb], o_ref.at[bb]
        for hd in range(heads):
            for part in range(split):
                q = qb[part * rows:(part + 1) * rows, hd * LANES:(hd + 1) * LANES]
                lane = lax.broadcasted_iota(jnp.int32, q.shape, 1)
                zero = jnp.zeros_like(q)
                qz = jnp.concatenate([jnp.where(lane < DK_B, q, zero), jnp.where(lane >= DK_B, q, zero)], axis=0)
                o = _softmax_pv(qz, _srcs(cache, kb, vb, kc, hd))
                dlt = o[:rows] - lam * o[rows:]
                ob[part * rows:(part + 1) * rows, hd * LANES:(hd + 1) * LANES] = (
                    _rms(dlt, gs_ref[...]) * (1.0 - lam_init)).astype(BF16)


def _attn_calls(qkv3, caches, layer, lam_params, g_sub, lam_init, tq_a, tq_b, kc, ha, hb, split_a, split_b, nb):
    b, n, _ = qkv3.shape
    has_cache = caches is not None
    assert b % nb == 0 and (nb == 1 or not has_cache)
    qa_blk =GQA_GROUP * HEAD_DIM_A // LANES
    k_a0 = N_KV_HEADS_A * qa_blk
    v_a0 = k_a0 + N_KV_HEADS_A
    q_b0 = v_a0 + N_KV_HEADS_A
    k_b0 = q_b0 + N_HEADS_B
    v_b0 = k_b0 + N_HEADS_B
    assert all(x % ha == 0 for x in (N_KV_HEADS_A, k_a0, v_a0)) and all(x % hb == 0 for x in (N_HEADS_B, q_b0, k_b0, v_b0))

    qw, kw = ha * GQA_GROUP * LANES, ha * LANES
    in_specs = [pl.BlockSpec((nb, tq_a, qw), lambda bi, h, i: (bi, i, h))]
    args = [qkv3]
    if has_cache:
        ck, cv = caches[0], caches[1]
        p = ck.shape[2]
        in_specs += [pl.BlockSpec((None, None, p, kw), lambda bi, h, i: (bi, layer, 0, h))] * 2
        args += [ck, cv]
    in_specs += [pl.BlockSpec((nb, n, kw), lambda bi, h, i: (bi, 0, k_a0 // ha + h)),
                 pl.BlockSpec((nb, n, kw), lambda bi, h, i: (bi, 0, v_a0 // ha + h))]
    args += [qkv3, qkv3]
    att_a = pl.pallas_call(
        functools.partial(_attn_a_kernel, has_cache=has_cache, tq=tq_a, kc=kc, heads=ha, split=split_a),
        out_shape=jax.ShapeDtypeStruct((b, n, N_KV_HEADS_A * GQA_GROUP * HEAD_DIM_A), BF16),
        grid=(b // nb, N_KV_HEADS_A // ha, n // tq_a),
        in_specs=in_specs,
        out_specs=pl.BlockSpec((nb, tq_a, qw), lambda bi, h, i: (bi, i, h)),
        compiler_params=_cparams(("parallel", "parallel", "arbitrary")),
        name="attn_a_lat" if has_cache else "attn_a_ctx",
    )(*args)

    bw = hb * LANES
    in_specs = [pl.BlockSpec((4, DK_B), lambda bi, h, i: (0, 0)),
                pl.BlockSpec((1, DV_B), lambda bi, h, i: (0, 0)),
                pl.BlockSpec((nb, tq_b, bw), lambda bi, h, i: (bi, i, q_b0 // hb + h))]
    args = [lam_params, g_sub, qkv3]
    if has_cache:
        ck, cv = caches[2], caches[3]
        p = ck.shape[2]
        in_specs += [pl.BlockSpec((None, None, p, bw), lambda bi, h, i: (bi, layer, 0, h))] * 2
        args += [ck, cv]
    in_specs += [pl.BlockSpec((nb, n, bw), lambda bi, h, i: (bi, 0, k_b0 // hb + h)),
                 pl.BlockSpec((nb, n, bw), lambda bi, h, i: (bi, 0, v_b0 // hb + h))]
    args += [qkv3, qkv3]
    att_b = pl.pallas_call(
        functools.partial(_attn_b_kernel, has_cache=has_cache, tq=tq_b, kc=kc, heads=hb, split=split_b, lam_init=lam_init),
        out_shape=jax.ShapeDtypeStruct((b, n, N_HEADS_B * DV_B), BF16),
        grid=(b // nb, N_HEADS_B // hb, n // tq_b),
        in_specs=in_specs,
        out_specs=pl.BlockSpec((nb, tq_b, bw), lambda bi, h, i: (bi, i, h)),
        compiler_params=_cparams(("parallel", "parallel", "arbitrary")),
        name="attn_b_lat" if has_cache else "attn_b_ctx",
    )(*args)
    return att_a, att_b


def _dft_cols_kernel(x_ref, mat_ref, o_ref, *, width, scale):
    p, g, w2 = x_ref.shape
    x = x_ref[...].reshape(p * g, w2)
    xs = jnp.concatenate([x[:, :width], x[:, width:]], axis=0)
    u = jnp.dot(mat_ref[...], xs, preferred_element_type=F32)
    o_ref[...] = (u * scale).astype(BF16).reshape(o_ref.shape)


def _dft_rows_kernel(x_ref, mat_ref, tc_ref, ts_ref, o_ref, *, width):
    tiles, rows, w2 = x_ref.shape
    pair = BF16_SUBLANES
    reps = width // LANES
    for p in range(rows // pair):
        x = x_ref[:, p * pair:(p + 1) * pair, :].reshape(tiles * pair, w2)
        xs = jnp.concatenate([x[:, :width], x[:, width:]], axis=0)
        u = jnp.dot(mat_ref[...], xs, preferred_element_type=F32)
        half = u.shape[0] // 2
        ur, ui = u[:half], u[half:]
        tc = jnp.concatenate([tc_ref[2 * p:2 * p + 2].reshape(half, LANES)] * reps, axis=1)
        ts = jnp.concatenate([ts_ref[2 * p:2 * p + 2].reshape(half, LANES)] * reps, axis=1)
        o_ref[2 * p:2 * p + 2, :, :width] = (ur * tc - ui * ts).astype(BF16).reshape(2, half // 2, width)
        o_ref[2 * p:2 * p + 2, :, width:] = (ur * ts + ui * tc).astype(BF16).reshape(2, half // 2, width)


def _dft_seq_kernel(x_ref, mat_ref, o_ref, *, width, scale):
    nb = x_ref.shape[0]
    xs = jnp.concatenate(
        [jnp.concatenate([x_ref[bb, :, :width], x_ref[bb, :, width:]], axis=0) for bb in range(nb)], axis=1)
    u = jnp.dot(mat_ref[...], xs, preferred_element_type=F32)
    for bb in range(nb):
        o_ref[bb] = (u[:, bb * width:(bb + 1) * width] * scale).astype(BF16)


def _cos_sin(n_rows, n_cols, period):
    a = jnp.arange(n_rows, dtype=jnp.int32)[:, None]
    b = jnp.arange(n_cols, dtype=jnp.int32)[None, :]
    ang = ((a * b) % period).astype(F32) * (2.0 * math.pi / period)
    return jnp.cos(ang), jnp.sin(ang)


def _fourier_call(y3, n1, n2, tile_n2=None):
    b, n, w2 = y3.shape
    width = w2 // 2
    scale = 1.0 / math.sqrt(n * FOURIER_GROUP_DIM)
    c1, s1 = _cos_sin(n1, n1, n1)
    w_real = jnp.stack([c1, -s1], axis=1)
    if n2 == 1:
        mat = w_real.reshape(n1, 2 * n1).astype(BF16)
        nb = math.gcd(b, CTX_SEQS)
        return pl.pallas_call(
            functools.partial(_dft_seq_kernel, width=width, scale=scale),
            out_shape=jax.ShapeDtypeStruct((b, n, width), BF16),
            grid=(b // nb,),
            in_specs=[pl.BlockSpec((nb, n, w2), lambda bi: (bi, 0, 0)),
                      pl.BlockSpec((n1, 2 * n1), lambda bi: (0, 0))],
            out_specs=pl.BlockSpec((nb, n, width), lambda bi: (bi, 0, 0)),
            compiler_params=_cparams(("parallel",)),
            name="dft_ctx",
        )(y3, mat)

    g = BF16_SUBLANES
    c2, s2 = _cos_sin(n2, n2, n2)
    w_cplx = jnp.stack([jnp.stack([c2, -s2], axis=1), jnp.stack([s2, c2], axis=1)], axis=0)

    def kron_cols(base, row_j):
        rows, cols = base.shape
        col = lax.broadcasted_iota(jnp.int32, (cols, cols * g), 1)
        expand = (col // g == lax.broadcasted_iota(jnp.int32, (cols, cols * g), 0)).astype(BF16)
        wide = jnp.dot(base.astype(BF16), expand, preferred_element_type=F32)
        keep = row_j[:, None] == (lax.broadcasted_iota(jnp.int32, (rows, cols * g), 1) % g)
        return jnp.where(keep, wide, 0.0).astype(BF16)

    base2 = jnp.broadcast_to(w_real.reshape(n1, 1, 2 * n1), (n1, g, 2 * n1)).reshape(n1 * g, 2 * n1)
    mat2 = kron_cols(base2, jnp.arange(n1 * g, dtype=jnp.int32) % g)
    tc, ts = _cos_sin(n1, n2, n)
    tc = jnp.broadcast_to(tc[:, :, None], (n1, n2, LANES))
    ts = jnp.broadcast_to(ts[:, :, None], (n1, n2, LANES))
    tiles = n2 // tile_n2
    tile_len = n1 * tile_n2
    mat1 = jnp.einsum('akbtl,ji->ajkbtil', w_cplx.reshape(2, n2, 2, tiles, tile_n2), jnp.eye(2, dtype=F32))
    mat1 = mat1.reshape(4 * n2, 4 * n2).astype(BF16)
    gp = 8
    t = pl.pallas_call(
        functools.partial(_dft_rows_kernel, width=width),
        out_shape=jax.ShapeDtypeStruct((b, n1, n2, w2), BF16),
        grid=(b, tile_len // (BF16_SUBLANES * gp)),
        in_specs=[pl.BlockSpec((None, tiles, BF16_SUBLANES * gp, w2), lambda bi, j: (bi, 0, j, 0)),
                  pl.BlockSpec(mat1.shape, lambda bi, j: (0, 0)),
                  pl.BlockSpec((2 * gp, n2, LANES), lambda bi, j: (j, 0, 0)),
                  pl.BlockSpec((2 * gp, n2, LANES), lambda bi, j: (j, 0, 0))],
        out_specs=pl.BlockSpec((None, 2 * gp, n2, w2), lambda bi, j: (bi, j, 0, 0)),
        compiler_params=_cparams(("parallel", "parallel")),
        name="dft_stage1",
    )(y3.reshape(b, tiles, tile_len, w2), mat1, tc, ts)
    out = pl.pallas_call(
        functools.partial(_dft_cols_kernel, width=width, scale=scale),
        out_shape=jax.ShapeDtypeStruct((b, n1, n2, width), BF16),
        grid=(b, n2 // g),
        in_specs=[pl.BlockSpec((None, n1, g, w2), lambda bi, j: (bi, 0, j, 0)),
                  pl.BlockSpec(mat2.shape, lambda bi, j: (0, 0), pipeline_mode=pl.Buffered(1))],
        out_specs=pl.BlockSpec((None, n1, g, width), lambda bi, j: (bi, 0, j, 0)),
        compiler_params=_cparams(("parallel", "parallel")),
        name="dft_stage2",
    )(t, mat2)
    return out.reshape(b, n, width)


def _out_kernel(a_ref, b_ref, f_ref, w_ref, x_ref, mod_ref, g_ref, x1_ref, h2_ref):
    ca = a_ref.shape[1]
    cb = b_ref.shape[1]
    m = mod_ref[...]
    half = a_ref.shape[0] // 2
    for r0 in (0, half):
        rows = slice(r0, r0 + half)
        acc = jnp.dot(a_ref[rows, :], w_ref[0:ca, :], preferred_element_type=F32)
        acc += jnp.dot(b_ref[rows, :], w_ref[ca:ca + cb, :], preferred_element_type=F32)
        acc += jnp.dot(f_ref[rows, :], w_ref[ca + cb:, :], preferred_element_type=F32)
        x1 = x_ref[rows, :] + m[2:3] * acc
        x1_ref[rows, :] = x1
        h = _rms(x1, g_ref[...])
        h2_ref[rows, :] = (h * (1.0 + m[4:5]) + m[3:4]).astype(BF16)


def _out_call(att_a, att_b, four, w_out_b, x2d, mod_l, cond_row_fn, norm_g, tm):
    t, d = x2d.shape
    ca, cb, cf = att_a.shape[1], att_b.shape[1], four.shape[1]
    w_out_b, w_layer = w_out_b
    return pl.pallas_call(
        _out_kernel,
        out_shape=[jax.ShapeDtypeStruct((t, d), F32), jax.ShapeDtypeStruct((t, d), BF16)],
        grid=(t // tm,),
        in_specs=[
            pl.BlockSpec((tm, ca), lambda i: (i, 0)),
            pl.BlockSpec((tm, cb), lambda i: (i, 0)),
            pl.BlockSpec((tm, cf), lambda i: (i, 0)),
            pl.BlockSpec((None, ca + cb + cf, d), lambda i: (w_layer, 0, 0), pipeline_mode=pl.Buffered(1)),
            pl.BlockSpec((tm, d), lambda i: (i, 0)),
            pl.BlockSpec((None, N_MOD, d), lambda i: (cond_row_fn(i), 0, 0)),
            pl.BlockSpec((1, d), lambda i: (0, 0)),
        ],
        out_specs=[pl.BlockSpec((tm, d), lambda i: (i, 0)), pl.BlockSpec((tm, d), lambda i: (i, 0))],
        compiler_params=_cparams(("parallel",)),
        name="out_proj",
    )(att_a, att_b, four, w_out_b, x2d, mod_l, norm_g)


HALO = BF16_SUBLANES


def _ffn_kernel(cw_ref, cb_ref, mod_ref, gf_ref, h_hbm, x1_hbm, wg_hbm, wu_hbm, wd_hbm,
                o_ref, hext_ref, x1_buf, wg_buf, wu_buf, wd_buf, sem, *, tm, tf, nj, layer, seq_len, final):
    i = pl.program_id(0)
    n_i = pl.num_programs(0)
    row0 = pl.multiple_of(i * tm, tm)

    def weight_copies(j, slot):
        cols = pl.ds(pl.multiple_of(j * tf, tf), tf)
        return (pltpu.make_async_copy(wg_hbm.at[layer, :, cols], wg_buf.at[slot], sem.at[0, slot]),
                pltpu.make_async_copy(wu_hbm.at[layer, :, cols], wu_buf.at[slot], sem.at[1, slot]),
                pltpu.make_async_copy(wd_hbm.at[layer, cols, :], wd_buf.at[slot], sem.at[2, slot]))

    def tile_copies(ti, slot):
        start = pl.multiple_of(ti * tm, tm)
        before = pl.multiple_of(jnp.maximum(start - HALO, 0), HALO)
        after = pl.multiple_of(jnp.minimum(start + tm, n_i * tm - HALO), HALO)
        dst = hext_ref.at[slot]
        return (pltpu.make_async_copy(h_hbm.at[pl.ds(before, HALO), :], dst.at[pl.ds(0, HALO), :], sem.at[3, slot]),
                pltpu.make_async_copy(h_hbm.at[pl.ds(start, tm), :], dst.at[pl.ds(HALO, tm), :], sem.at[4, slot]),
                pltpu.make_async_copy(h_hbm.at[pl.ds(after, HALO), :], dst.at[pl.ds(HALO + tm, HALO), :],
                                      sem.at[5, slot]))

    x1_copy = pltpu.make_async_copy(x1_hbm.at[pl.ds(row0, tm), :], x1_buf, sem.at[6, 0])
    hslot = i % 2

    @pl.when(i == 0)
    def _():
        for cp in tile_copies(0, 0) + weight_copies(0, 0):
            cp.start()

    x1_copy.start()

    @pl.when(i + 1 < n_i)
    def _():
        for cp in tile_copies(i + 1, 1 - hslot):
            cp.start()

    ext = tm + 2 * HALO
    pos = (i * tm + lax.broadcasted_iota(jnp.int32, (tm, 1), 0)) % seq_len
    for cp in tile_copies(i, hslot):
        cp.wait()
    hext = hext_ref.at[hslot]

    def chunk(j, first):
        step = i * nj + j
        slot = step % 2
        for cp in weight_copies(j, slot):
            cp.wait()

        @pl.when(step + 1 < n_i * nj)
        def _():
            for cp in weight_copies((j + 1) % nj, 1 - slot):
                cp.start()

        g = jnp.dot(hext[...], wg_buf[slot], preferred_element_type=F32)
        u = jnp.dot(hext[HALO:HALO + tm, :], wu_buf[slot], preferred_element_type=F32)
        g_prev = jnp.where(pos == 0, 0.0, pltpu.roll(g, 1, 0)[HALO:HALO + tm])
        g_next = jnp.where(pos == seq_len - 1, 0.0, pltpu.roll(g, ext - 1, 0)[HALO:HALO + tm])
        cw = cw_ref[j]
        gc = g_prev * cw[0:1] + g[HALO:HALO + tm] * cw[1:2] + g_next * cw[2:3] + cb_ref[j]
        act = (gc * jax.nn.sigmoid(gc)) * u
        y = jnp.dot(act.astype(BF16), wd_buf[slot], preferred_element_type=F32)
        if first:
            o_ref[...] = y
        else:
            o_ref[...] += y

    chunk(0, True)
    lax.fori_loop(1, nj, lambda j, carry: (chunk(j, False), carry)[1], 0)

    x1_copy.wait()
    m = mod_ref[...]
    x2 = x1_buf[...] + m[5:6] * o_ref[...]
    if final:
        x2 = _rms(x2, gf_ref[...])
    o_ref[...] = x2


def _ffn_call(h2, x1, weights, conv_w, conv_b, mod_l, cond_row_fn, final_g, seq_len, final, tm, tf):
    t, d = x1.shape
    w_gate_b, w_up_b, w_down_b, w_layer = weights
    f = w_gate_b.shape[2]
    nj = f // tf
    cw3 = conv_w.reshape(3, nj, tf).transpose(1, 0, 2)
    cb3 = conv_b.reshape(nj, 1, tf)
    hbm = pl.BlockSpec(memory_space=pl.ANY)
    return pl.pallas_call(
        functools.partial(_ffn_kernel, tm=tm, tf=tf, nj=nj, layer=w_layer, seq_len=seq_len, final=final),
        out_shape=jax.ShapeDtypeStruct((t, d), F32),
        grid=(t // tm,),
        in_specs=[
            pl.BlockSpec((nj, 3, tf), lambda i: (0, 0, 0)),
            pl.BlockSpec((nj, 1, tf), lambda i: (0, 0, 0)),
            pl.BlockSpec((None, N_MOD, d), lambda i: (cond_row_fn(i), 0, 0)),
            pl.BlockSpec((1, d), lambda i: (0, 0)),
            hbm, hbm, hbm, hbm, hbm,
        ],
        out_specs=pl.BlockSpec((tm, d), lambda i: (i, 0)),
        scratch_shapes=[pltpu.VMEM((2, tm + 2 * HALO, d), BF16), pltpu.VMEM((tm, d), F32),
                        pltpu.VMEM((2, d, tf), BF16), pltpu.VMEM((2, d, tf), BF16), pltpu.VMEM((2, tf, d), BF16),
                        pltpu.SemaphoreType.DMA((7, 2))],
        compiler_params=_cparams(("arbitrary",)),
        name="conv_ffn",
    )(cw3, cb3, mod_l, final_g, h2, x1, w_gate_b, w_up_b, w_down_b)


def _rope_tables(n, head_dim):
    rows = n // GRID_W
    t_row = jnp.repeat(jnp.arange(rows, dtype=F32), GRID_W)
    t_col = jnp.tile(jnp.arange(GRID_W, dtype=F32), rows)
    axis_dim = head_dim // 2
    inv = jnp.power(ROPE_BASE, -jnp.arange(0, axis_dim, 2, dtype=F32) / axis_dim)
    ar = t_row[:, None] * inv[None, :]
    ac = t_col[:, None] * inv[None, :]
    ang = jnp.concatenate([ar, ar, ac, ac], axis=-1)
    reps = LANES // head_dim
    quarter = head_dim // 4
    sign = jnp.where((jnp.arange(head_dim) % (2 * quarter)) < quarter, -1.0, 1.0).astype(F32)
    cos = jnp.tile(jnp.cos(ang), (1, reps))
    sin_signed = jnp.tile(jnp.sin(ang) * sign[None, :], (1, reps))
    return cos, sin_signed


def kernel(x_prompt, x_sample, cache_attn_k, cache_attn_v, cache_diff_k, cache_diff_v, c, c_ctx, norm1_g, norm2_g, w_ada, b_ada, w_in, attn_q_norm_g, attn_k_norm_g, diff_lambda_q1, diff_lambda_k1, diff_lambda_q2, diff_lambda_k2, diff_subnorm_g, w_out, ffn_w_gate, ffn_w_up, ffn_conv_w, ffn_conv_b, ffn_w_down, final_norm_g):
    depth = w_in.shape[0]
    bc, lc, d = x_prompt.shape
    bl, ll, _ = x_sample.shape
    past = cache_attn_k.shape[2]

    w_in_b = w_in.astype(BF16)
    w_out_b = w_out.astype(BF16)
    w_gate_b = ffn_w_gate.astype(BF16)
    w_up_b = ffn_w_up.astype(BF16)
    w_down_b = ffn_w_down.astype(BF16)

    n_rows = 8 * ((1 + bl + 7) // 8)
    cvec = jnp.concatenate([c_ctx[None, :], c, jnp.zeros((n_rows - 1 - bl, d), F32)], axis=0)
    mod = _ada_call(cvec, w_ada, b_ada).reshape(depth, n_rows, N_MOD, d)

    dft_c = jnp.concatenate(_cos_sin(FOURIER_GROUP_DIM, FOURIER_GROUP_DIM, FOURIER_GROUP_DIM), axis=1).astype(BF16)
    rope_tabs = _rope_tables(ll, HEAD_DIM_A) + _rope_tables(ll, DK_B)
    caches = (cache_attn_k.reshape(bl, depth, past, N_KV_HEADS_A * HEAD_DIM_A),
              cache_attn_v.reshape(bl, depth, past, N_KV_HEADS_A * HEAD_DIM_A),
              cache_diff_k.reshape(bl, depth, past, N_HEADS_B * 2 * DK_B),
              cache_diff_v.reshape(bl, depth, past, N_HEADS_B * DV_B))
    lam_all = jnp.stack([diff_lambda_q1, diff_lambda_k1, diff_lambda_q2, diff_lambda_k2], axis=1)

    def run_pass(x3, is_ctx):
        b, n, _ = x3.shape
        t = b * n
        tm = ROW_TILE
        tm_ffn = FFN_ROW_TILE
        x = x3.reshape(t, d)

        def cond_rows(tile):
            return (lambda i: 0) if is_ctx else (lambda i: 1 + (i * tile) // n)

        cond_row_fn = cond_rows(tm)
        new_kv = None
        for l in range(depth):
            lam_init = 0.8 - 0.6 * math.exp(-0.3 * l)
            res = _in_call(x, mod[l], cond_row_fn, norm1_g[l][None, :], (w_in_b, l),
                           attn_q_norm_g[l][None, :], attn_k_norm_g[l][None, :], dft_c,
                           None if is_ctx else rope_tabs, n, (depth, l, new_kv) if is_ctx else None, tm)
            qkv, y12 = res[0], res[1]
            if is_ctx:
                new_kv = res[2:]
            qkv3 = qkv.reshape(b, n, qkv.shape[1])
            att_a, att_b = _attn_calls(qkv3, None if is_ctx else caches, l, lam_all[l],
                                       diff_subnorm_g[l][None, :], lam_init,
                                       tq_a=min(ATTN_TQ_A, n), tq_b=min(ATTN_TQ_B, n), kc=ATTN_KC,
                                       ha=N_KV_HEADS_A if is_ctx else 1, hb=N_HEADS_B if is_ctx else 1,
                                       split_a=1 if is_ctx else ATTN_STREAMS, split_b=1 if is_ctx else ATTN_STREAMS,
                                       nb=math.gcd(b, CTX_SEQS) if is_ctx else 1)
            if is_ctx:
                four = _fourier_call(y12.reshape(b, n, y12.shape[1]), n, 1)
            else:
                four = _fourier_call(y12.reshape(b, n, y12.shape[1]), GRID_W, n // GRID_W, tm // GRID_W)
            x1, h2 = _out_call(att_a.reshape(t, -1), att_b.reshape(t, -1), four.reshape(t, -1), (w_out_b, l),
                               x, mod[l], cond_row_fn, norm2_g[l][None, :], tm)
            x = _ffn_call(h2, x1, (w_gate_b, w_up_b, w_down_b, l), ffn_conv_w[l], ffn_conv_b[l][None, :],
                          mod[l], cond_rows(tm_ffn), final_norm_g[None, :], n, l == depth - 1, tm_ffn, FFN_TF)
        return x.reshape(b, n, d), new_kv

    y_prompt, kvs = run_pass(x_prompt, True)
    new_attn_k = kvs[0].reshape(bc, depth, lc, N_KV_HEADS_A, HEAD_DIM_A)
    new_attn_v = kvs[1].reshape(bc, depth, lc, N_KV_HEADS_A, HEAD_DIM_A)
    new_diff_k = kvs[2].reshape(bc, depth, lc, N_HEADS_B, 2 * DK_B)
    new_diff_v = kvs[3].reshape(bc, depth, lc, N_HEADS_B, DV_B)

    y_sample, _ = run_pass(x_sample, False)
    return (y_prompt, y_sample, new_attn_k, new_attn_v, new_diff_k, new_diff_v)
```

```python
import functools
import math

import jax
import jax.numpy as jnp
from jax import lax
from jax.experimental import pallas as pl
from jax.experimental.pallas import tpu as pltpu

F32 = jnp.float32
BF16 = jnp.bfloat16

GRID_W = 64
ROPE_BASE = 10000.0
NORM_EPS = 1e-6
HEAD_DIM_A = 128
N_KV_HEADS_A = 2
GQA_GROUP = 4
N_HEADS_B = 4
DK_B = 64
DV_B = 128
N_FOURIER_GROUPS = 4
FOURIER_GROUP_DIM = 128
N_MOD = 6

LOG2E = math.log2(math.e)
LANES = 128
BF16_SUBLANES = 16
VMEM_LIMIT = 58 * 1024 * 1024

ROW_TILE = 512
FFN_ROW_TILE = 1024
FFN_TF = 512
ADA_TN = 1024
ATTN_TQ_A = 512
ATTN_TQ_B = 1024
ATTN_KC = 1024
ATTN_STREAMS = 4
CTX_SEQS = 4


def _cparams(sem):
    return pltpu.CompilerParams(dimension_semantics=sem, vmem_limit_bytes=VMEM_LIMIT)


def _rms(x, g):
    return x * lax.rsqrt(jnp.mean(x * x, axis=-1, keepdims=True) + NORM_EPS) * g


def _ada_kernel(c_ref, w_ref, b_ref, o_ref):
    c = c_ref[...]
    s = (c * jax.nn.sigmoid(c)).astype(BF16)
    w = w_ref[...].astype(BF16)
    o_ref[...] = jnp.dot(s, w, preferred_element_type=F32) + b_ref[...]


def _ada_call(cvec, w_ada, b_ada):
    depth, d, n = w_ada.shape
    rows = cvec.shape[0]
    tn = ADA_TN
    return pl.pallas_call(
        _ada_kernel,
        out_shape=jax.ShapeDtypeStruct((depth, rows, n), F32),
        grid=(depth, n // tn),
        in_specs=[
            pl.BlockSpec((rows, d), lambda l, j: (0, 0)),
            pl.BlockSpec((None, d, tn), lambda l, j: (l, 0, j)),
            pl.BlockSpec((None, 1, tn), lambda l, j: (l, 0, j)),
        ],
        out_specs=pl.BlockSpec((None, rows, tn), lambda l, j: (l, 0, j)),
        compiler_params=_cparams(("parallel", "parallel")),
        name="ada_mod",
    )(cvec, w_ada, b_ada.reshape(depth, 1, n))


def _rope(xs, cos, sin_signed, shift):
    w = xs.shape[-1]
    lane = lax.broadcasted_iota(jnp.int32, xs.shape, 1)
    first = (lane % (2 * shift)) < shift
    rot = jnp.where(first, pltpu.roll(xs, w - shift, 1), pltpu.roll(xs, shift, 1))
    return xs * cos + rot * sin_signed


def _in_kernel(*refs, rope, ctx_out, n_alias, cols):
    it = iter(refs)
    x_ref, mod_ref, g_ref, w_ref, gq_ref, gk_ref, dft_ref = (next(it) for _ in range(7))
    if rope:
        cos_a, sin_a, cos_b, sin_b = (next(it)[...] for _ in range(4))
        perm_ref = next(it)
    for _ in range(n_alias):
        next(it)
    qkv_ref, y_ref = next(it), next(it)
    if ctx_out:
        ka_ref, va_ref, kb_ref, vb_ref = (next(it) for _ in range(4))
    c_qa, c_ka, c_va, c_qb, c_kb, c_vb, c_f, c_end = cols

    def put_cache(ref, off, v):
        head = off // LANES
        seqs, rows = ref.shape[0], ref.shape[-2]
        heads = rows * seqs // v.shape[0]
        n = rows // heads
        for s in range(seqs):
            vs = v[s * n:(s + 1) * n, :]
            if len(ref.shape) == 4:
                for dd in range(ref.shape[1]):
                    ref[s, dd, pl.ds(head, n, stride=heads), :] = vs
            else:
                ref[s, pl.ds(head, n, stride=heads), :] = vs

    m = mod_ref[...]
    h = _rms(x_ref[...], g_ref[...])
    hb = (h * (1.0 + m[1:2]) + m[0:1]).astype(BF16)
    gq = gq_ref[...]
    gk = gk_ref[...]
    scale_a = HEAD_DIM_A ** -0.5 * LOG2E
    scale_b = DK_B ** -0.5 * LOG2E
    chunk = 4 * LANES

    for c0 in range(0, c_end, chunk):
        acc = jnp.dot(hb, w_ref[:, c0:c0 + chunk], preferred_element_type=F32)
        if rope and c0 >= c_f:
            acc = jnp.dot(perm_ref[...], acc.astype(BF16), preferred_element_type=F32)
        for s in range(chunk // LANES):
            col = c0 + s * LANES
            v = acc[:, s * LANES:(s + 1) * LANES]
            if col < c_ka:
                v = _rms(v, gq)
                if rope:
                    v = _rope(v, cos_a, sin_a, HEAD_DIM_A // 4)
                qkv_ref[:, col:col + LANES] = (v * scale_a).astype(BF16)
            elif col < c_va:
                v = _rms(v, gk)
                if ctx_out:
                    put_cache(ka_ref, col - c_ka, v)
                if rope:
                    v = _rope(v, cos_a, sin_a, HEAD_DIM_A // 4)
                qkv_ref[:, col:col + LANES] = v.astype(BF16)
            elif col < c_qb:
                if ctx_out:
                    put_cache(va_ref, col - c_va, v)
                qkv_ref[:, col:col + LANES] = v.astype(BF16)
            elif col < c_kb:
                if rope:
                    v = _rope(v, cos_b, sin_b, DK_B // 4)
                qkv_ref[:, col:col + LANES] = (v * scale_b).astype(BF16)
            elif col < c_vb:
                if ctx_out:
                    put_cache(kb_ref, col - c_kb, v)
                if rope:
                    v = _rope(v, cos_b, sin_b, DK_B // 4)
                qkv_ref[:, col:col + LANES] = v.astype(BF16)
            elif col < c_f:
                if ctx_out:
                    put_cache(vb_ref, col - c_vb, v)
                qkv_ref[:, col:col + LANES] = v.astype(BF16)
            else:
                yy = jnp.dot(v.astype(BF16), dft_ref[...], preferred_element_type=F32)
                gcol = col - c_f
                half = c_end - c_f
                y_ref[:, gcol:gcol + LANES] = yy[:, :LANES].astype(BF16)
                y_ref[:, half + gcol:half + gcol + LANES] = yy[:, LANES:].astype(BF16)


def _in_call(x2d, mod_l, cond_row_fn, norm_g, w_in_b, gq, gk, dft_c, rope_tabs, seq_len, new_cache, tm):
    ctx_out = new_cache is not None
    t, d = x2d.shape
    w_in_b, w_layer = w_in_b
    d_in = w_in_b.shape[2]
    c_qa = 0
    c_ka = N_KV_HEADS_A * GQA_GROUP * HEAD_DIM_A
    c_va = c_ka + N_KV_HEADS_A * HEAD_DIM_A
    c_qb = c_va + N_KV_HEADS_A * HEAD_DIM_A
    c_kb = c_qb + N_HEADS_B * 2 * DK_B
    c_vb = c_kb + N_HEADS_B * 2 * DK_B
    c_f = c_vb + N_HEADS_B * DV_B
    c_end = c_f + N_FOURIER_GROUPS * FOURIER_GROUP_DIM
    assert c_end == d_in
    cols = (c_qa, c_ka, c_va, c_qb, c_kb, c_vb, c_f, c_end)
    rope = rope_tabs is not None
    n_f = c_end - c_f

    in_specs = [
        pl.BlockSpec((tm, d), lambda i: (i, 0)),
        pl.BlockSpec((None, N_MOD, d), lambda i: (cond_row_fn(i), 0, 0)),
        pl.BlockSpec((1, d), lambda i: (0, 0)),
        pl.BlockSpec((None, d, d_in), lambda i: (w_layer, 0, 0), pipeline_mode=pl.Buffered(1)),
        pl.BlockSpec((1, HEAD_DIM_A), lambda i: (0, 0)),
        pl.BlockSpec((1, HEAD_DIM_A), lambda i: (0, 0)),
        pl.BlockSpec((FOURIER_GROUP_DIM, 2 * FOURIER_GROUP_DIM), lambda i: (0, 0)),
    ]
    args = [x2d, mod_l, norm_g, w_in_b, gq, gk, dft_c]
    if rope:
        nblk = seq_len // tm
        for tab in rope_tabs:
            in_specs.append(pl.BlockSpec((tm, LANES), lambda i: (i % nblk, 0)))
            args.append(tab)
        r_new = jnp.arange(tm, dtype=jnp.int32)
        r_old = (r_new % (tm // GRID_W)) * GRID_W + r_new // (tm // GRID_W)
        perm = (r_old[:, None] == jnp.arange(tm, dtype=jnp.int32)[None, :]).astype(BF16)
        in_specs.append(pl.BlockSpec((tm, tm), lambda i: (0, 0)))
        args.append(perm)
    out_shape = [jax.ShapeDtypeStruct((t, c_f), BF16), jax.ShapeDtypeStruct((t, 2 * n_f), BF16)]
    out_specs = [pl.BlockSpec((tm, c_f), lambda i: (i, 0)), pl.BlockSpec((tm, 2 * n_f), lambda i: (i, 0))]
    aliases = {}
    n_alias = 0
    if ctx_out:
        depth, layer, prev = new_cache
        nb = t // seq_len
        spt = tm // seq_len
        for width in (c_va - c_ka, c_qb - c_va, c_vb - c_kb, c_f - c_vb):
            rows = seq_len * (width // LANES)
            out_shape.append(jax.ShapeDtypeStruct((nb, depth, rows, LANES), F32))
            if prev is None:
                out_specs.append(pl.BlockSpec((spt, depth, rows, LANES), lambda i: (i, 0, 0, 0)))
            else:
                out_specs.append(pl.BlockSpec((spt, None, rows, LANES), lambda i: (i, layer, 0, 0)))
        if prev is not None:
            n_alias = len(prev)
            for a, arr in enumerate(prev):
                aliases[len(args)] = 2 + a
                in_specs.append(pl.BlockSpec(memory_space=pl.ANY))
                args.append(arr)
    return pl.pallas_call(
        functools.partial(_in_kernel, rope=rope, ctx_out=ctx_out, n_alias=n_alias, cols=cols),
        out_shape=out_shape,
        grid=(t // tm,),
        in_specs=in_specs,
        out_specs=out_specs,
        input_output_aliases=aliases,
        compiler_params=_cparams(("parallel",)),
        name="in_proj_ctx" if ctx_out else "in_proj_lat",
    )(*args)


def _softmax_pv(q, srcs):
    chunks = []
    for k_ref, v_ref, n_keys, kc, col in srcs:
        for c in range(n_keys // kc):
            chunks.append((k_ref, v_ref, c * kc, kc, col))

    def scores(ch):
        k_ref, _, off, kc, col = ch
        k = k_ref[off:off + kc, col:col + LANES].astype(BF16)
        return lax.dot_general(q, k, (((1,), (1,)), ((), ())), preferred_element_type=F32)

    def lane_fold(x, op):
        out = x[:, 0:LANES]
        for t in range(1, x.shape[1] // LANES):
            out = op(out, x[:, t * LANES:(t + 1) * LANES])
        return out

    mxu_sum = len(chunks) > 1
    m = acc = l_part = None
    s_next = scores(chunks[0])
    for i, ch in enumerate(chunks):
        s = s_next
        if i + 1 < len(chunks):
            s_next = scores(chunks[i + 1])
        _, v_ref, off, kc, col = ch
        row_max = jnp.max(lane_fold(s, jnp.maximum), axis=-1, keepdims=True)
        m_new = row_max if m is None else jnp.maximum(m, row_max)
        p = jnp.exp2(s - m_new)
        v = v_ref[off:off + kc, col:col + LANES].astype(BF16)
        if mxu_sum:
            ones_col = (lax.broadcasted_iota(jnp.int32, (kc, LANES), 1) == 0).astype(BF16)
            v = jnp.concatenate([v, ones_col], axis=1)
        else:
            l_part = lane_fold(p, jnp.add)
        pv = jnp.dot(p.astype(BF16), v, preferred_element_type=F32)
        acc = pv if m is None else jnp.exp2(m - m_new) * acc + pv
        m = m_new
    if mxu_sum:
        return acc[:, :LANES] / acc[:, LANES:LANES + 1]
    return acc / jnp.sum(l_part, axis=-1, keepdims=True)


def _srcs(cache_refs, k_ref, v_ref, kc, head):
    srcs = []
    col = head * LANES
    if cache_refs is not None:
        ck, cv = cache_refs
        srcs.append((ck, cv, ck.shape[0], ck.shape[0], col))
    n = k_ref.shape[0]
    srcs.append((k_ref, v_ref, n, min(kc, n), col))
    return srcs


def _attn_a_kernel(*refs, has_cache, tq, kc, heads, split):
    if has_cache:
        q_ref, ck_ref, cv_ref, k_ref, v_ref, o_ref = refs
        cache = (ck_ref, cv_ref)
    else:
        q_ref, k_ref, v_ref, o_ref = refs
        cache = None
    per = GQA_GROUP // split
    for bb in range(q_ref.shape[0]):
        qb, kb, vb, ob = q_ref.at[bb], k_ref.at[bb], v_ref.at[bb], o_ref.at[bb]
        for kh in range(heads):
            for part in range(split):
                cols = [(kh * GQA_GROUP + part * per + h) * LANES for h in range(per)]
                qs = jnp.concatenate([qb[:, c:c + LANES] for c in cols], axis=0)
                o = _softmax_pv(qs, _srcs(cache, kb, vb, kc, kh))
                for h, c in enumerate(cols):
                    ob[:, c:c + LANES] = o[h * tq:(h + 1) * tq].astype(BF16)


def _attn_b_kernel(*refs, has_cache, tq, kc, heads, split, lam_init):
    if has_cache:
        lam_ref, gs_ref, q_ref, ck_ref, cv_ref, k_ref, v_ref, o_ref = refs
        cache = (ck_ref, cv_ref)
    else:
        lam_ref, gs_ref, q_ref, k_ref, v_ref, o_ref = refs
        cache = None
    lp = lam_ref[...]
    lam = (jnp.exp(jnp.sum(lp[0:1] * lp[1:2], axis=-1, keepdims=True))
           - jnp.exp(jnp.sum(lp[2:3] * lp[3:4], axis=-1, keepdims=True)) + lam_init)
    rows = tq // split
    for bb in range(q_ref.shape[0]):
        qb, kb, vb, ob = q_ref.at[bb], k_ref.at[bb], v_ref.at[bb], o_ref.at[bb]
        for hd in range(heads):
            for part in range(split):
                q = qb[part * rows:(part + 1) * rows, hd * LANES:(hd + 1) * LANES]
                lane = lax.broadcasted_iota(jnp.int32, q.shape, 1)
                zero = jnp.zeros_like(q)
                qz = jnp.concatenate([jnp.where(lane < DK_B, q, zero), jnp.where(lane >= DK_B, q, zero)], axis=0)
                o = _softmax_pv(qz, _srcs(cache, kb, vb, kc, hd))
                dlt = o[:rows] - lam * o[rows:]
                ob[part * rows:(part + 1) * rows, hd * LANES:(hd + 1) * LANES] = (
                    _rms(dlt, gs_ref[...]) * (1.0 - lam_init)).astype(BF16)


def _attn_calls(qkv3, caches, layer, lam_params, g_sub, lam_init, tq_a, tq_b, kc, ha, hb, split_a, split_b, nb):
    b, n, _ = qkv3.shape
    has_cache = caches is not None
    assert b % nb == 0 and (nb == 1 or not has_cache)
    qa_blk =GQA_GROUP * HEAD_DIM_A // LANES
    k_a0 = N_KV_HEADS_A * qa_blk
    v_a0 = k_a0 + N_KV_HEADS_A
    q_b0 = v_a0 + N_KV_HEADS_A
    k_b0 = q_b0 + N_HEADS_B
    v_b0 = k_b0 + N_HEADS_B
    assert all(x % ha == 0 for x in (N_KV_HEADS_A, k_a0, v_a0)) and all(x % hb == 0 for x in (N_HEADS_B, q_b0, k_b0, v_b0))

    qw, kw = ha * GQA_GROUP * LANES, ha * LANES
    in_specs = [pl.BlockSpec((nb, tq_a, qw), lambda bi, h, i: (bi, i, h))]
    args = [qkv3]
    if has_cache:
        ck, cv = caches[0], caches[1]
        p = ck.shape[2]
        in_specs += [pl.BlockSpec((None, None, p, kw), lambda bi, h, i: (bi, layer, 0, h))] * 2
        args += [ck, cv]
    in_specs += [pl.BlockSpec((nb, n, kw), lambda bi, h, i: (bi, 0, k_a0 // ha + h)),
                 pl.BlockSpec((nb, n, kw), lambda bi, h, i: (bi, 0, v_a0 // ha + h))]
    args += [qkv3, qkv3]
    att_a = pl.pallas_call(
        functools.partial(_attn_a_kernel, has_cache=has_cache, tq=tq_a, kc=kc, heads=ha, split=split_a),
        out_shape=jax.ShapeDtypeStruct((b, n, N_KV_HEADS_A * GQA_GROUP * HEAD_DIM_A), BF16),
        grid=(b // nb, N_KV_HEADS_A // ha, n // tq_a),
        in_specs=in_specs,
        out_specs=pl.BlockSpec((nb, tq_a, qw), lambda bi, h, i: (bi, i, h)),
        compiler_params=_cparams(("parallel", "parallel", "arbitrary")),
        name="attn_a_lat" if has_cache else "attn_a_ctx",
    )(*args)

    bw = hb * LANES
    in_specs = [pl.BlockSpec((4, DK_B), lambda bi, h, i: (0, 0)),
                pl.BlockSpec((1, DV_B), lambda bi, h, i: (0, 0)),
                pl.BlockSpec((nb, tq_b, bw), lambda bi, h, i: (bi, i, q_b0 // hb + h))]
    args = [lam_params, g_sub, qkv3]
    if has_cache:
        ck, cv = caches[2], caches[3]
        p = ck.shape[2]
        in_specs += [pl.BlockSpec((None, None, p, bw), lambda bi, h, i: (bi, layer, 0, h))] * 2
        args += [ck, cv]
    in_specs += [pl.BlockSpec((nb, n, bw), lambda bi, h, i: (bi, 0, k_b0 // hb + h)),
                 pl.BlockSpec((nb, n, bw), lambda bi, h, i: (bi, 0, v_b0 // hb + h))]
    args += [qkv3, qkv3]
    att_b = pl.pallas_call(
        functools.partial(_attn_b_kernel, has_cache=has_cache, tq=tq_b, kc=kc, heads=hb, split=split_b, lam_init=lam_init),
        out_shape=jax.ShapeDtypeStruct((b, n, N_HEADS_B * DV_B), BF16),
        grid=(b // nb, N_HEADS_B // hb, n // tq_b),
        in_specs=in_specs,
        out_specs=pl.BlockSpec((nb, tq_b, bw), lambda bi, h, i: (bi, i, h)),
        compiler_params=_cparams(("parallel", "parallel", "arbitrary")),
        name="attn_b_lat" if has_cache else "attn_b_ctx",
    )(*args)
    return att_a, att_b


def _dft_cols_kernel(x_ref, mat_ref, o_ref, *, width, scale):
    p, g, w2 = x_ref.shape
    x = x_ref[...].reshape(p * g, w2)
    xs = jnp.concatenate([x[:, :width], x[:, width:]], axis=0)
    u = jnp.dot(mat_ref[...], xs, preferred_element_type=F32)
    o_ref[...] = (u * scale).astype(BF16).reshape(o_ref.shape)


def _dft_rows_kernel(x_ref, mat_ref, tc_ref, ts_ref, o_ref, *, width):
    tiles, rows, w2 = x_ref.shape
    pair = BF16_SUBLANES
    reps = width // LANES
    for p in range(rows // pair):
        x = x_ref[:, p * pair:(p + 1) * pair, :].reshape(tiles * pair, w2)
        xs = jnp.concatenate([x[:, :width], x[:, width:]], axis=0)
        u = jnp.dot(mat_ref[...], xs, preferred_element_type=F32)
        half = u.shape[0] // 2
        ur, ui = u[:half], u[half:]
        tc = jnp.concatenate([tc_ref[2 * p:2 * p + 2].reshape(half, LANES)] * reps, axis=1)
        ts = jnp.concatenate([ts_ref[2 * p:2 * p + 2].reshape(half, LANES)] * reps, axis=1)
        o_ref[2 * p:2 * p + 2, :, :width] = (ur * tc - ui * ts).astype(BF16).reshape(2, half // 2, width)
        o_ref[2 * p:2 * p + 2, :, width:] = (ur * ts + ui * tc).astype(BF16).reshape(2, half // 2, width)


def _dft_seq_kernel(x_ref, mat_ref, o_ref, *, width, scale):
    nb = x_ref.shape[0]
    xs = jnp.concatenate(
        [jnp.concatenate([x_ref[bb, :, :width], x_ref[bb, :, width:]], axis=0) for bb in range(nb)], axis=1)
    u = jnp.dot(mat_ref[...], xs, preferred_element_type=F32)
    for bb in range(nb):
        o_ref[bb] = (u[:, bb * width:(bb + 1) * width] * scale).astype(BF16)


def _cos_sin(n_rows, n_cols, period):
    a = jnp.arange(n_rows, dtype=jnp.int32)[:, None]
    b = jnp.arange(n_cols, dtype=jnp.int32)[None, :]
    ang = ((a * b) % period).astype(F32) * (2.0 * math.pi / period)
    return jnp.cos(ang), jnp.sin(ang)


def _fourier_call(y3, n1, n2, tile_n2=None):
    b, n, w2 = y3.shape
    width = w2 // 2
    scale = 1.0 / math.sqrt(n * FOURIER_GROUP_DIM)
    c1, s1 = _cos_sin(n1, n1, n1)
    w_real = jnp.stack([c1, -s1], axis=1)
    if n2 == 1:
        mat = w_real.reshape(n1, 2 * n1).astype(BF16)
        nb = math.gcd(b, CTX_SEQS)
        return pl.pallas_call(
            functools.partial(_dft_seq_kernel, width=width, scale=scale),
            out_shape=jax.ShapeDtypeStruct((b, n, width), BF16),
            grid=(b // nb,),
            in_specs=[pl.BlockSpec((nb, n, w2), lambda bi: (bi, 0, 0)),
                      pl.BlockSpec((n1, 2 * n1), lambda bi: (0, 0))],
            out_specs=pl.BlockSpec((nb, n, width), lambda bi: (bi, 0, 0)),
            compiler_params=_cparams(("parallel",)),
            name="dft_ctx",
        )(y3, mat)

    g = BF16_SUBLANES
    c2, s2 = _cos_sin(n2, n2, n2)
    w_cplx = jnp.stack([jnp.stack([c2, -s2], axis=1), jnp.stack([s2, c2], axis=1)], axis=0)

    def kron_cols(base, row_j):
        rows, cols = base.shape
        col = lax.broadcasted_iota(jnp.int32, (cols, cols * g), 1)
        expand = (col // g == lax.broadcasted_iota(jnp.int32, (cols, cols * g), 0)).astype(BF16)
        wide = jnp.dot(base.astype(BF16), expand, preferred_element_type=F32)
        keep = row_j[:, None] == (lax.broadcasted_iota(jnp.int32, (rows, cols * g), 1) % g)
        return jnp.where(keep, wide, 0.0).astype(BF16)

    base2 = jnp.broadcast_to(w_real.reshape(n1, 1, 2 * n1), (n1, g, 2 * n1)).reshape(n1 * g, 2 * n1)
    mat2 = kron_cols(base2, jnp.arange(n1 * g, dtype=jnp.int32) % g)
    tc, ts = _cos_sin(n1, n2, n)
    tc = jnp.broadcast_to(tc[:, :, None], (n1, n2, LANES))
    ts = jnp.broadcast_to(ts[:, :, None], (n1, n2, LANES))
    tiles = n2 // tile_n2
    tile_len = n1 * tile_n2
    mat1 = jnp.einsum('akbtl,ji->ajkbtil', w_cplx.reshape(2, n2, 2, tiles, tile_n2), jnp.eye(2, dtype=F32))
    mat1 = mat1.reshape(4 * n2, 4 * n2).astype(BF16)
    gp = 8
    t = pl.pallas_call(
        functools.partial(_dft_rows_kernel, width=width),
        out_shape=jax.ShapeDtypeStruct((b, n1, n2, w2), BF16),
        grid=(b, tile_len // (BF16_SUBLANES * gp)),
        in_specs=[pl.BlockSpec((None, tiles, BF16_SUBLANES * gp, w2), lambda bi, j: (bi, 0, j, 0)),
                  pl.BlockSpec(mat1.shape, lambda bi, j: (0, 0)),
                  pl.BlockSpec((2 * gp, n2, LANES), lambda bi, j: (j, 0, 0)),
                  pl.BlockSpec((2 * gp, n2, LANES), lambda bi, j: (j, 0, 0))],
        out_specs=pl.BlockSpec((None, 2 * gp, n2, w2), lambda bi, j: (bi, j, 0, 0)),
        compiler_params=_cparams(("parallel", "parallel")),
        name="dft_stage1",
    )(y3.reshape(b, tiles, tile_len, w2), mat1, tc, ts)
    out = pl.pallas_call(
        functools.partial(_dft_cols_kernel, width=width, scale=scale),
        out_shape=jax.ShapeDtypeStruct((b, n1, n2, width), BF16),
        grid=(b, n2 // g),
        in_specs=[pl.BlockSpec((None, n1, g, w2), lambda bi, j: (bi, 0, j, 0)),
                  pl.BlockSpec(mat2.shape, lambda bi, j: (0, 0), pipeline_mode=pl.Buffered(1))],
        out_specs=pl.BlockSpec((None, n1, g, width), lambda bi, j: (bi, 0, j, 0)),
        compiler_params=_cparams(("parallel", "parallel")),
        name="dft_stage2",
    )(t, mat2)
    return out.reshape(b, n, width)


def _out_kernel(a_ref, b_ref, f_ref, w_ref, x_ref, mod_ref, g_ref, x1_ref, h2_ref):
    ca = a_ref.shape[1]
    cb = b_ref.shape[1]
    m = mod_ref[...]
    half = a_ref.shape[0] // 2
    for r0 in (0, half):
        rows = slice(r0, r0 + half)
        acc = jnp.dot(a_ref[rows, :], w_ref[0:ca, :], preferred_element_type=F32)
        acc += jnp.dot(b_ref[rows, :], w_ref[ca:ca + cb, :], preferred_element_type=F32)
        acc += jnp.dot(f_ref[rows, :], w_ref[ca + cb:, :], preferred_element_type=F32)
        x1 = x_ref[rows, :] + m[2:3] * acc
        x1_ref[rows, :] = x1
        h = _rms(x1, g_ref[...])
        h2_ref[rows, :] = (h * (1.0 + m[4:5]) + m[3:4]).astype(BF16)


def _out_call(att_a, att_b, four, w_out_b, x2d, mod_l, cond_row_fn, norm_g, tm):
    t, d = x2d.shape
    ca, cb, cf = att_a.shape[1], att_b.shape[1], four.shape[1]
    w_out_b, w_layer = w_out_b
    return pl.pallas_call(
        _out_kernel,
        out_shape=[jax.ShapeDtypeStruct((t, d), F32), jax.ShapeDtypeStruct((t, d), BF16)],
        grid=(t // tm,),
        in_specs=[
            pl.BlockSpec((tm, ca), lambda i: (i, 0)),
            pl.BlockSpec((tm, cb), lambda i: (i, 0)),
            pl.BlockSpec((tm, cf), lambda i: (i, 0)),
            pl.BlockSpec((None, ca + cb + cf, d), lambda i: (w_layer, 0, 0), pipeline_mode=pl.Buffered(1)),
            pl.BlockSpec((tm, d), lambda i: (i, 0)),
            pl.BlockSpec((None, N_MOD, d), lambda i: (cond_row_fn(i), 0, 0)),
            pl.BlockSpec((1, d), lambda i: (0, 0)),
        ],
        out_specs=[pl.BlockSpec((tm, d), lambda i: (i, 0)), pl.BlockSpec((tm, d), lambda i: (i, 0))],
        compiler_params=_cparams(("parallel",)),
        name="out_proj",
    )(att_a, att_b, four, w_out_b, x2d, mod_l, norm_g)


HALO = BF16_SUBLANES


def _ffn_kernel(cw_ref, cb_ref, mod_ref, gf_ref, h_hbm, x1_hbm, wg_hbm, wu_hbm, wd_hbm,
                o_ref, hext_ref, x1_buf, wg_buf, wu_buf, wd_buf, sem, *, tm, tf, nj, layer, seq_len, final):
    i = pl.program_id(0)
    n_i = pl.num_programs(0)
    row0 = pl.multiple_of(i * tm, tm)

    def weight_copies(j, slot):
        cols = pl.ds(pl.multiple_of(j * tf, tf), tf)
        return (pltpu.make_async_copy(wg_hbm.at[layer, :, cols], wg_buf.at[slot], sem.at[0, slot]),
                pltpu.make_async_copy(wu_hbm.at[layer, :, cols], wu_buf.at[slot], sem.at[1, slot]),
                pltpu.make_async_copy(wd_hbm.at[layer, cols, :], wd_buf.at[slot], sem.at[2, slot]))

    def tile_copies(ti, slot):
        start = pl.multiple_of(ti * tm, tm)
        before = pl.multiple_of(jnp.maximum(start - HALO, 0), HALO)
        after = pl.multiple_of(jnp.minimum(start + tm, n_i * tm - HALO), HALO)
        dst = hext_ref.at[slot]
        return (pltpu.make_async_copy(h_hbm.at[pl.ds(before, HALO), :], dst.at[pl.ds(0, HALO), :], sem.at[3, slot]),
                pltpu.make_async_copy(h_hbm.at[pl.ds(start, tm), :], dst.at[pl.ds(HALO, tm), :], sem.at[4, slot]),
                pltpu.make_async_copy(h_hbm.at[pl.ds(after, HALO), :], dst.at[pl.ds(HALO + tm, HALO), :],
                                      sem.at[5, slot]))

    x1_copy = pltpu.make_async_copy(x1_hbm.at[pl.ds(row0, tm), :], x1_buf, sem.at[6, 0])
    hslot = i % 2

    @pl.when(i == 0)
    def _():
        for cp in tile_copies(0, 0) + weight_copies(0, 0):
            cp.start()

    x1_copy.start()

    @pl.when(i + 1 < n_i)
    def _():
        for cp in tile_copies(i + 1, 1 - hslot):
            cp.start()

    ext = tm + 2 * HALO
    pos = (i * tm + lax.broadcasted_iota(jnp.int32, (tm, 1), 0)) % seq_len
    for cp in tile_copies(i, hslot):
        cp.wait()
    hext = hext_ref.at[hslot]

    def chunk(j, first):
        step = i * nj + j
        slot = step % 2
        for cp in weight_copies(j, slot):
            cp.wait()

        @pl.when(step + 1 < n_i * nj)
        def _():
            for cp in weight_copies((j + 1) % nj, 1 - slot):
                cp.start()

        g = jnp.dot(hext[...], wg_buf[slot], preferred_element_type=F32)
        u = jnp.dot(hext[HALO:HALO + tm, :], wu_buf[slot], preferred_element_type=F32)
        g_prev = jnp.where(pos == 0, 0.0, pltpu.roll(g, 1, 0)[HALO:HALO + tm])
        g_next = jnp.where(pos == seq_len - 1, 0.0, pltpu.roll(g, ext - 1, 0)[HALO:HALO + tm])
        cw = cw_ref[j]
        gc = g_prev * cw[0:1] + g[HALO:HALO + tm] * cw[1:2] + g_next * cw[2:3] + cb_ref[j]
        act = (gc * jax.nn.sigmoid(gc)) * u
        y = jnp.dot(act.astype(BF16), wd_buf[slot], preferred_element_type=F32)
        if first:
            o_ref[...] = y
        else:
            o_ref[...] += y

    chunk(0, True)
    lax.fori_loop(1, nj, lambda j, carry: (chunk(j, False), carry)[1], 0)

    x1_copy.wait()
    m = mod_ref[...]
    x2 = x1_buf[...] + m[5:6] * o_ref[...]
    if final:
        x2 = _rms(x2, gf_ref[...])
    o_ref[...] = x2


def _ffn_call(h2, x1, weights, conv_w, conv_b, mod_l, cond_row_fn, final_g, seq_len, final, tm, tf):
    t, d = x1.shape
    w_gate_b, w_up_b, w_down_b, w_layer = weights
    f = w_gate_b.shape[2]
    nj = f // tf
    cw3 = conv_w.reshape(3, nj, tf).transpose(1, 0, 2)
    cb3 = conv_b.reshape(nj, 1, tf)
    hbm = pl.BlockSpec(memory_space=pl.ANY)
    return pl.pallas_call(
        functools.partial(_ffn_kernel, tm=tm, tf=tf, nj=nj, layer=w_layer, seq_len=seq_len, final=final),
        out_shape=jax.ShapeDtypeStruct((t, d), F32),
        grid=(t // tm,),
        in_specs=[
            pl.BlockSpec((nj, 3, tf), lambda i: (0, 0, 0)),
            pl.BlockSpec((nj, 1, tf), lambda i: (0, 0, 0)),
            pl.BlockSpec((None, N_MOD, d), lambda i: (cond_row_fn(i), 0, 0)),
            pl.BlockSpec((1, d), lambda i: (0, 0)),
            hbm, hbm, hbm, hbm, hbm,
        ],
        out_specs=pl.BlockSpec((tm, d), lambda i: (i, 0)),
        scratch_shapes=[pltpu.VMEM((2, tm + 2 * HALO, d), BF16), pltpu.VMEM((tm, d), F32),
                        pltpu.VMEM((2, d, tf), BF16), pltpu.VMEM((2, d, tf), BF16), pltpu.VMEM((2, tf, d), BF16),
                        pltpu.SemaphoreType.DMA((7, 2))],
        compiler_params=_cparams(("arbitrary",)),
        name="conv_ffn",
    )(cw3, cb3, mod_l, final_g, h2, x1, w_gate_b, w_up_b, w_down_b)


def _rope_tables(n, head_dim):
    rows = n // GRID_W
    t_row = jnp.repeat(jnp.arange(rows, dtype=F32), GRID_W)
    t_col = jnp.tile(jnp.arange(GRID_W, dtype=F32), rows)
    axis_dim = head_dim // 2
    inv = jnp.power(ROPE_BASE, -jnp.arange(0, axis_dim, 2, dtype=F32) / axis_dim)
    ar = t_row[:, None] * inv[None, :]
    ac = t_col[:, None] * inv[None, :]
    ang = jnp.concatenate([ar, ar, ac, ac], axis=-1)
    reps = LANES // head_dim
    quarter = head_dim // 4
    sign = jnp.where((jnp.arange(head_dim) % (2 * quarter)) < quarter, -1.0, 1.0).astype(F32)
    cos = jnp.tile(jnp.cos(ang), (1, reps))
    sin_signed = jnp.tile(jnp.sin(ang) * sign[None, :], (1, reps))
    return cos, sin_signed


def kernel(x_prompt, x_sample, cache_attn_k, cache_attn_v, cache_diff_k, cache_diff_v, c, c_ctx, norm1_g, norm2_g, w_ada, b_ada, w_in, attn_q_norm_g, attn_k_norm_g, diff_lambda_q1, diff_lambda_k1, diff_lambda_q2, diff_lambda_k2, diff_subnorm_g, w_out, ffn_w_gate, ffn_w_up, ffn_conv_w, ffn_conv_b, ffn_w_down, final_norm_g):
    depth = w_in.shape[0]
    bc, lc, d = x_prompt.shape
    bl, ll, _ = x_sample.shape
    past = cache_attn_k.shape[2]

    w_in_b = w_in.astype(BF16)
    w_out_b = w_out.astype(BF16)
    w_gate_b = ffn_w_gate.astype(BF16)
    w_up_b = ffn_w_up.astype(BF16)
    w_down_b = ffn_w_down.astype(BF16)

    n_rows = 8 * ((1 + bl + 7) // 8)
    cvec = jnp.concatenate([c_ctx[None, :], c, jnp.zeros((n_rows - 1 - bl, d), F32)], axis=0)
    mod = _ada_call(cvec, w_ada, b_ada).reshape(depth, n_rows, N_MOD, d)

    dft_c = jnp.concatenate(_cos_sin(FOURIER_GROUP_DIM, FOURIER_GROUP_DIM, FOURIER_GROUP_DIM), axis=1).astype(BF16)
    rope_tabs = _rope_tables(ll, HEAD_DIM_A) + _rope_tables(ll, DK_B)
    caches = (cache_attn_k.reshape(bl, depth, past, N_KV_HEADS_A * HEAD_DIM_A),
              cache_attn_v.reshape(bl, depth, past, N_KV_HEADS_A * HEAD_DIM_A),
              cache_diff_k.reshape(bl, depth, past, N_HEADS_B * 2 * DK_B),
              cache_diff_v.reshape(bl, depth, past, N_HEADS_B * DV_B))
    lam_all = jnp.stack([diff_lambda_q1, diff_lambda_k1, diff_lambda_q2, diff_lambda_k2], axis=1)

    def run_pass(x3, is_ctx):
        b, n, _ = x3.shape
        t = b * n
        tm = ROW_TILE
        tm_ffn = FFN_ROW_TILE
        x = x3.reshape(t, d)

        def cond_rows(tile):
            return (lambda i: 0) if is_ctx else (lambda i: 1 + (i * tile) // n)

        cond_row_fn = cond_rows(tm)
        new_kv = None
        for l in range(depth):
            lam_init = 0.8 - 0.6 * math.exp(-0.3 * l)
            res = _in_call(x, mod[l], cond_row_fn, norm1_g[l][None, :], (w_in_b, l),
                           attn_q_norm_g[l][None, :], attn_k_norm_g[l][None, :], dft_c,
                           None if is_ctx else rope_tabs, n, (depth, l, new_kv) if is_ctx else None, tm)
            qkv, y12 = res[0], res[1]
            if is_ctx:
                new_kv = res[2:]
            qkv3 = qkv.reshape(b, n, qkv.shape[1])
            att_a, att_b = _attn_calls(qkv3, None if is_ctx else caches, l, lam_all[l],
                                       diff_subnorm_g[l][None, :], lam_init,
                                       tq_a=min(ATTN_TQ_A, n), tq_b=min(ATTN_TQ_B, n), kc=ATTN_KC,
                                       ha=N_KV_HEADS_A if is_ctx else 1, hb=N_HEADS_B if is_ctx else 1,
                                       split_a=1 if is_ctx else ATTN_STREAMS, split_b=1 if is_ctx else ATTN_STREAMS,
                                       nb=math.gcd(b, CTX_SEQS) if is_ctx else 1)
            if is_ctx:
                four = _fourier_call(y12.reshape(b, n, y12.shape[1]), n, 1)
            else:
                four = _fourier_call(y12.reshape(b, n, y12.shape[1]), GRID_W, n // GRID_W, tm // GRID_W)
            x1, h2 = _out_call(att_a.reshape(t, -1), att_b.reshape(t, -1), four.reshape(t, -1), (w_out_b, l),
                               x, mod[l], cond_row_fn, norm2_g[l][None, :], tm)
            x = _ffn_call(h2, x1, (w_gate_b, w_up_b, w_down_b, l), ffn_conv_w[l], ffn_conv_b[l][None, :],
                          mod[l], cond_rows(tm_ffn), final_norm_g[None, :], n, l == depth - 1, tm_ffn, FFN_TF)
        return x.reshape(b, n, d), new_kv

    y_prompt, kvs = run_pass(x_prompt, True)
    new_attn_k = kvs[0].reshape(bc, depth, lc, N_KV_HEADS_A, HEAD_DIM_A)
    new_attn_v = kvs[1].reshape(bc, depth, lc, N_KV_HEADS_A, HEAD_DIM_A)
    new_diff_k = kvs[2].reshape(bc, depth, lc, N_HEADS_B, 2 * DK_B)
    new_diff_v = kvs[3].reshape(bc, depth, lc, N_HEADS_B, DV_B)

    y_sample, _ = run_pass(x_sample, False)
    return (y_prompt, y_sample, new_attn_k, new_attn_v, new_diff_k, new_diff_v)
```

```python
import functools
import math

import jax
import jax.numpy as jnp
from jax import lax
from jax.experimental import pallas as pl
from jax.experimental.pallas import tpu as pltpu

F32 = jnp.float32
BF16 = jnp.bfloat16

GRID_W = 64
ROPE_BASE = 10000.0
NORM_EPS = 1e-6
HEAD_DIM_A = 128
N_KV_HEADS_A = 2
GQA_GROUP = 4
N_HEADS_B = 4
DK_B = 64
DV_B = 128
N_FOURIER_GROUPS = 4
FOURIER_GROUP_DIM = 128
N_MOD = 6

LOG2E = math.log2(math.e)
LANES = 128
BF16_SUBLANES = 16
VMEM_LIMIT = 58 * 1024 * 1024

ROW_TILE = 512
FFN_ROW_TILE = 1024
FFN_TF = 512
ADA_TN = 1024
ATTN_TQ_A = 512
ATTN_TQ_B = 1024
ATTN_KC = 1024
ATTN_STREAMS = 4
CTX_SEQS = 4


def _cparams(sem):
    return pltpu.CompilerParams(dimension_semantics=sem, vmem_limit_bytes=VMEM_LIMIT)


def _rms(x, g):
    return x * lax.rsqrt(jnp.mean(x * x, axis=-1, keepdims=True) + NORM_EPS) * g


def _ada_kernel(c_ref, w_ref, b_ref, o_ref):
    c = c_ref[...]
    s = (c * jax.nn.sigmoid(c)).astype(BF16)
    w = w_ref[...].astype(BF16)
    o_ref[...] = jnp.dot(s, w, preferred_element_type=F32) + b_ref[...]


def _ada_call(cvec, w_ada, b_ada):
    depth, d, n = w_ada.shape
    rows = cvec.shape[0]
    tn = ADA_TN
    return pl.pallas_call(
        _ada_kernel,
        out_shape=jax.ShapeDtypeStruct((depth, rows, n), F32),
        grid=(depth, n // tn),
        in_specs=[
            pl.BlockSpec((rows, d), lambda l, j: (0, 0)),
            pl.BlockSpec((None, d, tn), lambda l, j: (l, 0, j)),
            pl.BlockSpec((None, 1, tn), lambda l, j: (l, 0, j)),
        ],
        out_specs=pl.BlockSpec((None, rows, tn), lambda l, j: (l, 0, j)),
        compiler_params=_cparams(("parallel", "parallel")),
        name="ada_mod",
    )(cvec, w_ada, b_ada.reshape(depth, 1, n))


def _rope(xs, cos, sin_signed, shift):
    w = xs.shape[-1]
    lane = lax.broadcasted_iota(jnp.int32, xs.shape, 1)
    first = (lane % (2 * shift)) < shift
    rot = jnp.where(first, pltpu.roll(xs, w - shift, 1), pltpu.roll(xs, shift, 1))
    return xs * cos + rot * sin_signed


def _in_kernel(*refs, rope, ctx_out, n_alias, cols):
    it = iter(refs)
    x_ref, mod_ref, g_ref, w_ref, gq_ref, gk_ref, dft_ref = (next(it) for _ in range(7))
    if rope:
        cos_a, sin_a, cos_b, sin_b = (next(it)[...] for _ in range(4))
        perm_ref = next(it)
    for _ in range(n_alias):
        next(it)
    qkv_ref, y_ref = next(it), next(it)
    if ctx_out:
        ka_ref, va_ref, kb_ref, vb_ref = (next(it) for _ in range(4))
    c_qa, c_ka, c_va, c_qb, c_kb, c_vb, c_f, c_end = cols

    def put_cache(ref, off, v):
        head = off // LANES
        seqs, rows = ref.shape[0], ref.shape[-2]
        heads = rows * seqs // v.shape[0]
        n = rows // heads
        for s in range(seqs):
            vs = v[s * n:(s + 1) * n, :]
            if len(ref.shape) == 4:
                for dd in range(ref.shape[1]):
                    ref[s, dd, pl.ds(head, n, stride=heads), :] = vs
            else:
                ref[s, pl.ds(head, n, stride=heads), :] = vs

    m = mod_ref[...]
    h = _rms(x_ref[...], g_ref[...])
    hb = (h * (1.0 + m[1:2]) + m[0:1]).astype(BF16)
    gq = gq_ref[...]
    gk = gk_ref[...]
    scale_a = HEAD_DIM_A ** -0.5 * LOG2E
    scale_b = DK_B ** -0.5 * LOG2E
    chunk = 4 * LANES

    for c0 in range(0, c_end, chunk):
        acc = jnp.dot(hb, w_ref[:, c0:c0 + chunk], preferred_element_type=F32)
        if rope and c0 >= c_f:
            acc = jnp.dot(perm_ref[...], acc.astype(BF16), preferred_element_type=F32)
        for s in range(chunk // LANES):
            col = c0 + s * LANES
            v = acc[:, s * LANES:(s + 1) * LANES]
            if col < c_ka:
                v = _rms(v, gq)
                if rope:
                    v = _rope(v, cos_a, sin_a, HEAD_DIM_A // 4)
                qkv_ref[:, col:col + LANES] = (v * scale_a).astype(BF16)
            elif col < c_va:
                v = _rms(v, gk)
                if ctx_out:
                    put_cache(ka_ref, col - c_ka, v)
                if rope:
                    v = _rope(v, cos_a, sin_a, HEAD_DIM_A // 4)
                qkv_ref[:, col:col + LANES] = v.astype(BF16)
            elif col < c_qb:
                if ctx_out:
                    put_cache(va_ref, col - c_va, v)
                qkv_ref[:, col:col + LANES] = v.astype(BF16)
            elif col < c_kb:
                if rope:
                    v = _rope(v, cos_b, sin_b, DK_B // 4)
                qkv_ref[:, col:col + LANES] = (v * scale_b).astype(BF16)
            elif col < c_vb:
                if ctx_out:
                    put_cache(kb_ref, col - c_kb, v)
                if rope:
                    v = _rope(v, cos_b, sin_b, DK_B // 4)
                qkv_ref[:, col:col + LANES] = v.astype(BF16)
            elif col < c_f:
                if ctx_out:
                    put_cache(vb_ref, col - c_vb, v)
                qkv_ref[:, col:col + LANES] = v.astype(BF16)
            else:
                yy = jnp.dot(v.astype(BF16), dft_ref[...], preferred_element_type=F32)
                gcol = col - c_f
                half = c_end - c_f
                y_ref[:, gcol:gcol + LANES] = yy[:, :LANES].astype(BF16)
                y_ref[:, half + gcol:half + gcol + LANES] = yy[:, LANES:].astype(BF16)


def _in_call(x2d, mod_l, cond_row_fn, norm_g, w_in_b, gq, gk, dft_c, rope_tabs, seq_len, new_cache, tm):
    ctx_out = new_cache is not None
    t, d = x2d.shape
    w_in_b, w_layer = w_in_b
    d_in = w_in_b.shape[2]
    c_qa = 0
    c_ka = N_KV_HEADS_A * GQA_GROUP * HEAD_DIM_A
    c_va = c_ka + N_KV_HEADS_A * HEAD_DIM_A
    c_qb = c_va + N_KV_HEADS_A * HEAD_DIM_A
    c_kb = c_qb + N_HEADS_B * 2 * DK_B
    c_vb = c_kb + N_HEADS_B * 2 * DK_B
    c_f = c_vb + N_HEADS_B * DV_B
    c_end = c_f + N_FOURIER_GROUPS * FOURIER_GROUP_DIM
    assert c_end == d_in
    cols = (c_qa, c_ka, c_va, c_qb, c_kb, c_vb, c_f, c_end)
    rope = rope_tabs is not None
    n_f = c_end - c_f

    in_specs = [
        pl.BlockSpec((tm, d), lambda i: (i, 0)),
        pl.BlockSpec((None, N_MOD, d), lambda i: (cond_row_fn(i), 0, 0)),
        pl.BlockSpec((1, d), lambda i: (0, 0)),
        pl.BlockSpec((None, d, d_in), lambda i: (w_layer, 0, 0), pipeline_mode=pl.Buffered(1)),
        pl.BlockSpec((1, HEAD_DIM_A), lambda i: (0, 0)),
        pl.BlockSpec((1, HEAD_DIM_A), lambda i: (0, 0)),
        pl.BlockSpec((FOURIER_GROUP_DIM, 2 * FOURIER_GROUP_DIM), lambda i: (0, 0)),
    ]
    args = [x2d, mod_l, norm_g, w_in_b, gq, gk, dft_c]
    if rope:
        nblk = seq_len // tm
        for tab in rope_tabs:
            in_specs.append(pl.BlockSpec((tm, LANES), lambda i: (i % nblk, 0)))
            args.append(tab)
        r_new = jnp.arange(tm, dtype=jnp.int32)
        r_old = (r_new % (tm // GRID_W)) * GRID_W + r_new // (tm // GRID_W)
        perm = (r_old[:, None] == jnp.arange(tm, dtype=jnp.int32)[None, :]).astype(BF16)
        in_specs.append(pl.BlockSpec((tm, tm), lambda i: (0, 0)))
        args.append(perm)
    out_shape = [jax.ShapeDtypeStruct((t, c_f), BF16), jax.ShapeDtypeStruct((t, 2 * n_f), BF16)]
    out_specs = [pl.BlockSpec((tm, c_f), lambda i: (i, 0)), pl.BlockSpec((tm, 2 * n_f), lambda i: (i, 0))]
    aliases = {}
    n_alias = 0
    if ctx_out:
        depth, layer, prev = new_cache
        nb = t // seq_len
        spt = tm // seq_len
        for width in (c_va - c_ka, c_qb - c_va, c_vb - c_kb, c_f - c_vb):
            rows = seq_len * (width // LANES)
            out_shape.append(jax.ShapeDtypeStruct((nb, depth, rows, LANES), F32))
            if prev is None:
                out_specs.append(pl.BlockSpec((spt, depth, rows, LANES), lambda i: (i, 0, 0, 0)))
            else:
                out_specs.append(pl.BlockSpec((spt, None, rows, LANES), lambda i: (i, layer, 0, 0)))
        if prev is not None:
            n_alias = len(prev)
            for a, arr in enumerate(prev):
                aliases[len(args)] = 2 + a
                in_specs.append(pl.BlockSpec(memory_space=pl.ANY))
                args.append(arr)
    return pl.pallas_call(
        functools.partial(_in_kernel, rope=rope, ctx_out=ctx_out, n_alias=n_alias, cols=cols),
        out_shape=out_shape,
        grid=(t // tm,),
        in_specs=in_specs,
        out_specs=out_specs,
        input_output_aliases=aliases,
        compiler_params=_cparams(("parallel",)),
        name="in_proj_ctx" if ctx_out else "in_proj_lat",
    )(*args)


def _softmax_pv(q, srcs):
    chunks = []
    for k_ref, v_ref, n_keys, kc, col in srcs:
        for c in range(n_keys // kc):
            chunks.append([(k_ref, v_ref, c * kc, kc, col)])
    if len(chunks) > 1 and chunks[0][0][3] < chunks[1][0][3]:
        chunks = [chunks[0] + chunks[1]] + chunks[2:]

    def scores(ch):
        parts = []
        for k_ref, _, off, kc, col in ch:
            k = k_ref[off:off + kc, col:col + LANES].astype(BF16)
            parts.append(lax.dot_general(q, k, (((1,), (1,)), ((), ())), preferred_element_type=F32))
        return parts[0] if len(parts) == 1 else jnp.concatenate(parts, axis=1)

    def lane_fold(x, op):
        out = x[:, 0:LANES]
        for t in range(1, x.shape[1] // LANES):
            out = op(out, x[:, t * LANES:(t + 1) * LANES])
        return out

    mxu_sum = len(chunks) > 1
    m = acc = l_part = None
    s_next = scores(chunks[0])
    for i, ch in enumerate(chunks):
        s = s_next
        if i + 1 < len(chunks):
            s_next = scores(chunks[i + 1])
        row_max = jnp.max(lane_fold(s, jnp.maximum), axis=-1, keepdims=True)
        m_new = row_max if m is None else jnp.maximum(m, row_max)
        p = jnp.exp2(s - m_new)
        if not mxu_sum:
            l_part = lane_fold(p, jnp.add)
        pb = p.astype(BF16)
        pv = None
        lo = 0
        for _, v_ref, off, kc, col in ch:
            v = v_ref[off:off + kc, col:col + LANES].astype(BF16)
            if mxu_sum:
                ones_col = (lax.broadcasted_iota(jnp.int32, (kc, LANES), 1) == 0).astype(BF16)
                v = jnp.concatenate([v, ones_col], axis=1)
            part = jnp.dot(pb[:, lo:lo + kc], v, preferred_element_type=F32)
            pv = part if pv is None else pv + part
            lo += kc
        acc = pv if m is None else jnp.exp2(m - m_new) * acc + pv
        m = m_new
    if mxu_sum:
        return acc[:, :LANES] / acc[:, LANES:LANES + 1]
    return acc / jnp.sum(l_part, axis=-1, keepdims=True)


def _srcs(cache_refs, k_ref, v_ref, kc, head):
    srcs = []
    col = head * LANES
    if cache_refs is not None:
        ck, cv = cache_refs
        srcs.append((ck, cv, ck.shape[0], ck.shape[0], col))
    n = k_ref.shape[0]
    srcs.append((k_ref, v_ref, n, min(kc, n), col))
    return srcs


def _attn_a_kernel(*refs, has_cache, tq, kc, heads, split):
    if has_cache:
        q_ref, ck_ref, cv_ref, k_ref, v_ref, o_ref = refs
        cache = (ck_ref, cv_ref)
    else:
        q_ref, k_ref, v_ref, o_ref = refs
        cache = None
    per = GQA_GROUP // split
    for bb in range(q_ref.shape[0]):
        qb, kb, vb, ob = q_ref.at[bb], k_ref.at[bb], v_ref.at[bb], o_ref.at[bb]
        for kh in range(heads):
            for part in range(split):
                cols = [(kh * GQA_GROUP + part * per + h) * LANES for h in range(per)]
                qs = jnp.concatenate([qb[:, c:c + LANES] for c in cols], axis=0)
                o = _softmax_pv(qs, _srcs(cache, kb, vb, kc, kh))
                for h, c in enumerate(cols):
                    ob[:, c:c + LANES] = o[h * tq:(h + 1) * tq].astype(BF16)


def _attn_b_kernel(*refs, has_cache, tq, kc, heads, split, lam_init):
    if has_cache:
        lam_ref, gs_ref, q_ref, ck_ref, cv_ref, k_ref, v_ref, o_ref = refs
        cache = (ck_ref, cv_ref)
    else:
        lam_ref, gs_ref, q_ref, k_ref, v_ref, o_ref = refs
        cache = None
    lp = lam_ref[...]
    lam = (jnp.exp(jnp.sum(lp[0:1] * lp[1:2], axis=-1, keepdims=True))
           - jnp.exp(jnp.sum(lp[2:3] * lp[3:4], axis=-1, keepdims=True)) + lam_init)
    rows = tq // split
    for bb in range(q_ref.shape[0]):
        qb, kb, vb, ob = q_ref.at[bb], k_ref.at[bb], v_ref.at[bb], o_ref.at[bb]
        for hd in range(heads):
            for part in range(split):
                q = qb[part * rows:(part + 1) * rows, hd * LANES:(hd + 1) * LANES]
                lane = lax.broadcasted_iota(jnp.int32, q.shape, 1)
                zero = jnp.zeros_like(q)
                qz = jnp.concatenate([jnp.where(lane < DK_B, q, zero), jnp.where(lane >= DK_B, q, zero)], axis=0)
                o = _softmax_pv(qz, _srcs(cache, kb, vb, kc, hd))
                dlt = o[:rows] - lam * o[rows:]
                ob[part * rows:(part + 1) * rows, hd * LANES:(hd + 1) * LANES] = (
                    _rms(dlt, gs_ref[...]) * (1.0 - lam_init)).astype(BF16)


def _attn_calls(qkv3, caches, layer, lam_params, g_sub, lam_init, tq_a, tq_b, kc, ha, hb, split_a, split_b, nb):
    b, n, _ = qkv3.shape
    has_cache = caches is not None
    assert b % nb == 0 and (nb == 1 or not has_cache)
    qa_blk =GQA_GROUP * HEAD_DIM_A // LANES
    k_a0 = N_KV_HEADS_A * qa_blk
    v_a0 = k_a0 + N_KV_HEADS_A
    q_b0 = v_a0 + N_KV_HEADS_A
    k_b0 = q_b0 + N_HEADS_B
    v_b0 = k_b0 + N_HEADS_B
    assert all(x % ha == 0 for x in (N_KV_HEADS_A, k_a0, v_a0)) and all(x % hb == 0 for x in (N_HEADS_B, q_b0, k_b0, v_b0))

    qw, kw = ha * GQA_GROUP * LANES, ha * LANES
    in_specs = [pl.BlockSpec((nb, tq_a, qw), lambda bi, h, i: (bi, i, h))]
    args = [qkv3]
    if has_cache:
        ck, cv = caches[0], caches[1]
        p = ck.shape[2]
        in_specs += [pl.BlockSpec((None, None, p, kw), lambda bi, h, i: (bi, layer, 0, h))] * 2
        args += [ck, cv]
    in_specs += [pl.BlockSpec((nb, n, kw), lambda bi, h, i: (bi, 0, k_a0 // ha + h)),
                 pl.BlockSpec((nb, n, kw), lambda bi, h, i: (bi, 0, v_a0 // ha + h))]
    args += [qkv3, qkv3]
    att_a = pl.pallas_call(
        functools.partial(_attn_a_kernel, has_cache=has_cache, tq=tq_a, kc=kc, heads=ha, split=split_a),
        out_shape=jax.ShapeDtypeStruct((b, n, N_KV_HEADS_A * GQA_GROUP * HEAD_DIM_A), BF16),
        grid=(b // nb, N_KV_HEADS_A // ha, n // tq_a),
        in_specs=in_specs,
        out_specs=pl.BlockSpec((nb, tq_a, qw), lambda bi, h, i: (bi, i, h)),
        compiler_params=_cparams(("parallel", "parallel", "arbitrary")),
        name="attn_a_lat" if has_cache else "attn_a_ctx",
    )(*args)

    bw = hb * LANES
    in_specs = [pl.BlockSpec((4, DK_B), lambda bi, h, i: (0, 0)),
                pl.BlockSpec((1, DV_B), lambda bi, h, i: (0, 0)),
                pl.BlockSpec((nb, tq_b, bw), lambda bi, h, i: (bi, i, q_b0 // hb + h))]
    args = [lam_params, g_sub, qkv3]
    if has_cache:
        ck, cv = caches[2], caches[3]
        p = ck.shape[2]
        in_specs += [pl.BlockSpec((None, None, p, bw), lambda bi, h, i: (bi, layer, 0, h))] * 2
        args += [ck, cv]
    in_specs += [pl.BlockSpec((nb, n, bw), lambda bi, h, i: (bi, 0, k_b0 // hb + h)),
                 pl.BlockSpec((nb, n, bw), lambda bi, h, i: (bi, 0, v_b0 // hb + h))]
    args += [qkv3, qkv3]
    att_b = pl.pallas_call(
        functools.partial(_attn_b_kernel, has_cache=has_cache, tq=tq_b, kc=kc, heads=hb, split=split_b, lam_init=lam_init),
        out_shape=jax.ShapeDtypeStruct((b, n, N_HEADS_B * DV_B), BF16),
        grid=(b // nb, N_HEADS_B // hb, n // tq_b),
        in_specs=in_specs,
        out_specs=pl.BlockSpec((nb, tq_b, bw), lambda bi, h, i: (bi, i, h)),
        compiler_params=_cparams(("parallel", "parallel", "arbitrary")),
        name="attn_b_lat" if has_cache else "attn_b_ctx",
    )(*args)
    return att_a, att_b


def _dft_cols_kernel(x_ref, mat_ref, o_ref, *, width, scale):
    p, g, w2 = x_ref.shape
    x = x_ref[...].reshape(p * g, w2)
    xs = jnp.concatenate([x[:, :width], x[:, width:]], axis=0)
    u = jnp.dot(mat_ref[...], xs, preferred_element_type=F32)
    o_ref[...] = (u * scale).astype(BF16).reshape(o_ref.shape)


def _dft_rows_kernel(x_ref, mat_ref, tc_ref, ts_ref, o_ref, *, width):
    tiles, rows, w2 = x_ref.shape
    pair = BF16_SUBLANES
    reps = width // LANES
    for p in range(rows // pair):
        x = x_ref[:, p * pair:(p + 1) * pair, :].reshape(tiles * pair, w2)
        xs = jnp.concatenate([x[:, :width], x[:, width:]], axis=0)
        u = jnp.dot(mat_ref[...], xs, preferred_element_type=F32)
        half = u.shape[0] // 2
        ur, ui = u[:half], u[half:]
        tc = jnp.concatenate([tc_ref[2 * p:2 * p + 2].reshape(half, LANES)] * reps, axis=1)
        ts = jnp.concatenate([ts_ref[2 * p:2 * p + 2].reshape(half, LANES)] * reps, axis=1)
        o_ref[2 * p:2 * p + 2, :, :width] = (ur * tc - ui * ts).astype(BF16).reshape(2, half // 2, width)
        o_ref[2 * p:2 * p + 2, :, width:] = (ur * ts + ui * tc).astype(BF16).reshape(2, half // 2, width)


def _dft_seq_kernel(x_ref, mat_ref, o_ref, *, width, scale):
    nb = x_ref.shape[0]
    xs = jnp.concatenate(
        [jnp.concatenate([x_ref[bb, :, :width], x_ref[bb, :, width:]], axis=0) for bb in range(nb)], axis=1)
    u = jnp.dot(mat_ref[...], xs, preferred_element_type=F32)
    for bb in range(nb):
        o_ref[bb] = (u[:, bb * width:(bb + 1) * width] * scale).astype(BF16)


def _cos_sin(n_rows, n_cols, period):
    a = jnp.arange(n_rows, dtype=jnp.int32)[:, None]
    b = jnp.arange(n_cols, dtype=jnp.int32)[None, :]
    ang = ((a * b) % period).astype(F32) * (2.0 * math.pi / period)
    return jnp.cos(ang), jnp.sin(ang)


def _fourier_call(y3, n1, n2, tile_n2=None):
    b, n, w2 = y3.shape
    width = w2 // 2
    scale = 1.0 / math.sqrt(n * FOURIER_GROUP_DIM)
    c1, s1 = _cos_sin(n1, n1, n1)
    w_real = jnp.stack([c1, -s1], axis=1)
    if n2 == 1:
        mat = w_real.reshape(n1, 2 * n1).astype(BF16)
        nb = math.gcd(b, CTX_SEQS)
        return pl.pallas_call(
            functools.partial(_dft_seq_kernel, width=width, scale=scale),
            out_shape=jax.ShapeDtypeStruct((b, n, width), BF16),
            grid=(b // nb,),
            in_specs=[pl.BlockSpec((nb, n, w2), lambda bi: (bi, 0, 0)),
                      pl.BlockSpec((n1, 2 * n1), lambda bi: (0, 0))],
            out_specs=pl.BlockSpec((nb, n, width), lambda bi: (bi, 0, 0)),
            compiler_params=_cparams(("parallel",)),
            name="dft_ctx",
        )(y3, mat)

    g = BF16_SUBLANES
    c2, s2 = _cos_sin(n2, n2, n2)
    w_cplx = jnp.stack([jnp.stack([c2, -s2], axis=1), jnp.stack([s2, c2], axis=1)], axis=0)

    def kron_cols(base, row_j):
        rows, cols = base.shape
        col = lax.broadcasted_iota(jnp.int32, (cols, cols * g), 1)
        expand = (col // g == lax.broadcasted_iota(jnp.int32, (cols, cols * g), 0)).astype(BF16)
        wide = jnp.dot(base.astype(BF16), expand, preferred_element_type=F32)
        keep = row_j[:, None] == (lax.broadcasted_iota(jnp.int32, (rows, cols * g), 1) % g)
        return jnp.where(keep, wide, 0.0).astype(BF16)

    base2 = jnp.broadcast_to(w_real.reshape(n1, 1, 2 * n1), (n1, g, 2 * n1)).reshape(n1 * g, 2 * n1)
    mat2 = kron_cols(base2, jnp.arange(n1 * g, dtype=jnp.int32) % g)
    tc, ts = _cos_sin(n1, n2, n)
    tc = jnp.broadcast_to(tc[:, :, None], (n1, n2, LANES))
    ts = jnp.broadcast_to(ts[:, :, None], (n1, n2, LANES))
    tiles = n2 // tile_n2
    tile_len = n1 * tile_n2
    mat1 = jnp.einsum('akbtl,ji->ajkbtil', w_cplx.reshape(2, n2, 2, tiles, tile_n2), jnp.eye(2, dtype=F32))
    mat1 = mat1.reshape(4 * n2, 4 * n2).astype(BF16)
    gp = 8
    t = pl.pallas_call(
        functools.partial(_dft_rows_kernel, width=width),
        out_shape=jax.ShapeDtypeStruct((b, n1, n2, w2), BF16),
        grid=(b, tile_len // (BF16_SUBLANES * gp)),
        in_specs=[pl.BlockSpec((None, tiles, BF16_SUBLANES * gp, w2), lambda bi, j: (bi, 0, j, 0)),
                  pl.BlockSpec(mat1.shape, lambda bi, j: (0, 0)),
                  pl.BlockSpec((2 * gp, n2, LANES), lambda bi, j: (j, 0, 0)),
                  pl.BlockSpec((2 * gp, n2, LANES), lambda bi, j: (j, 0, 0))],
        out_specs=pl.BlockSpec((None, 2 * gp, n2, w2), lambda bi, j: (bi, j, 0, 0)),
        compiler_params=_cparams(("parallel", "parallel")),
        name="dft_stage1",
    )(y3.reshape(b, tiles, tile_len, w2), mat1, tc, ts)
    out = pl.pallas_call(
        functools.partial(_dft_cols_kernel, width=width, scale=scale),
        out_shape=jax.ShapeDtypeStruct((b, n1, n2, width), BF16),
        grid=(b, n2 // g),
        in_specs=[pl.BlockSpec((None, n1, g, w2), lambda bi, j: (bi, 0, j, 0)),
                  pl.BlockSpec(mat2.shape, lambda bi, j: (0, 0), pipeline_mode=pl.Buffered(1))],
        out_specs=pl.BlockSpec((None, n1, g, width), lambda bi, j: (bi, 0, j, 0)),
        compiler_params=_cparams(("parallel", "parallel")),
        name="dft_stage2",
    )(t, mat2)
    return out.reshape(b, n, width)


def _out_kernel(a_ref, b_ref, f_ref, w_ref, x_ref, mod_ref, g_ref, x1_ref, h2_ref):
    ca = a_ref.shape[1]
    cb = b_ref.shape[1]
    m = mod_ref[...]
    half = a_ref.shape[0] // 2
    for r0 in (0, half):
        rows = slice(r0, r0 + half)
        acc = jnp.dot(a_ref[rows, :], w_ref[0:ca, :], preferred_element_type=F32)
        acc += jnp.dot(b_ref[rows, :], w_ref[ca:ca + cb, :], preferred_element_type=F32)
        acc += jnp.dot(f_ref[rows, :], w_ref[ca + cb:, :], preferred_element_type=F32)
        x1 = x_ref[rows, :] + m[2:3] * acc
        x1_ref[rows, :] = x1
        h = _rms(x1, g_ref[...])
        h2_ref[rows, :] = (h * (1.0 + m[4:5]) + m[3:4]).astype(BF16)


def _out_call(att_a, att_b, four, w_out_b, x2d, mod_l, cond_row_fn, norm_g, tm):
    t, d = x2d.shape
    ca, cb, cf = att_a.shape[1], att_b.shape[1], four.shape[1]
    w_out_b, w_layer = w_out_b
    return pl.pallas_call(
        _out_kernel,
        out_shape=[jax.ShapeDtypeStruct((t, d), F32), jax.ShapeDtypeStruct((t, d), BF16)],
        grid=(t // tm,),
        in_specs=[
            pl.BlockSpec((tm, ca), lambda i: (i, 0)),
            pl.BlockSpec((tm, cb), lambda i: (i, 0)),
            pl.BlockSpec((tm, cf), lambda i: (i, 0)),
            pl.BlockSpec((None, ca + cb + cf, d), lambda i: (w_layer, 0, 0), pipeline_mode=pl.Buffered(1)),
            pl.BlockSpec((tm, d), lambda i: (i, 0)),
            pl.BlockSpec((None, N_MOD, d), lambda i: (cond_row_fn(i), 0, 0)),
            pl.BlockSpec((1, d), lambda i: (0, 0)),
        ],
        out_specs=[pl.BlockSpec((tm, d), lambda i: (i, 0)), pl.BlockSpec((tm, d), lambda i: (i, 0))],
        compiler_params=_cparams(("parallel",)),
        name="out_proj",
    )(att_a, att_b, four, w_out_b, x2d, mod_l, norm_g)


HALO = BF16_SUBLANES


def _ffn_kernel(cw_ref, cb_ref, mod_ref, gf_ref, h_hbm, x1_hbm, wg_hbm, wu_hbm, wd_hbm,
                o_ref, hext_ref, x1_buf, wg_buf, wu_buf, wd_buf, sem, *, tm, tf, nj, layer, seq_len, final):
    i = pl.program_id(0)
    n_i = pl.num_programs(0)
    row0 = pl.multiple_of(i * tm, tm)

    def weight_copies(j, slot):
        cols = pl.ds(pl.multiple_of(j * tf, tf), tf)
        return (pltpu.make_async_copy(wg_hbm.at[layer, :, cols], wg_buf.at[slot], sem.at[0, slot]),
                pltpu.make_async_copy(wu_hbm.at[layer, :, cols], wu_buf.at[slot], sem.at[1, slot]),
                pltpu.make_async_copy(wd_hbm.at[layer, cols, :], wd_buf.at[slot], sem.at[2, slot]))

    def tile_copies(ti, slot):
        start = pl.multiple_of(ti * tm, tm)
        before = pl.multiple_of(jnp.maximum(start - HALO, 0), HALO)
        after = pl.multiple_of(jnp.minimum(start + tm, n_i * tm - HALO), HALO)
        dst = hext_ref.at[slot]
        return (pltpu.make_async_copy(h_hbm.at[pl.ds(before, HALO), :], dst.at[pl.ds(0, HALO), :], sem.at[3, slot]),
                pltpu.make_async_copy(h_hbm.at[pl.ds(start, tm), :], dst.at[pl.ds(HALO, tm), :], sem.at[4, slot]),
                pltpu.make_async_copy(h_hbm.at[pl.ds(after, HALO), :], dst.at[pl.ds(HALO + tm, HALO), :],
                                      sem.at[5, slot]))

    x1_copy = pltpu.make_async_copy(x1_hbm.at[pl.ds(row0, tm), :], x1_buf, sem.at[6, 0])
    hslot = i % 2

    @pl.when(i == 0)
    def _():
        for cp in tile_copies(0, 0) + weight_copies(0, 0):
            cp.start()

    x1_copy.start()

    @pl.when(i + 1 < n_i)
    def _():
        for cp in tile_copies(i + 1, 1 - hslot):
            cp.start()

    ext = tm + 2 * HALO
    pos = (i * tm + lax.broadcasted_iota(jnp.int32, (tm, 1), 0)) % seq_len
    for cp in tile_copies(i, hslot):
        cp.wait()
    hext = hext_ref.at[hslot]

    def chunk(j, first):
        step = i * nj + j
        slot = step % 2
        for cp in weight_copies(j, slot):
            cp.wait()

        @pl.when(step + 1 < n_i * nj)
        def _():
            for cp in weight_copies((j + 1) % nj, 1 - slot):
                cp.start()

        g = jnp.dot(hext[...], wg_buf[slot], preferred_element_type=F32)
        u = jnp.dot(hext[HALO:HALO + tm, :], wu_buf[slot], preferred_element_type=F32)
        g_prev = jnp.where(pos == 0, 0.0, pltpu.roll(g, 1, 0)[HALO:HALO + tm])
        g_next = jnp.where(pos == seq_len - 1, 0.0, pltpu.roll(g, ext - 1, 0)[HALO:HALO + tm])
        cw = cw_ref[j]
        gc = g_prev * cw[0:1] + g[HALO:HALO + tm] * cw[1:2] + g_next * cw[2:3] + cb_ref[j]
        act = (gc * jax.nn.sigmoid(gc)) * u
        y = jnp.dot(act.astype(BF16), wd_buf[slot], preferred_element_type=F32)
        if first:
            o_ref[...] = y
        else:
            o_ref[...] += y

    chunk(0, True)
    lax.fori_loop(1, nj, lambda j, carry: (chunk(j, False), carry)[1], 0)

    x1_copy.wait()
    m = mod_ref[...]
    x2 = x1_buf[...] + m[5:6] * o_ref[...]
    if final:
        x2 = _rms(x2, gf_ref[...])
    o_ref[...] = x2


def _ffn_call(h2, x1, weights, conv_w, conv_b, mod_l, cond_row_fn, final_g, seq_len, final, tm, tf):
    t, d = x1.shape
    w_gate_b, w_up_b, w_down_b, w_layer = weights
    f = w_gate_b.shape[2]
    nj = f // tf
    cw3 = conv_w.reshape(3, nj, tf).transpose(1, 0, 2)
    cb3 = conv_b.reshape(nj, 1, tf)
    hbm = pl.BlockSpec(memory_space=pl.ANY)
    return pl.pallas_call(
        functools.partial(_ffn_kernel, tm=tm, tf=tf, nj=nj, layer=w_layer, seq_len=seq_len, final=final),
        out_shape=jax.ShapeDtypeStruct((t, d), F32),
        grid=(t // tm,),
        in_specs=[
            pl.BlockSpec((nj, 3, tf), lambda i: (0, 0, 0)),
            pl.BlockSpec((nj, 1, tf), lambda i: (0, 0, 0)),
            pl.BlockSpec((None, N_MOD, d), lambda i: (cond_row_fn(i), 0, 0)),
            pl.BlockSpec((1, d), lambda i: (0, 0)),
            hbm, hbm, hbm, hbm, hbm,
        ],
        out_specs=pl.BlockSpec((tm, d), lambda i: (i, 0)),
        scratch_shapes=[pltpu.VMEM((2, tm + 2 * HALO, d), BF16), pltpu.VMEM((tm, d), F32),
                        pltpu.VMEM((2, d, tf), BF16), pltpu.VMEM((2, d, tf), BF16), pltpu.VMEM((2, tf, d), BF16),
                        pltpu.SemaphoreType.DMA((7, 2))],
        compiler_params=_cparams(("arbitrary",)),
        name="conv_ffn",
    )(cw3, cb3, mod_l, final_g, h2, x1, w_gate_b, w_up_b, w_down_b)


def _rope_tables(n, head_dim):
    rows = n // GRID_W
    t_row = jnp.repeat(jnp.arange(rows, dtype=F32), GRID_W)
    t_col = jnp.tile(jnp.arange(GRID_W, dtype=F32), rows)
    axis_dim = head_dim // 2
    inv = jnp.power(ROPE_BASE, -jnp.arange(0, axis_dim, 2, dtype=F32) / axis_dim)
    ar = t_row[:, None] * inv[None, :]
    ac = t_col[:, None] * inv[None, :]
    ang = jnp.concatenate([ar, ar, ac, ac], axis=-1)
    reps = LANES // head_dim
    quarter = head_dim // 4
    sign = jnp.where((jnp.arange(head_dim) % (2 * quarter)) < quarter, -1.0, 1.0).astype(F32)
    cos = jnp.tile(jnp.cos(ang), (1, reps))
    sin_signed = jnp.tile(jnp.sin(ang) * sign[None, :], (1, reps))
    return cos, sin_signed


def kernel(x_prompt, x_sample, cache_attn_k, cache_attn_v, cache_diff_k, cache_diff_v, c, c_ctx, norm1_g, norm2_g, w_ada, b_ada, w_in, attn_q_norm_g, attn_k_norm_g, diff_lambda_q1, diff_lambda_k1, diff_lambda_q2, diff_lambda_k2, diff_subnorm_g, w_out, ffn_w_gate, ffn_w_up, ffn_conv_w, ffn_conv_b, ffn_w_down, final_norm_g):
    depth = w_in.shape[0]
    bc, lc, d = x_prompt.shape
    bl, ll, _ = x_sample.shape
    past = cache_attn_k.shape[2]

    w_in_b = w_in.astype(BF16)
    w_out_b = w_out.astype(BF16)
    w_gate_b = ffn_w_gate.astype(BF16)
    w_up_b = ffn_w_up.astype(BF16)
    w_down_b = ffn_w_down.astype(BF16)

    n_rows = 8 * ((1 + bl + 7) // 8)
    cvec = jnp.concatenate([c_ctx[None, :], c, jnp.zeros((n_rows - 1 - bl, d), F32)], axis=0)
    mod = _ada_call(cvec, w_ada, b_ada).reshape(depth, n_rows, N_MOD, d)

    dft_c = jnp.concatenate(_cos_sin(FOURIER_GROUP_DIM, FOURIER_GROUP_DIM, FOURIER_GROUP_DIM), axis=1).astype(BF16)
    rope_tabs = _rope_tables(ll, HEAD_DIM_A) + _rope_tables(ll, DK_B)
    caches = (cache_attn_k.reshape(bl, depth, past, N_KV_HEADS_A * HEAD_DIM_A),
              cache_attn_v.reshape(bl, depth, past, N_KV_HEADS_A * HEAD_DIM_A),
              cache_diff_k.reshape(bl, depth, past, N_HEADS_B * 2 * DK_B),
              cache_diff_v.reshape(bl, depth, past, N_HEADS_B * DV_B))
    lam_all = jnp.stack([diff_lambda_q1, diff_lambda_k1, diff_lambda_q2, diff_lambda_k2], axis=1)

    def run_pass(x3, is_ctx):
        b, n, _ = x3.shape
        t = b * n
        tm = ROW_TILE
        tm_ffn = FFN_ROW_TILE
        x = x3.reshape(t, d)

        def cond_rows(tile):
            return (lambda i: 0) if is_ctx else (lambda i: 1 + (i * tile) // n)

        cond_row_fn = cond_rows(tm)
        new_kv = None
        for l in range(depth):
            lam_init = 0.8 - 0.6 * math.exp(-0.3 * l)
            res = _in_call(x, mod[l], cond_row_fn, norm1_g[l][None, :], (w_in_b, l),
                           attn_q_norm_g[l][None, :], attn_k_norm_g[l][None, :], dft_c,
                           None if is_ctx else rope_tabs, n, (depth, l, new_kv) if is_ctx else None, tm)
            qkv, y12 = res[0], res[1]
            if is_ctx:
                new_kv = res[2:]
            qkv3 = qkv.reshape(b, n, qkv.shape[1])
            att_a, att_b = _attn_calls(qkv3, None if is_ctx else caches, l, lam_all[l],
                                       diff_subnorm_g[l][None, :], lam_init,
                                       tq_a=min(ATTN_TQ_A, n), tq_b=min(ATTN_TQ_B, n), kc=ATTN_KC,
                                       ha=N_KV_HEADS_A if is_ctx else 1, hb=N_HEADS_B if is_ctx else 1,
                                       split_a=1 if is_ctx else ATTN_STREAMS, split_b=1 if is_ctx else ATTN_STREAMS,
                                       nb=math.gcd(b, CTX_SEQS) if is_ctx else 1)
            if is_ctx:
                four = _fourier_call(y12.reshape(b, n, y12.shape[1]), n, 1)
            else:
                four = _fourier_call(y12.reshape(b, n, y12.shape[1]), GRID_W, n // GRID_W, tm // GRID_W)
            x1, h2 = _out_call(att_a.reshape(t, -1), att_b.reshape(t, -1), four.reshape(t, -1), (w_out_b, l),
                               x, mod[l], cond_row_fn, norm2_g[l][None, :], tm)
            x = _ffn_call(h2, x1, (w_gate_b, w_up_b, w_down_b, l), ffn_conv_w[l], ffn_conv_b[l][None, :],
                          mod[l], cond_rows(tm_ffn), final_norm_g[None, :], n, l == depth - 1, tm_ffn, FFN_TF)
        return x.reshape(b, n, d), new_kv

    y_prompt, kvs = run_pass(x_prompt, True)
    new_attn_k = kvs[0].reshape(bc, depth, lc, N_KV_HEADS_A, HEAD_DIM_A)
    new_attn_v = kvs[1].reshape(bc, depth, lc, N_KV_HEADS_A, HEAD_DIM_A)
    new_diff_k = kvs[2].reshape(bc, depth, lc, N_HEADS_B, 2 * DK_B)
    new_diff_v = kvs[3].reshape(bc, depth, lc, N_HEADS_B, DV_B)

    y_sample, _ = run_pass(x_sample, False)
    return (y_prompt, y_sample, new_attn_k, new_attn_v, new_diff_k, new_diff_v)
```

```python
import functools
import math

import jax
import jax.numpy as jnp
from jax import lax
from jax.experimental import pallas as pl
from jax.experimental.pallas import tpu as pltpu

F32 = jnp.float32
BF16 = jnp.bfloat16

GRID_W = 64
ROPE_BASE = 10000.0
NORM_EPS = 1e-6
HEAD_DIM_A = 128
N_KV_HEADS_A = 2
GQA_GROUP = 4
N_HEADS_B = 4
DK_B = 64
DV_B = 128
N_FOURIER_GROUPS = 4
FOURIER_GROUP_DIM = 128
N_MOD = 6

LOG2E = math.log2(math.e)
LANES = 128
BF16_SUBLANES = 16
VMEM_LIMIT = 58 * 1024 * 1024

ROW_TILE = 512
FFN_ROW_TILE = 1024
FFN_TF = 512
ADA_TN = 1024
ATTN_TQ_A = 512
ATTN_TQ_B = 1024
ATTN_KC = 2048
ATTN_STREAMS = 4
CTX_SEQS = 4


def _cparams(sem):
    return pltpu.CompilerParams(dimension_semantics=sem, vmem_limit_bytes=VMEM_LIMIT)


def _rms(x, g):
    return x * lax.rsqrt(jnp.mean(x * x, axis=-1, keepdims=True) + NORM_EPS) * g


def _ada_kernel(c_ref, w_ref, b_ref, o_ref):
    c = c_ref[...]
    s = (c * jax.nn.sigmoid(c)).astype(BF16)
    w = w_ref[...].astype(BF16)
    o_ref[...] = jnp.dot(s, w, preferred_element_type=F32) + b_ref[...]


def _ada_call(cvec, w_ada, b_ada):
    depth, d, n = w_ada.shape
    rows = cvec.shape[0]
    tn = ADA_TN
    return pl.pallas_call(
        _ada_kernel,
        out_shape=jax.ShapeDtypeStruct((depth, rows, n), F32),
        grid=(depth, n // tn),
        in_specs=[
            pl.BlockSpec((rows, d), lambda l, j: (0, 0)),
            pl.BlockSpec((None, d, tn), lambda l, j: (l, 0, j)),
            pl.BlockSpec((None, 1, tn), lambda l, j: (l, 0, j)),
        ],
        out_specs=pl.BlockSpec((None, rows, tn), lambda l, j: (l, 0, j)),
        compiler_params=_cparams(("parallel", "parallel")),
        name="ada_mod",
    )(cvec, w_ada, b_ada.reshape(depth, 1, n))


def _rope(xs, cos, sin_signed, shift):
    w = xs.shape[-1]
    lane = lax.broadcasted_iota(jnp.int32, xs.shape, 1)
    first = (lane % (2 * shift)) < shift
    rot = jnp.where(first, pltpu.roll(xs, w - shift, 1), pltpu.roll(xs, shift, 1))
    return xs * cos + rot * sin_signed


def _in_kernel(*refs, rope, ctx_out, n_alias, cols):
    it = iter(refs)
    x_ref, mod_ref, g_ref, w_ref, gq_ref, gk_ref, dft_ref = (next(it) for _ in range(7))
    if rope:
        cos_a, sin_a, cos_b, sin_b = (next(it)[...] for _ in range(4))
        perm_ref = next(it)
    for _ in range(n_alias):
        next(it)
    qkv_ref, y_ref = next(it), next(it)
    if ctx_out:
        ka_ref, va_ref, kb_ref, vb_ref = (next(it) for _ in range(4))
    c_qa, c_ka, c_va, c_qb, c_kb, c_vb, c_f, c_end = cols

    def put_cache(ref, off, v):
        head = off // LANES
        seqs, rows = ref.shape[0], ref.shape[-2]
        heads = rows * seqs // v.shape[0]
        n = rows // heads
        for s in range(seqs):
            vs = v[s * n:(s + 1) * n, :]
            if len(ref.shape) == 4:
                for dd in range(ref.shape[1]):
                    ref[s, dd, pl.ds(head, n, stride=heads), :] = vs
            else:
                ref[s, pl.ds(head, n, stride=heads), :] = vs

    m = mod_ref[...]
    h = _rms(x_ref[...], g_ref[...])
    hb = (h * (1.0 + m[1:2]) + m[0:1]).astype(BF16)
    gq = gq_ref[...]
    gk = gk_ref[...]
    scale_a = HEAD_DIM_A ** -0.5 * LOG2E
    scale_b = DK_B ** -0.5 * LOG2E
    chunk = 4 * LANES

    for c0 in range(0, c_end, chunk):
        acc = jnp.dot(hb, w_ref[:, c0:c0 + chunk], preferred_element_type=F32)
        if rope and c0 >= c_f:
            acc = jnp.dot(perm_ref[...], acc.astype(BF16), preferred_element_type=F32)
        for s in range(chunk // LANES):
            col = c0 + s * LANES
            v = acc[:, s * LANES:(s + 1) * LANES]
            if col < c_ka:
                v = _rms(v, gq)
                if rope:
                    v = _rope(v, cos_a, sin_a, HEAD_DIM_A // 4)
                qkv_ref[:, col:col + LANES] = (v * scale_a).astype(BF16)
            elif col < c_va:
                v = _rms(v, gk)
                if ctx_out:
                    put_cache(ka_ref, col - c_ka, v)
                if rope:
                    v = _rope(v, cos_a, sin_a, HEAD_DIM_A // 4)
                qkv_ref[:, col:col + LANES] = v.astype(BF16)
            elif col < c_qb:
                if ctx_out:
                    put_cache(va_ref, col - c_va, v)
                qkv_ref[:, col:col + LANES] = v.astype(BF16)
            elif col < c_kb:
                if rope:
                    v = _rope(v, cos_b, sin_b, DK_B // 4)
                qkv_ref[:, col:col + LANES] = (v * scale_b).astype(BF16)
            elif col < c_vb:
                if ctx_out:
                    put_cache(kb_ref, col - c_kb, v)
                if rope:
                    v = _rope(v, cos_b, sin_b, DK_B // 4)
                qkv_ref[:, col:col + LANES] = v.astype(BF16)
            elif col < c_f:
                if ctx_out:
                    put_cache(vb_ref, col - c_vb, v)
                qkv_ref[:, col:col + LANES] = v.astype(BF16)
            else:
                yy = jnp.dot(v.astype(BF16), dft_ref[...], preferred_element_type=F32)
                gcol = col - c_f
                half = c_end - c_f
                y_ref[:, gcol:gcol + LANES] = yy[:, :LANES].astype(BF16)
                y_ref[:, half + gcol:half + gcol + LANES] = yy[:, LANES:].astype(BF16)


def _in_call(x2d, mod_l, cond_row_fn, norm_g, w_in_b, gq, gk, dft_c, rope_tabs, seq_len, new_cache, tm):
    ctx_out = new_cache is not None
    t, d = x2d.shape
    w_in_b, w_layer = w_in_b
    d_in = w_in_b.shape[2]
    c_qa = 0
    c_ka = N_KV_HEADS_A * GQA_GROUP * HEAD_DIM_A
    c_va = c_ka + N_KV_HEADS_A * HEAD_DIM_A
    c_qb = c_va + N_KV_HEADS_A * HEAD_DIM_A
    c_kb = c_qb + N_HEADS_B * 2 * DK_B
    c_vb = c_kb + N_HEADS_B * 2 * DK_B
    c_f = c_vb + N_HEADS_B * DV_B
    c_end = c_f + N_FOURIER_GROUPS * FOURIER_GROUP_DIM
    assert c_end == d_in
    cols = (c_qa, c_ka, c_va, c_qb, c_kb, c_vb, c_f, c_end)
    rope = rope_tabs is not None
    n_f = c_end - c_f

    in_specs = [
        pl.BlockSpec((tm, d), lambda i: (i, 0)),
        pl.BlockSpec((None, N_MOD, d), lambda i: (cond_row_fn(i), 0, 0)),
        pl.BlockSpec((1, d), lambda i: (0, 0)),
        pl.BlockSpec((None, d, d_in), lambda i: (w_layer, 0, 0), pipeline_mode=pl.Buffered(1)),
        pl.BlockSpec((1, HEAD_DIM_A), lambda i: (0, 0)),
        pl.BlockSpec((1, HEAD_DIM_A), lambda i: (0, 0)),
        pl.BlockSpec((FOURIER_GROUP_DIM, 2 * FOURIER_GROUP_DIM), lambda i: (0, 0)),
    ]
    args = [x2d, mod_l, norm_g, w_in_b, gq, gk, dft_c]
    if rope:
        nblk = seq_len // tm
        for tab in rope_tabs:
            in_specs.append(pl.BlockSpec((tm, LANES), lambda i: (i % nblk, 0)))
            args.append(tab)
        r_new = jnp.arange(tm, dtype=jnp.int32)
        r_old = (r_new % (tm // GRID_W)) * GRID_W + r_new // (tm // GRID_W)
        perm = (r_old[:, None] == jnp.arange(tm, dtype=jnp.int32)[None, :]).astype(BF16)
        in_specs.append(pl.BlockSpec((tm, tm), lambda i: (0, 0)))
        args.append(perm)
    out_shape = [jax.ShapeDtypeStruct((t, c_f), BF16), jax.ShapeDtypeStruct((t, 2 * n_f), BF16)]
    out_specs = [pl.BlockSpec((tm, c_f), lambda i: (i, 0)), pl.BlockSpec((tm, 2 * n_f), lambda i: (i, 0))]
    aliases = {}
    n_alias = 0
    if ctx_out:
        depth, layer, prev = new_cache
        nb = t // seq_len
        spt = tm // seq_len
        for width in (c_va - c_ka, c_qb - c_va, c_vb - c_kb, c_f - c_vb):
            rows = seq_len * (width // LANES)
            out_shape.append(jax.ShapeDtypeStruct((nb, depth, rows, LANES), F32))
            if prev is None:
                out_specs.append(pl.BlockSpec((spt, depth, rows, LANES), lambda i: (i, 0, 0, 0)))
            else:
                out_specs.append(pl.BlockSpec((spt, None, rows, LANES), lambda i: (i, layer, 0, 0)))
        if prev is not None:
            n_alias = len(prev)
            for a, arr in enumerate(prev):
                aliases[len(args)] = 2 + a
                in_specs.append(pl.BlockSpec(memory_space=pl.ANY))
                args.append(arr)
    return pl.pallas_call(
        functools.partial(_in_kernel, rope=rope, ctx_out=ctx_out, n_alias=n_alias, cols=cols),
        out_shape=out_shape,
        grid=(t // tm,),
        in_specs=in_specs,
        out_specs=out_specs,
        input_output_aliases=aliases,
        compiler_params=_cparams(("parallel",)),
        name="in_proj_ctx" if ctx_out else "in_proj_lat",
    )(*args)


def _softmax_pv(q, srcs):
    chunks = []
    for k_ref, v_ref, n_keys, kc, col in srcs:
        for c in range(n_keys // kc):
            chunks.append([(k_ref, v_ref, c * kc, kc, col)])
    if len(chunks) > 1 and chunks[0][0][3] < chunks[1][0][3]:
        chunks = [chunks[0] + chunks[1]] + chunks[2:]

    def scores(ch):
        parts = []
        for k_ref, _, off, kc, col in ch:
            k = k_ref[off:off + kc, col:col + LANES].astype(BF16)
            parts.append(lax.dot_general(q, k, (((1,), (1,)), ((), ())), preferred_element_type=F32))
        return parts[0] if len(parts) == 1 else jnp.concatenate(parts, axis=1)

    def lane_fold(x, op):
        out = x[:, 0:LANES]
        for t in range(1, x.shape[1] // LANES):
            out = op(out, x[:, t * LANES:(t + 1) * LANES])
        return out

    mxu_sum = len(chunks) > 1
    m = acc = l_part = None
    s_next = scores(chunks[0])
    for i, ch in enumerate(chunks):
        s = s_next
        if i + 1 < len(chunks):
            s_next = scores(chunks[i + 1])
        row_max = jnp.max(lane_fold(s, jnp.maximum), axis=-1, keepdims=True)
        m_new = row_max if m is None else jnp.maximum(m, row_max)
        p = jnp.exp2(s - m_new)
        if not mxu_sum:
            l_part = lane_fold(p, jnp.add)
        pb = p.astype(BF16)
        pv = None
        lo = 0
        for _, v_ref, off, kc, col in ch:
            v = v_ref[off:off + kc, col:col + LANES].astype(BF16)
            if mxu_sum:
                ones_col = (lax.broadcasted_iota(jnp.int32, (kc, LANES), 1) == 0).astype(BF16)
                v = jnp.concatenate([v, ones_col], axis=1)
            part = jnp.dot(pb[:, lo:lo + kc], v, preferred_element_type=F32)
            pv = part if pv is None else pv + part
            lo += kc
        acc = pv if m is None else jnp.exp2(m - m_new) * acc + pv
        m = m_new
    if mxu_sum:
        return acc[:, :LANES] / acc[:, LANES:LANES + 1]
    return acc / jnp.sum(l_part, axis=-1, keepdims=True)


def _srcs(cache_refs, k_ref, v_ref, kc, head):
    srcs = []
    col = head * LANES
    if cache_refs is not None:
        ck, cv = cache_refs
        srcs.append((ck, cv, ck.shape[0], ck.shape[0], col))
    n = k_ref.shape[0]
    srcs.append((k_ref, v_ref, n, min(kc, n), col))
    return srcs


def _attn_a_kernel(*refs, has_cache, tq, kc, heads, split):
    if has_cache:
        q_ref, ck_ref, cv_ref, k_ref, v_ref, o_ref = refs
        cache = (ck_ref, cv_ref)
    else:
        q_ref, k_ref, v_ref, o_ref = refs
        cache = None
    per = GQA_GROUP // split
    for bb in range(q_ref.shape[0]):
        qb, kb, vb, ob = q_ref.at[bb], k_ref.at[bb], v_ref.at[bb], o_ref.at[bb]
        for kh in range(heads):
            for part in range(split):
                cols = [(kh * GQA_GROUP + part * per + h) * LANES for h in range(per)]
                qs = jnp.concatenate([qb[:, c:c + LANES] for c in cols], axis=0)
                o = _softmax_pv(qs, _srcs(cache, kb, vb, kc, kh))
                for h, c in enumerate(cols):
                    ob[:, c:c + LANES] = o[h * tq:(h + 1) * tq].astype(BF16)


def _attn_b_kernel(*refs, has_cache, tq, kc, heads, split, lam_init):
    if has_cache:
        lam_ref, gs_ref, q_ref, ck_ref, cv_ref, k_ref, v_ref, o_ref = refs
        cache = (ck_ref, cv_ref)
    else:
        lam_ref, gs_ref, q_ref, k_ref, v_ref, o_ref = refs
        cache = None
    lp = lam_ref[...]
    lam = (jnp.exp(jnp.sum(lp[0:1] * lp[1:2], axis=-1, keepdims=True))
           - jnp.exp(jnp.sum(lp[2:3] * lp[3:4], axis=-1, keepdims=True)) + lam_init)
    rows = tq // split
    for bb in range(q_ref.shape[0]):
        qb, kb, vb, ob = q_ref.at[bb], k_ref.at[bb], v_ref.at[bb], o_ref.at[bb]
        for hd in range(heads):
            for part in range(split):
                q = qb[part * rows:(part + 1) * rows, hd * LANES:(hd + 1) * LANES]
                lane = lax.broadcasted_iota(jnp.int32, q.shape, 1)
                zero = jnp.zeros_like(q)
                qz = jnp.concatenate([jnp.where(lane < DK_B, q, zero), jnp.where(lane >= DK_B, q, zero)], axis=0)
                o = _softmax_pv(qz, _srcs(cache, kb, vb, kc, hd))
                dlt = o[:rows] - lam * o[rows:]
                ob[part * rows:(part + 1) * rows, hd * LANES:(hd + 1) * LANES] = (
                    _rms(dlt, gs_ref[...]) * (1.0 - lam_init)).astype(BF16)


def _attn_calls(qkv3, caches, layer, lam_params, g_sub, lam_init, tq_a, tq_b, kc, ha, hb, split_a, split_b, nb):
    b, n, _ = qkv3.shape
    has_cache = caches is not None
    assert b % nb == 0 and (nb == 1 or not has_cache)
    qa_blk =GQA_GROUP * HEAD_DIM_A // LANES
    k_a0 = N_KV_HEADS_A * qa_blk
    v_a0 = k_a0 + N_KV_HEADS_A
    q_b0 = v_a0 + N_KV_HEADS_A
    k_b0 = q_b0 + N_HEADS_B
    v_b0 = k_b0 + N_HEADS_B
    assert all(x % ha == 0 for x in (N_KV_HEADS_A, k_a0, v_a0)) and all(x % hb == 0 for x in (N_HEADS_B, q_b0, k_b0, v_b0))

    qw, kw = ha * GQA_GROUP * LANES, ha * LANES
    in_specs = [pl.BlockSpec((nb, tq_a, qw), lambda bi, h, i: (bi, i, h))]
    args = [qkv3]
    if has_cache:
        ck, cv = caches[0], caches[1]
        p = ck.shape[2]
        in_specs += [pl.BlockSpec((None, None, p, kw), lambda bi, h, i: (bi, layer, 0, h))] * 2
        args += [ck, cv]
    in_specs += [pl.BlockSpec((nb, n, kw), lambda bi, h, i: (bi, 0, k_a0 // ha + h)),
                 pl.BlockSpec((nb, n, kw), lambda bi, h, i: (bi, 0, v_a0 // ha + h))]
    args += [qkv3, qkv3]
    att_a = pl.pallas_call(
        functools.partial(_attn_a_kernel, has_cache=has_cache, tq=tq_a, kc=kc, heads=ha, split=split_a),
        out_shape=jax.ShapeDtypeStruct((b, n, N_KV_HEADS_A * GQA_GROUP * HEAD_DIM_A), BF16),
        grid=(b // nb, N_KV_HEADS_A // ha, n // tq_a),
        in_specs=in_specs,
        out_specs=pl.BlockSpec((nb, tq_a, qw), lambda bi, h, i: (bi, i, h)),
        compiler_params=_cparams(("parallel", "parallel", "arbitrary")),
        name="attn_a_lat" if has_cache else "attn_a_ctx",
    )(*args)

    bw = hb * LANES
    in_specs = [pl.BlockSpec((4, DK_B), lambda bi, h, i: (0, 0)),
                pl.BlockSpec((1, DV_B), lambda bi, h, i: (0, 0)),
                pl.BlockSpec((nb, tq_b, bw), lambda bi, h, i: (bi, i, q_b0 // hb + h))]
    args = [lam_params, g_sub, qkv3]
    if has_cache:
        ck, cv = caches[2], caches[3]
        p = ck.shape[2]
        in_specs += [pl.BlockSpec((None, None, p, bw), lambda bi, h, i: (bi, layer, 0, h))] * 2
        args += [ck, cv]
    in_specs += [pl.BlockSpec((nb, n, bw), lambda bi, h, i: (bi, 0, k_b0 // hb + h)),
                 pl.BlockSpec((nb, n, bw), lambda bi, h, i: (bi, 0, v_b0 // hb + h))]
    args += [qkv3, qkv3]
    att_b = pl.pallas_call(
        functools.partial(_attn_b_kernel, has_cache=has_cache, tq=tq_b, kc=kc, heads=hb, split=split_b, lam_init=lam_init),
        out_shape=jax.ShapeDtypeStruct((b, n, N_HEADS_B * DV_B), BF16),
        grid=(b // nb, N_HEADS_B // hb, n // tq_b),
        in_specs=in_specs,
        out_specs=pl.BlockSpec((nb, tq_b, bw), lambda bi, h, i: (bi, i, h)),
        compiler_params=_cparams(("parallel", "parallel", "arbitrary")),
        name="attn_b_lat" if has_cache else "attn_b_ctx",
    )(*args)
    return att_a, att_b


def _dft_cols_kernel(x_ref, mat_ref, o_ref, *, width, scale):
    p, g, w2 = x_ref.shape
    x = x_ref[...].reshape(p * g, w2)
    xs = jnp.concatenate([x[:, :width], x[:, width:]], axis=0)
    u = jnp.dot(mat_ref[...], xs, preferred_element_type=F32)
    o_ref[...] = (u * scale).astype(BF16).reshape(o_ref.shape)


def _dft_rows_kernel(x_ref, mat_ref, tc_ref, ts_ref, o_ref, *, width):
    tiles, rows, w2 = x_ref.shape
    pair = BF16_SUBLANES
    reps = width // LANES
    for p in range(rows // pair):
        x = x_ref[:, p * pair:(p + 1) * pair, :].reshape(tiles * pair, w2)
        xs = jnp.concatenate([x[:, :width], x[:, width:]], axis=0)
        u = jnp.dot(mat_ref[...], xs, preferred_element_type=F32)
        half = u.shape[0] // 2
        ur, ui = u[:half], u[half:]
        tc = jnp.concatenate([tc_ref[2 * p:2 * p + 2].reshape(half, LANES)] * reps, axis=1)
        ts = jnp.concatenate([ts_ref[2 * p:2 * p + 2].reshape(half, LANES)] * reps, axis=1)
        o_ref[2 * p:2 * p + 2, :, :width] = (ur * tc - ui * ts).astype(BF16).reshape(2, half // 2, width)
        o_ref[2 * p:2 * p + 2, :, width:] = (ur * ts + ui * tc).astype(BF16).reshape(2, half // 2, width)


def _dft_seq_kernel(x_ref, mat_ref, o_ref, *, width, scale):
    nb = x_ref.shape[0]
    xs = jnp.concatenate(
        [jnp.concatenate([x_ref[bb, :, :width], x_ref[bb, :, width:]], axis=0) for bb in range(nb)], axis=1)
    u = jnp.dot(mat_ref[...], xs, preferred_element_type=F32)
    for bb in range(nb):
        o_ref[bb] = (u[:, bb * width:(bb + 1) * width] * scale).astype(BF16)


def _cos_sin(n_rows, n_cols, period):
    a = jnp.arange(n_rows, dtype=jnp.int32)[:, None]
    b = jnp.arange(n_cols, dtype=jnp.int32)[None, :]
    ang = ((a * b) % period).astype(F32) * (2.0 * math.pi / period)
    return jnp.cos(ang), jnp.sin(ang)


def _fourier_call(y3, n1, n2, tile_n2=None):
    b, n, w2 = y3.shape
    width = w2 // 2
    scale = 1.0 / math.sqrt(n * FOURIER_GROUP_DIM)
    c1, s1 = _cos_sin(n1, n1, n1)
    w_real = jnp.stack([c1, -s1], axis=1)
    if n2 == 1:
        mat = w_real.reshape(n1, 2 * n1).astype(BF16)
        nb = math.gcd(b, CTX_SEQS)
        return pl.pallas_call(
            functools.partial(_dft_seq_kernel, width=width, scale=scale),
            out_shape=jax.ShapeDtypeStruct((b, n, width), BF16),
            grid=(b // nb,),
            in_specs=[pl.BlockSpec((nb, n, w2), lambda bi: (bi, 0, 0)),
                      pl.BlockSpec((n1, 2 * n1), lambda bi: (0, 0))],
            out_specs=pl.BlockSpec((nb, n, width), lambda bi: (bi, 0, 0)),
            compiler_params=_cparams(("parallel",)),
            name="dft_ctx",
        )(y3, mat)

    g = BF16_SUBLANES
    c2, s2 = _cos_sin(n2, n2, n2)
    w_cplx = jnp.stack([jnp.stack([c2, -s2], axis=1), jnp.stack([s2, c2], axis=1)], axis=0)

    def kron_cols(base, row_j):
        rows, cols = base.shape
        col = lax.broadcasted_iota(jnp.int32, (cols, cols * g), 1)
        expand = (col // g == lax.broadcasted_iota(jnp.int32, (cols, cols * g), 0)).astype(BF16)
        wide = jnp.dot(base.astype(BF16), expand, preferred_element_type=F32)
        keep = row_j[:, None] == (lax.broadcasted_iota(jnp.int32, (rows, cols * g), 1) % g)
        return jnp.where(keep, wide, 0.0).astype(BF16)

    base2 = jnp.broadcast_to(w_real.reshape(n1, 1, 2 * n1), (n1, g, 2 * n1)).reshape(n1 * g, 2 * n1)
    mat2 = kron_cols(base2, jnp.arange(n1 * g, dtype=jnp.int32) % g)
    tc, ts = _cos_sin(n1, n2, n)
    tc = jnp.broadcast_to(tc[:, :, None], (n1, n2, LANES))
    ts = jnp.broadcast_to(ts[:, :, None], (n1, n2, LANES))
    tiles = n2 // tile_n2
    tile_len = n1 * tile_n2
    mat1 = jnp.einsum('akbtl,ji->ajkbtil', w_cplx.reshape(2, n2, 2, tiles, tile_n2), jnp.eye(2, dtype=F32))
    mat1 = mat1.reshape(4 * n2, 4 * n2).astype(BF16)
    gp = 8
    t = pl.pallas_call(
        functools.partial(_dft_rows_kernel, width=width),
        out_shape=jax.ShapeDtypeStruct((b, n1, n2, w2), BF16),
        grid=(b, tile_len // (BF16_SUBLANES * gp)),
        in_specs=[pl.BlockSpec((None, tiles, BF16_SUBLANES * gp, w2), lambda bi, j: (bi, 0, j, 0)),
                  pl.BlockSpec(mat1.shape, lambda bi, j: (0, 0)),
                  pl.BlockSpec((2 * gp, n2, LANES), lambda bi, j: (j, 0, 0)),
                  pl.BlockSpec((2 * gp, n2, LANES), lambda bi, j: (j, 0, 0))],
        out_specs=pl.BlockSpec((None, 2 * gp, n2, w2), lambda bi, j: (bi, j, 0, 0)),
        compiler_params=_cparams(("parallel", "parallel")),
        name="dft_stage1",
    )(y3.reshape(b, tiles, tile_len, w2), mat1, tc, ts)
    out = pl.pallas_call(
        functools.partial(_dft_cols_kernel, width=width, scale=scale),
        out_shape=jax.ShapeDtypeStruct((b, n1, n2, width), BF16),
        grid=(b, n2 // g),
        in_specs=[pl.BlockSpec((None, n1, g, w2), lambda bi, j: (bi, 0, j, 0)),
                  pl.BlockSpec(mat2.shape, lambda bi, j: (0, 0), pipeline_mode=pl.Buffered(1))],
        out_specs=pl.BlockSpec((None, n1, g, width), lambda bi, j: (bi, 0, j, 0)),
        compiler_params=_cparams(("parallel", "parallel")),
        name="dft_stage2",
    )(t, mat2)
    return out.reshape(b, n, width)


def _out_kernel(a_ref, b_ref, f_ref, w_ref, x_ref, mod_ref, g_ref, x1_ref, h2_ref):
    ca = a_ref.shape[1]
    cb = b_ref.shape[1]
    m = mod_ref[...]
    half = a_ref.shape[0] // 2
    for r0 in (0, half):
        rows = slice(r0, r0 + half)
        acc = jnp.dot(a_ref[rows, :], w_ref[0:ca, :], preferred_element_type=F32)
        acc += jnp.dot(b_ref[rows, :], w_ref[ca:ca + cb, :], preferred_element_type=F32)
        acc += jnp.dot(f_ref[rows, :], w_ref[ca + cb:, :], preferred_element_type=F32)
        x1 = x_ref[rows, :] + m[2:3] * acc
        x1_ref[rows, :] = x1
        h = _rms(x1, g_ref[...])
        h2_ref[rows, :] = (h * (1.0 + m[4:5]) + m[3:4]).astype(BF16)


def _out_call(att_a, att_b, four, w_out_b, x2d, mod_l, cond_row_fn, norm_g, tm):
    t, d = x2d.shape
    ca, cb, cf = att_a.shape[1], att_b.shape[1], four.shape[1]
    w_out_b, w_layer = w_out_b
    return pl.pallas_call(
        _out_kernel,
        out_shape=[jax.ShapeDtypeStruct((t, d), F32), jax.ShapeDtypeStruct((t, d), BF16)],
        grid=(t // tm,),
        in_specs=[
            pl.BlockSpec((tm, ca), lambda i: (i, 0)),
            pl.BlockSpec((tm, cb), lambda i: (i, 0)),
            pl.BlockSpec((tm, cf), lambda i: (i, 0)),
            pl.BlockSpec((None, ca + cb + cf, d), lambda i: (w_layer, 0, 0), pipeline_mode=pl.Buffered(1)),
            pl.BlockSpec((tm, d), lambda i: (i, 0)),
            pl.BlockSpec((None, N_MOD, d), lambda i: (cond_row_fn(i), 0, 0)),
            pl.BlockSpec((1, d), lambda i: (0, 0)),
        ],
        out_specs=[pl.BlockSpec((tm, d), lambda i: (i, 0)), pl.BlockSpec((tm, d), lambda i: (i, 0))],
        compiler_params=_cparams(("parallel",)),
        name="out_proj",
    )(att_a, att_b, four, w_out_b, x2d, mod_l, norm_g)


HALO = BF16_SUBLANES


def _ffn_kernel(cw_ref, cb_ref, mod_ref, gf_ref, h_hbm, x1_hbm, wg_hbm, wu_hbm, wd_hbm,
                o_ref, hext_ref, x1_buf, wg_buf, wu_buf, wd_buf, sem, *, tm, tf, nj, layer, seq_len, final):
    i = pl.program_id(0)
    n_i = pl.num_programs(0)
    row0 = pl.multiple_of(i * tm, tm)

    def weight_copies(j, slot):
        cols = pl.ds(pl.multiple_of(j * tf, tf), tf)
        return (pltpu.make_async_copy(wg_hbm.at[layer, :, cols], wg_buf.at[slot], sem.at[0, slot]),
                pltpu.make_async_copy(wu_hbm.at[layer, :, cols], wu_buf.at[slot], sem.at[1, slot]),
                pltpu.make_async_copy(wd_hbm.at[layer, cols, :], wd_buf.at[slot], sem.at[2, slot]))

    def tile_copies(ti, slot):
        start = pl.multiple_of(ti * tm, tm)
        before = pl.multiple_of(jnp.maximum(start - HALO, 0), HALO)
        after = pl.multiple_of(jnp.minimum(start + tm, n_i * tm - HALO), HALO)
        dst = hext_ref.at[slot]
        return (pltpu.make_async_copy(h_hbm.at[pl.ds(before, HALO), :], dst.at[pl.ds(0, HALO), :], sem.at[3, slot]),
                pltpu.make_async_copy(h_hbm.at[pl.ds(start, tm), :], dst.at[pl.ds(HALO, tm), :], sem.at[4, slot]),
                pltpu.make_async_copy(h_hbm.at[pl.ds(after, HALO), :], dst.at[pl.ds(HALO + tm, HALO), :],
                                      sem.at[5, slot]))

    x1_copy = pltpu.make_async_copy(x1_hbm.at[pl.ds(row0, tm), :], x1_buf, sem.at[6, 0])
    hslot = i % 2

    @pl.when(i == 0)
    def _():
        for cp in tile_copies(0, 0) + weight_copies(0, 0):
            cp.start()

    x1_copy.start()

    @pl.when(i + 1 < n_i)
    def _():
        for cp in tile_copies(i + 1, 1 - hslot):
            cp.start()

    ext = tm + 2 * HALO
    pos = (i * tm + lax.broadcasted_iota(jnp.int32, (tm, 1), 0)) % seq_len
    for cp in tile_copies(i, hslot):
        cp.wait()
    hext = hext_ref.at[hslot]

    def chunk(j, first):
        step = i * nj + j
        slot = step % 2
        for cp in weight_copies(j, slot):
            cp.wait()

        @pl.when(step + 1 < n_i * nj)
        def _():
            for cp in weight_copies((j + 1) % nj, 1 - slot):
                cp.start()

        g = jnp.dot(hext[...], wg_buf[slot], preferred_element_type=F32)
        u = jnp.dot(hext[HALO:HALO + tm, :], wu_buf[slot], preferred_element_type=F32)
        g_prev = jnp.where(pos == 0, 0.0, pltpu.roll(g, 1, 0)[HALO:HALO + tm])
        g_next = jnp.where(pos == seq_len - 1, 0.0, pltpu.roll(g, ext - 1, 0)[HALO:HALO + tm])
        cw = cw_ref[j]
        gc = g_prev * cw[0:1] + g[HALO:HALO + tm] * cw[1:2] + g_next * cw[2:3] + cb_ref[j]
        act = (gc * jax.nn.sigmoid(gc)) * u
        y = jnp.dot(act.astype(BF16), wd_buf[slot], preferred_element_type=F32)
        if first:
            o_ref[...] = y
        else:
            o_ref[...] += y

    chunk(0, True)
    lax.fori_loop(1, nj, lambda j, carry: (chunk(j, False), carry)[1], 0)

    x1_copy.wait()
    m = mod_ref[...]
    x2 = x1_buf[...] + m[5:6] * o_ref[...]
    if final:
        x2 = _rms(x2, gf_ref[...])
    o_ref[...] = x2


def _ffn_call(h2, x1, weights, conv_w, conv_b, mod_l, cond_row_fn, final_g, seq_len, final, tm, tf):
    t, d = x1.shape
    w_gate_b, w_up_b, w_down_b, w_layer = weights
    f = w_gate_b.shape[2]
    nj = f // tf
    cw3 = conv_w.reshape(3, nj, tf).transpose(1, 0, 2)
    cb3 = conv_b.reshape(nj, 1, tf)
    hbm = pl.BlockSpec(memory_space=pl.ANY)
    return pl.pallas_call(
        functools.partial(_ffn_kernel, tm=tm, tf=tf, nj=nj, layer=w_layer, seq_len=seq_len, final=final),
        out_shape=jax.ShapeDtypeStruct((t, d), F32),
        grid=(t // tm,),
        in_specs=[
            pl.BlockSpec((nj, 3, tf), lambda i: (0, 0, 0)),
            pl.BlockSpec((nj, 1, tf), lambda i: (0, 0, 0)),
            pl.BlockSpec((None, N_MOD, d), lambda i: (cond_row_fn(i), 0, 0)),
            pl.BlockSpec((1, d), lambda i: (0, 0)),
            hbm, hbm, hbm, hbm, hbm,
        ],
        out_specs=pl.BlockSpec((tm, d), lambda i: (i, 0)),
        scratch_shapes=[pltpu.VMEM((2, tm + 2 * HALO, d), BF16), pltpu.VMEM((tm, d), F32),
                        pltpu.VMEM((2, d, tf), BF16), pltpu.VMEM((2, d, tf), BF16), pltpu.VMEM((2, tf, d), BF16),
                        pltpu.SemaphoreType.DMA((7, 2))],
        compiler_params=_cparams(("arbitrary",)),
        name="conv_ffn",
    )(cw3, cb3, mod_l, final_g, h2, x1, w_gate_b, w_up_b, w_down_b)


def _rope_tables(n, head_dim):
    rows = n // GRID_W
    t_row = jnp.repeat(jnp.arange(rows, dtype=F32), GRID_W)
    t_col = jnp.tile(jnp.arange(GRID_W, dtype=F32), rows)
    axis_dim = head_dim // 2
    inv = jnp.power(ROPE_BASE, -jnp.arange(0, axis_dim, 2, dtype=F32) / axis_dim)
    ar = t_row[:, None] * inv[None, :]
    ac = t_col[:, None] * inv[None, :]
    ang = jnp.concatenate([ar, ar, ac, ac], axis=-1)
    reps = LANES // head_dim
    quarter = head_dim // 4
    sign = jnp.where((jnp.arange(head_dim) % (2 * quarter)) < quarter, -1.0, 1.0).astype(F32)
    cos = jnp.tile(jnp.cos(ang), (1, reps))
    sin_signed = jnp.tile(jnp.sin(ang) * sign[None, :], (1, reps))
    return cos, sin_signed


def kernel(x_prompt, x_sample, cache_attn_k, cache_attn_v, cache_diff_k, cache_diff_v, c, c_ctx, norm1_g, norm2_g, w_ada, b_ada, w_in, attn_q_norm_g, attn_k_norm_g, diff_lambda_q1, diff_lambda_k1, diff_lambda_q2, diff_lambda_k2, diff_subnorm_g, w_out, ffn_w_gate, ffn_w_up, ffn_conv_w, ffn_conv_b, ffn_w_down, final_norm_g):
    depth = w_in.shape[0]
    bc, lc, d = x_prompt.shape
    bl, ll, _ = x_sample.shape
    past = cache_attn_k.shape[2]

    w_in_b = w_in.astype(BF16)
    w_out_b = w_out.astype(BF16)
    w_gate_b = ffn_w_gate.astype(BF16)
    w_up_b = ffn_w_up.astype(BF16)
    w_down_b = ffn_w_down.astype(BF16)

    n_rows = 8 * ((1 + bl + 7) // 8)
    cvec = jnp.concatenate([c_ctx[None, :], c, jnp.zeros((n_rows - 1 - bl, d), F32)], axis=0)
    mod = _ada_call(cvec, w_ada, b_ada).reshape(depth, n_rows, N_MOD, d)

    dft_c = jnp.concatenate(_cos_sin(FOURIER_GROUP_DIM, FOURIER_GROUP_DIM, FOURIER_GROUP_DIM), axis=1).astype(BF16)
    rope_tabs = _rope_tables(ll, HEAD_DIM_A) + _rope_tables(ll, DK_B)
    caches = (cache_attn_k.reshape(bl, depth, past, N_KV_HEADS_A * HEAD_DIM_A),
              cache_attn_v.reshape(bl, depth, past, N_KV_HEADS_A * HEAD_DIM_A),
              cache_diff_k.reshape(bl, depth, past, N_HEADS_B * 2 * DK_B),
              cache_diff_v.reshape(bl, depth, past, N_HEADS_B * DV_B))
    lam_all = jnp.stack([diff_lambda_q1, diff_lambda_k1, diff_lambda_q2, diff_lambda_k2], axis=1)

    def run_pass(x3, is_ctx):
        b, n, _ = x3.shape
        t = b * n
        tm = ROW_TILE
        tm_ffn = FFN_ROW_TILE
        x = x3.reshape(t, d)

        def cond_rows(tile):
            return (lambda i: 0) if is_ctx else (lambda i: 1 + (i * tile) // n)

        cond_row_fn = cond_rows(tm)
        new_kv = None
        for l in range(depth):
            lam_init = 0.8 - 0.6 * math.exp(-0.3 * l)
            res = _in_call(x, mod[l], cond_row_fn, norm1_g[l][None, :], (w_in_b, l),
                           attn_q_norm_g[l][None, :], attn_k_norm_g[l][None, :], dft_c,
                           None if is_ctx else rope_tabs, n, (depth, l, new_kv) if is_ctx else None, tm)
            qkv, y12 = res[0], res[1]
            if is_ctx:
                new_kv = res[2:]
            qkv3 = qkv.reshape(b, n, qkv.shape[1])
            att_a, att_b = _attn_calls(qkv3, None if is_ctx else caches, l, lam_all[l],
                                       diff_subnorm_g[l][None, :], lam_init,
                                       tq_a=min(ATTN_TQ_A, n), tq_b=min(ATTN_TQ_B, n), kc=ATTN_KC,
                                       ha=N_KV_HEADS_A if is_ctx else 1, hb=N_HEADS_B if is_ctx else 1,
                                       split_a=1 if is_ctx else ATTN_STREAMS, split_b=1 if is_ctx else ATTN_STREAMS,
                                       nb=math.gcd(b, CTX_SEQS) if is_ctx else 1)
            if is_ctx:
                four = _fourier_call(y12.reshape(b, n, y12.shape[1]), n, 1)
            else:
                four = _fourier_call(y12.reshape(b, n, y12.shape[1]), GRID_W, n // GRID_W, tm // GRID_W)
            x1, h2 = _out_call(att_a.reshape(t, -1), att_b.reshape(t, -1), four.reshape(t, -1), (w_out_b, l),
                               x, mod[l], cond_row_fn, norm2_g[l][None, :], tm)
            x = _ffn_call(h2, x1, (w_gate_b, w_up_b, w_down_b, l), ffn_conv_w[l], ffn_conv_b[l][None, :],
                          mod[l], cond_rows(tm_ffn), final_norm_g[None, :], n, l == depth - 1, tm_ffn, FFN_TF)
        return x.reshape(b, n, d), new_kv

    y_prompt, kvs = run_pass(x_prompt, True)
    new_attn_k = kvs[0].reshape(bc, depth, lc, N_KV_HEADS_A, HEAD_DIM_A)
    new_attn_v = kvs[1].reshape(bc, depth, lc, N_KV_HEADS_A, HEAD_DIM_A)
    new_diff_k = kvs[2].reshape(bc, depth, lc, N_HEADS_B, 2 * DK_B)
    new_diff_v = kvs[3].reshape(bc, depth, lc, N_HEADS_B, DV_B)

    y_sample, _ = run_pass(x_sample, False)
    return (y_prompt, y_sample, new_attn_k, new_attn_v, new_diff_k, new_diff_v)
```

```python
import functools
import math

import jax
import jax.numpy as jnp
from jax import lax
from jax.experimental import pallas as pl
from jax.experimental.pallas import tpu as pltpu

F32 = jnp.float32
BF16 = jnp.bfloat16

GRID_W = 64
ROPE_BASE = 10000.0
NORM_EPS = 1e-6
HEAD_DIM_A = 128
N_KV_HEADS_A = 2
GQA_GROUP = 4
N_HEADS_B = 4
DK_B = 64
DV_B = 128
N_FOURIER_GROUPS = 4
FOURIER_GROUP_DIM = 128
N_MOD = 6

LOG2E = math.log2(math.e)
LANES = 128
BF16_SUBLANES = 16
VMEM_LIMIT = 58 * 1024 * 1024

ROW_TILE = 512
FFN_ROW_TILE = 1024
FFN_TF = 512
OUT_TN = 512
ADA_TN = 1024
ATTN_TQ_A = 512
ATTN_TQ_B = 1024
ATTN_KC = 2048
ATTN_STREAMS = 4
CTX_SEQS = 4


def _cparams(sem):
    return pltpu.CompilerParams(dimension_semantics=sem, vmem_limit_bytes=VMEM_LIMIT)


def _rms(x, g):
    return x * lax.rsqrt(jnp.mean(x * x, axis=-1, keepdims=True) + NORM_EPS) * g


def _ada_kernel(c_ref, w_ref, b_ref, o_ref):
    c = c_ref[...]
    s = (c * jax.nn.sigmoid(c)).astype(BF16)
    w = w_ref[...].astype(BF16)
    o_ref[...] = jnp.dot(s, w, preferred_element_type=F32) + b_ref[...]


def _ada_call(cvec, w_ada, b_ada):
    depth, d, n = w_ada.shape
    rows = cvec.shape[0]
    tn = ADA_TN
    return pl.pallas_call(
        _ada_kernel,
        out_shape=jax.ShapeDtypeStruct((depth, rows, n), F32),
        grid=(depth, n // tn),
        in_specs=[
            pl.BlockSpec((rows, d), lambda l, j: (0, 0)),
            pl.BlockSpec((None, d, tn), lambda l, j: (l, 0, j)),
            pl.BlockSpec((None, 1, tn), lambda l, j: (l, 0, j)),
        ],
        out_specs=pl.BlockSpec((None, rows, tn), lambda l, j: (l, 0, j)),
        compiler_params=_cparams(("parallel", "parallel")),
        name="ada_mod",
    )(cvec, w_ada, b_ada.reshape(depth, 1, n))


def _rope(xs, cos, sin_signed, shift):
    w = xs.shape[-1]
    lane = lax.broadcasted_iota(jnp.int32, xs.shape, 1)
    first = (lane % (2 * shift)) < shift
    rot = jnp.where(first, pltpu.roll(xs, w - shift, 1), pltpu.roll(xs, shift, 1))
    return xs * cos + rot * sin_signed


def _in_kernel(*refs, rope, ctx_out, n_alias, cols):
    it = iter(refs)
    x_ref, mod_ref, g_ref, w_ref, gq_ref, gk_ref, dft_ref = (next(it) for _ in range(7))
    if rope:
        cos_a, sin_a, cos_b, sin_b = (next(it)[...] for _ in range(4))
        perm_ref = next(it)
    for _ in range(n_alias):
        next(it)
    qkv_ref, y_ref = next(it), next(it)
    if ctx_out:
        ka_ref, va_ref, kb_ref, vb_ref = (next(it) for _ in range(4))
    c_qa, c_ka, c_va, c_qb, c_kb, c_vb, c_f, c_end = cols

    def put_cache(ref, off, v):
        head = off // LANES
        seqs, rows = ref.shape[0], ref.shape[-2]
        heads = rows * seqs // v.shape[0]
        n = rows // heads
        for s in range(seqs):
            vs = v[s * n:(s + 1) * n, :]
            if len(ref.shape) == 4:
                for dd in range(ref.shape[1]):
                    ref[s, dd, pl.ds(head, n, stride=heads), :] = vs
            else:
                ref[s, pl.ds(head, n, stride=heads), :] = vs

    m = mod_ref[...]
    h = _rms(x_ref[...], g_ref[...])
    hb = (h * (1.0 + m[1:2]) + m[0:1]).astype(BF16)
    gq = gq_ref[...]
    gk = gk_ref[...]
    scale_a = HEAD_DIM_A ** -0.5 * LOG2E
    scale_b = DK_B ** -0.5 * LOG2E
    chunk = 4 * LANES

    for c0 in range(0, c_end, chunk):
        acc = jnp.dot(hb, w_ref[:, c0:c0 + chunk], preferred_element_type=F32)
        if rope and c0 >= c_f:
            acc = jnp.dot(perm_ref[...], acc.astype(BF16), preferred_element_type=F32)
        for s in range(chunk // LANES):
            col = c0 + s * LANES
            v = acc[:, s * LANES:(s + 1) * LANES]
            if col < c_ka:
                v = _rms(v, gq)
                if rope:
                    v = _rope(v, cos_a, sin_a, HEAD_DIM_A // 4)
                qkv_ref[:, col:col + LANES] = (v * scale_a).astype(BF16)
            elif col < c_va:
                v = _rms(v, gk)
                if ctx_out:
                    put_cache(ka_ref, col - c_ka, v)
                if rope:
                    v = _rope(v, cos_a, sin_a, HEAD_DIM_A // 4)
                qkv_ref[:, col:col + LANES] = v.astype(BF16)
            elif col < c_qb:
                if ctx_out:
                    put_cache(va_ref, col - c_va, v)
                qkv_ref[:, col:col + LANES] = v.astype(BF16)
            elif col < c_kb:
                if rope:
                    v = _rope(v, cos_b, sin_b, DK_B // 4)
                qkv_ref[:, col:col + LANES] = (v * scale_b).astype(BF16)
            elif col < c_vb:
                if ctx_out:
                    put_cache(kb_ref, col - c_kb, v)
                if rope:
                    v = _rope(v, cos_b, sin_b, DK_B // 4)
                qkv_ref[:, col:col + LANES] = v.astype(BF16)
            elif col < c_f:
                if ctx_out:
                    put_cache(vb_ref, col - c_vb, v)
                qkv_ref[:, col:col + LANES] = v.astype(BF16)
            else:
                yy = jnp.dot(v.astype(BF16), dft_ref[...], preferred_element_type=F32)
                gcol = col - c_f
                half = c_end - c_f
                y_ref[:, gcol:gcol + LANES] = yy[:, :LANES].astype(BF16)
                y_ref[:, half + gcol:half + gcol + LANES] = yy[:, LANES:].astype(BF16)


def _in_call(x2d, mod_l, cond_row_fn, norm_g, w_in_b, gq, gk, dft_c, rope_tabs, seq_len, new_cache, tm):
    ctx_out = new_cache is not None
    t, d = x2d.shape
    w_in_b, w_layer = w_in_b
    d_in = w_in_b.shape[2]
    c_qa = 0
    c_ka = N_KV_HEADS_A * GQA_GROUP * HEAD_DIM_A
    c_va = c_ka + N_KV_HEADS_A * HEAD_DIM_A
    c_qb = c_va + N_KV_HEADS_A * HEAD_DIM_A
    c_kb = c_qb + N_HEADS_B * 2 * DK_B
    c_vb = c_kb + N_HEADS_B * 2 * DK_B
    c_f = c_vb + N_HEADS_B * DV_B
    c_end = c_f + N_FOURIER_GROUPS * FOURIER_GROUP_DIM
    assert c_end == d_in
    cols = (c_qa, c_ka, c_va, c_qb, c_kb, c_vb, c_f, c_end)
    rope = rope_tabs is not None
    n_f = c_end - c_f

    in_specs = [
        pl.BlockSpec((tm, d), lambda i: (i, 0)),
        pl.BlockSpec((None, N_MOD, d), lambda i: (cond_row_fn(i), 0, 0)),
        pl.BlockSpec((1, d), lambda i: (0, 0)),
        pl.BlockSpec((None, d, d_in), lambda i: (w_layer, 0, 0), pipeline_mode=pl.Buffered(1)),
        pl.BlockSpec((1, HEAD_DIM_A), lambda i: (0, 0)),
        pl.BlockSpec((1, HEAD_DIM_A), lambda i: (0, 0)),
        pl.BlockSpec((FOURIER_GROUP_DIM, 2 * FOURIER_GROUP_DIM), lambda i: (0, 0)),
    ]
    args = [x2d, mod_l, norm_g, w_in_b, gq, gk, dft_c]
    if rope:
        nblk = seq_len // tm
        for tab in rope_tabs:
            in_specs.append(pl.BlockSpec((tm, LANES), lambda i: (i % nblk, 0)))
            args.append(tab)
        r_new = jnp.arange(tm, dtype=jnp.int32)
        r_old = (r_new % (tm // GRID_W)) * GRID_W + r_new // (tm // GRID_W)
        perm = (r_old[:, None] == jnp.arange(tm, dtype=jnp.int32)[None, :]).astype(BF16)
        in_specs.append(pl.BlockSpec((tm, tm), lambda i: (0, 0)))
        args.append(perm)
    out_shape = [jax.ShapeDtypeStruct((t, c_f), BF16), jax.ShapeDtypeStruct((t, 2 * n_f), BF16)]
    out_specs = [pl.BlockSpec((tm, c_f), lambda i: (i, 0)), pl.BlockSpec((tm, 2 * n_f), lambda i: (i, 0))]
    aliases = {}
    n_alias = 0
    if ctx_out:
        depth, layer, prev = new_cache
        nb = t // seq_len
        spt = tm // seq_len
        for width in (c_va - c_ka, c_qb - c_va, c_vb - c_kb, c_f - c_vb):
            rows = seq_len * (width // LANES)
            out_shape.append(jax.ShapeDtypeStruct((nb, depth, rows, LANES), F32))
            if prev is None:
                out_specs.append(pl.BlockSpec((spt, depth, rows, LANES), lambda i: (i, 0, 0, 0)))
            else:
                out_specs.append(pl.BlockSpec((spt, None, rows, LANES), lambda i: (i, layer, 0, 0)))
        if prev is not None:
            n_alias = len(prev)
            for a, arr in enumerate(prev):
                aliases[len(args)] = 2 + a
                in_specs.append(pl.BlockSpec(memory_space=pl.ANY))
                args.append(arr)
    return pl.pallas_call(
        functools.partial(_in_kernel, rope=rope, ctx_out=ctx_out, n_alias=n_alias, cols=cols),
        out_shape=out_shape,
        grid=(t // tm,),
        in_specs=in_specs,
        out_specs=out_specs,
        input_output_aliases=aliases,
        compiler_params=_cparams(("parallel",)),
        name="in_proj_ctx" if ctx_out else "in_proj_lat",
    )(*args)


def _softmax_pv(q, srcs):
    chunks = []
    for k_ref, v_ref, n_keys, kc, col in srcs:
        for c in range(n_keys // kc):
            chunks.append([(k_ref, v_ref, c * kc, kc, col)])
    if len(chunks) > 1 and chunks[0][0][3] < chunks[1][0][3]:
        chunks = [chunks[0] + chunks[1]] + chunks[2:]

    def scores(ch):
        parts = []
        for k_ref, _, off, kc, col in ch:
            k = k_ref[off:off + kc, col:col + LANES].astype(BF16)
            parts.append(lax.dot_general(q, k, (((1,), (1,)), ((), ())), preferred_element_type=F32))
        return parts[0] if len(parts) == 1 else jnp.concatenate(parts, axis=1)

    def lane_fold(x, op):
        out = x[:, 0:LANES]
        for t in range(1, x.shape[1] // LANES):
            out = op(out, x[:, t * LANES:(t + 1) * LANES])
        return out

    mxu_sum = len(chunks) > 1
    m = acc = l_part = None
    s_next = scores(chunks[0])
    for i, ch in enumerate(chunks):
        s = s_next
        if i + 1 < len(chunks):
            s_next = scores(chunks[i + 1])
        row_max = jnp.max(lane_fold(s, jnp.maximum), axis=-1, keepdims=True)
        m_new = row_max if m is None else jnp.maximum(m, row_max)
        p = jnp.exp2(s - m_new)
        if not mxu_sum:
            l_part = lane_fold(p, jnp.add)
        pb = p.astype(BF16)
        pv = None
        lo = 0
        for _, v_ref, off, kc, col in ch:
            v = v_ref[off:off + kc, col:col + LANES].astype(BF16)
            if mxu_sum:
                ones_col = (lax.broadcasted_iota(jnp.int32, (kc, LANES), 1) == 0).astype(BF16)
                v = jnp.concatenate([v, ones_col], axis=1)
            part = jnp.dot(pb[:, lo:lo + kc], v, preferred_element_type=F32)
            pv = part if pv is None else pv + part
            lo += kc
        acc = pv if m is None else jnp.exp2(m - m_new) * acc + pv
        m = m_new
    if mxu_sum:
        return acc[:, :LANES] / acc[:, LANES:LANES + 1]
    return acc / jnp.sum(l_part, axis=-1, keepdims=True)


def _srcs(cache_refs, k_ref, v_ref, kc, head):
    srcs = []
    col = head * LANES
    if cache_refs is not None:
        ck, cv = cache_refs
        srcs.append((ck, cv, ck.shape[0], ck.shape[0], col))
    n = k_ref.shape[0]
    srcs.append((k_ref, v_ref, n, min(kc, n), col))
    return srcs


def _attn_a_kernel(*refs, has_cache, tq, kc, heads, split):
    if has_cache:
        q_ref, ck_ref, cv_ref, k_ref, v_ref, o_ref = refs
        cache = (ck_ref, cv_ref)
    else:
        q_ref, k_ref, v_ref, o_ref = refs
        cache = None
    per = GQA_GROUP // split
    for bb in range(q_ref.shape[0]):
        qb, kb, vb, ob = q_ref.at[bb], k_ref.at[bb], v_ref.at[bb], o_ref.at[bb]
        for kh in range(heads):
            for part in range(split):
                cols = [(kh * GQA_GROUP + part * per + h) * LANES for h in range(per)]
                qs = jnp.concatenate([qb[:, c:c + LANES] for c in cols], axis=0)
                o = _softmax_pv(qs, _srcs(cache, kb, vb, kc, kh))
                for h, c in enumerate(cols):
                    ob[:, c:c + LANES] = o[h * tq:(h + 1) * tq].astype(BF16)


def _attn_b_kernel(*refs, has_cache, tq, kc, heads, split, lam_init):
    if has_cache:
        lam_ref, gs_ref, q_ref, ck_ref, cv_ref, k_ref, v_ref, o_ref = refs
        cache = (ck_ref, cv_ref)
    else:
        lam_ref, gs_ref, q_ref, k_ref, v_ref, o_ref = refs
        cache = None
    lp = lam_ref[...]
    lam = (jnp.exp(jnp.sum(lp[0:1] * lp[1:2], axis=-1, keepdims=True))
           - jnp.exp(jnp.sum(lp[2:3] * lp[3:4], axis=-1, keepdims=True)) + lam_init)
    rows = tq // split
    for bb in range(q_ref.shape[0]):
        qb, kb, vb, ob = q_ref.at[bb], k_ref.at[bb], v_ref.at[bb], o_ref.at[bb]
        for hd in range(heads):
            for part in range(split):
                q = qb[part * rows:(part + 1) * rows, hd * LANES:(hd + 1) * LANES]
                lane = lax.broadcasted_iota(jnp.int32, q.shape, 1)
                zero = jnp.zeros_like(q)
                qz = jnp.concatenate([jnp.where(lane < DK_B, q, zero), jnp.where(lane >= DK_B, q, zero)], axis=0)
                o = _softmax_pv(qz, _srcs(cache, kb, vb, kc, hd))
                dlt = o[:rows] - lam * o[rows:]
                ob[part * rows:(part + 1) * rows, hd * LANES:(hd + 1) * LANES] = (
                    _rms(dlt, gs_ref[...]) * (1.0 - lam_init)).astype(BF16)


def _attn_calls(qkv3, caches, layer, lam_params, g_sub, lam_init, tq_a, tq_b, kc, ha, hb, split_a, split_b, nb):
    b, n, _ = qkv3.shape
    has_cache = caches is not None
    assert b % nb == 0 and (nb == 1 or not has_cache)
    qa_blk =GQA_GROUP * HEAD_DIM_A // LANES
    k_a0 = N_KV_HEADS_A * qa_blk
    v_a0 = k_a0 + N_KV_HEADS_A
    q_b0 = v_a0 + N_KV_HEADS_A
    k_b0 = q_b0 + N_HEADS_B
    v_b0 = k_b0 + N_HEADS_B
    assert all(x % ha == 0 for x in (N_KV_HEADS_A, k_a0, v_a0)) and all(x % hb == 0 for x in (N_HEADS_B, q_b0, k_b0, v_b0))

    qw, kw = ha * GQA_GROUP * LANES, ha * LANES
    in_specs = [pl.BlockSpec((nb, tq_a, qw), lambda bi, h, i: (bi, i, h))]
    args = [qkv3]
    if has_cache:
        ck, cv = caches[0], caches[1]
        p = ck.shape[2]
        in_specs += [pl.BlockSpec((None, None, p, kw), lambda bi, h, i: (bi, layer, 0, h))] * 2
        args += [ck, cv]
    in_specs += [pl.BlockSpec((nb, n, kw), lambda bi, h, i: (bi, 0, k_a0 // ha + h)),
                 pl.BlockSpec((nb, n, kw), lambda bi, h, i: (bi, 0, v_a0 // ha + h))]
    args += [qkv3, qkv3]
    att_a = pl.pallas_call(
        functools.partial(_attn_a_kernel, has_cache=has_cache, tq=tq_a, kc=kc, heads=ha, split=split_a),
        out_shape=jax.ShapeDtypeStruct((b, n, N_KV_HEADS_A * GQA_GROUP * HEAD_DIM_A), BF16),
        grid=(b // nb, N_KV_HEADS_A // ha, n // tq_a),
        in_specs=in_specs,
        out_specs=pl.BlockSpec((nb, tq_a, qw), lambda bi, h, i: (bi, i, h)),
        compiler_params=_cparams(("parallel", "parallel", "arbitrary")),
        name="attn_a_lat" if has_cache else "attn_a_ctx",
    )(*args)

    bw = hb * LANES
    in_specs = [pl.BlockSpec((4, DK_B), lambda bi, h, i: (0, 0)),
                pl.BlockSpec((1, DV_B), lambda bi, h, i: (0, 0)),
                pl.BlockSpec((nb, tq_b, bw), lambda bi, h, i: (bi, i, q_b0 // hb + h))]
    args = [lam_params, g_sub, qkv3]
    if has_cache:
        ck, cv = caches[2], caches[3]
        p = ck.shape[2]
        in_specs += [pl.BlockSpec((None, None, p, bw), lambda bi, h, i: (bi, layer, 0, h))] * 2
        args += [ck, cv]
    in_specs += [pl.BlockSpec((nb, n, bw), lambda bi, h, i: (bi, 0, k_b0 // hb + h)),
                 pl.BlockSpec((nb, n, bw), lambda bi, h, i: (bi, 0, v_b0 // hb + h))]
    args += [qkv3, qkv3]
    att_b = pl.pallas_call(
        functools.partial(_attn_b_kernel, has_cache=has_cache, tq=tq_b, kc=kc, heads=hb, split=split_b, lam_init=lam_init),
        out_shape=jax.ShapeDtypeStruct((b, n, N_HEADS_B * DV_B), BF16),
        grid=(b // nb, N_HEADS_B // hb, n // tq_b),
        in_specs=in_specs,
        out_specs=pl.BlockSpec((nb, tq_b, bw), lambda bi, h, i: (bi, i, h)),
        compiler_params=_cparams(("parallel", "parallel", "arbitrary")),
        name="attn_b_lat" if has_cache else "attn_b_ctx",
    )(*args)
    return att_a, att_b


def _dft_cols_kernel(x_ref, mat_ref, o_ref, *, width, scale):
    p, g, w2 = x_ref.shape
    x = x_ref[...].reshape(p * g, w2)
    xs = jnp.concatenate([x[:, :width], x[:, width:]], axis=0)
    u = jnp.dot(mat_ref[...], xs, preferred_element_type=F32)
    o_ref[...] = (u * scale).astype(BF16).reshape(o_ref.shape)


def _dft_rows_kernel(x_ref, mat_ref, tc_ref, ts_ref, o_ref, *, width):
    tiles, rows, w2 = x_ref.shape
    pair = BF16_SUBLANES
    reps = width // LANES
    for p in range(rows // pair):
        x = x_ref[:, p * pair:(p + 1) * pair, :].reshape(tiles * pair, w2)
        xs = jnp.concatenate([x[:, :width], x[:, width:]], axis=0)
        u = jnp.dot(mat_ref[...], xs, preferred_element_type=F32)
        half = u.shape[0] // 2
        ur, ui = u[:half], u[half:]
        tc = jnp.concatenate([tc_ref[2 * p:2 * p + 2].reshape(half, LANES)] * reps, axis=1)
        ts = jnp.concatenate([ts_ref[2 * p:2 * p + 2].reshape(half, LANES)] * reps, axis=1)
        o_ref[2 * p:2 * p + 2, :, :width] = (ur * tc - ui * ts).astype(BF16).reshape(2, half // 2, width)
        o_ref[2 * p:2 * p + 2, :, width:] = (ur * ts + ui * tc).astype(BF16).reshape(2, half // 2, width)


def _dft_seq_kernel(x_ref, mat_ref, o_ref, *, width, scale):
    nb = x_ref.shape[0]
    xs = jnp.concatenate(
        [jnp.concatenate([x_ref[bb, :, :width], x_ref[bb, :, width:]], axis=0) for bb in range(nb)], axis=1)
    u = jnp.dot(mat_ref[...], xs, preferred_element_type=F32)
    for bb in range(nb):
        o_ref[bb] = (u[:, bb * width:(bb + 1) * width] * scale).astype(BF16)


def _cos_sin(n_rows, n_cols, period):
    a = jnp.arange(n_rows, dtype=jnp.int32)[:, None]
    b = jnp.arange(n_cols, dtype=jnp.int32)[None, :]
    ang = ((a * b) % period).astype(F32) * (2.0 * math.pi / period)
    return jnp.cos(ang), jnp.sin(ang)


def _fourier_call(y3, n1, n2, tile_n2=None):
    b, n, w2 = y3.shape
    width = w2 // 2
    scale = 1.0 / math.sqrt(n * FOURIER_GROUP_DIM)
    c1, s1 = _cos_sin(n1, n1, n1)
    w_real = jnp.stack([c1, -s1], axis=1)
    if n2 == 1:
        mat = w_real.reshape(n1, 2 * n1).astype(BF16)
        nb = math.gcd(b, CTX_SEQS)
        return pl.pallas_call(
            functools.partial(_dft_seq_kernel, width=width, scale=scale),
            out_shape=jax.ShapeDtypeStruct((b, n, width), BF16),
            grid=(b // nb,),
            in_specs=[pl.BlockSpec((nb, n, w2), lambda bi: (bi, 0, 0)),
                      pl.BlockSpec((n1, 2 * n1), lambda bi: (0, 0))],
            out_specs=pl.BlockSpec((nb, n, width), lambda bi: (bi, 0, 0)),
            compiler_params=_cparams(("parallel",)),
            name="dft_ctx",
        )(y3, mat)

    g = BF16_SUBLANES
    c2, s2 = _cos_sin(n2, n2, n2)
    w_cplx = jnp.stack([jnp.stack([c2, -s2], axis=1), jnp.stack([s2, c2], axis=1)], axis=0)

    def kron_cols(base, row_j):
        rows, cols = base.shape
        col = lax.broadcasted_iota(jnp.int32, (cols, cols * g), 1)
        expand = (col // g == lax.broadcasted_iota(jnp.int32, (cols, cols * g), 0)).astype(BF16)
        wide = jnp.dot(base.astype(BF16), expand, preferred_element_type=F32)
        keep = row_j[:, None] == (lax.broadcasted_iota(jnp.int32, (rows, cols * g), 1) % g)
        return jnp.where(keep, wide, 0.0).astype(BF16)

    base2 = jnp.broadcast_to(w_real.reshape(n1, 1, 2 * n1), (n1, g, 2 * n1)).reshape(n1 * g, 2 * n1)
    mat2 = kron_cols(base2, jnp.arange(n1 * g, dtype=jnp.int32) % g)
    tc, ts = _cos_sin(n1, n2, n)
    tc = jnp.broadcast_to(tc[:, :, None], (n1, n2, LANES))
    ts = jnp.broadcast_to(ts[:, :, None], (n1, n2, LANES))
    tiles = n2 // tile_n2
    tile_len = n1 * tile_n2
    mat1 = jnp.einsum('akbtl,ji->ajkbtil', w_cplx.reshape(2, n2, 2, tiles, tile_n2), jnp.eye(2, dtype=F32))
    mat1 = mat1.reshape(4 * n2, 4 * n2).astype(BF16)
    gp = 8
    t = pl.pallas_call(
        functools.partial(_dft_rows_kernel, width=width),
        out_shape=jax.ShapeDtypeStruct((b, n1, n2, w2), BF16),
        grid=(b, tile_len // (BF16_SUBLANES * gp)),
        in_specs=[pl.BlockSpec((None, tiles, BF16_SUBLANES * gp, w2), lambda bi, j: (bi, 0, j, 0)),
                  pl.BlockSpec(mat1.shape, lambda bi, j: (0, 0)),
                  pl.BlockSpec((2 * gp, n2, LANES), lambda bi, j: (j, 0, 0)),
                  pl.BlockSpec((2 * gp, n2, LANES), lambda bi, j: (j, 0, 0))],
        out_specs=pl.BlockSpec((None, 2 * gp, n2, w2), lambda bi, j: (bi, j, 0, 0)),
        compiler_params=_cparams(("parallel", "parallel")),
        name="dft_stage1",
    )(y3.reshape(b, tiles, tile_len, w2), mat1, tc, ts)
    out = pl.pallas_call(
        functools.partial(_dft_cols_kernel, width=width, scale=scale),
        out_shape=jax.ShapeDtypeStruct((b, n1, n2, width), BF16),
        grid=(b, n2 // g),
        in_specs=[pl.BlockSpec((None, n1, g, w2), lambda bi, j: (bi, 0, j, 0)),
                  pl.BlockSpec(mat2.shape, lambda bi, j: (0, 0), pipeline_mode=pl.Buffered(1))],
        out_specs=pl.BlockSpec((None, n1, g, width), lambda bi, j: (bi, 0, j, 0)),
        compiler_params=_cparams(("parallel", "parallel")),
        name="dft_stage2",
    )(t, mat2)
    return out.reshape(b, n, width)


def _out_kernel(a_ref, b_ref, f_ref, w_ref, x_ref, mod_ref, g_ref, x1_ref, h2_ref):
    ca = a_ref.shape[1]
    cb = b_ref.shape[1]
    m = mod_ref[...]
    d = x_ref.shape[1]
    gain = g_ref[...] * (1.0 + m[4:5])
    half = a_ref.shape[0] // 2
    for r0 in (0, half):
        rows = slice(r0, r0 + half)
        ssq = None
        for c0 in range(0, d, OUT_TN):
            cols = slice(c0, c0 + OUT_TN)
            acc = jnp.dot(a_ref[rows, :], w_ref[0:ca, cols], preferred_element_type=F32)
            acc += jnp.dot(b_ref[rows, :], w_ref[ca:ca + cb, cols], preferred_element_type=F32)
            acc += jnp.dot(f_ref[rows, :], w_ref[ca + cb:, cols], preferred_element_type=F32)
            x1 = x_ref[rows, cols] + m[2:3, cols] * acc
            x1_ref[rows, cols] = x1
            part = jnp.sum(x1 * x1, axis=-1, keepdims=True)
            ssq = part if ssq is None else ssq + part
        r = lax.rsqrt(ssq * (1.0 / d) + NORM_EPS)
        h2_ref[rows, :] = (x1_ref[rows, :] * r * gain + m[3:4]).astype(BF16)


def _out_call(att_a, att_b, four, w_out_b, x2d, mod_l, cond_row_fn, norm_g, tm):
    t, d = x2d.shape
    ca, cb, cf = att_a.shape[1], att_b.shape[1], four.shape[1]
    w_out_b, w_layer = w_out_b
    return pl.pallas_call(
        _out_kernel,
        out_shape=[jax.ShapeDtypeStruct((t, d), F32), jax.ShapeDtypeStruct((t, d), BF16)],
        grid=(t // tm,),
        in_specs=[
            pl.BlockSpec((tm, ca), lambda i: (i, 0)),
            pl.BlockSpec((tm, cb), lambda i: (i, 0)),
            pl.BlockSpec((tm, cf), lambda i: (i, 0)),
            pl.BlockSpec((None, ca + cb + cf, d), lambda i: (w_layer, 0, 0), pipeline_mode=pl.Buffered(1)),
            pl.BlockSpec((tm, d), lambda i: (i, 0)),
            pl.BlockSpec((None, N_MOD, d), lambda i: (cond_row_fn(i), 0, 0)),
            pl.BlockSpec((1, d), lambda i: (0, 0)),
        ],
        out_specs=[pl.BlockSpec((tm, d), lambda i: (i, 0)), pl.BlockSpec((tm, d), lambda i: (i, 0))],
        compiler_params=_cparams(("parallel",)),
        name="out_proj",
    )(att_a, att_b, four, w_out_b, x2d, mod_l, norm_g)


HALO = BF16_SUBLANES


def _ffn_kernel(cw_ref, cb_ref, mod_ref, gf_ref, h_hbm, x1_hbm, wg_hbm, wu_hbm, wd_hbm,
                o_ref, hext_ref, x1_buf, wg_buf, wu_buf, wd_buf, sem, *, tm, tf, nj, layer, seq_len, final):
    i = pl.program_id(0)
    n_i = pl.num_programs(0)
    row0 = pl.multiple_of(i * tm, tm)

    def weight_copies(j, slot):
        cols = pl.ds(pl.multiple_of(j * tf, tf), tf)
        return (pltpu.make_async_copy(wg_hbm.at[layer, :, cols], wg_buf.at[slot], sem.at[0, slot]),
                pltpu.make_async_copy(wu_hbm.at[layer, :, cols], wu_buf.at[slot], sem.at[1, slot]),
                pltpu.make_async_copy(wd_hbm.at[layer, cols, :], wd_buf.at[slot], sem.at[2, slot]))

    def tile_copies(ti, slot):
        start = pl.multiple_of(ti * tm, tm)
        before = pl.multiple_of(jnp.maximum(start - HALO, 0), HALO)
        after = pl.multiple_of(jnp.minimum(start + tm, n_i * tm - HALO), HALO)
        dst = hext_ref.at[slot]
        return (pltpu.make_async_copy(h_hbm.at[pl.ds(before, HALO), :], dst.at[pl.ds(0, HALO), :], sem.at[3, slot]),
                pltpu.make_async_copy(h_hbm.at[pl.ds(start, tm), :], dst.at[pl.ds(HALO, tm), :], sem.at[4, slot]),
                pltpu.make_async_copy(h_hbm.at[pl.ds(after, HALO), :], dst.at[pl.ds(HALO + tm, HALO), :],
                                      sem.at[5, slot]))

    x1_copy = pltpu.make_async_copy(x1_hbm.at[pl.ds(row0, tm), :], x1_buf, sem.at[6, 0])
    hslot = i % 2

    @pl.when(i == 0)
    def _():
        for cp in tile_copies(0, 0) + weight_copies(0, 0):
            cp.start()

    x1_copy.start()

    @pl.when(i + 1 < n_i)
    def _():
        for cp in tile_copies(i + 1, 1 - hslot):
            cp.start()

    ext = tm + 2 * HALO
    pos = (i * tm + lax.broadcasted_iota(jnp.int32, (tm, 1), 0)) % seq_len
    for cp in tile_copies(i, hslot):
        cp.wait()
    hext = hext_ref.at[hslot]

    def chunk(j, first):
        step = i * nj + j
        slot = step % 2
        for cp in weight_copies(j, slot):
            cp.wait()

        @pl.when(step + 1 < n_i * nj)
        def _():
            for cp in weight_copies((j + 1) % nj, 1 - slot):
                cp.start()

        g = jnp.dot(hext[...], wg_buf[slot], preferred_element_type=F32)
        u = jnp.dot(hext[HALO:HALO + tm, :], wu_buf[slot], preferred_element_type=F32)
        g_prev = jnp.where(pos == 0, 0.0, pltpu.roll(g, 1, 0)[HALO:HALO + tm])
        g_next = jnp.where(pos == seq_len - 1, 0.0, pltpu.roll(g, ext - 1, 0)[HALO:HALO + tm])
        cw = cw_ref[j]
        gc = g_prev * cw[0:1] + g[HALO:HALO + tm] * cw[1:2] + g_next * cw[2:3] + cb_ref[j]
        act = (gc * jax.nn.sigmoid(gc)) * u
        y = jnp.dot(act.astype(BF16), wd_buf[slot], preferred_element_type=F32)
        if first:
            o_ref[...] = y
        else:
            o_ref[...] += y

    chunk(0, True)
    lax.fori_loop(1, nj, lambda j, carry: (chunk(j, False), carry)[1], 0)

    x1_copy.wait()
    m = mod_ref[...]
    x2 = x1_buf[...] + m[5:6] * o_ref[...]
    if final:
        x2 = _rms(x2, gf_ref[...])
    o_ref[...] = x2


def _ffn_call(h2, x1, weights, conv_w, conv_b, mod_l, cond_row_fn, final_g, seq_len, final, tm, tf):
    t, d = x1.shape
    w_gate_b, w_up_b, w_down_b, w_layer = weights
    f = w_gate_b.shape[2]
    nj = f // tf
    cw3 = conv_w.reshape(3, nj, tf).transpose(1, 0, 2)
    cb3 = conv_b.reshape(nj, 1, tf)
    hbm = pl.BlockSpec(memory_space=pl.ANY)
    return pl.pallas_call(
        functools.partial(_ffn_kernel, tm=tm, tf=tf, nj=nj, layer=w_layer, seq_len=seq_len, final=final),
        out_shape=jax.ShapeDtypeStruct((t, d), F32),
        grid=(t // tm,),
        in_specs=[
            pl.BlockSpec((nj, 3, tf), lambda i: (0, 0, 0)),
            pl.BlockSpec((nj, 1, tf), lambda i: (0, 0, 0)),
            pl.BlockSpec((None, N_MOD, d), lambda i: (cond_row_fn(i), 0, 0)),
            pl.BlockSpec((1, d), lambda i: (0, 0)),
            hbm, hbm, hbm, hbm, hbm,
        ],
        out_specs=pl.BlockSpec((tm, d), lambda i: (i, 0)),
        scratch_shapes=[pltpu.VMEM((2, tm + 2 * HALO, d), BF16), pltpu.VMEM((tm, d), F32),
                        pltpu.VMEM((2, d, tf), BF16), pltpu.VMEM((2, d, tf), BF16), pltpu.VMEM((2, tf, d), BF16),
                        pltpu.SemaphoreType.DMA((7, 2))],
        compiler_params=_cparams(("arbitrary",)),
        name="conv_ffn",
    )(cw3, cb3, mod_l, final_g, h2, x1, w_gate_b, w_up_b, w_down_b)


def _rope_tables(n, head_dim):
    rows = n // GRID_W
    t_row = jnp.repeat(jnp.arange(rows, dtype=F32), GRID_W)
    t_col = jnp.tile(jnp.arange(GRID_W, dtype=F32), rows)
    axis_dim = head_dim // 2
    inv = jnp.power(ROPE_BASE, -jnp.arange(0, axis_dim, 2, dtype=F32) / axis_dim)
    ar = t_row[:, None] * inv[None, :]
    ac = t_col[:, None] * inv[None, :]
    ang = jnp.concatenate([ar, ar, ac, ac], axis=-1)
    reps = LANES // head_dim
    quarter = head_dim // 4
    sign = jnp.where((jnp.arange(head_dim) % (2 * quarter)) < quarter, -1.0, 1.0).astype(F32)
    cos = jnp.tile(jnp.cos(ang), (1, reps))
    sin_signed = jnp.tile(jnp.sin(ang) * sign[None, :], (1, reps))
    return cos, sin_signed


def kernel(x_prompt, x_sample, cache_attn_k, cache_attn_v, cache_diff_k, cache_diff_v, c, c_ctx, norm1_g, norm2_g, w_ada, b_ada, w_in, attn_q_norm_g, attn_k_norm_g, diff_lambda_q1, diff_lambda_k1, diff_lambda_q2, diff_lambda_k2, diff_subnorm_g, w_out, ffn_w_gate, ffn_w_up, ffn_conv_w, ffn_conv_b, ffn_w_down, final_norm_g):
    depth = w_in.shape[0]
    bc, lc, d = x_prompt.shape
    bl, ll, _ = x_sample.shape
    past = cache_attn_k.shape[2]

    w_in_b = w_in.astype(BF16)
    w_out_b = w_out.astype(BF16)
    w_gate_b = ffn_w_gate.astype(BF16)
    w_up_b = ffn_w_up.astype(BF16)
    w_down_b = ffn_w_down.astype(BF16)

    n_rows = 8 * ((1 + bl + 7) // 8)
    cvec = jnp.concatenate([c_ctx[None, :], c, jnp.zeros((n_rows - 1 - bl, d), F32)], axis=0)
    mod = _ada_call(cvec, w_ada, b_ada).reshape(depth, n_rows, N_MOD, d)

    dft_c = jnp.concatenate(_cos_sin(FOURIER_GROUP_DIM, FOURIER_GROUP_DIM, FOURIER_GROUP_DIM), axis=1).astype(BF16)
    rope_tabs = _rope_tables(ll, HEAD_DIM_A) + _rope_tables(ll, DK_B)
    caches = (cache_attn_k.reshape(bl, depth, past, N_KV_HEADS_A * HEAD_DIM_A),
              cache_attn_v.reshape(bl, depth, past, N_KV_HEADS_A * HEAD_DIM_A),
              cache_diff_k.reshape(bl, depth, past, N_HEADS_B * 2 * DK_B),
              cache_diff_v.reshape(bl, depth, past, N_HEADS_B * DV_B))
    lam_all = jnp.stack([diff_lambda_q1, diff_lambda_k1, diff_lambda_q2, diff_lambda_k2], axis=1)

    def run_pass(x3, is_ctx):
        b, n, _ = x3.shape
        t = b * n
        tm = ROW_TILE
        tm_ffn = FFN_ROW_TILE
        x = x3.reshape(t, d)

        def cond_rows(tile):
            return (lambda i: 0) if is_ctx else (lambda i: 1 + (i * tile) // n)

        cond_row_fn = cond_rows(tm)
        new_kv = None
        for l in range(depth):
            lam_init = 0.8 - 0.6 * math.exp(-0.3 * l)
            res = _in_call(x, mod[l], cond_row_fn, norm1_g[l][None, :], (w_in_b, l),
                           attn_q_norm_g[l][None, :], attn_k_norm_g[l][None, :], dft_c,
                           None if is_ctx else rope_tabs, n, (depth, l, new_kv) if is_ctx else None, tm)
            qkv, y12 = res[0], res[1]
            if is_ctx:
                new_kv = res[2:]
            qkv3 = qkv.reshape(b, n, qkv.shape[1])
            att_a, att_b = _attn_calls(qkv3, None if is_ctx else caches, l, lam_all[l],
                                       diff_subnorm_g[l][None, :], lam_init,
                                       tq_a=min(ATTN_TQ_A, n), tq_b=min(ATTN_TQ_B, n), kc=ATTN_KC,
                                       ha=N_KV_HEADS_A if is_ctx else 1, hb=N_HEADS_B if is_ctx else 1,
                                       split_a=1 if is_ctx else ATTN_STREAMS, split_b=1 if is_ctx else ATTN_STREAMS,
                                       nb=math.gcd(b, CTX_SEQS) if is_ctx else 1)
            if is_ctx:
                four = _fourier_call(y12.reshape(b, n, y12.shape[1]), n, 1)
            else:
                four = _fourier_call(y12.reshape(b, n, y12.shape[1]), GRID_W, n // GRID_W, tm // GRID_W)
            x1, h2 = _out_call(att_a.reshape(t, -1), att_b.reshape(t, -1), four.reshape(t, -1), (w_out_b, l),
                               x, mod[l], cond_row_fn, norm2_g[l][None, :], tm)
            x = _ffn_call(h2, x1, (w_gate_b, w_up_b, w_down_b, l), ffn_conv_w[l], ffn_conv_b[l][None, :],
                          mod[l], cond_rows(tm_ffn), final_norm_g[None, :], n, l == depth - 1, tm_ffn, FFN_TF)
        return x.reshape(b, n, d), new_kv

    y_prompt, kvs = run_pass(x_prompt, True)
    new_attn_k = kvs[0].reshape(bc, depth, lc, N_KV_HEADS_A, HEAD_DIM_A)
    new_attn_v = kvs[1].reshape(bc, depth, lc, N_KV_HEADS_A, HEAD_DIM_A)
    new_diff_k = kvs[2].reshape(bc, depth, lc, N_HEADS_B, 2 * DK_B)
    new_diff_v = kvs[3].reshape(bc, depth, lc, N_HEADS_B, DV_B)

    y_sample, _ = run_pass(x_sample, False)
    return (y_prompt, y_sample, new_attn_k, new_attn_v, new_diff_k, new_diff_v)
```
